```python
import math
import jax, jax.numpy as jnp
from jax import lax
import numpy as np

D_MODEL = 1024
BATCH = 8
SEQ = 2048
DEPTH = 1

CHUNK = 64
Q_BLOCK = 128
EPS = 1e-6
ATTN_HEADS = 16
HEAD_DIM = 64
Q_RANK = 256
KV_RANK = 128
IDX_HEADS = 16
IDX_DIM = 64
TOPK_MAX = 256
NUM_BUCKETS = 32
MAX_DISTANCE = 128
D_INNER = 2 * D_MODEL
SSD_HEADDIM = 64
SSD_HEADS = D_INNER // SSD_HEADDIM
SSD_GROUPS = 8
D_STATE = 128
CONV_W = 4
CONV_DIM = D_INNER + 2 * SSD_GROUPS * D_STATE
D_FF = -(-8 * D_MODEL // (3 * 256)) * 256
IN_SIZES = [Q_RANK, KV_RANK, IDX_DIM, IDX_HEADS, D_INNER, CONV_DIM, SSD_HEADS, D_MODEL, D_MODEL]
IN_COLS = int(sum(IN_SIZES))
IN_SPLITS = [int(v) for v in np.cumsum(IN_SIZES)[:-1]]

kernel_name = "hybrid_dsa_ssd_gated_block"


def rmsnorm(x, w):
    x32 = x.astype(jnp.float32)
    y = x32 * lax.rsqrt(jnp.mean(x32 * x32, axis=-1, keepdims=True) + EPS)
    return (y * w.astype(jnp.float32)).astype(x.dtype)


def t5_bucket(rel):
    half = NUM_BUCKETS // 2
    max_exact = half // 2
    side = jnp.where(rel > 0, half, 0)
    n = jnp.abs(rel)
    large = max_exact + (jnp.log(jnp.maximum(n, max_exact).astype(jnp.float32) / max_exact)
                         / math.log(MAX_DISTANCE / max_exact) * (half - max_exact)).astype(jnp.int32)
    large = jnp.minimum(large, half - 1)
    return side + jnp.where(n < max_exact, n, large)


def sparse_mla_attention(q_lat, kv_lat, k_idx_raw, w_idx_raw, positions,
                         q_norm, kv_norm, w_uq, w_uv, w_qidx, kidx_norm, rel_bias):
    Bsz, L, _ = q_lat.shape
    nb = L // Q_BLOCK
    k_sel_n = min(TOPK_MAX, L // 4)
    qn = rmsnorm(q_lat, q_norm)
    q = (qn @ w_uq).reshape(Bsz, L, ATTN_HEADS, KV_RANK)
    q_idx = (qn @ w_qidx).reshape(Bsz, L, IDX_HEADS, IDX_DIM)
    kv = rmsnorm(kv_lat, kv_norm)
    k_idx = rmsnorm(k_idx_raw, kidx_norm)
    w_idx = w_idx_raw * (IDX_HEADS ** -0.5 * IDX_DIM ** -0.5)
    key_chunk = positions // CHUNK
    scale = KV_RANK ** -0.5

    def to_blocks(a):
        return jnp.moveaxis(a.reshape((Bsz, nb, Q_BLOCK) + a.shape[2:]), 1, 0)

    def block_fn(args):
        q_b, qi_b, w_b, pos_b = args
        q_chunk = pos_b // CHUNK
        rel = jax.nn.relu(jnp.einsum('bqhd,bsd->bqhs', qi_b, k_idx).astype(jnp.float32))
        iscore = jnp.einsum('bqh,bqhs->bqs', w_b.astype(jnp.float32), rel)
        admissible = key_chunk[:, None, :] <= q_chunk[:, :, None]
        iscore = jnp.where(admissible, iscore, -jnp.inf)
        _, idx = lax.top_k(iscore, k_sel_n)
        kv_sel = jax.vmap(lambda kv_b, i_b: kv_b[i_b])(kv, idx)
        pos_sel = jax.vmap(lambda p_b, i_b: p_b[i_b])(positions, idx)
        valid = (pos_sel // CHUNK) <= q_chunk[..., None]
        bias = rel_bias[t5_bucket(pos_sel - pos_b[..., None])]
        logits = jnp.einsum('bqhc,bqkc->bhqk', q_b, kv_sel).astype(jnp.float32) * scale
        logits = logits + jnp.transpose(bias, (0, 3, 1, 2)).astype(jnp.float32)
        logits = jnp.where(valid[:, None], logits, -jnp.inf)
        probs = jax.nn.softmax(logits, axis=-1).astype(kv_sel.dtype)
        return jnp.einsum('bhqk,bqkc->bqhc', probs, kv_sel)

    o = lax.map(block_fn, (to_blocks(q), to_blocks(q_idx), to_blocks(w_idx), to_blocks(positions)))
    o = jnp.moveaxis(o, 0, 1).reshape(Bsz, L, ATTN_HEADS, KV_RANK)
    o = jnp.einsum('blhc,hcd->blhd', o, w_uv)
    return o.reshape(Bsz, L, ATTN_HEADS * HEAD_DIM)


def ssd_mixer(z, xbc, dt_raw, conv_w, conv_b, dt_bias, a_log, d_skip, ssd_norm):
    Bsz, L, _ = xbc.shape
    nc = L // CHUNK
    R = SSD_HEADS // SSD_GROUPS
    xbc = lax.conv_general_dilated(xbc, conv_w[:, None, :], window_strides=(1,),
                                   padding=[(CONV_W - 1, 0)],
                                   dimension_numbers=('NWC', 'WIO', 'NWC'),
                                   feature_group_count=CONV_DIM) + conv_b
    xbc = jax.nn.silu(xbc)
    xs, Bm, Cm = jnp.split(xbc, [D_INNER, D_INNER + SSD_GROUPS * D_STATE], axis=-1)
    dt = jax.nn.softplus((dt_raw + dt_bias).astype(jnp.float32))
    A = -jnp.exp(a_log.astype(jnp.float32))
    xh = xs.reshape(Bsz, L, SSD_HEADS, SSD_HEADDIM)
    X = (xh * dt[..., None]).reshape(Bsz, nc, CHUNK, SSD_GROUPS, R, SSD_HEADDIM)
    Bc = Bm.reshape(Bsz, nc, CHUNK, SSD_GROUPS, D_STATE)
    Cc = Cm.reshape(Bsz, nc, CHUNK, SSD_GROUPS, D_STATE)
    dA = (dt * A).reshape(Bsz, nc, CHUNK, SSD_GROUPS, R).transpose(0, 1, 3, 4, 2)
    a_cum = jnp.cumsum(dA, axis=-1)
    causal = jnp.tril(jnp.ones((CHUNK, CHUNK), dtype=bool))
    seg = jnp.where(causal, a_cum[..., :, None] - a_cum[..., None, :], -jnp.inf)
    Lmat = jnp.exp(seg)
    CB = jnp.einsum('bclgn,bcsgn->bcgls', Cc, Bc)
    y_diag = jnp.einsum('bcgls,bcgrls,bcsgrp->bclgrp', CB, Lmat, X)
    decay = jnp.exp(a_cum[..., -1:] - a_cum)
    states = jnp.einsum('bclgn,bcgrl,bclgrp->bcgrpn', Bc, decay, X)
    chunk_decay = jnp.exp(a_cum[..., -1])

    def step(h, inp):
        s_c, d_c = inp
        return h * d_c[..., None, None] + s_c, h

    h0 = jnp.zeros_like(states[:, 0])
    _, prev = lax.scan(step, h0, (jnp.moveaxis(states, 1, 0), jnp.moveaxis(chunk_decay, 1, 0)))
    prev = jnp.moveaxis(prev, 0, 1)
    y_off = jnp.einsum('bclgn,bcgrpn,bcgrl->bclgrp', Cc, prev, jnp.exp(a_cum))
    y = (y_diag + y_off).reshape(Bsz, L, SSD_HEADS, SSD_HEADDIM) + d_skip[:, None] * xh
    y = y.reshape(Bsz, L, D_INNER)
    return rmsnorm(y * jax.nn.silu(z), ssd_norm)


def setup_inputs(seed: int = 0) -> dict:
    key = jax.random.key(seed)
    ks = jax.random.split(key, 32)
    nrm = lambda k, shape, s: jax.random.normal(k, shape, jnp.float32) * s
    gain = lambda k, n: 1.0 + 0.05 * jax.random.normal(k, (n,), jnp.float32)
    offset = jax.random.randint(ks[2], (BATCH, 1), 0, 16) * CHUNK
    positions = (offset + jnp.arange(SEQ, dtype=jnp.int32)[None, :]).astype(jnp.int32)
    dt0 = jnp.exp(jax.random.uniform(ks[20], (SSD_HEADS,), jnp.float32, math.log(1e-3), math.log(1e-1)))
    return {
        "x": nrm(ks[0], (BATCH, SEQ, D_MODEL), 1.0),
        "c": nrm(ks[1], (BATCH, D_MODEL), 1.0),
        "positions": positions,
        "ada_w": nrm(ks[3], (D_MODEL, 6 * D_MODEL), 0.5 * D_MODEL ** -0.5),
        "ada_b": nrm(ks[4], (6 * D_MODEL,), 0.02),
        "pre_norm_mix": gain(ks[5], D_MODEL),
        "post_norm_mix": gain(ks[6], D_MODEL),
        "pre_norm_ffn": gain(ks[7], D_MODEL),
        "post_norm_ffn": gain(ks[8], D_MODEL),
        "w_in": nrm(ks[9], (D_MODEL, IN_COLS), D_MODEL ** -0.5),
        "q_norm": gain(ks[10], Q_RANK),
        "kv_norm": gain(ks[11], KV_RANK),
        "w_uq": nrm(ks[12], (Q_RANK, ATTN_HEADS * KV_RANK), Q_RANK ** -0.5),
        "w_uv": nrm(ks[13], (ATTN_HEADS, KV_RANK, HEAD_DIM), KV_RANK ** -0.5),
        "rel_bias": nrm(ks[14], (NUM_BUCKETS, ATTN_HEADS), 0.3),
        "w_qidx": nrm(ks[15], (Q_RANK, IDX_HEADS * IDX_DIM), Q_RANK ** -0.5),
        "kidx_norm": gain(ks[16], IDX_DIM),
        "conv_w": nrm(ks[17], (CONV_W, CONV_DIM), CONV_W ** -0.5),
        "conv_b": nrm(ks[18], (CONV_DIM,), 0.02),
        "dt_bias": dt0 + jnp.log(-jnp.expm1(-dt0)),
        "a_log": jnp.log(jax.random.uniform(ks[21], (SSD_HEADS,), jnp.float32, 1.0, 16.0)),
        "d_skip": 1.0 + 0.1 * jax.random.normal(ks[22], (SSD_HEADS,), jnp.float32),
        "ssd_norm": gain(ks[23], D_INNER),
        "w_o_attn": nrm(ks[24], (ATTN_HEADS * HEAD_DIM, D_MODEL), (ATTN_HEADS * HEAD_DIM) ** -0.5),
        "w_o_ssd": nrm(ks[25], (D_INNER, D_MODEL), D_INNER ** -0.5),
        "w_out": nrm(ks[26], (D_MODEL, D_MODEL), D_MODEL ** -0.5),
        "w_ffn_in": nrm(ks[27], (D_MODEL, 2 * D_FF), D_MODEL ** -0.5),
        "w_ffn_out": nrm(ks[28], (D_FF, D_MODEL), D_FF ** -0.5),
    }


def reference(x, c, positions, ada_w, ada_b, pre_norm_mix, post_norm_mix, pre_norm_ffn,
              post_norm_ffn, w_in, q_norm, kv_norm, w_uq, w_uv, rel_bias, w_qidx, kidx_norm,
              conv_w, conv_b, dt_bias, a_log, d_skip, ssd_norm, w_o_attn, w_o_ssd, w_out,
              w_ffn_in, w_ffn_out):
    mod = (jax.nn.silu(c) @ ada_w + ada_b)[:, None, :]
    shift_m, scale_m, gate_m, shift_f, scale_f, gate_f = jnp.split(mod, 6, axis=-1)
    for _ in range(DEPTH):
        h = rmsnorm(x, pre_norm_mix) * (1.0 + scale_m) + shift_m
        proj = h @ w_in
        q_lat, kv_lat, k_idx, w_idx, z, xbc, dt_raw, g_a, g_b = jnp.split(proj, IN_SPLITS, axis=-1)
        y_a = sparse_mla_attention(q_lat, kv_lat, k_idx, w_idx, positions, q_norm, kv_norm,
                                   w_uq, w_uv, w_qidx, kidx_norm, rel_bias) @ w_o_attn
        y_b = ssd_mixer(z, xbc, dt_raw, conv_w, conv_b, dt_bias, a_log, d_skip, ssd_norm) @ w_o_ssd
        mix = (jax.nn.sigmoid(g_a) * y_a + jax.nn.sigmoid(g_b) * y_b) @ w_out
        x = x + gate_m * rmsnorm(mix, post_norm_mix)
        h2 = rmsnorm(x, pre_norm_ffn) * (1.0 + scale_f) + shift_f
        u_gate, u_up = jnp.split(h2 @ w_ffn_in, 2, axis=-1)
        f = (jax.nn.silu(u_gate) * u_up) @ w_ffn_out
        x = x + gate_f * rmsnorm(f, post_norm_ffn)
    return x
```

```python
import functools
import math

import numpy as np
import jax
import jax.numpy as jnp
from jax import lax
from jax.experimental import pallas as pl
from jax.experimental.pallas import tpu as pltpu

F32 = jnp.float32
BF16 = jnp.bfloat16

D_MODEL = 1024
CHUNK = 64
Q_BLOCK = 128
EPS = 1e-6
ATTN_HEADS = 16
HEAD_DIM = 64
Q_RANK = 256
KV_RANK = 128
IDX_HEADS = 16
IDX_DIM = 64
TOPK_MAX = 256
NUM_BUCKETS = 32
MAX_DISTANCE = 128
D_INNER = 2 * D_MODEL
SSD_HEADDIM = 64
SSD_HEADS = D_INNER // SSD_HEADDIM
SSD_GROUPS = 8
D_STATE = 128
CONV_W = 4
CONV_DIM = D_INNER + 2 * SSD_GROUPS * D_STATE
D_FF = -(-8 * D_MODEL // (3 * 256)) * 256

LANES = 128
KEY_TILE = 256
SSD_Q = 128
VMEM_LIMIT = 56 * 1024 * 1024
BISECT_MAX_ITERS = 320

COL_HEAD, COL_GA, COL_Z, COL_XBC, COL_GB = 0, 1024, 2048, 4096, 8192
PROJ_COLS = 9216
HEAD_KV, HEAD_KIDX, HEAD_W, HEAD_DT = 256, 384, 512, 640


def _cparams(sem):
    return pltpu.CompilerParams(dimension_semantics=sem, vmem_limit_bytes=VMEM_LIMIT)


def _const_spec(shape):
    nd = len(shape)
    return pl.BlockSpec(shape, lambda *_: (0,) * nd, pipeline_mode=pl.Buffered(1))


def _rms(x, w, n=None):
    n = x.shape[-1] if n is None else n
    return x * lax.rsqrt(jnp.sum(x * x, axis=-1, keepdims=True) * (1.0 / n) + EPS) * w


def _silu(x):
    return x * jax.nn.sigmoid(x)


def _mod_kernel(c_ref, w_ref, b_ref, o_ref):
    c = c_ref[...]
    s = _silu(c).astype(BF16)
    o_ref[...] = jnp.dot(s, w_ref[...].astype(BF16), preferred_element_type=F32) + b_ref[...]


def _mod(c, ada_w, ada_b):
    bsz = c.shape[0]
    return pl.pallas_call(
        _mod_kernel,
        grid=(6,),
        in_specs=[pl.BlockSpec((bsz, D_MODEL), lambda j: (0, 0)),
                  pl.BlockSpec((D_MODEL, D_MODEL), lambda j: (0, j)),
                  pl.BlockSpec((1, D_MODEL), lambda j: (0, j))],
        out_specs=pl.BlockSpec((bsz, D_MODEL), lambda j: (0, j)),
        out_shape=jax.ShapeDtypeStruct((bsz, 6 * D_MODEL), F32),
        compiler_params=_cparams(("parallel",)),
        name="mod",
    )(c, ada_w, ada_b.reshape(1, 6 * D_MODEL))


def _t5_bucket_np(rel):
    half = NUM_BUCKETS // 2
    max_exact = half // 2
    side = np.where(rel > 0, half, 0)
    n = np.abs(rel)
    large = max_exact + (np.log(np.maximum(n, max_exact).astype(np.float64) / max_exact)
                         / math.log(MAX_DISTANCE / max_exact) * (half - max_exact)).astype(np.int64)
    large = np.minimum(large, half - 1)
    return (side + np.where(n < max_exact, n, large)).astype(np.int32)


def _bias_kernel(idx_ref, rb_ref, o_ref):
    h = pl.program_id(0)
    idx = idx_ref[...]
    far = rb_ref[h, NUM_BUCKETS // 2 - 1]
    acc = jnp.zeros(idx.shape, F32)
    for b in range(NUM_BUCKETS):
        acc = jnp.where(idx == b, rb_ref[h, b] - far, acc)
    o_ref[0] = acc


def _bias_tables(rel_bias):
    ql = np.arange(Q_BLOCK)[:, None]
    kk = np.arange(4 * LANES)[None, :]
    idx = jnp.asarray(_t5_bucket_np(kk - 2 * LANES - ql))
    t = pl.pallas_call(
        _bias_kernel,
        grid=(ATTN_HEADS,),
        in_specs=[pl.BlockSpec((Q_BLOCK, 4 * LANES), lambda h: (0, 0)),
                  pl.BlockSpec(memory_space=pltpu.SMEM)],
        out_specs=pl.BlockSpec((1, Q_BLOCK, 4 * LANES), lambda h: (h, 0, 0)),
        out_shape=jax.ShapeDtypeStruct((ATTN_HEADS, Q_BLOCK, 4 * LANES), F32),
        compiler_params=_cparams(("arbitrary",)),
        name="bias",
    )(idx, rel_bias.T)
    return jnp.stack([t[..., 256:512], t[..., 128:384], t[..., 0:256], jnp.zeros_like(t[..., 0:256])])


def _inproj_kernel(x_ref, mod_ref, nw_ref, w_ref, o_ref, hn_ref):
    @pl.when(pl.program_id(2) == 0)
    def _():
        m = mod_ref[0]
        y = _rms(x_ref[0], nw_ref[...])
        hn_ref[...] = (y * (1.0 + m[1:2]) + m[0:1]).astype(BF16)

    o_ref[0] = jnp.dot(hn_ref[...], w_ref[...], preferred_element_type=F32)


def _pack_w_in(w_in):
    sizes = [Q_RANK, KV_RANK, IDX_DIM, IDX_HEADS, D_INNER, CONV_DIM, SSD_HEADS, D_MODEL, D_MODEL]
    offs = np.cumsum([0] + sizes)
    q, kv, ki, wi, z, xbc, dt, ga, gb = [w_in[:, offs[i]:offs[i + 1]] for i in range(9)]

    def zc(n):
        return jnp.zeros((D_MODEL, n), w_in.dtype)

    head = jnp.concatenate([q, kv, ki, zc(LANES - IDX_DIM), wi, zc(LANES - IDX_HEADS),
                            dt, zc(LANES - SSD_HEADS), zc(2 * LANES)], axis=1)
    return jnp.concatenate([head, ga, z, xbc, gb], axis=1).astype(BF16)


def _inproj(x, mod3, pre_norm, w_packed):
    bsz, seq, _ = x.shape
    tm = min(seq, 1024)
    tn = 1024
    return pl.pallas_call(
        _inproj_kernel,
        grid=(bsz, seq // tm, PROJ_COLS // tn),
        in_specs=[pl.BlockSpec((1, tm, D_MODEL), lambda b, i, j: (b, i, 0)),
                  pl.BlockSpec((1, 6, D_MODEL), lambda b, i, j: (b, 0, 0)),
                  pl.BlockSpec((1, D_MODEL), lambda b, i, j: (0, 0)),
                  pl.BlockSpec((D_MODEL, tn), lambda b, i, j: (0, j))],
        out_specs=pl.BlockSpec((1, tm, tn), lambda b, i, j: (b, i, j)),
        out_shape=jax.ShapeDtypeStruct((bsz, seq, PROJ_COLS), F32),
        scratch_shapes=[pltpu.VMEM((tm, D_MODEL), BF16)],
        compiler_params=_cparams(("parallel", "parallel", "arbitrary")),
        name="inproj",
    )(x, mod3, pre_norm.reshape(1, D_MODEL), w_packed)


def _prep_kernel(p_ref, qn_ref, kvn_ref, kin_ref, wuq_ref, wqi_ref,
                 q_ref, qi_ref, kv_ref, kvt_ref, kit_ref, w_ref, *, tc):
    p = p_ref[0]
    qn = _rms(p[:, :Q_RANK], qn_ref[...]).astype(BF16)
    qf = jnp.dot(qn, wuq_ref[...], preferred_element_type=F32) * (KV_RANK ** -0.5)
    qif = jnp.dot(qn, wqi_ref[...], preferred_element_type=F32)
    for h in range(ATTN_HEADS):
        q_ref[0, h] = qf[:, h * LANES:(h + 1) * LANES].astype(BF16)
        qi_ref[0, h] = qif[:, h * LANES:(h + 1) * LANES].astype(BF16)
    kv = _rms(p[:, HEAD_KV:HEAD_KV + KV_RANK], kvn_ref[...])
    kv_ref[0] = kv.astype(BF16)
    kvt = kv.T
    ki = _rms(p[:, HEAD_KIDX:HEAD_KIDX + LANES], kin_ref[...], n=IDX_DIM)
    kit = ki.T
    for c in range(tc // KEY_TILE):
        kvt_ref[0, c] = kvt[:, c * KEY_TILE:(c + 1) * KEY_TILE].astype(BF16)
        kit_ref[0, c] = kit[:, c * KEY_TILE:(c + 1) * KEY_TILE].astype(BF16)
    w_ref[0] = p[:, HEAD_W:HEAD_W + LANES] * (IDX_HEADS ** -0.5 * IDX_DIM ** -0.5)


def _prep(proj, q_norm, kv_norm, kidx_norm, w_uq, w_qidx):
    bsz, seq, _ = proj.shape
    tc = min(seq, 512)
    wqi = w_qidx.reshape(Q_RANK, IDX_HEADS, IDX_DIM)
    wqi = jnp.concatenate([wqi, jnp.zeros_like(wqi)], axis=-1).reshape(Q_RANK, IDX_HEADS * LANES).astype(BF16)
    kin = jnp.concatenate([kidx_norm, jnp.zeros((LANES - IDX_DIM,), F32)]).reshape(1, LANES)
    hm = jax.ShapeDtypeStruct((bsz, ATTN_HEADS, seq, LANES), BF16)
    tr = jax.ShapeDtypeStruct((bsz, seq // KEY_TILE, LANES, KEY_TILE), BF16)
    hm_spec = pl.BlockSpec((1, ATTN_HEADS, tc, LANES), lambda b, i: (b, 0, i, 0))
    tr_spec = pl.BlockSpec((1, tc // KEY_TILE, LANES, KEY_TILE), lambda b, i: (b, i, 0, 0))
    row_spec = pl.BlockSpec((1, tc, LANES), lambda b, i: (b, i, 0))
    return pl.pallas_call(
        functools.partial(_prep_kernel, tc=tc),
        grid=(bsz, seq // tc),
        in_specs=[pl.BlockSpec((1, tc, 1024), lambda b, i: (b, i, 0)),
                  _const_spec((1, Q_RANK)), _const_spec((1, KV_RANK)), _const_spec((1, LANES)),
                  _const_spec((Q_RANK, ATTN_HEADS * KV_RANK)), _const_spec((Q_RANK, IDX_HEADS * LANES))],
        out_specs=[hm_spec, hm_spec, row_spec, tr_spec, tr_spec, row_spec],
        out_shape=[hm, hm, jax.ShapeDtypeStruct((bsz, seq, LANES), BF16), tr, tr,
                   jax.ShapeDtypeStruct((bsz, seq, LANES), F32)],
        compiler_params=_cparams(("parallel", "parallel")),
        name="prep",
    )(proj, q_norm.reshape(1, Q_RANK), kv_norm.reshape(1, KV_RANK), kin, w_uq.astype(BF16), wqi)


def _dsa_kernel(q_ref, qi_ref, w_ref, kv_ref, kvt_ref, kit_ref, tb_ref, wuv_ref, o_ref,
                isc_ref, sbuf_ref, pbuf_ref, mrun_ref, lrun_ref, acc_ref, *, k_sel):
    i = pl.program_id(1)
    last = i // 2
    odd = i % 2
    n_tiles = last + 1
    qs = q_ref[0].reshape(ATTN_HEADS * Q_BLOCK, KV_RANK)
    qis = qi_ref[0].reshape(IDX_HEADS * Q_BLOCK, LANES)
    w = w_ref[0]
    row = lax.broadcasted_iota(jnp.int32, (Q_BLOCK, KEY_TILE), 0)
    col = lax.broadcasted_iota(jnp.int32, (Q_BLOCK, KEY_TILE), 1)
    key_limit = i * Q_BLOCK + jnp.where(row < CHUNK, CHUNK, 2 * CHUNK)

    def idx_body(j, carry):
        s = jnp.dot(qis, kit_ref[0, j], preferred_element_type=F32)
        acc = jnp.zeros((Q_BLOCK, KEY_TILE), F32)
        for h in range(IDX_HEADS):
            acc = acc + w[:, h:h + 1] * jnp.maximum(s[h * Q_BLOCK:(h + 1) * Q_BLOCK], 0.0)
        isc_ref[j] = jnp.where(j * KEY_TILE + col < key_limit, acc, -jnp.inf)
        return carry

    lax.fori_loop(0, n_tiles, idx_body, 0)

    def halves(x):
        return x[:, :LANES], x[:, LANES:]

    def lane_all(x, op):
        return jnp.broadcast_to(op(x, axis=-1, keepdims=True), (Q_BLOCK, LANES))

    def count_ge(t):
        def body(j, c):
            a, b = halves(isc_ref[j])
            return c + jnp.where(a >= t, 1.0, 0.0) + jnp.where(b >= t, 1.0, 0.0)
        return lane_all(lax.fori_loop(0, n_tiles, body, jnp.zeros((Q_BLOCK, LANES), F32)), jnp.sum)

    def minmax_body(j, c):
        lo, hi = c
        a, b = halves(isc_ref[j])
        lo = jnp.minimum(lo, jnp.minimum(jnp.where(a == -jnp.inf, jnp.inf, a),
                                         jnp.where(b == -jnp.inf, jnp.inf, b)))
        return lo, jnp.maximum(hi, jnp.maximum(a, b))

    lo, hi = lax.fori_loop(0, n_tiles, minmax_body,
                           (jnp.full((Q_BLOCK, LANES), jnp.inf, F32), jnp.full((Q_BLOCK, LANES), -jnp.inf, F32)))
    lo = lane_all(lo, jnp.min)
    hi = lane_all(hi, jnp.max)
    kf = float(k_sel)

    def bis_cond(c):
        _, _, cnt, stalled, it = c
        active = jnp.where(jnp.logical_and(cnt > kf, stalled == 0.0), 1.0, 0.0)
        return jnp.logical_and(it < BISECT_MAX_ITERS, jnp.max(active) > 0.0)

    def bis_body(c):
        lo, hi, cnt, stalled, it = c
        mid = 0.5 * lo + 0.5 * hi
        cm = count_ge(mid)
        active = jnp.logical_and(cnt > kf, stalled == 0.0)
        noprog = jnp.logical_or(mid <= lo, mid >= hi)
        move = jnp.logical_and(active, jnp.logical_not(noprog))
        up = jnp.logical_and(move, cm >= kf)
        down = jnp.logical_and(move, cm < kf)
        return (jnp.where(up, mid, lo), jnp.where(down, mid, hi), jnp.where(up, cm, cnt),
                jnp.where(jnp.logical_and(active, noprog), 1.0, stalled), it + 1)

    thr, _, _, _, _ = lax.while_loop(
        bis_cond, bis_body, (lo, hi, count_ge(lo), jnp.zeros((Q_BLOCK, LANES), F32), jnp.int32(0)))

    for h in range(ATTN_HEADS):
        mrun_ref[h] = jnp.full((Q_BLOCK, LANES), -jnp.inf, F32)
        lrun_ref[h] = jnp.zeros((Q_BLOCK, LANES), F32)
    acc_ref[...] = jnp.zeros_like(acc_ref)

    def p1_body(j, carry):
        table = jnp.where(j == last, odd, jnp.where(jnp.logical_and(j == last - 1, odd == 0), 2, 3))
        s = jnp.dot(qs, kvt_ref[0, j], preferred_element_type=F32)
        a, b = halves(isc_ref[j])
        neg = jnp.concatenate([jnp.where(a >= thr, 0.0, -jnp.inf), jnp.where(b >= thr, 0.0, -jnp.inf)], axis=1)
        for h in range(ATTN_HEADS):
            sh = s[h * Q_BLOCK:(h + 1) * Q_BLOCK] + neg + tb_ref[table, h]
            sbuf_ref[j, h] = sh
            sa, sb = halves(sh)
            mrun_ref[h] = jnp.maximum(mrun_ref[h], jnp.maximum(sa, sb))
        return carry

    lax.fori_loop(0, n_tiles, p1_body, 0)
    for h in range(ATTN_HEADS):
        mrun_ref[h] = lane_all(mrun_ref[h], jnp.max)

    def p2_body(j, carry):
        for h in range(ATTN_HEADS):
            sa, sb = halves(sbuf_ref[j, h])
            m = mrun_ref[h]
            pa = jnp.exp(sa - m)
            pb = jnp.exp(sb - m)
            lrun_ref[h] = lrun_ref[h] + pa + pb
            pbuf_ref[h * Q_BLOCK:(h + 1) * Q_BLOCK, :] = jnp.concatenate([pa, pb], axis=1).astype(BF16)
        kv_tile = kv_ref[0, pl.ds(pl.multiple_of(j * KEY_TILE, KEY_TILE), KEY_TILE), :]
        acc_ref[...] += jnp.dot(pbuf_ref[...], kv_tile, preferred_element_type=F32)
        return carry

    lax.fori_loop(0, n_tiles, p2_body, 0)

    outs = []
    for h in range(ATTN_HEADS):
        l = jnp.sum(lrun_ref[h], axis=-1, keepdims=True)
        outs.append((acc_ref[h * Q_BLOCK:(h + 1) * Q_BLOCK, :] / l).astype(BF16))
    o_ref[0] = jnp.dot(jnp.concatenate(outs, axis=1), wuv_ref[...], preferred_element_type=F32).astype(BF16)


def _pack_w_uv(w_uv):
    eye = jnp.eye(ATTN_HEADS, dtype=w_uv.dtype)
    w = w_uv[:, :, None, :] * eye[:, None, :, None]
    return w.reshape(ATTN_HEADS * KV_RANK, ATTN_HEADS * HEAD_DIM).astype(BF16)


def _dsa(q, qi, w, kv, kvt, kit, tables, wuv):
    bsz, _, seq, _ = q.shape
    nkt = seq // KEY_TILE
    k_sel = min(TOPK_MAX, seq // 4)
    hm_spec = pl.BlockSpec((1, ATTN_HEADS, Q_BLOCK, LANES), lambda b, i: (b, 0, i, 0))
    tr_spec = pl.BlockSpec((1, nkt, LANES, KEY_TILE), lambda b, i: (b, 0, 0, 0))
    return pl.pallas_call(
        functools.partial(_dsa_kernel, k_sel=k_sel),
        grid=(bsz, seq // Q_BLOCK),
        in_specs=[hm_spec, hm_spec,
                  pl.BlockSpec((1, Q_BLOCK, LANES), lambda b, i: (b, i, 0)),
                  pl.BlockSpec((1, seq, KV_RANK), lambda b, i: (b, 0, 0)),
                  tr_spec, tr_spec,
                  _const_spec((4, ATTN_HEADS, Q_BLOCK, KEY_TILE)),
                  _const_spec((ATTN_HEADS * KV_RANK, ATTN_HEADS * HEAD_DIM))],
        out_specs=pl.BlockSpec((1, Q_BLOCK, ATTN_HEADS * HEAD_DIM), lambda b, i: (b, i, 0)),
        out_shape=jax.ShapeDtypeStruct((bsz, seq, ATTN_HEADS * HEAD_DIM), BF16),
        scratch_shapes=[pltpu.VMEM((nkt, Q_BLOCK, KEY_TILE), F32),
                        pltpu.VMEM((nkt, ATTN_HEADS, Q_BLOCK, KEY_TILE), F32),
                        pltpu.VMEM((ATTN_HEADS * Q_BLOCK, KEY_TILE), BF16),
                        pltpu.VMEM((ATTN_HEADS, Q_BLOCK, LANES), F32),
                        pltpu.VMEM((ATTN_HEADS, Q_BLOCK, LANES), F32),
                        pltpu.VMEM((ATTN_HEADS * Q_BLOCK, KV_RANK), F32)],
        compiler_params=_cparams(("parallel", "arbitrary")),
        name="dsa",
    )(q, qi, w, kv, kvt, kit, tables, wuv)


def _split3(v):
    hi = v.astype(BF16)
    r = v - hi.astype(F32)
    mid = r.astype(BF16)
    lo = (r - mid.astype(F32)).astype(BF16)
    return hi, mid, lo


def _expand_heads(v, e):
    return sum(jnp.dot(p, e, preferred_element_type=F32) for p in _split3(v))


def _cumsum_rows(x):
    n = x.shape[0]
    r = lax.broadcasted_iota(jnp.int32, x.shape, 0)
    s = 1
    while s < n:
        x = x + jnp.where(r >= s, pltpu.roll(x, s, axis=0), 0.0)
        s *= 2
    return x


def _ssd_kernel(z_ref, xbc_ref, dt_ref, cw_ref, cb_ref, dtb_ref, alog_ref, dsk_ref, nw_ref, e_ref,
                y_ref, ext_ref, state_ref):
    nq = SSD_Q

    @pl.when(pl.program_id(1) == 0)
    def _():
        ext_ref[0:8, :] = jnp.zeros((8, CONV_DIM), F32)
        state_ref[...] = jnp.zeros_like(state_ref)

    xb = xbc_ref[0]
    ext_ref[8:8 + nq, :] = xb
    conv = cb_ref[...] + cw_ref[0:1, :] * ext_ref[pl.ds(8 - (CONV_W - 1), nq), :]
    for k in range(1, CONV_W):
        conv = conv + cw_ref[k:k + 1, :] * ext_ref[pl.ds(8 - (CONV_W - 1) + k, nq), :]
    ext_ref[0:8, :] = xb[nq - 8:nq, :]
    u = _silu(conv)
    xs = u[:, :D_INNER]
    bm = u[:, D_INNER:D_INNER + SSD_GROUPS * D_STATE]
    cm = u[:, D_INNER + SSD_GROUPS * D_STATE:]

    t = dt_ref[0] + dtb_ref[...]
    dt = jnp.maximum(t, 0.0) + jnp.log1p(jnp.exp(-jnp.abs(t)))
    a_cum = _cumsum_rows(dt * (-jnp.exp(alog_ref[...])))
    a_last = a_cum[nq - 1:nq, :]
    e = e_ref[...]
    dt_e = _expand_heads(dt, e)
    decay_e = _expand_heads(jnp.exp(a_last - a_cum), e)
    expa_e = _expand_heads(jnp.exp(a_cum), e)
    a_cum_t = a_cum.T

    xdt = xs * dt_e
    xdt_b = xdt.astype(BF16)
    xdec_b = (xdt * decay_e).astype(BF16)
    r = lax.broadcasted_iota(jnp.int32, (nq, nq), 0)
    c = lax.broadcasted_iota(jnp.int32, (nq, nq), 1)
    causal = r >= c
    lane = lax.broadcasted_iota(jnp.int32, (nq, LANES), 1)
    heads_per_group = SSD_HEADS // SSD_GROUPS
    gw = heads_per_group * SSD_HEADDIM

    ys = []
    for g in range(SSD_GROUPS):
        bg = bm[:, g * D_STATE:(g + 1) * D_STATE]
        cg = cm[:, g * D_STATE:(g + 1) * D_STATE].astype(BF16)
        bgt = bg.T.astype(BF16)
        cb = jnp.dot(cg, bgt, preferred_element_type=F32)
        prev = state_ref[g]
        y_off = jnp.dot(cg, prev.astype(BF16), preferred_element_type=F32) * expa_e[:, g * gw:(g + 1) * gw]
        pairs = []
        for pp in range(heads_per_group // 2):
            blk = g * (heads_per_group // 2) + pp
            xp = xdt_b[:, blk * LANES:(blk + 1) * LANES]
            yh = []
            for hh in range(2):
                h = 2 * blk + hh
                seg = a_cum[:, h:h + 1] - a_cum_t[h:h + 1, :]
                m = (cb * jnp.where(causal, jnp.exp(seg), 0.0)).astype(BF16)
                yh.append(jnp.dot(m, xp, preferred_element_type=F32))
            pairs.append(jnp.where(lane < SSD_HEADDIM, yh[0], yh[1]))
        ys.append(jnp.concatenate(pairs, axis=1) + y_off)
        new = jnp.dot(bgt, xdec_b[:, g * gw:(g + 1) * gw], preferred_element_type=F32)
        state_ref[g] = prev * expa_e[nq - 1:nq, g * gw:(g + 1) * gw] + new
    y = jnp.concatenate(ys, axis=1) + dsk_ref[...] * xs
    y_ref[0] = _rms(y * _silu(z_ref[0]), nw_ref[...]).astype(BF16)


def _ssd(proj, conv_w, conv_b, dt_bias, a_log, d_skip, ssd_norm):
    bsz, seq, _ = proj.shape
    nq = SSD_Q

    def pad_heads(v):
        return jnp.concatenate([v, jnp.zeros((LANES - SSD_HEADS,), F32)]).reshape(1, LANES)

    e = jnp.asarray(np.kron(np.eye(LANES, SSD_HEADS), np.ones((1, SSD_HEADDIM)))[:, :D_INNER], BF16)
    return pl.pallas_call(
        _ssd_kernel,
        grid=(bsz, seq // nq),
        in_specs=[pl.BlockSpec((1, nq, D_INNER), lambda b, i: (b, i, COL_Z // D_INNER)),
                  pl.BlockSpec((1, nq, CONV_DIM), lambda b, i: (b, i, COL_XBC // CONV_DIM)),
                  pl.BlockSpec((1, nq, LANES), lambda b, i: (b, i, HEAD_DT // LANES)),
                  _const_spec((CONV_W, CONV_DIM)), _const_spec((1, CONV_DIM)),
                  _const_spec((1, LANES)), _const_spec((1, LANES)),
                  _const_spec((1, D_INNER)), _const_spec((1, D_INNER)),
                  _const_spec((LANES, D_INNER))],
        out_specs=pl.BlockSpec((1, nq, D_INNER), lambda b, i: (b, i, 0)),
        out_shape=jax.ShapeDtypeStruct((bsz, seq, D_INNER), BF16),
        scratch_shapes=[pltpu.VMEM((8 + nq, CONV_DIM), F32),
                        pltpu.VMEM((SSD_GROUPS, D_STATE, 4 * SSD_HEADDIM), F32)],
        compiler_params=_cparams(("parallel", "arbitrary")),
        name="ssd",
    )(proj, proj, proj, conv_w, conv_b.reshape(1, CONV_DIM), pad_heads(dt_bias), pad_heads(a_log),
      jnp.repeat(d_skip, SSD_HEADDIM).reshape(1, D_INNER), ssd_norm.reshape(1, D_INNER), e)


def _mix_kernel(ao_ref, sy_ref, ga_ref, gb_ref, x_ref, mod_ref, nw_ref, woa_ref, wos_ref, wout_ref, o_ref):
    ya = jnp.dot(ao_ref[0], woa_ref[...], preferred_element_type=F32)
    yb = jnp.dot(sy_ref[0], wos_ref[...], preferred_element_type=F32)
    mix = jax.nn.sigmoid(ga_ref[0]) * ya + jax.nn.sigmoid(gb_ref[0]) * yb
    m2 = jnp.dot(mix.astype(BF16), wout_ref[...], preferred_element_type=F32)
    o_ref[0] = x_ref[0] + mod_ref[0][2:3] * _rms(m2, nw_ref[...])


def _mix(attn_o, ssd_y, proj, x, mod3, post_norm, w_o_attn, w_o_ssd, w_out):
    bsz, seq, _ = x.shape
    tm = min(seq, 512)

    def rows(width, col_block=0):
        return pl.BlockSpec((1, tm, width), lambda b, i: (b, i, col_block))

    return pl.pallas_call(
        _mix_kernel,
        grid=(bsz, seq // tm),
        in_specs=[rows(D_MODEL), rows(D_INNER), rows(D_MODEL, COL_GA // D_MODEL), rows(D_MODEL, COL_GB // D_MODEL),
                  rows(D_MODEL), pl.BlockSpec((1, 6, D_MODEL), lambda b, i: (b, 0, 0)),
                  _const_spec((1, D_MODEL)), _const_spec((D_MODEL, D_MODEL)),
                  _const_spec((D_INNER, D_MODEL)), _const_spec((D_MODEL, D_MODEL))],
        out_specs=rows(D_MODEL),
        out_shape=jax.ShapeDtypeStruct((bsz, seq, D_MODEL), F32),
        compiler_params=_cparams(("parallel", "parallel")),
        name="mix",
    )(attn_o, ssd_y, proj, proj, x, mod3, post_norm.reshape(1, D_MODEL),
      w_o_attn.astype(BF16), w_o_ssd.astype(BF16), w_out.astype(BF16))


def _ffn_kernel(x_ref, mod_ref, nw1_ref, nw2_ref, wg_ref, wu_ref, wo_ref, o_ref):
    x = x_ref[0]
    m = mod_ref[0]
    h2 = (_rms(x, nw1_ref[...]) * (1.0 + m[4:5]) + m[3:4]).astype(BF16)
    ug = jnp.dot(h2, wg_ref[...], preferred_element_type=F32)
    uu = jnp.dot(h2, wu_ref[...], preferred_element_type=F32)
    f = jnp.dot((_silu(ug) * uu).astype(BF16), wo_ref[...], preferred_element_type=F32)
    o_ref[0] = x + m[5:6] * _rms(f, nw2_ref[...])


def _ffn(x, mod3, pre_norm, post_norm, w_ffn_in, w_ffn_out):
    bsz, seq, _ = x.shape
    tm = min(seq, 512)
    rows = pl.BlockSpec((1, tm, D_MODEL), lambda b, i: (b, i, 0))
    return pl.pallas_call(
        _ffn_kernel,
        grid=(bsz, seq // tm),
        in_specs=[rows, pl.BlockSpec((1, 6, D_MODEL), lambda b, i: (b, 0, 0)),
                  _const_spec((1, D_MODEL)), _const_spec((1, D_MODEL)),
                  _const_spec((D_MODEL, D_FF)), _const_spec((D_MODEL, D_FF)), _const_spec((D_FF, D_MODEL))],
        out_specs=rows,
        out_shape=jax.ShapeDtypeStruct((bsz, seq, D_MODEL), F32),
        compiler_params=_cparams(("parallel", "parallel")),
        name="ffn",
    )(x, mod3, pre_norm.reshape(1, D_MODEL), post_norm.reshape(1, D_MODEL),
      w_ffn_in[:, :D_FF].astype(BF16), w_ffn_in[:, D_FF:].astype(BF16), w_ffn_out.astype(BF16))


def kernel(x, c, positions, ada_w, ada_b, pre_norm_mix, post_norm_mix, pre_norm_ffn, post_norm_ffn, w_in, q_norm, kv_norm, w_uq, w_uv, rel_bias, w_qidx, kidx_norm, conv_w, conv_b, dt_bias, a_log, d_skip, ssd_norm, w_o_attn, w_o_ssd, w_out, w_ffn_in, w_ffn_out):
    del positions
    bsz, seq, _ = x.shape
    assert seq % (2 * KEY_TILE) == 0 and x.shape[-1] == D_MODEL
    mod3 = _mod(c, ada_w, ada_b).reshape(bsz, 6, D_MODEL)
    proj = _inproj(x, mod3, pre_norm_mix, _pack_w_in(w_in))
    q, qi, kv, kvt, kit, w = _prep(proj, q_norm, kv_norm, kidx_norm, w_uq, w_qidx)
    attn_o = _dsa(q, qi, w, kv, kvt, kit, _bias_tables(rel_bias), _pack_w_uv(w_uv))
    ssd_y = _ssd(proj, conv_w, conv_b, dt_bias, a_log, d_skip, ssd_norm)
    x1 = _mix(attn_o, ssd_y, proj, x, mod3, post_norm_mix, w_o_attn, w_o_ssd, w_out)
    return _ffn(x1, mod3, pre_norm_ffn, post_norm_ffn, w_ffn_in, w_ffn_out)
```

```python
import functools
import math

import numpy as np
import jax
import jax.numpy as jnp
from jax import lax
from jax.experimental import pallas as pl
from jax.experimental.pallas import tpu as pltpu

F32 = jnp.float32
BF16 = jnp.bfloat16

D_MODEL = 1024
CHUNK = 64
Q_BLOCK = 128
EPS = 1e-6
ATTN_HEADS = 16
HEAD_DIM = 64
Q_RANK = 256
KV_RANK = 128
IDX_HEADS = 16
IDX_DIM = 64
TOPK_MAX = 256
NUM_BUCKETS = 32
MAX_DISTANCE = 128
D_INNER = 2 * D_MODEL
SSD_HEADDIM = 64
SSD_HEADS = D_INNER // SSD_HEADDIM
SSD_GROUPS = 8
D_STATE = 128
CONV_W = 4
CONV_DIM = D_INNER + 2 * SSD_GROUPS * D_STATE
D_FF = -(-8 * D_MODEL // (3 * 256)) * 256

LANES = 128
SUBLANES = 8
KEY_TILE = 256
SSD_Q = 128
VMEM_LIMIT = 56 * 1024 * 1024
BISECT_MAX_ITERS = 320
BISECT_UNROLL = 4
LOG2E = math.log2(math.e)

COL_HEAD, COL_GA, COL_Z, COL_XBC, COL_GB = 0, 1024, 2048, 4096, 8192
PROJ_COLS = 9216
HEAD_KV, HEAD_KIDX, HEAD_W, HEAD_DT = 256, 384, 512, 640


def _cparams(sem):
    return pltpu.CompilerParams(dimension_semantics=sem, vmem_limit_bytes=VMEM_LIMIT)


def _const_spec(shape):
    nd = len(shape)
    return pl.BlockSpec(shape, lambda *_: (0,) * nd, pipeline_mode=pl.Buffered(1))


def _rms(x, w, n=None):
    n = x.shape[-1] if n is None else n
    return x * lax.rsqrt(jnp.sum(x * x, axis=-1, keepdims=True) * (1.0 / n) + EPS) * w


def _silu(x):
    return x * jax.nn.sigmoid(x)


def _mod_kernel(c_ref, w_ref, b_ref, o_ref):
    c = c_ref[...]
    s = _silu(c).astype(BF16)
    o_ref[...] = jnp.dot(s, w_ref[...].astype(BF16), preferred_element_type=F32) + b_ref[...]


def _mod(c, ada_w, ada_b):
    bsz = c.shape[0]
    return pl.pallas_call(
        _mod_kernel,
        grid=(6,),
        in_specs=[pl.BlockSpec((bsz, D_MODEL), lambda j: (0, 0)),
                  pl.BlockSpec((D_MODEL, D_MODEL), lambda j: (0, j)),
                  pl.BlockSpec((1, D_MODEL), lambda j: (0, j))],
        out_specs=pl.BlockSpec((bsz, D_MODEL), lambda j: (0, j)),
        out_shape=jax.ShapeDtypeStruct((bsz, 6 * D_MODEL), F32),
        compiler_params=_cparams(("parallel",)),
        name="mod",
    )(c, ada_w, ada_b.reshape(1, 6 * D_MODEL))


def _t5_bucket_np(rel):
    half = NUM_BUCKETS // 2
    max_exact = half // 2
    side = np.where(rel > 0, half, 0)
    n = np.abs(rel)
    large = max_exact + (np.log(np.maximum(n, max_exact).astype(np.float64) / max_exact)
                         / math.log(MAX_DISTANCE / max_exact) * (half - max_exact)).astype(np.int64)
    large = np.minimum(large, half - 1)
    return (side + np.where(n < max_exact, n, large)).astype(np.int32)


def _bias_kernel(idx_ref, rb_ref, o_ref):
    h = pl.program_id(0)
    idx = idx_ref[...]
    far = rb_ref[h, NUM_BUCKETS // 2 - 1]
    acc = jnp.zeros(idx.shape, F32)
    for b in range(NUM_BUCKETS):
        acc = jnp.where(idx == b, (rb_ref[h, b] - far) * LOG2E, acc)
    o_ref[0] = acc


def _bias_tables(rel_bias):
    ql = np.arange(Q_BLOCK)[:, None]
    kk = np.arange(4 * LANES)[None, :]
    idx = jnp.asarray(_t5_bucket_np(kk - 2 * LANES - ql))
    t = pl.pallas_call(
        _bias_kernel,
        grid=(ATTN_HEADS,),
        in_specs=[pl.BlockSpec((Q_BLOCK, 4 * LANES), lambda h: (0, 0)),
                  pl.BlockSpec(memory_space=pltpu.SMEM)],
        out_specs=pl.BlockSpec((1, Q_BLOCK, 4 * LANES), lambda h: (h, 0, 0)),
        out_shape=jax.ShapeDtypeStruct((ATTN_HEADS, Q_BLOCK, 4 * LANES), F32),
        compiler_params=_cparams(("arbitrary",)),
        name="bias",
    )(idx, rel_bias.T)
    return jnp.stack([t[..., 256:512], t[..., 128:384], t[..., 0:256]])


def _inproj_kernel(x_ref, mod_ref, nw_ref, w_ref, o_ref, hn_ref):
    @pl.when(pl.program_id(2) == 0)
    def _():
        m = mod_ref[0]
        y = _rms(x_ref[0], nw_ref[...])
        hn_ref[...] = (y * (1.0 + m[1:2]) + m[0:1]).astype(BF16)

    o_ref[0] = jnp.dot(hn_ref[...], w_ref[...], preferred_element_type=F32)


def _pack_w_in(w_in):
    sizes = [Q_RANK, KV_RANK, IDX_DIM, IDX_HEADS, D_INNER, CONV_DIM, SSD_HEADS, D_MODEL, D_MODEL]
    offs = np.cumsum([0] + sizes)
    q, kv, ki, wi, z, xbc, dt, ga, gb = [w_in[:, offs[i]:offs[i + 1]] for i in range(9)]

    def zc(n):
        return jnp.zeros((D_MODEL, n), w_in.dtype)

    head = jnp.concatenate([q, kv, ki, zc(LANES - IDX_DIM), wi, zc(LANES - IDX_HEADS),
                            dt, zc(LANES - SSD_HEADS), zc(2 * LANES)], axis=1)
    return jnp.concatenate([head, ga, z, xbc, gb], axis=1).astype(BF16)


def _inproj(x, mod3, pre_norm, w_packed):
    bsz, seq, _ = x.shape
    tm = min(seq, 1024)
    tn = 1024
    return pl.pallas_call(
        _inproj_kernel,
        grid=(bsz, seq // tm, PROJ_COLS // tn),
        in_specs=[pl.BlockSpec((1, tm, D_MODEL), lambda b, i, j: (b, i, 0)),
                  pl.BlockSpec((1, 6, D_MODEL), lambda b, i, j: (b, 0, 0)),
                  pl.BlockSpec((1, D_MODEL), lambda b, i, j: (0, 0)),
                  pl.BlockSpec((D_MODEL, tn), lambda b, i, j: (0, j))],
        out_specs=pl.BlockSpec((1, tm, tn), lambda b, i, j: (b, i, j)),
        out_shape=jax.ShapeDtypeStruct((bsz, seq, PROJ_COLS), F32),
        scratch_shapes=[pltpu.VMEM((tm, D_MODEL), BF16)],
        compiler_params=_cparams(("parallel", "parallel", "arbitrary")),
        name="inproj",
    )(x, mod3, pre_norm.reshape(1, D_MODEL), w_packed)


def _prep_kernel(p_ref, qn_ref, kvn_ref, kin_ref, wuq_ref, wqi_ref,
                 q_ref, qi_ref, kv_ref, kvt_ref, kit_ref, w_ref, *, tc):
    p = p_ref[0]
    qn = _rms(p[:, :Q_RANK], qn_ref[...]).astype(BF16)
    qf = jnp.dot(qn, wuq_ref[...], preferred_element_type=F32) * (KV_RANK ** -0.5 * LOG2E)
    qif = jnp.dot(qn, wqi_ref[...], preferred_element_type=F32)
    for h in range(ATTN_HEADS):
        q_ref[0, h] = qf[:, h * LANES:(h + 1) * LANES].astype(BF16)
        qi_ref[0, h] = qif[:, h * LANES:(h + 1) * LANES].astype(BF16)
    kv = _rms(p[:, HEAD_KV:HEAD_KV + KV_RANK], kvn_ref[...])
    kv_ref[0] = kv.astype(BF16)
    kvt = kv.T
    ki = _rms(p[:, HEAD_KIDX:HEAD_KIDX + LANES], kin_ref[...], n=IDX_DIM)
    kit = ki.T
    for c in range(tc // KEY_TILE):
        kvt_ref[0, c] = kvt[:, c * KEY_TILE:(c + 1) * KEY_TILE].astype(BF16)
        kit_ref[0, c] = kit[:, c * KEY_TILE:(c + 1) * KEY_TILE].astype(BF16)
    w_ref[0] = p[:, HEAD_W:HEAD_W + LANES] * (IDX_HEADS ** -0.5 * IDX_DIM ** -0.5)


def _prep(proj, q_norm, kv_norm, kidx_norm, w_uq, w_qidx):
    bsz, seq, _ = proj.shape
    tc = min(seq, 512)
    wqi = w_qidx.reshape(Q_RANK, IDX_HEADS, IDX_DIM)
    wqi = jnp.concatenate([wqi, jnp.zeros_like(wqi)], axis=-1).reshape(Q_RANK, IDX_HEADS * LANES).astype(BF16)
    kin = jnp.concatenate([kidx_norm, jnp.zeros((LANES - IDX_DIM,), F32)]).reshape(1, LANES)
    hm = jax.ShapeDtypeStruct((bsz, ATTN_HEADS, seq, LANES), BF16)
    tr = jax.ShapeDtypeStruct((bsz, seq // KEY_TILE, LANES, KEY_TILE), BF16)
    hm_spec = pl.BlockSpec((1, ATTN_HEADS, tc, LANES), lambda b, i: (b, 0, i, 0))
    tr_spec = pl.BlockSpec((1, tc // KEY_TILE, LANES, KEY_TILE), lambda b, i: (b, i, 0, 0))
    row_spec = pl.BlockSpec((1, tc, LANES), lambda b, i: (b, i, 0))
    return pl.pallas_call(
        functools.partial(_prep_kernel, tc=tc),
        grid=(bsz, seq // tc),
        in_specs=[pl.BlockSpec((1, tc, 1024), lambda b, i: (b, i, 0)),
                  _const_spec((1, Q_RANK)), _const_spec((1, KV_RANK)), _const_spec((1, LANES)),
                  _const_spec((Q_RANK, ATTN_HEADS * KV_RANK)), _const_spec((Q_RANK, IDX_HEADS * LANES))],
        out_specs=[hm_spec, hm_spec, row_spec, tr_spec, tr_spec, row_spec],
        out_shape=[hm, hm, jax.ShapeDtypeStruct((bsz, seq, LANES), BF16), tr, tr,
                   jax.ShapeDtypeStruct((bsz, seq, LANES), F32)],
        compiler_params=_cparams(("parallel", "parallel")),
        name="prep",
    )(proj, q_norm.reshape(1, Q_RANK), kv_norm.reshape(1, KV_RANK), kin, w_uq.astype(BF16), wqi)


def _dsa_kernel(q_ref, qi_ref, w_ref, kv_ref, kvt_ref, kit_ref, tb_ref, wuv_ref, o_ref,
                isc_ref, isct_ref, sbuf_ref, pbuf_ref, mrun_ref, lrun_ref, acc_ref, *, k_sel):
    i = pl.program_id(1)
    last = i // 2
    odd = i % 2
    n_tiles = last + 1
    qs = q_ref[0].reshape(ATTN_HEADS * Q_BLOCK, KV_RANK)
    qis = qi_ref[0].reshape(IDX_HEADS * Q_BLOCK, LANES)
    w = w_ref[0]
    row = lax.broadcasted_iota(jnp.int32, (Q_BLOCK, KEY_TILE), 0)
    col = lax.broadcasted_iota(jnp.int32, (Q_BLOCK, KEY_TILE), 1)
    key_limit = i * Q_BLOCK + jnp.where(row < CHUNK, CHUNK, 2 * CHUNK)

    def idx_body(j, carry):
        s = jnp.dot(qis, kit_ref[0, j], preferred_element_type=F32)
        acc = jnp.zeros((Q_BLOCK, KEY_TILE), F32)
        for h in range(IDX_HEADS):
            acc = acc + w[:, h:h + 1] * jnp.maximum(s[h * Q_BLOCK:(h + 1) * Q_BLOCK], 0.0)
        tile = jnp.where(j * KEY_TILE + col < key_limit, acc, -jnp.inf)
        isc_ref[j] = tile
        isct_ref[j] = tile.T
        return carry

    lax.fori_loop(0, n_tiles, idx_body, 0)

    def halves(x):
        return x[:, :LANES], x[:, LANES:]

    def lane_all(x, op):
        return jnp.broadcast_to(op(x, axis=-1, keepdims=True), (Q_BLOCK, LANES))

    def rows_all(x, op):
        return jnp.broadcast_to(op(x, axis=0, keepdims=True), (SUBLANES, LANES))

    ACCS = 4

    def tile_rows(j):
        return isct_ref[j].reshape(KEY_TILE // (ACCS * SUBLANES), ACCS, SUBLANES, LANES)

    def count_where(pred):
        def body(j, c):
            return c + jnp.sum(jnp.where(pred(tile_rows(j)), 1.0, 0.0), axis=0)
        c = lax.fori_loop(0, n_tiles, body, jnp.zeros((ACCS, SUBLANES, LANES), F32))
        return rows_all(jnp.sum(c, axis=0), jnp.sum)

    def minmax_body(j, c):
        lo, hi = c
        x = tile_rows(j)
        return (jnp.minimum(lo, jnp.min(jnp.where(x == -jnp.inf, jnp.inf, x), axis=0)),
                jnp.maximum(hi, jnp.max(x, axis=0)))

    lo, hi = lax.fori_loop(0, n_tiles, minmax_body,
                           (jnp.full((ACCS, SUBLANES, LANES), jnp.inf, F32),
                            jnp.full((ACCS, SUBLANES, LANES), -jnp.inf, F32)))
    lo = rows_all(jnp.min(lo, axis=0), jnp.min)
    hi = rows_all(jnp.max(hi, axis=0), jnp.max)
    kf = float(k_sel)
    cnt_lo = count_where(lambda x: x >= lo)
    cnt_hi = count_where(lambda x: x >= hi)
    at_max = cnt_hi >= kf
    lo = jnp.where(at_max, hi, lo)
    cnt = jnp.where(at_max, cnt_hi, cnt_lo)

    def bis_step(c):
        lo, hi, cnt, stalled = c
        mid = 0.5 * lo + 0.5 * hi
        cm = count_where(lambda x: x >= mid)
        active = jnp.logical_and(cnt > kf, stalled == 0.0)
        noprog = jnp.logical_or(mid <= lo, mid >= hi)
        move = jnp.logical_and(active, jnp.logical_not(noprog))
        up = jnp.logical_and(move, cm >= kf)
        down = jnp.logical_and(move, cm < kf)
        return (jnp.where(up, mid, lo), jnp.where(down, mid, hi), jnp.where(up, cm, cnt),
                jnp.where(jnp.logical_and(active, noprog), 1.0, stalled))

    def bis_cond(c):
        _, _, cnt, stalled, it = c
        active = jnp.where(jnp.logical_and(cnt > kf, stalled == 0.0), 1.0, 0.0)
        return jnp.logical_and(it < BISECT_MAX_ITERS, jnp.max(active) > 0.0)

    def bis_body(c):
        state = c[:4]
        for _ in range(BISECT_UNROLL):
            state = bis_step(state)
        return state + (c[4] + BISECT_UNROLL,)

    thr8, _, cnt, _, _ = lax.while_loop(
        bis_cond, bis_body, (lo, hi, cnt, jnp.zeros((SUBLANES, LANES), F32), jnp.int32(0)))

    tied = cnt > kf

    @pl.when(jnp.max(jnp.where(tied, 1.0, 0.0)) > 0.0)
    def _():
        need = kf - count_where(lambda x: x > thr8)

        def body(j, seen):
            x = isct_ref[j]
            eq = jnp.where(x == thr8[0:1], 1.0, 0.0)
            inc = _cumsum_rows(eq)
            rank = inc - eq + seen[0:1]
            drop = jnp.logical_and(jnp.logical_and(tied[0:1], eq > 0.0), rank >= need[0:1])
            isc_ref[j] = jnp.where(drop, -jnp.inf, x).T
            return seen + inc[KEY_TILE - 1:KEY_TILE]

        lax.fori_loop(0, n_tiles, body, jnp.zeros((SUBLANES, LANES), F32))

    thr = jnp.broadcast_to(thr8[0:1], (Q_BLOCK, LANES)).T

    for h in range(ATTN_HEADS):
        mrun_ref[h] = jnp.full((Q_BLOCK, LANES), -jnp.inf, F32)
        lrun_ref[h] = jnp.zeros((Q_BLOCK, LANES), F32)
    acc_ref[...] = jnp.zeros_like(acc_ref)

    def p1_tile(j, near):
        s = jnp.dot(qs, kvt_ref[0, j], preferred_element_type=F32)
        a, b = halves(isc_ref[j])
        neg = jnp.concatenate([jnp.where(a >= thr, 0.0, -jnp.inf), jnp.where(b >= thr, 0.0, -jnp.inf)], axis=1)
        table = jnp.where(j == last, odd, 2)
        for h in range(ATTN_HEADS):
            sh = s[h * Q_BLOCK:(h + 1) * Q_BLOCK] + neg
            if near:
                sh = sh + tb_ref[table, h]
            sbuf_ref[j, h] = sh
            sa, sb = halves(sh)
            mrun_ref[h] = jnp.maximum(mrun_ref[h], jnp.maximum(sa, sb))

    n_far = jnp.maximum(last - 1 + odd, 0)
    lax.fori_loop(0, n_far, lambda j, c: (p1_tile(j, False), c)[1], 0)
    lax.fori_loop(n_far, n_tiles, lambda j, c: (p1_tile(j, True), c)[1], 0)
    for h in range(ATTN_HEADS):
        mrun_ref[h] = lane_all(mrun_ref[h], jnp.max)

    def p2_body(j, carry):
        for h in range(ATTN_HEADS):
            sa, sb = halves(sbuf_ref[j, h])
            m = mrun_ref[h]
            pa = jnp.exp2(sa - m)
            pb = jnp.exp2(sb - m)
            lrun_ref[h] = lrun_ref[h] + pa + pb
            pbuf_ref[h * Q_BLOCK:(h + 1) * Q_BLOCK, :] = jnp.concatenate([pa, pb], axis=1).astype(BF16)
        kv_tile = kv_ref[0, pl.ds(pl.multiple_of(j * KEY_TILE, KEY_TILE), KEY_TILE), :]
        acc_ref[...] += jnp.dot(pbuf_ref[...], kv_tile, preferred_element_type=F32)
        return carry

    lax.fori_loop(0, n_tiles, p2_body, 0)

    outs = []
    for h in range(ATTN_HEADS):
        l = jnp.sum(lrun_ref[h], axis=-1, keepdims=True)
        outs.append((acc_ref[h * Q_BLOCK:(h + 1) * Q_BLOCK, :] / l).astype(BF16))
    for p in range(ATTN_HEADS // 2):
        pair = jnp.concatenate(outs[2 * p:2 * p + 2], axis=1)
        o_ref[0, :, p * LANES:(p + 1) * LANES] = jnp.dot(
            pair, wuv_ref[p], preferred_element_type=F32).astype(BF16)


def _pack_w_uv(w_uv):
    eye = jnp.eye(2, dtype=w_uv.dtype)
    w = w_uv.reshape(ATTN_HEADS // 2, 2, KV_RANK, 1, HEAD_DIM) * eye[None, :, None, :, None]
    return w.reshape(ATTN_HEADS // 2, 2 * KV_RANK, 2 * HEAD_DIM).astype(BF16)


def _dsa(q, qi, w, kv, kvt, kit, tables, wuv):
    bsz, _, seq, _ = q.shape
    nkt = seq // KEY_TILE
    k_sel = min(TOPK_MAX, seq // 4)
    hm_spec = pl.BlockSpec((1, ATTN_HEADS, Q_BLOCK, LANES), lambda b, i: (b, 0, i, 0))
    tr_spec = pl.BlockSpec((1, nkt, LANES, KEY_TILE), lambda b, i: (b, 0, 0, 0))
    return pl.pallas_call(
        functools.partial(_dsa_kernel, k_sel=k_sel),
        grid=(bsz, seq // Q_BLOCK),
        in_specs=[hm_spec, hm_spec,
                  pl.BlockSpec((1, Q_BLOCK, LANES), lambda b, i: (b, i, 0)),
                  pl.BlockSpec((1, seq, KV_RANK), lambda b, i: (b, 0, 0)),
                  tr_spec, tr_spec,
                  _const_spec((3, ATTN_HEADS, Q_BLOCK, KEY_TILE)),
                  _const_spec((ATTN_HEADS // 2, 2 * KV_RANK, 2 * HEAD_DIM))],
        out_specs=pl.BlockSpec((1, Q_BLOCK, ATTN_HEADS * HEAD_DIM), lambda b, i: (b, i, 0)),
        out_shape=jax.ShapeDtypeStruct((bsz, seq, ATTN_HEADS * HEAD_DIM), BF16),
        scratch_shapes=[pltpu.VMEM((nkt, Q_BLOCK, KEY_TILE), F32),
                        pltpu.VMEM((nkt, KEY_TILE, Q_BLOCK), F32),
                        pltpu.VMEM((nkt, ATTN_HEADS, Q_BLOCK, KEY_TILE), F32),
                        pltpu.VMEM((ATTN_HEADS * Q_BLOCK, KEY_TILE), BF16),
                        pltpu.VMEM((ATTN_HEADS, Q_BLOCK, LANES), F32),
                        pltpu.VMEM((ATTN_HEADS, Q_BLOCK, LANES), F32),
                        pltpu.VMEM((ATTN_HEADS * Q_BLOCK, KV_RANK), F32)],
        compiler_params=_cparams(("parallel", "arbitrary")),
        name="dsa",
    )(q, qi, w, kv, kvt, kit, tables, wuv)


def _split3(v):
    hi = v.astype(BF16)
    r = v - hi.astype(F32)
    mid = r.astype(BF16)
    lo = (r - mid.astype(F32)).astype(BF16)
    return hi, mid, lo


def _expand_heads(v, e):
    return sum(jnp.dot(p, e, preferred_element_type=F32) for p in _split3(v))


def _cumsum_rows(x):
    n = x.shape[0]
    r = lax.broadcasted_iota(jnp.int32, x.shape, 0)
    s = 1
    while s < n:
        x = x + jnp.where(r >= s, pltpu.roll(x, s, axis=0), 0.0)
        s *= 2
    return x


def _ssd_kernel(z_ref, xbc_ref, dt_ref, cw_ref, cb_ref, dtb_ref, alog_ref, dsk_ref, nw_ref, e_ref,
                y_ref, ext_ref, state_ref):
    nq = SSD_Q

    @pl.when(pl.program_id(1) == 0)
    def _():
        ext_ref[0:SUBLANES, :] = jnp.zeros((SUBLANES, CONV_DIM), F32)
        state_ref[...] = jnp.zeros_like(state_ref)

    xb = xbc_ref[0]
    ext_ref[SUBLANES:SUBLANES + nq, :] = xb
    first = SUBLANES - (CONV_W - 1)
    conv = cb_ref[...] + cw_ref[0:1, :] * ext_ref[pl.ds(first, nq), :]
    for k in range(1, CONV_W):
        conv = conv + cw_ref[k:k + 1, :] * ext_ref[pl.ds(first + k, nq), :]
    ext_ref[0:SUBLANES, :] = xb[nq - SUBLANES:nq, :]
    u = _silu(conv)
    xs = u[:, :D_INNER]
    bm = u[:, D_INNER:D_INNER + SSD_GROUPS * D_STATE]
    cm = u[:, D_INNER + SSD_GROUPS * D_STATE:]

    t = dt_ref[0] + dtb_ref[...]
    dt = jnp.maximum(t, 0.0) + jnp.log1p(jnp.exp(-jnp.abs(t)))
    a_cum = _cumsum_rows(dt * (-jnp.exp(alog_ref[...])))
    a_last = a_cum[nq - 1:nq, :]
    e = e_ref[...]
    dt_e = _expand_heads(dt, e)
    decay_e = _expand_heads(jnp.exp(a_last - a_cum), e)
    expa_e = _expand_heads(jnp.exp(a_cum), e)
    a_cum_t = a_cum.T

    xdt = xs * dt_e
    xdt_b = xdt.astype(BF16)
    xdec_b = (xdt * decay_e).astype(BF16)
    r = lax.broadcasted_iota(jnp.int32, (nq, nq), 0)
    c = lax.broadcasted_iota(jnp.int32, (nq, nq), 1)
    causal = r >= c
    lane = lax.broadcasted_iota(jnp.int32, (nq, LANES), 1)
    heads_per_group = SSD_HEADS // SSD_GROUPS
    gw = heads_per_group * SSD_HEADDIM

    ys = []
    for g in range(SSD_GROUPS):
        bg = bm[:, g * D_STATE:(g + 1) * D_STATE]
        cg = cm[:, g * D_STATE:(g + 1) * D_STATE].astype(BF16)
        bgt = bg.T.astype(BF16)
        cb = jnp.dot(cg, bgt, preferred_element_type=F32)
        prev = state_ref[g]
        y_off = jnp.dot(cg, prev.astype(BF16), preferred_element_type=F32) * expa_e[:, g * gw:(g + 1) * gw]
        pairs = []
        for pp in range(heads_per_group // 2):
            blk = g * (heads_per_group // 2) + pp
            xp = xdt_b[:, blk * LANES:(blk + 1) * LANES]
            yh = []
            for hh in range(2):
                h = 2 * blk + hh
                seg = a_cum[:, h:h + 1] - a_cum_t[h:h + 1, :]
                m = (cb * jnp.where(causal, jnp.exp(seg), 0.0)).astype(BF16)
                yh.append(jnp.dot(m, xp, preferred_element_type=F32))
            pairs.append(jnp.where(lane < SSD_HEADDIM, yh[0], yh[1]))
        ys.append(jnp.concatenate(pairs, axis=1) + y_off)
        new = jnp.dot(bgt, xdec_b[:, g * gw:(g + 1) * gw], preferred_element_type=F32)
        state_ref[g] = prev * expa_e[nq - 1:nq, g * gw:(g + 1) * gw] + new
    y = jnp.concatenate(ys, axis=1) + dsk_ref[...] * xs
    y_ref[0] = _rms(y * _silu(z_ref[0]), nw_ref[...]).astype(BF16)


def _ssd(proj, conv_w, conv_b, dt_bias, a_log, d_skip, ssd_norm):
    bsz, seq, _ = proj.shape
    nq = SSD_Q

    def pad_heads(v):
        return jnp.concatenate([v, jnp.zeros((LANES - SSD_HEADS,), F32)]).reshape(1, LANES)

    e = jnp.asarray(np.kron(np.eye(LANES, SSD_HEADS), np.ones((1, SSD_HEADDIM)))[:, :D_INNER], BF16)
    return pl.pallas_call(
        _ssd_kernel,
        grid=(bsz, seq // nq),
        in_specs=[pl.BlockSpec((1, nq, D_INNER), lambda b, i: (b, i, COL_Z // D_INNER)),
                  pl.BlockSpec((1, nq, CONV_DIM), lambda b, i: (b, i, COL_XBC // CONV_DIM)),
                  pl.BlockSpec((1, nq, LANES), lambda b, i: (b, i, HEAD_DT // LANES)),
                  _const_spec((CONV_W, CONV_DIM)), _const_spec((1, CONV_DIM)),
                  _const_spec((1, LANES)), _const_spec((1, LANES)),
                  _const_spec((1, D_INNER)), _const_spec((1, D_INNER)),
                  _const_spec((LANES, D_INNER))],
        out_specs=pl.BlockSpec((1, nq, D_INNER), lambda b, i: (b, i, 0)),
        out_shape=jax.ShapeDtypeStruct((bsz, seq, D_INNER), BF16),
        scratch_shapes=[pltpu.VMEM((SUBLANES + nq, CONV_DIM), F32),
                        pltpu.VMEM((SSD_GROUPS, D_STATE, 4 * SSD_HEADDIM), F32)],
        compiler_params=_cparams(("parallel", "arbitrary")),
        name="ssd",
    )(proj, proj, proj, conv_w, conv_b.reshape(1, CONV_DIM), pad_heads(dt_bias), pad_heads(a_log),
      jnp.repeat(d_skip, SSD_HEADDIM).reshape(1, D_INNER), ssd_norm.reshape(1, D_INNER), e)


def _mix_kernel(ao_ref, sy_ref, ga_ref, gb_ref, x_ref, mod_ref, nw_ref, woa_ref, wos_ref, wout_ref, o_ref):
    ya = jnp.dot(ao_ref[0], woa_ref[...], preferred_element_type=F32)
    yb = jnp.dot(sy_ref[0], wos_ref[...], preferred_element_type=F32)
    mix = jax.nn.sigmoid(ga_ref[0]) * ya + jax.nn.sigmoid(gb_ref[0]) * yb
    m2 = jnp.dot(mix.astype(BF16), wout_ref[...], preferred_element_type=F32)
    o_ref[0] = x_ref[0] + mod_ref[0][2:3] * _rms(m2, nw_ref[...])


def _mix(attn_o, ssd_y, proj, x, mod3, post_norm, w_o_attn, w_o_ssd, w_out):
    bsz, seq, _ = x.shape
    tm = min(seq, 512)

    def rows(width, col_block=0):
        return pl.BlockSpec((1, tm, width), lambda b, i: (b, i, col_block))

    return pl.pallas_call(
        _mix_kernel,
        grid=(bsz, seq // tm),
        in_specs=[rows(D_MODEL), rows(D_INNER), rows(D_MODEL, COL_GA // D_MODEL), rows(D_MODEL, COL_GB // D_MODEL),
                  rows(D_MODEL), pl.BlockSpec((1, 6, D_MODEL), lambda b, i: (b, 0, 0)),
                  _const_spec((1, D_MODEL)), _const_spec((D_MODEL, D_MODEL)),
                  _const_spec((D_INNER, D_MODEL)), _const_spec((D_MODEL, D_MODEL))],
        out_specs=rows(D_MODEL),
        out_shape=jax.ShapeDtypeStruct((bsz, seq, D_MODEL), F32),
        compiler_params=_cparams(("parallel", "parallel")),
        name="mix",
    )(attn_o, ssd_y, proj, proj, x, mod3, post_norm.reshape(1, D_MODEL),
      w_o_attn.astype(BF16), w_o_ssd.astype(BF16), w_out.astype(BF16))


def _ffn_kernel(x_ref, mod_ref, nw1_ref, nw2_ref, wg_ref, wu_ref, wo_ref, o_ref):
    x = x_ref[0]
    m = mod_ref[0]
    h2 = (_rms(x, nw1_ref[...]) * (1.0 + m[4:5]) + m[3:4]).astype(BF16)
    ug = jnp.dot(h2, wg_ref[...], preferred_element_type=F32)
    uu = jnp.dot(h2, wu_ref[...], preferred_element_type=F32)
    f = jnp.dot((_silu(ug) * uu).astype(BF16), wo_ref[...], preferred_element_type=F32)
    o_ref[0] = x + m[5:6] * _rms(f, nw2_ref[...])


def _ffn(x, mod3, pre_norm, post_norm, w_ffn_in, w_ffn_out):
    bsz, seq, _ = x.shape
    tm = min(seq, 512)
    rows = pl.BlockSpec((1, tm, D_MODEL), lambda b, i: (b, i, 0))
    return pl.pallas_call(
        _ffn_kernel,
        grid=(bsz, seq // tm),
        in_specs=[rows, pl.BlockSpec((1, 6, D_MODEL), lambda b, i: (b, 0, 0)),
                  _const_spec((1, D_MODEL)), _const_spec((1, D_MODEL)),
                  _const_spec((D_MODEL, D_FF)), _const_spec((D_MODEL, D_FF)), _const_spec((D_FF, D_MODEL))],
        out_specs=rows,
        out_shape=jax.ShapeDtypeStruct((bsz, seq, D_MODEL), F32),
        compiler_params=_cparams(("parallel", "parallel")),
        name="ffn",
    )(x, mod3, pre_norm.reshape(1, D_MODEL), post_norm.reshape(1, D_MODEL),
      w_ffn_in[:, :D_FF].astype(BF16), w_ffn_in[:, D_FF:].astype(BF16), w_ffn_out.astype(BF16))


def kernel(x, c, positions, ada_w, ada_b, pre_norm_mix, post_norm_mix, pre_norm_ffn, post_norm_ffn, w_in, q_norm, kv_norm, w_uq, w_uv, rel_bias, w_qidx, kidx_norm, conv_w, conv_b, dt_bias, a_log, d_skip, ssd_norm, w_o_attn, w_o_ssd, w_out, w_ffn_in, w_ffn_out):
    del positions
    bsz, seq, _ = x.shape
    assert seq % (2 * KEY_TILE) == 0 and x.shape[-1] == D_MODEL
    mod3 = _mod(c, ada_w, ada_b).reshape(bsz, 6, D_MODEL)
    proj = _inproj(x, mod3, pre_norm_mix, _pack_w_in(w_in))
    q, qi, kv, kvt, kit, w = _prep(proj, q_norm, kv_norm, kidx_norm, w_uq, w_qidx)
    attn_o = _dsa(q, qi, w, kv, kvt, kit, _bias_tables(rel_bias), _pack_w_uv(w_uv))
    ssd_y = _ssd(proj, conv_w, conv_b, dt_bias, a_log, d_skip, ssd_norm)
    x1 = _mix(attn_o, ssd_y, proj, x, mod3, post_norm_mix, w_o_attn, w_o_ssd, w_out)
    return _ffn(x1, mod3, pre_norm_ffn, post_norm_ffn, w_ffn_in, w_ffn_out)
```

```python
import functools
import math

import numpy as np
import jax
import jax.numpy as jnp
from jax import lax
from jax.experimental import pallas as pl
from jax.experimental.pallas import tpu as pltpu

F32 = jnp.float32
BF16 = jnp.bfloat16

D_MODEL = 1024
CHUNK = 64
Q_BLOCK = 128
EPS = 1e-6
ATTN_HEADS = 16
HEAD_DIM = 64
Q_RANK = 256
KV_RANK = 128
IDX_HEADS = 16
IDX_DIM = 64
TOPK_MAX = 256
NUM_BUCKETS = 32
MAX_DISTANCE = 128
D_INNER = 2 * D_MODEL
SSD_HEADDIM = 64
SSD_HEADS = D_INNER // SSD_HEADDIM
SSD_GROUPS = 8
D_STATE = 128
CONV_W = 4
CONV_DIM = D_INNER + 2 * SSD_GROUPS * D_STATE
D_FF = -(-8 * D_MODEL // (3 * 256)) * 256

LANES = 128
SUBLANES = 8
KEY_TILE = 256
SSD_Q = 128
VMEM_LIMIT = 56 * 1024 * 1024
BISECT_MAX_ITERS = 320
BISECT_UNROLL = 4
LOG2E = math.log2(math.e)

COL_HEAD, COL_GA, COL_Z, COL_XBC, COL_GB = 0, 1024, 2048, 4096, 8192
PROJ_COLS = 9216
HEAD_KV, HEAD_KIDX, HEAD_W, HEAD_DT = 256, 384, 512, 640


def _cparams(sem):
    return pltpu.CompilerParams(dimension_semantics=sem, vmem_limit_bytes=VMEM_LIMIT)


def _const_spec(shape):
    nd = len(shape)
    return pl.BlockSpec(shape, lambda *_: (0,) * nd, pipeline_mode=pl.Buffered(1))


def _rms(x, w, n=None):
    n = x.shape[-1] if n is None else n
    return x * lax.rsqrt(jnp.sum(x * x, axis=-1, keepdims=True) * (1.0 / n) + EPS) * w


def _silu(x):
    return x * jax.nn.sigmoid(x)


def _mod_kernel(c_ref, w_ref, b_ref, o_ref):
    c = c_ref[...]
    s = _silu(c).astype(BF16)
    o_ref[...] = jnp.dot(s, w_ref[...].astype(BF16), preferred_element_type=F32) + b_ref[...]


def _mod(c, ada_w, ada_b):
    bsz = c.shape[0]
    return pl.pallas_call(
        _mod_kernel,
        grid=(6,),
        in_specs=[pl.BlockSpec((bsz, D_MODEL), lambda j: (0, 0)),
                  pl.BlockSpec((D_MODEL, D_MODEL), lambda j: (0, j)),
                  pl.BlockSpec((1, D_MODEL), lambda j: (0, j))],
        out_specs=pl.BlockSpec((bsz, D_MODEL), lambda j: (0, j)),
        out_shape=jax.ShapeDtypeStruct((bsz, 6 * D_MODEL), F32),
        compiler_params=_cparams(("parallel",)),
        name="mod",
    )(c, ada_w, ada_b.reshape(1, 6 * D_MODEL))


def _t5_bucket_np(rel):
    half = NUM_BUCKETS // 2
    max_exact = half // 2
    side = np.where(rel > 0, half, 0)
    n = np.abs(rel)
    large = max_exact + (np.log(np.maximum(n, max_exact).astype(np.float64) / max_exact)
                         / math.log(MAX_DISTANCE / max_exact) * (half - max_exact)).astype(np.int64)
    large = np.minimum(large, half - 1)
    return (side + np.where(n < max_exact, n, large)).astype(np.int32)


def _bias_kernel(idx_ref, rb_ref, o_ref):
    h = pl.program_id(0)
    idx = idx_ref[...]
    far = rb_ref[h, NUM_BUCKETS // 2 - 1]
    acc = jnp.zeros(idx.shape, F32)
    for b in range(NUM_BUCKETS):
        acc = jnp.where(idx == b, (rb_ref[h, b] - far) * LOG2E, acc)
    o_ref[0] = acc


def _bias_tables(rel_bias):
    kk = np.arange(2 * KEY_TILE)[:, None]
    ql = np.arange(Q_BLOCK)[None, :]
    idx = jnp.asarray(_t5_bucket_np(kk - KEY_TILE - ql))
    t = pl.pallas_call(
        _bias_kernel,
        grid=(ATTN_HEADS,),
        in_specs=[pl.BlockSpec((2 * KEY_TILE, Q_BLOCK), lambda h: (0, 0)),
                  pl.BlockSpec(memory_space=pltpu.SMEM)],
        out_specs=pl.BlockSpec((1, 2 * KEY_TILE, Q_BLOCK), lambda h: (h, 0, 0)),
        out_shape=jax.ShapeDtypeStruct((ATTN_HEADS, 2 * KEY_TILE, Q_BLOCK), F32),
        compiler_params=_cparams(("arbitrary",)),
        name="bias",
    )(idx, rel_bias.T)
    return jnp.stack([t[:, 256:512], t[:, 128:384], t[:, 0:256]])


def _inproj_kernel(x_ref, mod_ref, nw_ref, w_ref, o_ref, hn_ref):
    @pl.when(pl.program_id(2) == 0)
    def _():
        m = mod_ref[0]
        y = _rms(x_ref[0], nw_ref[...])
        hn_ref[...] = (y * (1.0 + m[1:2]) + m[0:1]).astype(BF16)

    o_ref[0] = jnp.dot(hn_ref[...], w_ref[...], preferred_element_type=F32)


def _pack_w_in(w_in):
    sizes = [Q_RANK, KV_RANK, IDX_DIM, IDX_HEADS, D_INNER, CONV_DIM, SSD_HEADS, D_MODEL, D_MODEL]
    offs = np.cumsum([0] + sizes)
    q, kv, ki, wi, z, xbc, dt, ga, gb = [w_in[:, offs[i]:offs[i + 1]] for i in range(9)]

    def zc(n):
        return jnp.zeros((D_MODEL, n), w_in.dtype)

    head = jnp.concatenate([q, kv, ki, zc(LANES - IDX_DIM), wi, zc(LANES - IDX_HEADS),
                            dt, zc(LANES - SSD_HEADS), zc(2 * LANES)], axis=1)
    return jnp.concatenate([head, ga, z, xbc, gb], axis=1).astype(BF16)


def _inproj(x, mod3, pre_norm, w_packed):
    bsz, seq, _ = x.shape
    tm = min(seq, 1024)
    tn = 1024
    return pl.pallas_call(
        _inproj_kernel,
        grid=(bsz, seq // tm, PROJ_COLS // tn),
        in_specs=[pl.BlockSpec((1, tm, D_MODEL), lambda b, i, j: (b, i, 0)),
                  pl.BlockSpec((1, 6, D_MODEL), lambda b, i, j: (b, 0, 0)),
                  pl.BlockSpec((1, D_MODEL), lambda b, i, j: (0, 0)),
                  pl.BlockSpec((D_MODEL, tn), lambda b, i, j: (0, j))],
        out_specs=pl.BlockSpec((1, tm, tn), lambda b, i, j: (b, i, j)),
        out_shape=jax.ShapeDtypeStruct((bsz, seq, PROJ_COLS), F32),
        scratch_shapes=[pltpu.VMEM((tm, D_MODEL), BF16)],
        compiler_params=_cparams(("parallel", "parallel", "arbitrary")),
        name="inproj",
    )(x, mod3, pre_norm.reshape(1, D_MODEL), w_packed)


def _prep_kernel(p_ref, qn_ref, kvn_ref, kin_ref, wuqt_ref, wqit_ref,
                 qt_ref, qit_ref, kv_ref, kvt_ref, ki_ref, wt_ref, *, tc):
    p = p_ref[0]
    qnt = _rms(p[:, :Q_RANK], qn_ref[...]).T.astype(BF16)
    qt = (jnp.dot(wuqt_ref[...], qnt, preferred_element_type=F32) * (KV_RANK ** -0.5 * LOG2E)).astype(BF16)
    qit = jnp.dot(wqit_ref[...], qnt, preferred_element_type=F32).astype(BF16)
    wt = (p[:, HEAD_W:HEAD_W + LANES] * (IDX_HEADS ** -0.5 * IDX_DIM ** -0.5)).T
    for blk in range(tc // Q_BLOCK):
        cols = slice(blk * Q_BLOCK, (blk + 1) * Q_BLOCK)
        for h in range(ATTN_HEADS):
            qt_ref[0, blk, :, h * LANES:(h + 1) * LANES] = qt[h * LANES:(h + 1) * LANES, cols]
            qit_ref[0, blk, :, h * LANES:(h + 1) * LANES] = qit[h * LANES:(h + 1) * LANES, cols]
        wt_ref[0, blk] = wt[0:IDX_HEADS, cols]
    kv = _rms(p[:, HEAD_KV:HEAD_KV + KV_RANK], kvn_ref[...])
    kv_ref[0] = kv.astype(BF16)
    kvt = kv.T
    for c in range(tc // KEY_TILE):
        kvt_ref[0, c] = kvt[:, c * KEY_TILE:(c + 1) * KEY_TILE].astype(BF16)
    ki_ref[0] = _rms(p[:, HEAD_KIDX:HEAD_KIDX + LANES], kin_ref[...], n=IDX_DIM).astype(BF16)


def _prep(proj, q_norm, kv_norm, kidx_norm, w_uq, w_qidx):
    bsz, seq, _ = proj.shape
    tc = min(seq, 512)
    wqi = w_qidx.reshape(Q_RANK, IDX_HEADS, IDX_DIM)
    wqit = jnp.concatenate([wqi, jnp.zeros_like(wqi)], axis=-1).reshape(Q_RANK, IDX_HEADS * LANES).T.astype(BF16)
    kin = jnp.concatenate([kidx_norm, jnp.zeros((LANES - IDX_DIM,), F32)]).reshape(1, LANES)
    nb = seq // Q_BLOCK
    slab = jax.ShapeDtypeStruct((bsz, nb, LANES, ATTN_HEADS * Q_BLOCK), BF16)
    slab_spec = pl.BlockSpec((1, tc // Q_BLOCK, LANES, ATTN_HEADS * Q_BLOCK), lambda b, i: (b, i, 0, 0))
    row_spec = pl.BlockSpec((1, tc, LANES), lambda b, i: (b, i, 0))
    return pl.pallas_call(
        functools.partial(_prep_kernel, tc=tc),
        grid=(bsz, seq // tc),
        in_specs=[pl.BlockSpec((1, tc, 1024), lambda b, i: (b, i, 0)),
                  _const_spec((1, Q_RANK)), _const_spec((1, KV_RANK)), _const_spec((1, LANES)),
                  _const_spec((ATTN_HEADS * KV_RANK, Q_RANK)), _const_spec((IDX_HEADS * LANES, Q_RANK))],
        out_specs=[slab_spec, slab_spec, row_spec,
                   pl.BlockSpec((1, tc // KEY_TILE, LANES, KEY_TILE), lambda b, i: (b, i, 0, 0)),
                   row_spec,
                   pl.BlockSpec((1, tc // Q_BLOCK, IDX_HEADS, Q_BLOCK), lambda b, i: (b, i, 0, 0))],
        out_shape=[slab, slab, jax.ShapeDtypeStruct((bsz, seq, LANES), BF16),
                   jax.ShapeDtypeStruct((bsz, seq // KEY_TILE, LANES, KEY_TILE), BF16),
                   jax.ShapeDtypeStruct((bsz, seq, LANES), BF16),
                   jax.ShapeDtypeStruct((bsz, nb, IDX_HEADS, Q_BLOCK), F32)],
        compiler_params=_cparams(("parallel", "parallel")),
        name="prep",
    )(proj, q_norm.reshape(1, Q_RANK), kv_norm.reshape(1, KV_RANK), kin, w_uq.T.astype(BF16), wqit)


def _dsa_kernel(qt_ref, qit_ref, wt_ref, ki_ref, kv_ref, kvt_ref, tb_ref, wuv_ref, o_ref,
                isct_ref, sbuf_ref, pbuf_ref, acc_ref, *, k_sel):
    i = pl.program_id(1)
    last = i // 2
    odd = i % 2
    n_tiles = last + 1
    qt = qt_ref[0, 0]
    qit = qit_ref[0, 0]
    wt = wt_ref[0, 0]
    row = lax.broadcasted_iota(jnp.int32, (KEY_TILE, Q_BLOCK), 0)
    col = lax.broadcasted_iota(jnp.int32, (KEY_TILE, Q_BLOCK), 1)
    key_limit = i * Q_BLOCK + jnp.where(col < CHUNK, CHUNK, 2 * CHUNK)

    def key_rows(j):
        return pl.ds(pl.multiple_of(j * KEY_TILE, KEY_TILE), KEY_TILE)

    def head(x, h):
        return x[:, h * Q_BLOCK:(h + 1) * Q_BLOCK]

    def idx_body(j, carry):
        s = jnp.dot(ki_ref[0, key_rows(j), :], qit, preferred_element_type=F32)
        acc = jnp.zeros((KEY_TILE, Q_BLOCK), F32)
        for h in range(IDX_HEADS):
            acc = acc + wt[h:h + 1, :] * jnp.maximum(head(s, h), 0.0)
        isct_ref[j] = jnp.where(j * KEY_TILE + row < key_limit, acc, -jnp.inf)
        return carry

    def tile_loop(n, body, carry):
        def pair(jj, c):
            return body(2 * jj + 1, body(2 * jj, c))
        carry = lax.fori_loop(0, n // 2, pair, carry)
        return lax.cond(n % 2 == 1, lambda c: body(n - 1, c), lambda c: c, carry)

    tile_loop(n_tiles, idx_body, 0)

    def rows_all(x, op):
        return jnp.broadcast_to(op(x, axis=0, keepdims=True), (SUBLANES, LANES))

    ACCS = 4

    def tile_rows(j):
        return isct_ref[j].reshape(KEY_TILE // (ACCS * SUBLANES), ACCS, SUBLANES, LANES)

    def count_where(pred):
        def body(j, c):
            return c + jnp.sum(jnp.where(pred(tile_rows(j)), 1.0, 0.0), axis=0)
        c = lax.fori_loop(0, n_tiles, body, jnp.zeros((ACCS, SUBLANES, LANES), F32))
        return rows_all(jnp.sum(c, axis=0), jnp.sum)

    def minmax_body(j, c):
        lo, hi = c
        x = tile_rows(j)
        return (jnp.minimum(lo, jnp.min(jnp.where(x == -jnp.inf, jnp.inf, x), axis=0)),
                jnp.maximum(hi, jnp.max(x, axis=0)))

    lo, hi = lax.fori_loop(0, n_tiles, minmax_body,
                           (jnp.full((ACCS, SUBLANES, LANES), jnp.inf, F32),
                            jnp.full((ACCS, SUBLANES, LANES), -jnp.inf, F32)))
    lo = rows_all(jnp.min(lo, axis=0), jnp.min)
    hi = rows_all(jnp.max(hi, axis=0), jnp.max)
    kf = float(k_sel)
    cnt_lo = count_where(lambda x: x >= lo)
    cnt_hi = count_where(lambda x: x >= hi)
    at_max = cnt_hi >= kf
    lo = jnp.where(at_max, hi, lo)
    cnt = jnp.where(at_max, cnt_hi, cnt_lo)

    def bis_step(c):
        lo, hi, cnt, stalled = c
        mid = 0.5 * lo + 0.5 * hi
        cm = count_where(lambda x: x >= mid)
        active = jnp.logical_and(cnt > kf, stalled == 0.0)
        noprog = jnp.logical_or(mid <= lo, mid >= hi)
        move = jnp.logical_and(active, jnp.logical_not(noprog))
        up = jnp.logical_and(move, cm >= kf)
        down = jnp.logical_and(move, cm < kf)
        return (jnp.where(up, mid, lo), jnp.where(down, mid, hi), jnp.where(up, cm, cnt),
                jnp.where(jnp.logical_and(active, noprog), 1.0, stalled))

    def bis_cond(c):
        _, _, cnt, stalled, it = c
        active = jnp.where(jnp.logical_and(cnt > kf, stalled == 0.0), 1.0, 0.0)
        return jnp.logical_and(it < BISECT_MAX_ITERS, jnp.max(active) > 0.0)

    def bis_body(c):
        state = c[:4]
        for _ in range(BISECT_UNROLL):
            state = bis_step(state)
        return state + (c[4] + BISECT_UNROLL,)

    thr8, _, cnt, _, _ = lax.while_loop(
        bis_cond, bis_body, (lo, hi, cnt, jnp.zeros((SUBLANES, LANES), F32), jnp.int32(0)))

    tied = cnt > kf

    @pl.when(jnp.max(jnp.where(tied, 1.0, 0.0)) > 0.0)
    def _():
        need = kf - count_where(lambda x: x > thr8)

        def body(j, seen):
            x = isct_ref[j]
            eq = jnp.where(x == thr8[0:1], 1.0, 0.0)
            inc = _cumsum_rows(eq)
            rank = inc - eq + seen[0:1]
            drop = jnp.logical_and(jnp.logical_and(tied[0:1], eq > 0.0), rank >= need[0:1])
            isct_ref[j] = jnp.where(drop, -jnp.inf, x)
            return seen + inc[KEY_TILE - 1:KEY_TILE]

        lax.fori_loop(0, n_tiles, body, jnp.zeros((SUBLANES, LANES), F32))

    thr = thr8[0:1]

    def fold_rows(x, op):
        x = x.reshape(KEY_TILE // (ACCS * SUBLANES), ACCS, SUBLANES, LANES)
        return op(op(x, axis=0), axis=0)

    def p1_tile(j, m, near):
        s = jnp.dot(kv_ref[0, key_rows(j), :], qt, preferred_element_type=F32)
        neg = jnp.where(isct_ref[j] >= thr, 0.0, -jnp.inf)
        table = jnp.where(j == last, odd, 2)
        out = []
        for h in range(ATTN_HEADS):
            sh = head(s, h) + neg
            if near:
                sh = sh + tb_ref[table, h]
            sbuf_ref[j, h] = sh
            out.append(jnp.maximum(m[h], fold_rows(sh, jnp.max)))
        return jnp.stack(out)

    n_far = jnp.maximum(last - 1 + odd, 0)
    m = jnp.full((ATTN_HEADS, SUBLANES, LANES), -jnp.inf, F32)
    m = tile_loop(n_far, lambda j, m: p1_tile(j, m, False), m)
    m = lax.fori_loop(n_far, n_tiles, lambda j, m: p1_tile(j, m, True), m)
    m = jnp.max(m, axis=1, keepdims=True)

    acc_ref[...] = jnp.zeros_like(acc_ref)

    def p2_body(j, l):
        out = []
        for h in range(ATTN_HEADS):
            p = jnp.exp2(sbuf_ref[j, h] - m[h])
            out.append(l[h] + fold_rows(p, jnp.sum))
            pbuf_ref[:, h * Q_BLOCK:(h + 1) * Q_BLOCK] = p.astype(BF16)
        acc_ref[...] += jnp.dot(kvt_ref[0, j], pbuf_ref[...], preferred_element_type=F32)
        return jnp.stack(out)

    l = tile_loop(n_tiles, p2_body, jnp.zeros((ATTN_HEADS, SUBLANES, LANES), F32))
    l = jnp.sum(l, axis=1, keepdims=True)

    outs = [(acc_ref[:, h * Q_BLOCK:(h + 1) * Q_BLOCK] / l[h]).T.astype(BF16)
            for h in range(ATTN_HEADS)]
    for p in range(ATTN_HEADS // 2):
        pair = jnp.concatenate(outs[2 * p:2 * p + 2], axis=1)
        o_ref[0, :, p * LANES:(p + 1) * LANES] = jnp.dot(
            pair, wuv_ref[p], preferred_element_type=F32).astype(BF16)


def _pack_w_uv(w_uv):
    eye = jnp.eye(2, dtype=w_uv.dtype)
    w = w_uv.reshape(ATTN_HEADS // 2, 2, KV_RANK, 1, HEAD_DIM) * eye[None, :, None, :, None]
    return w.reshape(ATTN_HEADS // 2, 2 * KV_RANK, 2 * HEAD_DIM).astype(BF16)


def _dsa(qt, qit, wt, ki, kv, kvt, tables, wuv):
    bsz, seq, _ = kv.shape
    nkt = seq // KEY_TILE
    k_sel = min(TOPK_MAX, seq // 4)
    slab_spec = pl.BlockSpec((1, 1, LANES, ATTN_HEADS * Q_BLOCK), lambda b, i: (b, i, 0, 0))
    seq_spec = pl.BlockSpec((1, seq, LANES), lambda b, i: (b, 0, 0))
    return pl.pallas_call(
        functools.partial(_dsa_kernel, k_sel=k_sel),
        grid=(bsz, seq // Q_BLOCK),
        in_specs=[slab_spec, slab_spec,
                  pl.BlockSpec((1, 1, IDX_HEADS, Q_BLOCK), lambda b, i: (b, i, 0, 0)),
                  seq_spec, seq_spec,
                  pl.BlockSpec((1, nkt, LANES, KEY_TILE), lambda b, i: (b, 0, 0, 0)),
                  _const_spec((3, ATTN_HEADS, KEY_TILE, Q_BLOCK)),
                  _const_spec((ATTN_HEADS // 2, 2 * KV_RANK, 2 * HEAD_DIM))],
        out_specs=pl.BlockSpec((1, Q_BLOCK, ATTN_HEADS * HEAD_DIM), lambda b, i: (b, i, 0)),
        out_shape=jax.ShapeDtypeStruct((bsz, seq, ATTN_HEADS * HEAD_DIM), BF16),
        scratch_shapes=[pltpu.VMEM((nkt, KEY_TILE, Q_BLOCK), F32),
                        pltpu.VMEM((nkt, ATTN_HEADS, KEY_TILE, Q_BLOCK), F32),
                        pltpu.VMEM((KEY_TILE, ATTN_HEADS * Q_BLOCK), BF16),
                        pltpu.VMEM((KV_RANK, ATTN_HEADS * Q_BLOCK), F32)],
        compiler_params=_cparams(("parallel", "arbitrary")),
        name="dsa",
    )(qt, qit, wt, ki, kv, kvt, tables, wuv)


def _split3(v):
    hi = v.astype(BF16)
    r = v - hi.astype(F32)
    mid = r.astype(BF16)
    lo = (r - mid.astype(F32)).astype(BF16)
    return hi, mid, lo


def _expand_heads(v, e):
    return sum(jnp.dot(p, e, preferred_element_type=F32) for p in _split3(v))


def _cumsum_rows(x):
    n = x.shape[0]
    r = lax.broadcasted_iota(jnp.int32, x.shape, 0)
    s = 1
    while s < n:
        x = x + jnp.where(r >= s, pltpu.roll(x, s, axis=0), 0.0)
        s *= 2
    return x


def _ssd_kernel(z_ref, xbc_ref, dt_ref, cw_ref, cb_ref, dtb_ref, alog_ref, dsk_ref, nw_ref, e_ref,
                y_ref, ext_ref, state_ref):
    nq = SSD_Q

    @pl.when(pl.program_id(1) == 0)
    def _():
        ext_ref[0:SUBLANES, :] = jnp.zeros((SUBLANES, CONV_DIM), F32)
        state_ref[...] = jnp.zeros_like(state_ref)

    xb = xbc_ref[0]
    ext_ref[SUBLANES:SUBLANES + nq, :] = xb
    first = SUBLANES - (CONV_W - 1)
    conv = cb_ref[...] + cw_ref[0:1, :] * ext_ref[pl.ds(first, nq), :]
    for k in range(1, CONV_W):
        conv = conv + cw_ref[k:k + 1, :] * ext_ref[pl.ds(first + k, nq), :]
    ext_ref[0:SUBLANES, :] = xb[nq - SUBLANES:nq, :]
    u = _silu(conv)
    xs = u[:, :D_INNER]
    bm = u[:, D_INNER:D_INNER + SSD_GROUPS * D_STATE]
    cm = u[:, D_INNER + SSD_GROUPS * D_STATE:]

    t = dt_ref[0] + dtb_ref[...]
    dt = jnp.maximum(t, 0.0) + jnp.log1p(jnp.exp(-jnp.abs(t)))
    a_cum = _cumsum_rows(dt * (-jnp.exp(alog_ref[...])))
    a_last = a_cum[nq - 1:nq, :]
    e = e_ref[...]
    dt_e = _expand_heads(dt, e)
    decay_e = _expand_heads(jnp.exp(a_last - a_cum), e)
    expa_e = _expand_heads(jnp.exp(a_cum), e)
    a_cum_t = a_cum.T

    xdt = xs * dt_e
    xdt_b = xdt.astype(BF16)
    xdec_b = (xdt * decay_e).astype(BF16)
    r = lax.broadcasted_iota(jnp.int32, (nq, nq), 0)
    c = lax.broadcasted_iota(jnp.int32, (nq, nq), 1)
    causal = r >= c
    lane = lax.broadcasted_iota(jnp.int32, (nq, LANES), 1)
    heads_per_group = SSD_HEADS // SSD_GROUPS
    gw = heads_per_group * SSD_HEADDIM

    ys = []
    for g in range(SSD_GROUPS):
        bg = bm[:, g * D_STATE:(g + 1) * D_STATE]
        cg = cm[:, g * D_STATE:(g + 1) * D_STATE].astype(BF16)
        bgt = bg.T.astype(BF16)
        cb = jnp.dot(cg, bgt, preferred_element_type=F32)
        prev = state_ref[g]
        y_off = jnp.dot(cg, prev.astype(BF16), preferred_element_type=F32) * expa_e[:, g * gw:(g + 1) * gw]
        pairs = []
        for pp in range(heads_per_group // 2):
            blk = g * (heads_per_group // 2) + pp
            xp = xdt_b[:, blk * LANES:(blk + 1) * LANES]
            yh = []
            for hh in range(2):
                h = 2 * blk + hh
                seg = a_cum[:, h:h + 1] - a_cum_t[h:h + 1, :]
                m = (cb * jnp.where(causal, jnp.exp(seg), 0.0)).astype(BF16)
                yh.append(jnp.dot(m, xp, preferred_element_type=F32))
            pairs.append(jnp.where(lane < SSD_HEADDIM, yh[0], yh[1]))
        ys.append(jnp.concatenate(pairs, axis=1) + y_off)
        new = jnp.dot(bgt, xdec_b[:, g * gw:(g + 1) * gw], preferred_element_type=F32)
        state_ref[g] = prev * expa_e[nq - 1:nq, g * gw:(g + 1) * gw] + new
    y = jnp.concatenate(ys, axis=1) + dsk_ref[...] * xs
    y_ref[0] = _rms(y * _silu(z_ref[0]), nw_ref[...]).astype(BF16)


def _ssd(proj, conv_w, conv_b, dt_bias, a_log, d_skip, ssd_norm):
    bsz, seq, _ = proj.shape
    nq = SSD_Q

    def pad_heads(v):
        return jnp.concatenate([v, jnp.zeros((LANES - SSD_HEADS,), F32)]).reshape(1, LANES)

    e = jnp.asarray(np.kron(np.eye(LANES, SSD_HEADS), np.ones((1, SSD_HEADDIM)))[:, :D_INNER], BF16)
    return pl.pallas_call(
        _ssd_kernel,
        grid=(bsz, seq // nq),
        in_specs=[pl.BlockSpec((1, nq, D_INNER), lambda b, i: (b, i, COL_Z // D_INNER)),
                  pl.BlockSpec((1, nq, CONV_DIM), lambda b, i: (b, i, COL_XBC // CONV_DIM)),
                  pl.BlockSpec((1, nq, LANES), lambda b, i: (b, i, HEAD_DT // LANES)),
                  _const_spec((CONV_W, CONV_DIM)), _const_spec((1, CONV_DIM)),
                  _const_spec((1, LANES)), _const_spec((1, LANES)),
                  _const_spec((1, D_INNER)), _const_spec((1, D_INNER)),
                  _const_spec((LANES, D_INNER))],
        out_specs=pl.BlockSpec((1, nq, D_INNER), lambda b, i: (b, i, 0)),
        out_shape=jax.ShapeDtypeStruct((bsz, seq, D_INNER), BF16),
        scratch_shapes=[pltpu.VMEM((SUBLANES + nq, CONV_DIM), F32),
                        pltpu.VMEM((SSD_GROUPS, D_STATE, 4 * SSD_HEADDIM), F32)],
        compiler_params=_cparams(("parallel", "arbitrary")),
        name="ssd",
    )(proj, proj, proj, conv_w, conv_b.reshape(1, CONV_DIM), pad_heads(dt_bias), pad_heads(a_log),
      jnp.repeat(d_skip, SSD_HEADDIM).reshape(1, D_INNER), ssd_norm.reshape(1, D_INNER), e)


def _mix_kernel(ao_ref, sy_ref, ga_ref, gb_ref, x_ref, mod_ref, nw_ref, woa_ref, wos_ref, wout_ref, o_ref):
    ya = jnp.dot(ao_ref[0], woa_ref[...], preferred_element_type=F32)
    yb = jnp.dot(sy_ref[0], wos_ref[...], preferred_element_type=F32)
    mix = jax.nn.sigmoid(ga_ref[0]) * ya + jax.nn.sigmoid(gb_ref[0]) * yb
    m2 = jnp.dot(mix.astype(BF16), wout_ref[...], preferred_element_type=F32)
    o_ref[0] = x_ref[0] + mod_ref[0][2:3] * _rms(m2, nw_ref[...])


def _mix(attn_o, ssd_y, proj, x, mod3, post_norm, w_o_attn, w_o_ssd, w_out):
    bsz, seq, _ = x.shape
    tm = min(seq, 512)

    def rows(width, col_block=0):
        return pl.BlockSpec((1, tm, width), lambda b, i: (b, i, col_block))

    return pl.pallas_call(
        _mix_kernel,
        grid=(bsz, seq // tm),
        in_specs=[rows(D_MODEL), rows(D_INNER), rows(D_MODEL, COL_GA // D_MODEL), rows(D_MODEL, COL_GB // D_MODEL),
                  rows(D_MODEL), pl.BlockSpec((1, 6, D_MODEL), lambda b, i: (b, 0, 0)),
                  _const_spec((1, D_MODEL)), _const_spec((D_MODEL, D_MODEL)),
                  _const_spec((D_INNER, D_MODEL)), _const_spec((D_MODEL, D_MODEL))],
        out_specs=rows(D_MODEL),
        out_shape=jax.ShapeDtypeStruct((bsz, seq, D_MODEL), F32),
        compiler_params=_cparams(("parallel", "parallel")),
        name="mix",
    )(attn_o, ssd_y, proj, proj, x, mod3, post_norm.reshape(1, D_MODEL),
      w_o_attn.astype(BF16), w_o_ssd.astype(BF16), w_out.astype(BF16))


def _ffn_kernel(x_ref, mod_ref, nw1_ref, nw2_ref, wg_ref, wu_ref, wo_ref, o_ref):
    x = x_ref[0]
    m = mod_ref[0]
    h2 = (_rms(x, nw1_ref[...]) * (1.0 + m[4:5]) + m[3:4]).astype(BF16)
    ug = jnp.dot(h2, wg_ref[...], preferred_element_type=F32)
    uu = jnp.dot(h2, wu_ref[...], preferred_element_type=F32)
    f = jnp.dot((_silu(ug) * uu).astype(BF16), wo_ref[...], preferred_element_type=F32)
    o_ref[0] = x + m[5:6] * _rms(f, nw2_ref[...])


def _ffn(x, mod3, pre_norm, post_norm, w_ffn_in, w_ffn_out):
    bsz, seq, _ = x.shape
    tm = min(seq, 512)
    rows = pl.BlockSpec((1, tm, D_MODEL), lambda b, i: (b, i, 0))
    return pl.pallas_call(
        _ffn_kernel,
        grid=(bsz, seq // tm),
        in_specs=[rows, pl.BlockSpec((1, 6, D_MODEL), lambda b, i: (b, 0, 0)),
                  _const_spec((1, D_MODEL)), _const_spec((1, D_MODEL)),
                  _const_spec((D_MODEL, D_FF)), _const_spec((D_MODEL, D_FF)), _const_spec((D_FF, D_MODEL))],
        out_specs=rows,
        out_shape=jax.ShapeDtypeStruct((bsz, seq, D_MODEL), F32),
        compiler_params=_cparams(("parallel", "parallel")),
        name="ffn",
    )(x, mod3, pre_norm.reshape(1, D_MODEL), post_norm.reshape(1, D_MODEL),
      w_ffn_in[:, :D_FF].astype(BF16), w_ffn_in[:, D_FF:].astype(BF16), w_ffn_out.astype(BF16))


def kernel(x, c, positions, ada_w, ada_b, pre_norm_mix, post_norm_mix, pre_norm_ffn, post_norm_ffn, w_in, q_norm, kv_norm, w_uq, w_uv, rel_bias, w_qidx, kidx_norm, conv_w, conv_b, dt_bias, a_log, d_skip, ssd_norm, w_o_attn, w_o_ssd, w_out, w_ffn_in, w_ffn_out):
    del positions
    bsz, seq, _ = x.shape
    assert seq % (2 * KEY_TILE) == 0 and x.shape[-1] == D_MODEL
    mod3 = _mod(c, ada_w, ada_b).reshape(bsz, 6, D_MODEL)
    proj = _inproj(x, mod3, pre_norm_mix, _pack_w_in(w_in))
    qt, qit, kv, kvt, ki, wt = _prep(proj, q_norm, kv_norm, kidx_norm, w_uq, w_qidx)
    attn_o = _dsa(qt, qit, wt, ki, kv, kvt, _bias_tables(rel_bias), _pack_w_uv(w_uv))
    ssd_y = _ssd(proj, conv_w, conv_b, dt_bias, a_log, d_skip, ssd_norm)
    x1 = _mix(attn_o, ssd_y, proj, x, mod3, post_norm_mix, w_o_attn, w_o_ssd, w_out)
    return _ffn(x1, mod3, pre_norm_ffn, post_norm_ffn, w_ffn_in, w_ffn_out)
```

```python
import functools
import math

import numpy as np
import jax
import jax.numpy as jnp
from jax import lax
from jax.experimental import pallas as pl
from jax.experimental.pallas import tpu as pltpu

F32 = jnp.float32
BF16 = jnp.bfloat16

D_MODEL = 1024
CHUNK = 64
Q_BLOCK = 128
EPS = 1e-6
ATTN_HEADS = 16
HEAD_DIM = 64
Q_RANK = 256
KV_RANK = 128
IDX_HEADS = 16
IDX_DIM = 64
TOPK_MAX = 256
NUM_BUCKETS = 32
MAX_DISTANCE = 128
D_INNER = 2 * D_MODEL
SSD_HEADDIM = 64
SSD_HEADS = D_INNER // SSD_HEADDIM
SSD_GROUPS = 8
D_STATE = 128
CONV_W = 4
CONV_DIM = D_INNER + 2 * SSD_GROUPS * D_STATE
D_FF = -(-8 * D_MODEL // (3 * 256)) * 256

LANES = 128
SUBLANES = 8
KEY_TILE = 256
SSD_Q = 128
CONV_BLOCK = 512
VMEM_LIMIT = 56 * 1024 * 1024
BISECT_MAX_ITERS = 320
BISECT_UNROLL = 8
LOG2E = math.log2(math.e)

HEAD_COLS = 1024
HEAD_KV, HEAD_KIDX, HEAD_W, HEAD_DT = 256, 384, 512, 640
WIDE_XBC, WIDE_Z, WIDE_GA, WIDE_GB = 0, 4096, 6144, 7168
WIDE_COLS = 8192


def _cparams(sem):
    return pltpu.CompilerParams(dimension_semantics=sem, vmem_limit_bytes=VMEM_LIMIT)


def _const_spec(shape):
    nd = len(shape)
    return pl.BlockSpec(shape, lambda *_: (0,) * nd, pipeline_mode=pl.Buffered(1))


def _rms(x, w, n=None):
    n = x.shape[-1] if n is None else n
    return x * lax.rsqrt(jnp.sum(x * x, axis=-1, keepdims=True) * (1.0 / n) + EPS) * w


def _silu(x):
    h = 0.5 * x
    return h + h * jnp.tanh(h)


def _mod_kernel(c_ref, w_ref, b_ref, o_ref):
    c = c_ref[...]
    s = _silu(c).astype(BF16)
    o_ref[...] = jnp.dot(s, w_ref[...].astype(BF16), preferred_element_type=F32) + b_ref[...]


def _mod(c, ada_w, ada_b):
    bsz = c.shape[0]
    return pl.pallas_call(
        _mod_kernel,
        grid=(6,),
        in_specs=[pl.BlockSpec((bsz, D_MODEL), lambda j: (0, 0)),
                  pl.BlockSpec((D_MODEL, D_MODEL), lambda j: (0, j)),
                  pl.BlockSpec((1, D_MODEL), lambda j: (0, j))],
        out_specs=pl.BlockSpec((bsz, D_MODEL), lambda j: (0, j)),
        out_shape=jax.ShapeDtypeStruct((bsz, 6 * D_MODEL), F32),
        compiler_params=_cparams(("parallel",)),
        name="mod",
    )(c, ada_w, ada_b.reshape(1, 6 * D_MODEL))


def _t5_bucket_np(rel):
    half = NUM_BUCKETS // 2
    max_exact = half // 2
    side = np.where(rel > 0, half, 0)
    n = np.abs(rel)
    large = max_exact + (np.log(np.maximum(n, max_exact).astype(np.float64) / max_exact)
                         / math.log(MAX_DISTANCE / max_exact) * (half - max_exact)).astype(np.int64)
    large = np.minimum(large, half - 1)
    return (side + np.where(n < max_exact, n, large)).astype(np.int32)


def _bias_kernel(idx_ref, rb_ref, o_ref):
    h = pl.program_id(0)
    idx = idx_ref[...]
    far = rb_ref[h, NUM_BUCKETS // 2 - 1]
    acc = jnp.zeros(idx.shape, F32)
    for b in range(NUM_BUCKETS):
        acc = jnp.where(idx == b, (rb_ref[h, b] - far) * LOG2E, acc)
    o_ref[0] = acc


def _bias_tables(rel_bias):
    kk = np.arange(2 * KEY_TILE)[:, None]
    ql = np.arange(Q_BLOCK)[None, :]
    idx = jnp.asarray(_t5_bucket_np(kk - KEY_TILE - ql))
    t = pl.pallas_call(
        _bias_kernel,
        grid=(ATTN_HEADS,),
        in_specs=[pl.BlockSpec((2 * KEY_TILE, Q_BLOCK), lambda h: (0, 0)),
                  pl.BlockSpec(memory_space=pltpu.SMEM)],
        out_specs=pl.BlockSpec((1, 2 * KEY_TILE, Q_BLOCK), lambda h: (h, 0, 0)),
        out_shape=jax.ShapeDtypeStruct((ATTN_HEADS, 2 * KEY_TILE, Q_BLOCK), F32),
        compiler_params=_cparams(("arbitrary",)),
        name="bias",
    )(idx, rel_bias.T)
    return jnp.stack([t[:, 256:512], t[:, 128:384], t[:, 0:256]])


def _inproj_kernel(x_ref, mod_ref, nw_ref, w_ref, head_ref, wide_ref, hn_ref):
    j = pl.program_id(2)

    @pl.when(j == 0)
    def _():
        m = mod_ref[0]
        y = _rms(x_ref[0], nw_ref[...])
        hn = (y * (1.0 + m[1:2]) + m[0:1]).astype(BF16)
        hn_ref[...] = hn
        head_ref[0] = jnp.dot(hn, w_ref[...], preferred_element_type=F32)

    @pl.when(j > 0)
    def _():
        wide_ref[0] = jnp.dot(hn_ref[...], w_ref[...], preferred_element_type=F32).astype(BF16)


def _pack_w_in(w_in):
    sizes = [Q_RANK, KV_RANK, IDX_DIM, IDX_HEADS, D_INNER, CONV_DIM, SSD_HEADS, D_MODEL, D_MODEL]
    offs = np.cumsum([0] + sizes)
    q, kv, ki, wi, z, xbc, dt, ga, gb = [w_in[:, offs[i]:offs[i + 1]] for i in range(9)]

    def zc(n):
        return jnp.zeros((D_MODEL, n), w_in.dtype)

    head = jnp.concatenate([q, kv, ki, zc(LANES - IDX_DIM), wi, zc(LANES - IDX_HEADS),
                            dt, zc(LANES - SSD_HEADS), zc(2 * LANES)], axis=1)
    return jnp.concatenate([head, xbc, z, ga, gb], axis=1).astype(BF16)


def _inproj(x, mod3, pre_norm, w_packed):
    bsz, seq, _ = x.shape
    tm = min(seq, 1024)
    tn = HEAD_COLS
    return pl.pallas_call(
        _inproj_kernel,
        grid=(bsz, seq // tm, (HEAD_COLS + WIDE_COLS) // tn),
        in_specs=[pl.BlockSpec((1, tm, D_MODEL), lambda b, i, j: (b, i, 0)),
                  pl.BlockSpec((1, 6, D_MODEL), lambda b, i, j: (b, 0, 0)),
                  pl.BlockSpec((1, D_MODEL), lambda b, i, j: (0, 0)),
                  pl.BlockSpec((D_MODEL, tn), lambda b, i, j: (0, j))],
        out_specs=[pl.BlockSpec((1, tm, tn), lambda b, i, j: (b, i, 0)),
                   pl.BlockSpec((1, tm, tn), lambda b, i, j: (b, i, jnp.maximum(j - 1, 0)))],
        out_shape=[jax.ShapeDtypeStruct((bsz, seq, HEAD_COLS), F32),
                   jax.ShapeDtypeStruct((bsz, seq, WIDE_COLS), BF16)],
        scratch_shapes=[pltpu.VMEM((tm, D_MODEL), BF16)],
        compiler_params=_cparams(("parallel", "parallel", "arbitrary")),
        name="inproj",
    )(x, mod3, pre_norm.reshape(1, D_MODEL), w_packed)


def _prep_kernel(p_ref, qn_ref, kvn_ref, kin_ref, wuqt_ref, wqit_ref,
                 qt_ref, qit_ref, kv_ref, kvt_ref, ki_ref, wt_ref, *, tc):
    p = p_ref[0]
    qnt = _rms(p[:, :Q_RANK], qn_ref[...]).T.astype(BF16)
    qt = (jnp.dot(wuqt_ref[...], qnt, preferred_element_type=F32) * (KV_RANK ** -0.5 * LOG2E)).astype(BF16)
    qit = jnp.dot(wqit_ref[...], qnt, preferred_element_type=F32).astype(BF16)
    wt = (p[:, HEAD_W:HEAD_W + LANES] * (IDX_HEADS ** -0.5 * IDX_DIM ** -0.5)).T
    for blk in range(tc // Q_BLOCK):
        cols = slice(blk * Q_BLOCK, (blk + 1) * Q_BLOCK)
        for h in range(ATTN_HEADS):
            qt_ref[0, blk, :, h * LANES:(h + 1) * LANES] = qt[h * LANES:(h + 1) * LANES, cols]
            qit_ref[0, blk, :, h * LANES:(h + 1) * LANES] = qit[h * LANES:(h + 1) * LANES, cols]
        wt_ref[0, blk] = wt[0:IDX_HEADS, cols]
    kv = _rms(p[:, HEAD_KV:HEAD_KV + KV_RANK], kvn_ref[...])
    kv_ref[0] = kv.astype(BF16)
    kvt = kv.T
    for c in range(tc // KEY_TILE):
        kvt_ref[0, c] = kvt[:, c * KEY_TILE:(c + 1) * KEY_TILE].astype(BF16)
    ki_ref[0] = _rms(p[:, HEAD_KIDX:HEAD_KIDX + LANES], kin_ref[...], n=IDX_DIM).astype(BF16)


def _prep(proj, q_norm, kv_norm, kidx_norm, w_uq, w_qidx):
    bsz, seq, _ = proj.shape
    tc = min(seq, 512)
    wqi = w_qidx.reshape(Q_RANK, IDX_HEADS, IDX_DIM)
    wqit = jnp.concatenate([wqi, jnp.zeros_like(wqi)], axis=-1).reshape(Q_RANK, IDX_HEADS * LANES).T.astype(BF16)
    kin = jnp.concatenate([kidx_norm, jnp.zeros((LANES - IDX_DIM,), F32)]).reshape(1, LANES)
    nb = seq // Q_BLOCK
    slab = jax.ShapeDtypeStruct((bsz, nb, LANES, ATTN_HEADS * Q_BLOCK), BF16)
    slab_spec = pl.BlockSpec((1, tc // Q_BLOCK, LANES, ATTN_HEADS * Q_BLOCK), lambda b, i: (b, i, 0, 0))
    row_spec = pl.BlockSpec((1, tc, LANES), lambda b, i: (b, i, 0))
    return pl.pallas_call(
        functools.partial(_prep_kernel, tc=tc),
        grid=(bsz, seq // tc),
        in_specs=[pl.BlockSpec((1, tc, 1024), lambda b, i: (b, i, 0)),
                  _const_spec((1, Q_RANK)), _const_spec((1, KV_RANK)), _const_spec((1, LANES)),
                  _const_spec((ATTN_HEADS * KV_RANK, Q_RANK)), _const_spec((IDX_HEADS * LANES, Q_RANK))],
        out_specs=[slab_spec, slab_spec, row_spec,
                   pl.BlockSpec((1, tc // KEY_TILE, LANES, KEY_TILE), lambda b, i: (b, i, 0, 0)),
                   row_spec,
                   pl.BlockSpec((1, tc // Q_BLOCK, IDX_HEADS, Q_BLOCK), lambda b, i: (b, i, 0, 0))],
        out_shape=[slab, slab, jax.ShapeDtypeStruct((bsz, seq, LANES), BF16),
                   jax.ShapeDtypeStruct((bsz, seq // KEY_TILE, LANES, KEY_TILE), BF16),
                   jax.ShapeDtypeStruct((bsz, seq, LANES), BF16),
                   jax.ShapeDtypeStruct((bsz, nb, IDX_HEADS, Q_BLOCK), F32)],
        compiler_params=_cparams(("parallel", "parallel")),
        name="prep",
    )(proj, q_norm.reshape(1, Q_RANK), kv_norm.reshape(1, KV_RANK), kin, w_uq.T.astype(BF16), wqit)


def _dsa_kernel(qt_ref, qit_ref, wt_ref, ki_ref, kv_ref, kvt_ref, tb_ref, wuv_ref, o_ref,
                isct_ref, sbuf_ref, pbuf_ref, acc_ref, *, k_sel):
    i = pl.program_id(1)
    last = i // 2
    odd = i % 2
    n_tiles = last + 1
    qt = qt_ref[0, 0]
    qit = qit_ref[0, 0]
    wt = wt_ref[0, 0]
    row = lax.broadcasted_iota(jnp.int32, (KEY_TILE, Q_BLOCK), 0)
    col = lax.broadcasted_iota(jnp.int32, (KEY_TILE, Q_BLOCK), 1)
    key_limit = i * Q_BLOCK + jnp.where(col < CHUNK, CHUNK, 2 * CHUNK)

    def key_rows(j):
        return pl.ds(pl.multiple_of(j * KEY_TILE, KEY_TILE), KEY_TILE)

    def head(x, h):
        return x[:, h * Q_BLOCK:(h + 1) * Q_BLOCK]

    def idx_body(j, carry):
        s = jnp.dot(ki_ref[0, key_rows(j), :], qit, preferred_element_type=F32)
        acc = jnp.zeros((KEY_TILE, Q_BLOCK), F32)
        for h in range(IDX_HEADS):
            acc = acc + wt[h:h + 1, :] * jnp.maximum(head(s, h), 0.0)
        isct_ref[j] = jnp.where(j * KEY_TILE + row < key_limit, acc, -jnp.inf)
        return carry

    def tile_loop(n, body, carry):
        def pair(jj, c):
            return body(2 * jj + 1, body(2 * jj, c))
        carry = lax.fori_loop(0, n // 2, pair, carry)
        return lax.cond(n % 2 == 1, lambda c: body(n - 1, c), lambda c: c, carry)

    tile_loop(n_tiles, idx_body, 0)

    def rows_all(x, op):
        return jnp.broadcast_to(op(x, axis=0, keepdims=True), (SUBLANES, LANES))

    ACCS = 4

    def tile_rows(j):
        return isct_ref[j].reshape(KEY_TILE // (ACCS * SUBLANES), ACCS, SUBLANES, LANES)

    def count_where(pred):
        def body(j, c):
            return c + jnp.sum(jnp.where(pred(tile_rows(j)), 1.0, 0.0), axis=0)
        c = lax.fori_loop(0, n_tiles, body, jnp.zeros((ACCS, SUBLANES, LANES), F32))
        return rows_all(jnp.sum(c, axis=0), jnp.sum)

    def minmax_body(j, c):
        lo, hi = c
        x = tile_rows(j)
        return (jnp.minimum(lo, jnp.min(jnp.where(x == -jnp.inf, jnp.inf, x), axis=0)),
                jnp.maximum(hi, jnp.max(x, axis=0)))

    lo, hi = lax.fori_loop(0, n_tiles, minmax_body,
                           (jnp.full((ACCS, SUBLANES, LANES), jnp.inf, F32),
                            jnp.full((ACCS, SUBLANES, LANES), -jnp.inf, F32)))
    lo = rows_all(jnp.min(lo, axis=0), jnp.min)
    hi = rows_all(jnp.max(hi, axis=0), jnp.max)
    kf = float(k_sel)
    cnt_lo = count_where(lambda x: x >= lo)
    cnt_hi = count_where(lambda x: x >= hi)
    at_max = cnt_hi >= kf
    lo = jnp.where(at_max, hi, lo)
    cnt = jnp.where(at_max, cnt_hi, cnt_lo)

    def bis_step(c):
        lo, hi, cnt, stalled = c
        mid = 0.5 * lo + 0.5 * hi
        cm = count_where(lambda x: x >= mid)
        active = jnp.logical_and(cnt > kf, stalled == 0.0)
        noprog = jnp.logical_or(mid <= lo, mid >= hi)
        move = jnp.logical_and(active, jnp.logical_not(noprog))
        up = jnp.logical_and(move, cm >= kf)
        down = jnp.logical_and(move, cm < kf)
        return (jnp.where(up, mid, lo), jnp.where(down, mid, hi), jnp.where(up, cm, cnt),
                jnp.where(jnp.logical_and(active, noprog), 1.0, stalled))

    def bis_cond(c):
        _, _, cnt, stalled, it = c
        active = jnp.where(jnp.logical_and(cnt > kf, stalled == 0.0), 1.0, 0.0)
        return jnp.logical_and(it < BISECT_MAX_ITERS, jnp.max(active) > 0.0)

    def bis_body(c):
        state = c[:4]
        for _ in range(BISECT_UNROLL):
            state = bis_step(state)
        return state + (c[4] + BISECT_UNROLL,)

    thr8, _, cnt, _, _ = lax.while_loop(
        bis_cond, bis_body, (lo, hi, cnt, jnp.zeros((SUBLANES, LANES), F32), jnp.int32(0)))

    tied = cnt > kf

    @pl.when(jnp.max(jnp.where(tied, 1.0, 0.0)) > 0.0)
    def _():
        need = kf - count_where(lambda x: x > thr8)

        def body(j, seen):
            x = isct_ref[j]
            eq = jnp.where(x == thr8[0:1], 1.0, 0.0)
            inc = _cumsum_rows(eq)
            rank = inc - eq + seen[0:1]
            drop = jnp.logical_and(jnp.logical_and(tied[0:1], eq > 0.0), rank >= need[0:1])
            isct_ref[j] = jnp.where(drop, -jnp.inf, x)
            return seen + inc[KEY_TILE - 1:KEY_TILE]

        lax.fori_loop(0, n_tiles, body, jnp.zeros((SUBLANES, LANES), F32))

    thr = thr8[0:1]

    def fold_rows(x, op):
        x = x.reshape(KEY_TILE // (ACCS * SUBLANES), ACCS, SUBLANES, LANES)
        return op(op(x, axis=0), axis=0)

    def p1_tile(j, m, near):
        s = jnp.dot(kv_ref[0, key_rows(j), :], qt, preferred_element_type=F32)
        neg = jnp.where(isct_ref[j] >= thr, 0.0, -jnp.inf)
        table = jnp.where(j == last, odd, 2)
        out = []
        for h in range(ATTN_HEADS):
            sh = head(s, h) + neg
            if near:
                sh = sh + tb_ref[table, h]
            sbuf_ref[j, h] = sh
            out.append(jnp.maximum(m[h], fold_rows(sh, jnp.max)))
        return jnp.stack(out)

    n_far = jnp.maximum(last - 1 + odd, 0)
    m = jnp.full((ATTN_HEADS, SUBLANES, LANES), -jnp.inf, F32)
    m = tile_loop(n_far, lambda j, m: p1_tile(j, m, False), m)
    m = lax.fori_loop(n_far, n_tiles, lambda j, m: p1_tile(j, m, True), m)
    m = jnp.max(m, axis=1, keepdims=True)

    acc_ref[...] = jnp.zeros_like(acc_ref)

    def p2_body(j, l):
        out = []
        for h in range(ATTN_HEADS):
            p = jnp.exp2(sbuf_ref[j, h] - m[h])
            out.append(l[h] + fold_rows(p, jnp.sum))
            pbuf_ref[:, h * Q_BLOCK:(h + 1) * Q_BLOCK] = p.astype(BF16)
        acc_ref[...] += jnp.dot(kvt_ref[0, j], pbuf_ref[...], preferred_element_type=F32)
        return jnp.stack(out)

    l = tile_loop(n_tiles, p2_body, jnp.zeros((ATTN_HEADS, SUBLANES, LANES), F32))
    l = jnp.sum(l, axis=1, keepdims=True)

    outs = [(acc_ref[:, h * Q_BLOCK:(h + 1) * Q_BLOCK] / l[h]).T.astype(BF16)
            for h in range(ATTN_HEADS)]
    for p in range(ATTN_HEADS // 2):
        pair = jnp.concatenate(outs[2 * p:2 * p + 2], axis=1)
        o_ref[0, :, p * LANES:(p + 1) * LANES] = jnp.dot(
            pair, wuv_ref[p], preferred_element_type=F32).astype(BF16)


def _pack_w_uv(w_uv):
    eye = jnp.eye(2, dtype=w_uv.dtype)
    w = w_uv.reshape(ATTN_HEADS // 2, 2, KV_RANK, 1, HEAD_DIM) * eye[None, :, None, :, None]
    return w.reshape(ATTN_HEADS // 2, 2 * KV_RANK, 2 * HEAD_DIM).astype(BF16)


def _dsa(qt, qit, wt, ki, kv, kvt, tables, wuv):
    bsz, seq, _ = kv.shape
    nkt = seq // KEY_TILE
    k_sel = min(TOPK_MAX, seq // 4)
    slab_spec = pl.BlockSpec((1, 1, LANES, ATTN_HEADS * Q_BLOCK), lambda b, i: (b, i, 0, 0))
    seq_spec = pl.BlockSpec((1, seq, LANES), lambda b, i: (b, 0, 0))
    return pl.pallas_call(
        functools.partial(_dsa_kernel, k_sel=k_sel),
        grid=(bsz, seq // Q_BLOCK),
        in_specs=[slab_spec, slab_spec,
                  pl.BlockSpec((1, 1, IDX_HEADS, Q_BLOCK), lambda b, i: (b, i, 0, 0)),
                  seq_spec, seq_spec,
                  pl.BlockSpec((1, nkt, LANES, KEY_TILE), lambda b, i: (b, 0, 0, 0)),
                  _const_spec((3, ATTN_HEADS, KEY_TILE, Q_BLOCK)),
                  _const_spec((ATTN_HEADS // 2, 2 * KV_RANK, 2 * HEAD_DIM))],
        out_specs=pl.BlockSpec((1, Q_BLOCK, ATTN_HEADS * HEAD_DIM), lambda b, i: (b, i, 0)),
        out_shape=jax.ShapeDtypeStruct((bsz, seq, ATTN_HEADS * HEAD_DIM), BF16),
        scratch_shapes=[pltpu.VMEM((nkt, KEY_TILE, Q_BLOCK), F32),
                        pltpu.VMEM((nkt, ATTN_HEADS, KEY_TILE, Q_BLOCK), F32),
                        pltpu.VMEM((KEY_TILE, ATTN_HEADS * Q_BLOCK), BF16),
                        pltpu.VMEM((KV_RANK, ATTN_HEADS * Q_BLOCK), F32)],
        compiler_params=_cparams(("parallel", "arbitrary")),
        name="dsa",
    )(qt, qit, wt, ki, kv, kvt, tables, wuv)


def _pack3(v):
    lane = lax.broadcasted_iota(jnp.int32, v.shape, 1)
    v = jnp.where(lane < SSD_HEADS, v, 0.0)
    hi = v.astype(BF16).astype(F32)
    r = v - hi
    mid = r.astype(BF16).astype(F32)
    lo = r - mid
    return (hi + pltpu.roll(mid, SSD_HEADS, axis=1) + pltpu.roll(lo, 2 * SSD_HEADS, axis=1)).astype(BF16)


def _cumsum_rows(x):
    n = x.shape[0]
    r = lax.broadcasted_iota(jnp.int32, x.shape, 0)
    s = 1
    while s < n:
        x = x + jnp.where(r >= s, pltpu.roll(x, s, axis=0), 0.0)
        s *= 2
    return x


def _shift_rows(x, s):
    r = pltpu.roll(x, s, axis=1)
    prev = jnp.concatenate([r[-1:], r[:-1]], axis=0)
    sub = lax.broadcasted_iota(jnp.int32, x.shape, 1)
    return jnp.where(sub >= s, r, prev)


def _ssd_kernel(z_ref, xbc_ref, dt_ref, cw_ref, cb_ref, dtb_ref, alog_ref, dsk_ref, nw_ref, e_ref,
                y_ref, tail_ref, u_ref, g_ref, state_ref):
    nq = SSD_Q

    @pl.when(pl.program_id(1) == 0)
    def _():
        tail_ref[...] = jnp.zeros_like(tail_ref)
        state_ref[...] = jnp.zeros_like(state_ref)

    assert CONV_W == 4
    for blk in range(CONV_DIM // CONV_BLOCK):
        cols = slice(blk * CONV_BLOCK, (blk + 1) * CONV_BLOCK)
        ext = jnp.concatenate([tail_ref[:, cols], xbc_ref[0, :, cols].astype(F32)], axis=0)
        ext = ext.reshape(1 + nq // SUBLANES, SUBLANES, CONV_BLOCK)
        s1 = _shift_rows(ext, 1)
        a = cw_ref[3:4, cols] * ext + cw_ref[2:3, cols] * s1 + cb_ref[:, cols]
        b = cw_ref[1:2, cols] * ext + cw_ref[0:1, cols] * s1
        conv = (a + _shift_rows(b, 2))[1:].reshape(nq, CONV_BLOCK)
        u_ref[:, cols] = _silu(conv)
        tail_ref[:, cols] = ext[nq // SUBLANES]

    t = dt_ref[0] + dtb_ref[...]
    dt = jnp.maximum(t, 0.0) + jnp.log1p(jnp.exp(-jnp.abs(t)))
    a2 = _cumsum_rows(dt * (-jnp.exp(alog_ref[...]))) * LOG2E
    a2_t = a2.T
    dt_p = _pack3(dt)
    dec_p = _pack3(dt * jnp.exp2(a2[nq - 1:nq, :] - a2))
    expa_p = _pack3(jnp.exp2(a2))

    r = lax.broadcasted_iota(jnp.int32, (nq, nq), 0)
    c = lax.broadcasted_iota(jnp.int32, (nq, nq), 1)
    causal = r >= c
    lane = lax.broadcasted_iota(jnp.int32, (nq, LANES), 1)
    heads_per_group = SSD_HEADS // SSD_GROUPS
    gw = heads_per_group * SSD_HEADDIM
    b_col = D_INNER
    c_col = D_INNER + SSD_GROUPS * D_STATE
    ssq = jnp.zeros((nq, LANES), F32)

    for g in range(SSD_GROUPS):
        gcols = slice(g * gw, (g + 1) * gw)
        eg = e_ref[:, gcols]
        dt_e = jnp.dot(dt_p, eg, preferred_element_type=F32)
        dec_e = jnp.dot(dec_p, eg, preferred_element_type=F32)
        expa_e = jnp.dot(expa_p, eg, preferred_element_type=F32)
        xs = u_ref[:, gcols]
        xdt_b = (xs * dt_e).astype(BF16)
        xdec_b = (xs * dec_e).astype(BF16)
        cg = u_ref[:, c_col + g * D_STATE:c_col + (g + 1) * D_STATE].astype(BF16)
        bgt = u_ref[:, b_col + g * D_STATE:b_col + (g + 1) * D_STATE].T.astype(BF16)
        cb = jnp.dot(cg, bgt, preferred_element_type=F32)
        prev = state_ref[g]
        y_off = jnp.dot(cg, prev.astype(BF16), preferred_element_type=F32) * expa_e
        pairs = []
        for pp in range(heads_per_group // 2):
            xp = xdt_b[:, pp * LANES:(pp + 1) * LANES]
            yh = []
            for hh in range(2):
                h = heads_per_group * g + 2 * pp + hh
                seg = a2[:, h:h + 1] - a2_t[h:h + 1, :]
                m = (cb * jnp.where(causal, jnp.exp2(seg), 0.0)).astype(BF16)
                yh.append(jnp.dot(m, xp, preferred_element_type=F32))
            pairs.append(jnp.where(lane < SSD_HEADDIM, yh[0], yh[1]))
        y = jnp.concatenate(pairs, axis=1) + y_off + dsk_ref[:, gcols] * xs
        gated = y * _silu(z_ref[0, :, gcols].astype(F32))
        g_ref[:, gcols] = gated
        sq = gated * gated
        ssq = ssq + sq[:, :LANES] + sq[:, LANES:]
        new = jnp.dot(bgt, xdec_b, preferred_element_type=F32)
        state_ref[g] = prev * expa_e[nq - 1:nq, :] + new
    scale = lax.rsqrt(jnp.sum(ssq, axis=-1, keepdims=True) * (1.0 / D_INNER) + EPS)
    y_ref[0] = (g_ref[...] * scale * nw_ref[...]).astype(BF16)


def _ssd(head, wide, conv_w, conv_b, dt_bias, a_log, d_skip, ssd_norm):
    bsz, seq, _ = head.shape
    nq = SSD_Q

    def pad_heads(v):
        return jnp.concatenate([v, jnp.zeros((LANES - SSD_HEADS,), F32)]).reshape(1, LANES)

    sel = np.concatenate([np.eye(SSD_HEADS)] * 3 + [np.zeros((LANES - 3 * SSD_HEADS, SSD_HEADS))], axis=0)
    e = jnp.asarray(np.kron(sel, np.ones((1, SSD_HEADDIM))), BF16)
    return pl.pallas_call(
        _ssd_kernel,
        grid=(bsz, seq // nq),
        in_specs=[pl.BlockSpec((1, nq, D_INNER), lambda b, i: (b, i, WIDE_Z // D_INNER)),
                  pl.BlockSpec((1, nq, CONV_DIM), lambda b, i: (b, i, WIDE_XBC // CONV_DIM)),
                  pl.BlockSpec((1, nq, LANES), lambda b, i: (b, i, HEAD_DT // LANES)),
                  _const_spec((CONV_W, CONV_DIM)), _const_spec((1, CONV_DIM)),
                  _const_spec((1, LANES)), _const_spec((1, LANES)),
                  _const_spec((1, D_INNER)), _const_spec((1, D_INNER)),
                  _const_spec((LANES, D_INNER))],
        out_specs=pl.BlockSpec((1, nq, D_INNER), lambda b, i: (b, i, 0)),
        out_shape=jax.ShapeDtypeStruct((bsz, seq, D_INNER), BF16),
        scratch_shapes=[pltpu.VMEM((SUBLANES, CONV_DIM), F32),
                        pltpu.VMEM((nq, CONV_DIM), F32),
                        pltpu.VMEM((nq, D_INNER), F32),
                        pltpu.VMEM((SSD_GROUPS, D_STATE, 4 * SSD_HEADDIM), F32)],
        compiler_params=_cparams(("parallel", "arbitrary")),
        name="ssd",
    )(wide, wide, head, conv_w, conv_b.reshape(1, CONV_DIM), pad_heads(dt_bias), pad_heads(a_log),
      jnp.repeat(d_skip, SSD_HEADDIM).reshape(1, D_INNER), ssd_norm.reshape(1, D_INNER), e)


def _mix_kernel(ao_ref, sy_ref, ga_ref, gb_ref, x_ref, mod_ref, nw_ref, woa_ref, wos_ref, wout_ref, o_ref):
    ya = jnp.dot(ao_ref[0], woa_ref[...], preferred_element_type=F32)
    yb = jnp.dot(sy_ref[0], wos_ref[...], preferred_element_type=F32)
    mix = jax.nn.sigmoid(ga_ref[0].astype(F32)) * ya + jax.nn.sigmoid(gb_ref[0].astype(F32)) * yb
    m2 = jnp.dot(mix.astype(BF16), wout_ref[...], preferred_element_type=F32)
    o_ref[0] = x_ref[0] + mod_ref[0][2:3] * _rms(m2, nw_ref[...])


def _mix(attn_o, ssd_y, wide, x, mod3, post_norm, w_o_attn, w_o_ssd, w_out):
    bsz, seq, _ = x.shape
    tm = min(seq, 512)

    def rows(width, col_block=0):
        return pl.BlockSpec((1, tm, width), lambda b, i: (b, i, col_block))

    return pl.pallas_call(
        _mix_kernel,
        grid=(bsz, seq // tm),
        in_specs=[rows(D_MODEL), rows(D_INNER), rows(D_MODEL, WIDE_GA // D_MODEL), rows(D_MODEL, WIDE_GB // D_MODEL),
                  rows(D_MODEL), pl.BlockSpec((1, 6, D_MODEL), lambda b, i: (b, 0, 0)),
                  _const_spec((1, D_MODEL)), _const_spec((D_MODEL, D_MODEL)),
                  _const_spec((D_INNER, D_MODEL)), _const_spec((D_MODEL, D_MODEL))],
        out_specs=rows(D_MODEL),
        out_shape=jax.ShapeDtypeStruct((bsz, seq, D_MODEL), F32),
        compiler_params=_cparams(("parallel", "parallel")),
        name="mix",
    )(attn_o, ssd_y, wide, wide, x, mod3, post_norm.reshape(1, D_MODEL),
      w_o_attn.astype(BF16), w_o_ssd.astype(BF16), w_out.astype(BF16))


def _ffn_kernel(x_ref, mod_ref, nw1_ref, nw2_ref, wg_ref, wu_ref, wo_ref, o_ref):
    x = x_ref[0]
    m = mod_ref[0]
    h2 = (_rms(x, nw1_ref[...]) * (1.0 + m[4:5]) + m[3:4]).astype(BF16)
    ug = jnp.dot(h2, wg_ref[...], preferred_element_type=F32)
    uu = jnp.dot(h2, wu_ref[...], preferred_element_type=F32)
    f = jnp.dot((_silu(ug) * uu).astype(BF16), wo_ref[...], preferred_element_type=F32)
    o_ref[0] = x + m[5:6] * _rms(f, nw2_ref[...])


def _ffn(x, mod3, pre_norm, post_norm, w_ffn_in, w_ffn_out):
    bsz, seq, _ = x.shape
    tm = min(seq, 512)
    rows = pl.BlockSpec((1, tm, D_MODEL), lambda b, i: (b, i, 0))
    w_in = w_ffn_in.astype(BF16)

    def half(k):
        return pl.BlockSpec((D_MODEL, D_FF), lambda b, i: (0, k), pipeline_mode=pl.Buffered(1))

    return pl.pallas_call(
        _ffn_kernel,
        grid=(bsz, seq // tm),
        in_specs=[rows, pl.BlockSpec((1, 6, D_MODEL), lambda b, i: (b, 0, 0)),
                  _const_spec((1, D_MODEL)), _const_spec((1, D_MODEL)),
                  half(0), half(1), _const_spec((D_FF, D_MODEL))],
        out_specs=rows,
        out_shape=jax.ShapeDtypeStruct((bsz, seq, D_MODEL), F32),
        compiler_params=_cparams(("parallel", "parallel")),
        name="ffn",
    )(x, mod3, pre_norm.reshape(1, D_MODEL), post_norm.reshape(1, D_MODEL), w_in, w_in, w_ffn_out.astype(BF16))


def kernel(x, c, positions, ada_w, ada_b, pre_norm_mix, post_norm_mix, pre_norm_ffn, post_norm_ffn, w_in, q_norm, kv_norm, w_uq, w_uv, rel_bias, w_qidx, kidx_norm, conv_w, conv_b, dt_bias, a_log, d_skip, ssd_norm, w_o_attn, w_o_ssd, w_out, w_ffn_in, w_ffn_out):
    del positions
    bsz, seq, _ = x.shape
    assert seq % (2 * KEY_TILE) == 0 and x.shape[-1] == D_MODEL
    mod3 = _mod(c, ada_w, ada_b).reshape(bsz, 6, D_MODEL)
    head, wide = _inproj(x, mod3, pre_norm_mix, _pack_w_in(w_in))
    qt, qit, kv, kvt, ki, wt = _prep(head, q_norm, kv_norm, kidx_norm, w_uq, w_qidx)
    attn_o = _dsa(qt, qit, wt, ki, kv, kvt, _bias_tables(rel_bias), _pack_w_uv(w_uv))
    ssd_y = _ssd(head, wide, conv_w, conv_b, dt_bias, a_log, d_skip, ssd_norm)
    x1 = _mix(attn_o, ssd_y, wide, x, mod3, post_norm_mix, w_o_attn, w_o_ssd, w_out)
    return _ffn(x1, mod3, pre_norm_ffn, post_norm_ffn, w_ffn_in, w_ffn_out)
```

```python
import functools
import math

import numpy as np
import jax
import jax.numpy as jnp
from jax import lax
from jax.experimental import pallas as pl
from jax.experimental.pallas import tpu as pltpu

F32 = jnp.float32
BF16 = jnp.bfloat16

D_MODEL = 1024
CHUNK = 64
Q_BLOCK = 128
EPS = 1e-6
ATTN_HEADS = 16
HEAD_DIM = 64
Q_RANK = 256
KV_RANK = 128
IDX_HEADS = 16
IDX_DIM = 64
TOPK_MAX = 256
NUM_BUCKETS = 32
MAX_DISTANCE = 128
D_INNER = 2 * D_MODEL
SSD_HEADDIM = 64
SSD_HEADS = D_INNER // SSD_HEADDIM
SSD_GROUPS = 8
D_STATE = 128
CONV_W = 4
CONV_DIM = D_INNER + 2 * SSD_GROUPS * D_STATE
D_FF = -(-8 * D_MODEL // (3 * 256)) * 256

LANES = 128
SUBLANES = 8
KEY_TILE = 256
SSD_Q = 128
CONV_BLOCK = 512
VMEM_LIMIT = 56 * 1024 * 1024
BISECT_MAX_ITERS = 320
BISECT_UNROLL = 4
BISECT_COARSE_ITERS = 14
LOG2E = math.log2(math.e)

HEAD_COLS = 1024
HEAD_KV, HEAD_KIDX, HEAD_W, HEAD_DT = 256, 384, 512, 640
WIDE_XBC, WIDE_Z, WIDE_GA, WIDE_GB = 0, 4096, 6144, 7168
WIDE_COLS = 8192


def _cparams(sem):
    return pltpu.CompilerParams(dimension_semantics=sem, vmem_limit_bytes=VMEM_LIMIT)


def _const_spec(shape):
    nd = len(shape)
    return pl.BlockSpec(shape, lambda *_: (0,) * nd, pipeline_mode=pl.Buffered(1))


def _rms(x, w, n=None):
    n = x.shape[-1] if n is None else n
    return x * lax.rsqrt(jnp.sum(x * x, axis=-1, keepdims=True) * (1.0 / n) + EPS) * w


def _silu(x):
    h = 0.5 * x
    return h + h * jnp.tanh(h)


def _mod_kernel(c_ref, w_ref, b_ref, o_ref):
    c = c_ref[...]
    s = _silu(c).astype(BF16)
    o_ref[...] = jnp.dot(s, w_ref[...].astype(BF16), preferred_element_type=F32) + b_ref[...]


def _mod(c, ada_w, ada_b):
    bsz = c.shape[0]
    return pl.pallas_call(
        _mod_kernel,
        grid=(6,),
        in_specs=[pl.BlockSpec((bsz, D_MODEL), lambda j: (0, 0)),
                  pl.BlockSpec((D_MODEL, D_MODEL), lambda j: (0, j)),
                  pl.BlockSpec((1, D_MODEL), lambda j: (0, j))],
        out_specs=pl.BlockSpec((bsz, D_MODEL), lambda j: (0, j)),
        out_shape=jax.ShapeDtypeStruct((bsz, 6 * D_MODEL), F32),
        compiler_params=_cparams(("parallel",)),
        name="mod",
    )(c, ada_w, ada_b.reshape(1, 6 * D_MODEL))


def _t5_bucket_np(rel):
    half = NUM_BUCKETS // 2
    max_exact = half // 2
    side = np.where(rel > 0, half, 0)
    n = np.abs(rel)
    large = max_exact + (np.log(np.maximum(n, max_exact).astype(np.float64) / max_exact)
                         / math.log(MAX_DISTANCE / max_exact) * (half - max_exact)).astype(np.int64)
    large = np.minimum(large, half - 1)
    return (side + np.where(n < max_exact, n, large)).astype(np.int32)


def _bias_kernel(idx_ref, rb_ref, o_ref):
    h = pl.program_id(0)
    idx = idx_ref[...]
    far = rb_ref[h, NUM_BUCKETS // 2 - 1]
    acc = jnp.zeros(idx.shape, F32)
    for b in range(NUM_BUCKETS):
        acc = jnp.where(idx == b, (rb_ref[h, b] - far) * LOG2E, acc)
    o_ref[0] = acc


def _bias_tables(rel_bias):
    kk = np.arange(2 * KEY_TILE)[:, None]
    ql = np.arange(Q_BLOCK)[None, :]
    idx = jnp.asarray(_t5_bucket_np(kk - KEY_TILE - ql))
    t = pl.pallas_call(
        _bias_kernel,
        grid=(ATTN_HEADS,),
        in_specs=[pl.BlockSpec((2 * KEY_TILE, Q_BLOCK), lambda h: (0, 0)),
                  pl.BlockSpec(memory_space=pltpu.SMEM)],
        out_specs=pl.BlockSpec((1, 2 * KEY_TILE, Q_BLOCK), lambda h: (h, 0, 0)),
        out_shape=jax.ShapeDtypeStruct((ATTN_HEADS, 2 * KEY_TILE, Q_BLOCK), F32),
        compiler_params=_cparams(("arbitrary",)),
        name="bias",
    )(idx, rel_bias.T)
    return jnp.stack([t[:, 256:512], t[:, 128:384], t[:, 0:256]])


def _inproj_kernel(x_ref, mod_ref, nw_ref, w_ref, head_ref, wide_ref, hn_ref):
    j = pl.program_id(2)

    @pl.when(j == 0)
    def _():
        m = mod_ref[0]
        y = _rms(x_ref[0], nw_ref[...])
        hn = (y * (1.0 + m[1:2]) + m[0:1]).astype(BF16)
        hn_ref[...] = hn
        head_ref[0] = jnp.dot(hn, w_ref[...], preferred_element_type=F32)

    @pl.when(j > 0)
    def _():
        wide_ref[0] = jnp.dot(hn_ref[...], w_ref[...], preferred_element_type=F32).astype(BF16)


def _pack_w_in(w_in):
    sizes = [Q_RANK, KV_RANK, IDX_DIM, IDX_HEADS, D_INNER, CONV_DIM, SSD_HEADS, D_MODEL, D_MODEL]
    offs = np.cumsum([0] + sizes)
    q, kv, ki, wi, z, xbc, dt, ga, gb = [w_in[:, offs[i]:offs[i + 1]] for i in range(9)]

    def zc(n):
        return jnp.zeros((D_MODEL, n), w_in.dtype)

    head = jnp.concatenate([q, kv, ki, zc(LANES - IDX_DIM), wi, zc(LANES - IDX_HEADS),
                            dt, zc(LANES - SSD_HEADS), zc(2 * LANES)], axis=1)
    return jnp.concatenate([head, xbc, z, ga, gb], axis=1).astype(BF16)


def _inproj(x, mod3, pre_norm, w_packed):
    bsz, seq, _ = x.shape
    tm = min(seq, 1024)
    tn = HEAD_COLS
    return pl.pallas_call(
        _inproj_kernel,
        grid=(bsz, seq // tm, (HEAD_COLS + WIDE_COLS) // tn),
        in_specs=[pl.BlockSpec((1, tm, D_MODEL), lambda b, i, j: (b, i, 0)),
                  pl.BlockSpec((1, 6, D_MODEL), lambda b, i, j: (b, 0, 0)),
                  pl.BlockSpec((1, D_MODEL), lambda b, i, j: (0, 0)),
                  pl.BlockSpec((D_MODEL, tn), lambda b, i, j: (0, j))],
        out_specs=[pl.BlockSpec((1, tm, tn), lambda b, i, j: (b, i, 0)),
                   pl.BlockSpec((1, tm, tn), lambda b, i, j: (b, i, jnp.maximum(j - 1, 0)))],
        out_shape=[jax.ShapeDtypeStruct((bsz, seq, HEAD_COLS), F32),
                   jax.ShapeDtypeStruct((bsz, seq, WIDE_COLS), BF16)],
        scratch_shapes=[pltpu.VMEM((tm, D_MODEL), BF16)],
        compiler_params=_cparams(("parallel", "parallel", "arbitrary")),
        name="inproj",
    )(x, mod3, pre_norm.reshape(1, D_MODEL), w_packed)


def _prep_kernel(p_ref, qn_ref, kvn_ref, kin_ref, wuqt_ref, wqit_ref,
                 qt_ref, qit_ref, kv_ref, kvt_ref, ki_ref, wt_ref, *, tc):
    p = p_ref[0]
    qnt = _rms(p[:, :Q_RANK], qn_ref[...]).T.astype(BF16)
    qt = (jnp.dot(wuqt_ref[...], qnt, preferred_element_type=F32) * (KV_RANK ** -0.5 * LOG2E)).astype(BF16)
    qit = jnp.dot(wqit_ref[...], qnt, preferred_element_type=F32).astype(BF16)
    wt = (p[:, HEAD_W:HEAD_W + LANES] * (IDX_HEADS ** -0.5 * IDX_DIM ** -0.5)).T
    for blk in range(tc // Q_BLOCK):
        cols = slice(blk * Q_BLOCK, (blk + 1) * Q_BLOCK)
        for h in range(ATTN_HEADS):
            qt_ref[0, blk, :, h * LANES:(h + 1) * LANES] = qt[h * LANES:(h + 1) * LANES, cols]
            qit_ref[0, blk, :, h * LANES:(h + 1) * LANES] = qit[h * LANES:(h + 1) * LANES, cols]
        wt_ref[0, blk] = wt[0:IDX_HEADS, cols]
    kv = _rms(p[:, HEAD_KV:HEAD_KV + KV_RANK], kvn_ref[...])
    kv_ref[0] = kv.astype(BF16)
    kvt = kv.T
    for c in range(tc // KEY_TILE):
        kvt_ref[0, c] = kvt[:, c * KEY_TILE:(c + 1) * KEY_TILE].astype(BF16)
    ki_ref[0] = _rms(p[:, HEAD_KIDX:HEAD_KIDX + LANES], kin_ref[...], n=IDX_DIM).astype(BF16)


def _prep(proj, q_norm, kv_norm, kidx_norm, w_uq, w_qidx):
    bsz, seq, _ = proj.shape
    tc = min(seq, 512)
    wqi = w_qidx.reshape(Q_RANK, IDX_HEADS, IDX_DIM)
    wqit = jnp.concatenate([wqi, jnp.zeros_like(wqi)], axis=-1).reshape(Q_RANK, IDX_HEADS * LANES).T.astype(BF16)
    kin = jnp.concatenate([kidx_norm, jnp.zeros((LANES - IDX_DIM,), F32)]).reshape(1, LANES)
    nb = seq // Q_BLOCK
    slab = jax.ShapeDtypeStruct((bsz, nb, LANES, ATTN_HEADS * Q_BLOCK), BF16)
    slab_spec = pl.BlockSpec((1, tc // Q_BLOCK, LANES, ATTN_HEADS * Q_BLOCK), lambda b, i: (b, i, 0, 0))
    row_spec = pl.BlockSpec((1, tc, LANES), lambda b, i: (b, i, 0))
    return pl.pallas_call(
        functools.partial(_prep_kernel, tc=tc),
        grid=(bsz, seq // tc),
        in_specs=[pl.BlockSpec((1, tc, 1024), lambda b, i: (b, i, 0)),
                  _const_spec((1, Q_RANK)), _const_spec((1, KV_RANK)), _const_spec((1, LANES)),
                  _const_spec((ATTN_HEADS * KV_RANK, Q_RANK)), _const_spec((IDX_HEADS * LANES, Q_RANK))],
        out_specs=[slab_spec, slab_spec, row_spec,
                   pl.BlockSpec((1, tc // KEY_TILE, LANES, KEY_TILE), lambda b, i: (b, i, 0, 0)),
                   row_spec,
                   pl.BlockSpec((1, tc // Q_BLOCK, IDX_HEADS, Q_BLOCK), lambda b, i: (b, i, 0, 0))],
        out_shape=[slab, slab, jax.ShapeDtypeStruct((bsz, seq, LANES), BF16),
                   jax.ShapeDtypeStruct((bsz, seq // KEY_TILE, LANES, KEY_TILE), BF16),
                   jax.ShapeDtypeStruct((bsz, seq, LANES), BF16),
                   jax.ShapeDtypeStruct((bsz, nb, IDX_HEADS, Q_BLOCK), F32)],
        compiler_params=_cparams(("parallel", "parallel")),
        name="prep",
    )(proj, q_norm.reshape(1, Q_RANK), kv_norm.reshape(1, KV_RANK), kin, w_uq.T.astype(BF16), wqit)


def _dsa_kernel(qt_ref, qit_ref, wt_ref, ki_ref, kv_ref, kvt_ref, tb_ref, wuv_ref, o_ref,
                isct_ref, sbuf_ref, acc_ref, *, k_sel):
    i = pl.program_id(1)
    last = i // 2
    odd = i % 2
    n_tiles = last + 1
    qt = qt_ref[0, 0]
    qit = qit_ref[0, 0]
    wt = wt_ref[0, 0]
    row = lax.broadcasted_iota(jnp.int32, (KEY_TILE, Q_BLOCK), 0)
    col = lax.broadcasted_iota(jnp.int32, (KEY_TILE, Q_BLOCK), 1)
    key_limit = i * Q_BLOCK + jnp.where(col < CHUNK, CHUNK, 2 * CHUNK)

    def key_rows(j):
        return pl.ds(pl.multiple_of(j * KEY_TILE, KEY_TILE), KEY_TILE)

    n_pairs = ATTN_HEADS // 2

    def pair_cols(p):
        return slice(p * KEY_TILE, (p + 1) * KEY_TILE)

    def half(x, hh):
        return x[:, hh * Q_BLOCK:(hh + 1) * Q_BLOCK]

    def idx_body(j, carry):
        keys = ki_ref[0, key_rows(j), :]
        acc = jnp.zeros((KEY_TILE, Q_BLOCK), F32)
        for p in range(n_pairs):
            s = jnp.dot(keys, qit[:, pair_cols(p)], preferred_element_type=F32)
            for hh in range(2):
                h = 2 * p + hh
                acc = acc + wt[h:h + 1, :] * jnp.maximum(half(s, hh), 0.0)
        isct_ref[j] = jnp.where(j * KEY_TILE + row < key_limit, acc, -jnp.inf)
        return carry

    def tile_loop(n, body, carry):
        def pair(jj, c):
            return body(2 * jj + 1, body(2 * jj, c))
        carry = lax.fori_loop(0, n // 2, pair, carry)
        return lax.cond(n % 2 == 1, lambda c: body(n - 1, c), lambda c: c, carry)

    tile_loop(n_tiles, idx_body, 0)

    def rows_all(x, op):
        return jnp.broadcast_to(op(x, axis=0, keepdims=True), (SUBLANES, LANES))

    ACCS = 4

    def tile_rows(j):
        return isct_ref[j].reshape(KEY_TILE // (ACCS * SUBLANES), ACCS, SUBLANES, LANES)

    def count_where(pred):
        def body(j, c):
            return c + jnp.sum(jnp.where(pred(tile_rows(j)), 1.0, 0.0), axis=0)
        c = lax.fori_loop(0, n_tiles, body, jnp.zeros((ACCS, SUBLANES, LANES), F32))
        return rows_all(jnp.sum(c, axis=0), jnp.sum)

    def minmax_body(j, c):
        lo, hi = c
        x = tile_rows(j)
        return (jnp.minimum(lo, jnp.min(jnp.where(x == -jnp.inf, jnp.inf, x), axis=0)),
                jnp.maximum(hi, jnp.max(x, axis=0)))

    lo, hi = lax.fori_loop(0, n_tiles, minmax_body,
                           (jnp.full((ACCS, SUBLANES, LANES), jnp.inf, F32),
                            jnp.full((ACCS, SUBLANES, LANES), -jnp.inf, F32)))
    lo = rows_all(jnp.min(lo, axis=0), jnp.min)
    hi = rows_all(jnp.max(hi, axis=0), jnp.max)
    kf = float(k_sel)
    cnt_lo = count_where(lambda x: x >= lo)
    cnt_hi = count_where(lambda x: x >= hi)
    at_max = cnt_hi >= kf
    lo = jnp.where(at_max, hi, lo)
    cnt = jnp.where(at_max, cnt_hi, cnt_lo)

    def is_open(cnt, stalled):
        return jnp.logical_and(cnt > kf, stalled == 0.0)

    def any_lane(mask):
        return jnp.max(jnp.where(mask, 1.0, 0.0)) > 0.0

    def bis_step(c):
        lo, hi, cnt, cnt_hi, stalled = c
        mid = 0.5 * lo + 0.5 * hi
        cm = count_where(lambda x: x >= mid)
        active = is_open(cnt, stalled)
        noprog = jnp.logical_or(mid <= lo, mid >= hi)
        move = jnp.logical_and(active, jnp.logical_not(noprog))
        up = jnp.logical_and(move, cm >= kf)
        down = jnp.logical_and(move, cm < kf)
        return (jnp.where(up, mid, lo), jnp.where(down, mid, hi), jnp.where(up, cm, cnt),
                jnp.where(down, cm, cnt_hi), jnp.where(jnp.logical_and(active, noprog), 1.0, stalled))

    def bisect(state, max_iters):
        def cond(c):
            return jnp.logical_and(c[5] < max_iters, any_lane(is_open(c[2], c[4])))

        def body(c):
            state = c[:5]
            for _ in range(BISECT_UNROLL):
                state = bis_step(state)
            return state + (c[5] + BISECT_UNROLL,)

        return lax.while_loop(cond, body, state + (jnp.int32(0),))[:5]

    def max_below(t):
        def body(j, c):
            x = tile_rows(j)
            return jnp.maximum(c, jnp.max(jnp.where(x < t, x, -jnp.inf), axis=0))
        c = lax.fori_loop(0, n_tiles, body, jnp.full((ACCS, SUBLANES, LANES), -jnp.inf, F32))
        return rows_all(jnp.max(c, axis=0), jnp.max)

    def walk(c):
        top, left = c
        return jnp.where(left > 0.0, max_below(top), top), jnp.maximum(left - 1.0, 0.0)

    def coarse(state):
        for _ in range(BISECT_COARSE_ITERS):
            state = bis_step(state)
        lo, hi, cnt, cnt_hi, stalled = state
        left = jnp.where(is_open(cnt, stalled), kf - cnt_hi, 0.0)
        return state + walk(walk((hi, left)))

    no_stall = jnp.zeros((SUBLANES, LANES), F32)
    state = (lo, hi, cnt, cnt_hi, no_stall)
    lo, hi, cnt, cnt_hi, stalled, top, left = lax.cond(
        any_lane(is_open(cnt, no_stall)), coarse, lambda s: s + (hi, no_stall), state)
    opened = is_open(cnt, stalled)
    top, _ = lax.while_loop(lambda c: any_lane(c[1] > 0.0), walk, (top, left))
    lo = jnp.where(opened, top, lo)
    cnt = count_where(lambda x: x >= lo)
    thr8, _, cnt, _, _ = bisect((lo, hi, cnt, cnt_hi, stalled), BISECT_MAX_ITERS)

    tied = cnt > kf

    @pl.when(jnp.max(jnp.where(tied, 1.0, 0.0)) > 0.0)
    def _():
        need = kf - count_where(lambda x: x > thr8)

        def body(j, seen):
            x = isct_ref[j]
            eq = jnp.where(x == thr8[0:1], 1.0, 0.0)
            inc = _cumsum_rows(eq)
            rank = inc - eq + seen[0:1]
            drop = jnp.logical_and(jnp.logical_and(tied[0:1], eq > 0.0), rank >= need[0:1])
            isct_ref[j] = jnp.where(drop, -jnp.inf, x)
            return seen + inc[KEY_TILE - 1:KEY_TILE]

        lax.fori_loop(0, n_tiles, body, jnp.zeros((SUBLANES, LANES), F32))

    thr = thr8[0:1]

    def fold_rows(x, op):
        x = x.reshape(KEY_TILE // (ACCS * SUBLANES), ACCS, SUBLANES, LANES)
        return op(op(x, axis=0), axis=0)

    def p1_tile(j, m, near):
        keys = kv_ref[0, key_rows(j), :]
        neg = jnp.where(isct_ref[j] >= thr, 0.0, -jnp.inf)
        table = jnp.where(j == last, odd, 2)
        out = []
        for p in range(n_pairs):
            s = jnp.dot(keys, qt[:, pair_cols(p)], preferred_element_type=F32)
            for hh in range(2):
                h = 2 * p + hh
                sh = half(s, hh) + neg
                if near:
                    sh = sh + tb_ref[table, h]
                sbuf_ref[j, h] = sh
                out.append(jnp.maximum(m[h], fold_rows(sh, jnp.max)))
        return jnp.stack(out)

    n_far = jnp.maximum(last - 1 + odd, 0)
    m = jnp.full((ATTN_HEADS, SUBLANES, LANES), -jnp.inf, F32)
    m = tile_loop(n_far, lambda j, m: p1_tile(j, m, False), m)
    m = lax.fori_loop(n_far, n_tiles, lambda j, m: p1_tile(j, m, True), m)
    m = jnp.max(m, axis=1, keepdims=True)

    acc_ref[...] = jnp.zeros_like(acc_ref)

    def p2_body(j, l):
        values_t = kvt_ref[0, j]
        out = []
        for p in range(n_pairs):
            probs = []
            for hh in range(2):
                h = 2 * p + hh
                e = jnp.exp2(sbuf_ref[j, h] - m[h])
                out.append(l[h] + fold_rows(e, jnp.sum))
                probs.append(e.astype(BF16))
            acc_ref[:, pair_cols(p)] += jnp.dot(values_t, jnp.concatenate(probs, axis=1),
                                                preferred_element_type=F32)
        return jnp.stack(out)

    l = tile_loop(n_tiles, p2_body, jnp.zeros((ATTN_HEADS, SUBLANES, LANES), F32))
    l = jnp.sum(l, axis=1, keepdims=True)

    outs = [(acc_ref[:, h * Q_BLOCK:(h + 1) * Q_BLOCK] / l[h]).T.astype(BF16)
            for h in range(ATTN_HEADS)]
    for p in range(ATTN_HEADS // 2):
        pair = jnp.concatenate(outs[2 * p:2 * p + 2], axis=1)
        o_ref[0, :, p * LANES:(p + 1) * LANES] = jnp.dot(
            pair, wuv_ref[p], preferred_element_type=F32).astype(BF16)


def _pack_w_uv(w_uv):
    eye = jnp.eye(2, dtype=w_uv.dtype)
    w = w_uv.reshape(ATTN_HEADS // 2, 2, KV_RANK, 1, HEAD_DIM) * eye[None, :, None, :, None]
    return w.reshape(ATTN_HEADS // 2, 2 * KV_RANK, 2 * HEAD_DIM).astype(BF16)


def _dsa(qt, qit, wt, ki, kv, kvt, tables, wuv):
    bsz, seq, _ = kv.shape
    nkt = seq // KEY_TILE
    k_sel = min(TOPK_MAX, seq // 4)
    slab_spec = pl.BlockSpec((1, 1, LANES, ATTN_HEADS * Q_BLOCK), lambda b, i: (b, i, 0, 0))
    seq_spec = pl.BlockSpec((1, seq, LANES), lambda b, i: (b, 0, 0))
    return pl.pallas_call(
        functools.partial(_dsa_kernel, k_sel=k_sel),
        grid=(bsz, seq // Q_BLOCK),
        in_specs=[slab_spec, slab_spec,
                  pl.BlockSpec((1, 1, IDX_HEADS, Q_BLOCK), lambda b, i: (b, i, 0, 0)),
                  seq_spec, seq_spec,
                  pl.BlockSpec((1, nkt, LANES, KEY_TILE), lambda b, i: (b, 0, 0, 0)),
                  _const_spec((3, ATTN_HEADS, KEY_TILE, Q_BLOCK)),
                  _const_spec((ATTN_HEADS // 2, 2 * KV_RANK, 2 * HEAD_DIM))],
        out_specs=pl.BlockSpec((1, Q_BLOCK, ATTN_HEADS * HEAD_DIM), lambda b, i: (b, i, 0)),
        out_shape=jax.ShapeDtypeStruct((bsz, seq, ATTN_HEADS * HEAD_DIM), BF16),
        scratch_shapes=[pltpu.VMEM((nkt, KEY_TILE, Q_BLOCK), F32),
                        pltpu.VMEM((nkt, ATTN_HEADS, KEY_TILE, Q_BLOCK), F32),
                        pltpu.VMEM((KV_RANK, ATTN_HEADS * Q_BLOCK), F32)],
        compiler_params=_cparams(("parallel", "arbitrary")),
        name="dsa",
    )(qt, qit, wt, ki, kv, kvt, tables, wuv)


def _pack3(v):
    lane = lax.broadcasted_iota(jnp.int32, v.shape, 1)
    v = jnp.where(lane < SSD_HEADS, v, 0.0)
    hi = v.astype(BF16).astype(F32)
    r = v - hi
    mid = r.astype(BF16).astype(F32)
    lo = r - mid
    return (hi + pltpu.roll(mid, SSD_HEADS, axis=1) + pltpu.roll(lo, 2 * SSD_HEADS, axis=1)).astype(BF16)


def _cumsum_rows(x):
    n = x.shape[0]
    r = lax.broadcasted_iota(jnp.int32, x.shape, 0)
    s = 1
    while s < n:
        x = x + jnp.where(r >= s, pltpu.roll(x, s, axis=0), 0.0)
        s *= 2
    return x


def _shift_rows(x, s):
    r = pltpu.roll(x, s, axis=1)
    prev = jnp.concatenate([r[-1:], r[:-1]], axis=0)
    sub = lax.broadcasted_iota(jnp.int32, x.shape, 1)
    return jnp.where(sub >= s, r, prev)


def _ssd_kernel(z_ref, xbc_ref, dt_ref, cw_ref, cb_ref, dtb_ref, alog_ref, dsk_ref, nw_ref, e_ref,
                y_ref, tail_ref, u_ref, g_ref, state_ref):
    nq = SSD_Q

    @pl.when(pl.program_id(1) == 0)
    def _():
        tail_ref[...] = jnp.zeros_like(tail_ref)
        state_ref[...] = jnp.zeros_like(state_ref)

    assert CONV_W == 4
    for blk in range(CONV_DIM // CONV_BLOCK):
        cols = slice(blk * CONV_BLOCK, (blk + 1) * CONV_BLOCK)
        ext = jnp.concatenate([tail_ref[:, cols], xbc_ref[0, :, cols].astype(F32)], axis=0)
        ext = ext.reshape(1 + nq // SUBLANES, SUBLANES, CONV_BLOCK)
        s1 = _shift_rows(ext, 1)
        a = cw_ref[3:4, cols] * ext + cw_ref[2:3, cols] * s1 + cb_ref[:, cols]
        b = cw_ref[1:2, cols] * ext + cw_ref[0:1, cols] * s1
        conv = (a + _shift_rows(b, 2))[1:].reshape(nq, CONV_BLOCK)
        u_ref[:, cols] = _silu(conv)
        tail_ref[:, cols] = ext[nq // SUBLANES]

    t = dt_ref[0] + dtb_ref[...]
    dt = jnp.maximum(t, 0.0) + jnp.log1p(jnp.exp(-jnp.abs(t)))
    a2 = _cumsum_rows(dt * (-jnp.exp(alog_ref[...]))) * LOG2E
    a2_t = a2.T
    dt_p = _pack3(dt)
    dec_p = _pack3(dt * jnp.exp2(a2[nq - 1:nq, :] - a2))
    expa_p = _pack3(jnp.exp2(a2))

    r = lax.broadcasted_iota(jnp.int32, (nq, nq), 0)
    c = lax.broadcasted_iota(jnp.int32, (nq, nq), 1)
    causal = r >= c
    lane = lax.broadcasted_iota(jnp.int32, (nq, LANES), 1)
    heads_per_group = SSD_HEADS // SSD_GROUPS
    gw = heads_per_group * SSD_HEADDIM
    b_col = D_INNER
    c_col = D_INNER + SSD_GROUPS * D_STATE
    ssq = jnp.zeros((nq, LANES), F32)

    for g in range(SSD_GROUPS):
        gcols = slice(g * gw, (g + 1) * gw)
        eg = e_ref[:, gcols]
        dt_e = jnp.dot(dt_p, eg, preferred_element_type=F32)
        dec_e = jnp.dot(dec_p, eg, preferred_element_type=F32)
        expa_e = jnp.dot(expa_p, eg, preferred_element_type=F32)
        xs = u_ref[:, gcols]
        xdt_b = (xs * dt_e).astype(BF16)
        xdec_b = (xs * dec_e).astype(BF16)
        cg = u_ref[:, c_col + g * D_STATE:c_col + (g + 1) * D_STATE].astype(BF16)
        bgt = u_ref[:, b_col + g * D_STATE:b_col + (g + 1) * D_STATE].T.astype(BF16)
        cb = jnp.dot(cg, bgt, preferred_element_type=F32)
        prev = state_ref[g]
        y_off = jnp.dot(cg, prev.astype(BF16), preferred_element_type=F32) * expa_e
        pairs = []
        for pp in range(heads_per_group // 2):
            xp = xdt_b[:, pp * LANES:(pp + 1) * LANES]
            yh = []
            for hh in range(2):
                h = heads_per_group * g + 2 * pp + hh
                seg = a2[:, h:h + 1] - a2_t[h:h + 1, :]
                m = (cb * jnp.where(causal, jnp.exp2(seg), 0.0)).astype(BF16)
                yh.append(jnp.dot(m, xp, preferred_element_type=F32))
            pairs.append(jnp.where(lane < SSD_HEADDIM, yh[0], yh[1]))
        y = jnp.concatenate(pairs, axis=1) + y_off + dsk_ref[:, gcols] * xs
        gated = y * _silu(z_ref[0, :, gcols].astype(F32))
        g_ref[:, gcols] = gated
        sq = gated * gated
        ssq = ssq + sq[:, :LANES] + sq[:, LANES:]
        new = jnp.dot(bgt, xdec_b, preferred_element_type=F32)
        state_ref[g] = prev * expa_e[nq - 1:nq, :] + new
    scale = lax.rsqrt(jnp.sum(ssq, axis=-1, keepdims=True) * (1.0 / D_INNER) + EPS)
    y_ref[0] = (g_ref[...] * scale * nw_ref[...]).astype(BF16)


def _ssd(head, wide, conv_w, conv_b, dt_bias, a_log, d_skip, ssd_norm):
    bsz, seq, _ = head.shape
    nq = SSD_Q

    def pad_heads(v):
        return jnp.concatenate([v, jnp.zeros((LANES - SSD_HEADS,), F32)]).reshape(1, LANES)

    sel = np.concatenate([np.eye(SSD_HEADS)] * 3 + [np.zeros((LANES - 3 * SSD_HEADS, SSD_HEADS))], axis=0)
    e = jnp.asarray(np.kron(sel, np.ones((1, SSD_HEADDIM))), BF16)
    return pl.pallas_call(
        _ssd_kernel,
        grid=(bsz, seq // nq),
        in_specs=[pl.BlockSpec((1, nq, D_INNER), lambda b, i: (b, i, WIDE_Z // D_INNER)),
                  pl.BlockSpec((1, nq, CONV_DIM), lambda b, i: (b, i, WIDE_XBC // CONV_DIM)),
                  pl.BlockSpec((1, nq, LANES), lambda b, i: (b, i, HEAD_DT // LANES)),
                  _const_spec((CONV_W, CONV_DIM)), _const_spec((1, CONV_DIM)),
                  _const_spec((1, LANES)), _const_spec((1, LANES)),
                  _const_spec((1, D_INNER)), _const_spec((1, D_INNER)),
                  _const_spec((LANES, D_INNER))],
        out_specs=pl.BlockSpec((1, nq, D_INNER), lambda b, i: (b, i, 0)),
        out_shape=jax.ShapeDtypeStruct((bsz, seq, D_INNER), BF16),
        scratch_shapes=[pltpu.VMEM((SUBLANES, CONV_DIM), F32),
                        pltpu.VMEM((nq, CONV_DIM), F32),
                        pltpu.VMEM((nq, D_INNER), F32),
                        pltpu.VMEM((SSD_GROUPS, D_STATE, 4 * SSD_HEADDIM), F32)],
        compiler_params=_cparams(("parallel", "arbitrary")),
        name="ssd",
    )(wide, wide, head, conv_w, conv_b.reshape(1, CONV_DIM), pad_heads(dt_bias), pad_heads(a_log),
      jnp.repeat(d_skip, SSD_HEADDIM).reshape(1, D_INNER), ssd_norm.reshape(1, D_INNER), e)


def _mix_kernel(ao_ref, sy_ref, ga_ref, gb_ref, x_ref, mod_ref, nw_ref, woa_ref, wos_ref, wout_ref, o_ref):
    ya = jnp.dot(ao_ref[0], woa_ref[...], preferred_element_type=F32)
    yb = jnp.dot(sy_ref[0], wos_ref[...], preferred_element_type=F32)
    mix = jax.nn.sigmoid(ga_ref[0].astype(F32)) * ya + jax.nn.sigmoid(gb_ref[0].astype(F32)) * yb
    m2 = jnp.dot(mix.astype(BF16), wout_ref[...], preferred_element_type=F32)
    o_ref[0] = x_ref[0] + mod_ref[0][2:3] * _rms(m2, nw_ref[...])


def _mix(attn_o, ssd_y, wide, x, mod3, post_norm, w_o_attn, w_o_ssd, w_out):
    bsz, seq, _ = x.shape
    tm = min(seq, 512)

    def rows(width, col_block=0):
        return pl.BlockSpec((1, tm, width), lambda b, i: (b, i, col_block))

    return pl.pallas_call(
        _mix_kernel,
        grid=(bsz, seq // tm),
        in_specs=[rows(D_MODEL), rows(D_INNER), rows(D_MODEL, WIDE_GA // D_MODEL), rows(D_MODEL, WIDE_GB // D_MODEL),
                  rows(D_MODEL), pl.BlockSpec((1, 6, D_MODEL), lambda b, i: (b, 0, 0)),
                  _const_spec((1, D_MODEL)), _const_spec((D_MODEL, D_MODEL)),
                  _const_spec((D_INNER, D_MODEL)), _const_spec((D_MODEL, D_MODEL))],
        out_specs=rows(D_MODEL),
        out_shape=jax.ShapeDtypeStruct((bsz, seq, D_MODEL), F32),
        compiler_params=_cparams(("parallel", "parallel")),
        name="mix",
    )(attn_o, ssd_y, wide, wide, x, mod3, post_norm.reshape(1, D_MODEL),
      w_o_attn.astype(BF16), w_o_ssd.astype(BF16), w_out.astype(BF16))


def _ffn_kernel(x_ref, mod_ref, nw1_ref, nw2_ref, wg_ref, wu_ref, wo_ref, o_ref):
    x = x_ref[0]
    m = mod_ref[0]
    h2 = (_rms(x, nw1_ref[...]) * (1.0 + m[4:5]) + m[3:4]).astype(BF16)
    ug = jnp.dot(h2, wg_ref[...], preferred_element_type=F32)
    uu = jnp.dot(h2, wu_ref[...], preferred_element_type=F32)
    f = jnp.dot((_silu(ug) * uu).astype(BF16), wo_ref[...], preferred_element_type=F32)
    o_ref[0] = x + m[5:6] * _rms(f, nw2_ref[...])


def _ffn(x, mod3, pre_norm, post_norm, w_ffn_in, w_ffn_out):
    bsz, seq, _ = x.shape
    tm = min(seq, 512)
    rows = pl.BlockSpec((1, tm, D_MODEL), lambda b, i: (b, i, 0))
    w_in = w_ffn_in.astype(BF16)

    def half(k):
        return pl.BlockSpec((D_MODEL, D_FF), lambda b, i: (0, k), pipeline_mode=pl.Buffered(1))

    return pl.pallas_call(
        _ffn_kernel,
        grid=(bsz, seq // tm),
        in_specs=[rows, pl.BlockSpec((1, 6, D_MODEL), lambda b, i: (b, 0, 0)),
                  _const_spec((1, D_MODEL)), _const_spec((1, D_MODEL)),
                  half(0), half(1), _const_spec((D_FF, D_MODEL))],
        out_specs=rows,
        out_shape=jax.ShapeDtypeStruct((bsz, seq, D_MODEL), F32),
        compiler_params=_cparams(("parallel", "parallel")),
        name="ffn",
    )(x, mod3, pre_norm.reshape(1, D_MODEL), post_norm.reshape(1, D_MODEL), w_in, w_in, w_ffn_out.astype(BF16))


def kernel(x, c, positions, ada_w, ada_b, pre_norm_mix, post_norm_mix, pre_norm_ffn, post_norm_ffn, w_in, q_norm, kv_norm, w_uq, w_uv, rel_bias, w_qidx, kidx_norm, conv_w, conv_b, dt_bias, a_log, d_skip, ssd_norm, w_o_attn, w_o_ssd, w_out, w_ffn_in, w_ffn_out):
    del positions
    bsz, seq, _ = x.shape
    assert seq % (2 * KEY_TILE) == 0 and x.shape[-1] == D_MODEL
    mod3 = _mod(c, ada_w, ada_b).reshape(bsz, 6, D_MODEL)
    head, wide = _inproj(x, mod3, pre_norm_mix, _pack_w_in(w_in))
    qt, qit, kv, kvt, ki, wt = _prep(head, q_norm, kv_norm, kidx_norm, w_uq, w_qidx)
    attn_o = _dsa(qt, qit, wt, ki, kv, kvt, _bias_tables(rel_bias), _pack_w_uv(w_uv))
    ssd_y = _ssd(head, wide, conv_w, conv_b, dt_bias, a_log, d_skip, ssd_norm)
    x1 = _mix(attn_o, ssd_y, wide, x, mod3, post_norm_mix, w_o_attn, w_o_ssd, w_out)
    return _ffn(x1, mod3, pre_norm_ffn, post_norm_ffn, w_ffn_in, w_ffn_out)
```

```python
import functools
import math

import numpy as np
import jax
import jax.numpy as jnp
from jax import lax
from jax.experimental import pallas as pl
from jax.experimental.pallas import tpu as pltpu

F32 = jnp.float32
BF16 = jnp.bfloat16

D_MODEL = 1024
CHUNK = 64
Q_BLOCK = 128
EPS = 1e-6
ATTN_HEADS = 16
HEAD_DIM = 64
Q_RANK = 256
KV_RANK = 128
IDX_HEADS = 16
IDX_DIM = 64
TOPK_MAX = 256
NUM_BUCKETS = 32
MAX_DISTANCE = 128
D_INNER = 2 * D_MODEL
SSD_HEADDIM = 64
SSD_HEADS = D_INNER // SSD_HEADDIM
SSD_GROUPS = 8
D_STATE = 128
CONV_W = 4
CONV_DIM = D_INNER + 2 * SSD_GROUPS * D_STATE
D_FF = -(-8 * D_MODEL // (3 * 256)) * 256

LANES = 128
SUBLANES = 8
KEY_TILE = 256
SSD_Q = 128
CONV_BLOCK = 512
VMEM_LIMIT = 56 * 1024 * 1024
BISECT_MAX_ITERS = 320
BISECT_UNROLL = 4
BISECT_COARSE_ITERS = 14
LOG2E = math.log2(math.e)

HEAD_COLS = 1024
HEAD_KV, HEAD_KIDX, HEAD_W, HEAD_DT = 256, 384, 512, 640
WIDE_XBC, WIDE_Z, WIDE_GA, WIDE_GB = 0, 4096, 6144, 7168
WIDE_COLS = 8192
INPROJ_TN = 1024


def _cparams(sem):
    return pltpu.CompilerParams(dimension_semantics=sem, vmem_limit_bytes=VMEM_LIMIT)


def _const_spec(shape):
    nd = len(shape)
    return pl.BlockSpec(shape, lambda *_: (0,) * nd, pipeline_mode=pl.Buffered(1))


def _rms(x, w, n=None):
    n = x.shape[-1] if n is None else n
    return x * lax.rsqrt(jnp.sum(x * x, axis=-1, keepdims=True) * (1.0 / n) + EPS) * w


def _silu(x):
    h = 0.5 * x
    return h + h * jnp.tanh(h)


def _mod_kernel(c_ref, w_ref, b_ref, o_ref):
    c = c_ref[...]
    s = _silu(c).astype(BF16)
    o_ref[...] = jnp.dot(s, w_ref[...].astype(BF16), preferred_element_type=F32) + b_ref[...]


def _mod(c, ada_w, ada_b):
    bsz = c.shape[0]
    return pl.pallas_call(
        _mod_kernel,
        grid=(6,),
        in_specs=[pl.BlockSpec((bsz, D_MODEL), lambda j: (0, 0)),
                  pl.BlockSpec((D_MODEL, D_MODEL), lambda j: (0, j)),
                  pl.BlockSpec((1, D_MODEL), lambda j: (0, j))],
        out_specs=pl.BlockSpec((bsz, D_MODEL), lambda j: (0, j)),
        out_shape=jax.ShapeDtypeStruct((bsz, 6 * D_MODEL), F32),
        compiler_params=_cparams(("parallel",)),
        name="mod",
    )(c, ada_w, ada_b.reshape(1, 6 * D_MODEL))


def _t5_bucket_np(rel):
    half = NUM_BUCKETS // 2
    max_exact = half // 2
    side = np.where(rel > 0, half, 0)
    n = np.abs(rel)
    large = max_exact + (np.log(np.maximum(n, max_exact).astype(np.float64) / max_exact)
                         / math.log(MAX_DISTANCE / max_exact) * (half - max_exact)).astype(np.int64)
    large = np.minimum(large, half - 1)
    return (side + np.where(n < max_exact, n, large)).astype(np.int32)


def _bias_kernel(idx_ref, rb_ref, o_ref):
    h = pl.program_id(0)
    idx = idx_ref[...]
    far = rb_ref[h, NUM_BUCKETS // 2 - 1]
    acc = jnp.zeros(idx.shape, F32)
    for b in range(NUM_BUCKETS):
        acc = jnp.where(idx == b, (rb_ref[h, b] - far) * LOG2E, acc)
    o_ref[0] = acc


def _bias_tables(rel_bias):
    kk = np.arange(2 * KEY_TILE)[:, None]
    ql = np.arange(Q_BLOCK)[None, :]
    idx = jnp.asarray(_t5_bucket_np(kk - KEY_TILE - ql))
    t = pl.pallas_call(
        _bias_kernel,
        grid=(ATTN_HEADS,),
        in_specs=[pl.BlockSpec((2 * KEY_TILE, Q_BLOCK), lambda h: (0, 0)),
                  pl.BlockSpec(memory_space=pltpu.SMEM)],
        out_specs=pl.BlockSpec((1, 2 * KEY_TILE, Q_BLOCK), lambda h: (h, 0, 0)),
        out_shape=jax.ShapeDtypeStruct((ATTN_HEADS, 2 * KEY_TILE, Q_BLOCK), F32),
        compiler_params=_cparams(("arbitrary",)),
        name="bias",
    )(idx, rel_bias.T)
    return jnp.stack([t[:, 256:512], t[:, 128:384], t[:, 0:256]])


def _inproj_kernel(x_ref, mod_ref, nw_ref, w_ref, head_ref, wide_ref):
    m = mod_ref[0]
    hn = (_rms(x_ref[0], nw_ref[...]) * (1.0 + m[1:2]) + m[0:1]).astype(BF16)
    head_ref[0] = _dot_nt(hn, w_ref[0:HEAD_COLS, :])
    for c in range(WIDE_COLS // INPROJ_TN):
        rows = slice(HEAD_COLS + c * INPROJ_TN, HEAD_COLS + (c + 1) * INPROJ_TN)
        wide_ref[0, :, c * INPROJ_TN:(c + 1) * INPROJ_TN] = _dot_nt(hn, w_ref[rows, :]).astype(BF16)


def _pack_w_in(w_in):
    sizes = [Q_RANK, KV_RANK, IDX_DIM, IDX_HEADS, D_INNER, CONV_DIM, SSD_HEADS, D_MODEL, D_MODEL]
    offs = np.cumsum([0] + sizes)
    wt = w_in.T
    q, kv, ki, wi, z, xbc, dt, ga, gb = [wt[offs[i]:offs[i + 1]] for i in range(9)]

    def zr(n):
        return jnp.zeros((n, D_MODEL), w_in.dtype)

    return jnp.concatenate([q, kv, ki, zr(LANES - IDX_DIM), wi, zr(LANES - IDX_HEADS),
                            dt, zr(LANES - SSD_HEADS), zr(2 * LANES), xbc, z, ga, gb], axis=0).astype(BF16)


def _dot_nt(a, b):
    return lax.dot_general(a, b, (((1,), (1,)), ((), ())), preferred_element_type=F32)


def _inproj(x, mod3, pre_norm, w_packed):
    bsz, seq, _ = x.shape
    tm = min(seq, 512)
    return pl.pallas_call(
        _inproj_kernel,
        grid=(bsz, seq // tm),
        in_specs=[pl.BlockSpec((1, tm, D_MODEL), lambda b, i: (b, i, 0)),
                  pl.BlockSpec((1, 6, D_MODEL), lambda b, i: (b, 0, 0)),
                  _const_spec((1, D_MODEL)),
                  _const_spec((HEAD_COLS + WIDE_COLS, D_MODEL))],
        out_specs=[pl.BlockSpec((1, tm, HEAD_COLS), lambda b, i: (b, i, 0)),
                   pl.BlockSpec((1, tm, WIDE_COLS), lambda b, i: (b, i, 0))],
        out_shape=[jax.ShapeDtypeStruct((bsz, seq, HEAD_COLS), F32),
                   jax.ShapeDtypeStruct((bsz, seq, WIDE_COLS), BF16)],
        compiler_params=_cparams(("parallel", "parallel")),
        name="inproj",
    )(x, mod3, pre_norm.reshape(1, D_MODEL), w_packed)


def _prep_kernel(p_ref, qn_ref, kvn_ref, kin_ref, wuqt_ref, wqit_ref,
                 qt_ref, qit_ref, kv_ref, kvt_ref, ki_ref, wt_ref, *, tc):
    p = p_ref[0]
    qnt = _rms(p[:, :Q_RANK], qn_ref[...]).T.astype(BF16)
    qt = (jnp.dot(wuqt_ref[...], qnt, preferred_element_type=F32) * (KV_RANK ** -0.5 * LOG2E)).astype(BF16)
    qit = jnp.dot(wqit_ref[...], qnt, preferred_element_type=F32).astype(BF16)
    wt = (p[:, HEAD_W:HEAD_W + LANES] * (IDX_HEADS ** -0.5 * IDX_DIM ** -0.5)).T
    for blk in range(tc // Q_BLOCK):
        cols = slice(blk * Q_BLOCK, (blk + 1) * Q_BLOCK)
        for h in range(ATTN_HEADS):
            qt_ref[0, blk, :, h * LANES:(h + 1) * LANES] = qt[h * LANES:(h + 1) * LANES, cols]
            qit_ref[0, blk, :, h * LANES:(h + 1) * LANES] = qit[h * LANES:(h + 1) * LANES, cols]
        wt_ref[0, blk] = wt[0:IDX_HEADS, cols]
    kv = _rms(p[:, HEAD_KV:HEAD_KV + KV_RANK], kvn_ref[...])
    kv_ref[0] = kv.astype(BF16)
    kvt = kv.T
    for c in range(tc // KEY_TILE):
        kvt_ref[0, c] = kvt[:, c * KEY_TILE:(c + 1) * KEY_TILE].astype(BF16)
    ki_ref[0] = _rms(p[:, HEAD_KIDX:HEAD_KIDX + LANES], kin_ref[...], n=IDX_DIM).astype(BF16)


def _prep(proj, q_norm, kv_norm, kidx_norm, w_uq, w_qidx):
    bsz, seq, _ = proj.shape
    tc = min(seq, 512)
    wqi = w_qidx.reshape(Q_RANK, IDX_HEADS, IDX_DIM)
    wqit = jnp.concatenate([wqi, jnp.zeros_like(wqi)], axis=-1).reshape(Q_RANK, IDX_HEADS * LANES).T.astype(BF16)
    kin = jnp.concatenate([kidx_norm, jnp.zeros((LANES - IDX_DIM,), F32)]).reshape(1, LANES)
    nb = seq // Q_BLOCK
    slab = jax.ShapeDtypeStruct((bsz, nb, LANES, ATTN_HEADS * Q_BLOCK), BF16)
    slab_spec = pl.BlockSpec((1, tc // Q_BLOCK, LANES, ATTN_HEADS * Q_BLOCK), lambda b, i: (b, i, 0, 0))
    row_spec = pl.BlockSpec((1, tc, LANES), lambda b, i: (b, i, 0))
    return pl.pallas_call(
        functools.partial(_prep_kernel, tc=tc),
        grid=(bsz, seq // tc),
        in_specs=[pl.BlockSpec((1, tc, 1024), lambda b, i: (b, i, 0)),
                  _const_spec((1, Q_RANK)), _const_spec((1, KV_RANK)), _const_spec((1, LANES)),
                  _const_spec((ATTN_HEADS * KV_RANK, Q_RANK)), _const_spec((IDX_HEADS * LANES, Q_RANK))],
        out_specs=[slab_spec, slab_spec, row_spec,
                   pl.BlockSpec((1, tc // KEY_TILE, LANES, KEY_TILE), lambda b, i: (b, i, 0, 0)),
                   row_spec,
                   pl.BlockSpec((1, tc // Q_BLOCK, IDX_HEADS, Q_BLOCK), lambda b, i: (b, i, 0, 0))],
        out_shape=[slab, slab, jax.ShapeDtypeStruct((bsz, seq, LANES), BF16),
                   jax.ShapeDtypeStruct((bsz, seq // KEY_TILE, LANES, KEY_TILE), BF16),
                   jax.ShapeDtypeStruct((bsz, seq, LANES), BF16),
                   jax.ShapeDtypeStruct((bsz, nb, IDX_HEADS, Q_BLOCK), F32)],
        compiler_params=_cparams(("parallel", "parallel")),
        name="prep",
    )(proj, q_norm.reshape(1, Q_RANK), kv_norm.reshape(1, KV_RANK), kin, w_uq.T.astype(BF16), wqit)


def _dsa_kernel(qt_ref, qit_ref, wt_ref, ki_ref, kv_ref, kvt_ref, tb_ref, wuv_ref, o_ref,
                isct_ref, sbuf_ref, acc_ref, *, k_sel):
    i = pl.program_id(1)
    last = i // 2
    odd = i % 2
    n_tiles = last + 1
    qt = qt_ref[0, 0]
    qit = qit_ref[0, 0]
    wt = wt_ref[0, 0]
    row = lax.broadcasted_iota(jnp.int32, (KEY_TILE, Q_BLOCK), 0)
    col = lax.broadcasted_iota(jnp.int32, (KEY_TILE, Q_BLOCK), 1)
    key_limit = i * Q_BLOCK + jnp.where(col < CHUNK, CHUNK, 2 * CHUNK)

    def key_rows(j):
        return pl.ds(pl.multiple_of(j * KEY_TILE, KEY_TILE), KEY_TILE)

    n_pairs = ATTN_HEADS // 2

    def pair_cols(p):
        return slice(p * KEY_TILE, (p + 1) * KEY_TILE)

    def half(x, hh):
        return x[:, hh * Q_BLOCK:(hh + 1) * Q_BLOCK]

    def idx_body(j, carry):
        keys = ki_ref[0, key_rows(j), :]
        acc = jnp.zeros((KEY_TILE, Q_BLOCK), F32)
        for p in range(n_pairs):
            s = jnp.dot(keys, qit[:, pair_cols(p)], preferred_element_type=F32)
            for hh in range(2):
                h = 2 * p + hh
                acc = acc + wt[h:h + 1, :] * jnp.maximum(half(s, hh), 0.0)
        isct_ref[j] = jnp.where(j * KEY_TILE + row < key_limit, acc, -jnp.inf)
        return carry

    def tile_loop(n, body, carry):
        def pair(jj, c):
            return body(2 * jj + 1, body(2 * jj, c))
        carry = lax.fori_loop(0, n // 2, pair, carry)
        return lax.cond(n % 2 == 1, lambda c: body(n - 1, c), lambda c: c, carry)

    tile_loop(n_tiles, idx_body, 0)

    def rows_all(x, op):
        return jnp.broadcast_to(op(x, axis=0, keepdims=True), (SUBLANES, LANES))

    ACCS = 4

    def tile_rows(j):
        return isct_ref[j].reshape(KEY_TILE // (ACCS * SUBLANES), ACCS, SUBLANES, LANES)

    def count_where(pred):
        def body(j, c):
            return c + jnp.sum(jnp.where(pred(tile_rows(j)), 1.0, 0.0), axis=0)
        c = lax.fori_loop(0, n_tiles, body, jnp.zeros((ACCS, SUBLANES, LANES), F32))
        return rows_all(jnp.sum(c, axis=0), jnp.sum)

    def minmax_body(j, c):
        lo, hi = c
        x = tile_rows(j)
        return (jnp.minimum(lo, jnp.min(jnp.where(x == -jnp.inf, jnp.inf, x), axis=0)),
                jnp.maximum(hi, jnp.max(x, axis=0)))

    lo, hi = lax.fori_loop(0, n_tiles, minmax_body,
                           (jnp.full((ACCS, SUBLANES, LANES), jnp.inf, F32),
                            jnp.full((ACCS, SUBLANES, LANES), -jnp.inf, F32)))
    lo = rows_all(jnp.min(lo, axis=0), jnp.min)
    hi = rows_all(jnp.max(hi, axis=0), jnp.max)
    kf = float(k_sel)
    cnt_lo = count_where(lambda x: x >= lo)
    cnt_hi = count_where(lambda x: x >= hi)
    at_max = cnt_hi >= kf
    lo = jnp.where(at_max, hi, lo)
    cnt = jnp.where(at_max, cnt_hi, cnt_lo)

    def is_open(cnt, stalled):
        return jnp.logical_and(cnt > kf, stalled == 0.0)

    def any_lane(mask):
        return jnp.max(jnp.where(mask, 1.0, 0.0)) > 0.0

    def bis_step(c):
        lo, hi, cnt, cnt_hi, stalled = c
        mid = 0.5 * lo + 0.5 * hi
        cm = count_where(lambda x: x >= mid)
        active = is_open(cnt, stalled)
        noprog = jnp.logical_or(mid <= lo, mid >= hi)
        move = jnp.logical_and(active, jnp.logical_not(noprog))
        up = jnp.logical_and(move, cm >= kf)
        down = jnp.logical_and(move, cm < kf)
        return (jnp.where(up, mid, lo), jnp.where(down, mid, hi), jnp.where(up, cm, cnt),
                jnp.where(down, cm, cnt_hi), jnp.where(jnp.logical_and(active, noprog), 1.0, stalled))

    def bisect(state, max_iters):
        def cond(c):
            return jnp.logical_and(c[5] < max_iters, any_lane(is_open(c[2], c[4])))

        def body(c):
            state = c[:5]
            for _ in range(BISECT_UNROLL):
                state = bis_step(state)
            return state + (c[5] + BISECT_UNROLL,)

        return lax.while_loop(cond, body, state + (jnp.int32(0),))[:5]

    def max_below(t):
        def body(j, c):
            x = tile_rows(j)
            return jnp.maximum(c, jnp.max(jnp.where(x < t, x, -jnp.inf), axis=0))
        c = lax.fori_loop(0, n_tiles, body, jnp.full((ACCS, SUBLANES, LANES), -jnp.inf, F32))
        return rows_all(jnp.max(c, axis=0), jnp.max)

    def walk(c):
        top, left = c
        return jnp.where(left > 0.0, max_below(top), top), jnp.maximum(left - 1.0, 0.0)

    def coarse(state):
        for _ in range(BISECT_COARSE_ITERS):
            state = bis_step(state)
        lo, hi, cnt, cnt_hi, stalled = state
        left = jnp.where(is_open(cnt, stalled), kf - cnt_hi, 0.0)
        return state + walk(walk((hi, left)))

    no_stall = jnp.zeros((SUBLANES, LANES), F32)
    state = (lo, hi, cnt, cnt_hi, no_stall)
    lo, hi, cnt, cnt_hi, stalled, top, left = lax.cond(
        any_lane(is_open(cnt, no_stall)), coarse, lambda s: s + (hi, no_stall), state)
    opened = is_open(cnt, stalled)
    top, _ = lax.while_loop(lambda c: any_lane(c[1] > 0.0), walk, (top, left))
    lo = jnp.where(opened, top, lo)
    cnt = count_where(lambda x: x >= lo)
    thr8, _, cnt, _, _ = bisect((lo, hi, cnt, cnt_hi, stalled), BISECT_MAX_ITERS)

    tied = cnt > kf

    @pl.when(jnp.max(jnp.where(tied, 1.0, 0.0)) > 0.0)
    def _():
        need = kf - count_where(lambda x: x > thr8)

        def body(j, seen):
            x = isct_ref[j]
            eq = jnp.where(x == thr8[0:1], 1.0, 0.0)
            inc = _cumsum_rows(eq)
            rank = inc - eq + seen[0:1]
            drop = jnp.logical_and(jnp.logical_and(tied[0:1], eq > 0.0), rank >= need[0:1])
            isct_ref[j] = jnp.where(drop, -jnp.inf, x)
            return seen + inc[KEY_TILE - 1:KEY_TILE]

        lax.fori_loop(0, n_tiles, body, jnp.zeros((SUBLANES, LANES), F32))

    thr = thr8[0:1]

    def fold_rows(x, op):
        x = x.reshape(KEY_TILE // (ACCS * SUBLANES), ACCS, SUBLANES, LANES)
        return op(op(x, axis=0), axis=0)

    def p1_tile(j, m, near):
        keys = kv_ref[0, key_rows(j), :]
        neg = jnp.where(isct_ref[j] >= thr, 0.0, -jnp.inf)
        table = jnp.where(j == last, odd, 2)
        out = []
        for p in range(n_pairs):
            s = jnp.dot(keys, qt[:, pair_cols(p)], preferred_element_type=F32)
            for hh in range(2):
                h = 2 * p + hh
                sh = half(s, hh) + neg
                if near:
                    sh = sh + tb_ref[table, h]
                sbuf_ref[j, h] = sh
                out.append(jnp.maximum(m[h], fold_rows(sh, jnp.max)))
        return jnp.stack(out)

    n_far = jnp.maximum(last - 1 + odd, 0)
    m = jnp.full((ATTN_HEADS, SUBLANES, LANES), -jnp.inf, F32)
    m = tile_loop(n_far, lambda j, m: p1_tile(j, m, False), m)
    m = lax.fori_loop(n_far, n_tiles, lambda j, m: p1_tile(j, m, True), m)
    m = jnp.max(m, axis=1, keepdims=True)

    acc_ref[...] = jnp.zeros_like(acc_ref)

    def p2_body(j, l):
        values_t = kvt_ref[0, j]
        out = []
        for p in range(n_pairs):
            probs = []
            for hh in range(2):
                h = 2 * p + hh
                e = jnp.exp2(sbuf_ref[j, h] - m[h])
                out.append(l[h] + fold_rows(e, jnp.sum))
                probs.append(e.astype(BF16))
            acc_ref[:, pair_cols(p)] += jnp.dot(values_t, jnp.concatenate(probs, axis=1),
                                                preferred_element_type=F32)
        return jnp.stack(out)

    l = tile_loop(n_tiles, p2_body, jnp.zeros((ATTN_HEADS, SUBLANES, LANES), F32))
    l = jnp.sum(l, axis=1, keepdims=True)

    outs = [(acc_ref[:, h * Q_BLOCK:(h + 1) * Q_BLOCK] / l[h]).T.astype(BF16)
            for h in range(ATTN_HEADS)]
    for p in range(ATTN_HEADS // 2):
        pair = jnp.concatenate(outs[2 * p:2 * p + 2], axis=1)
        o_ref[0, :, p * LANES:(p + 1) * LANES] = jnp.dot(
            pair, wuv_ref[p], preferred_element_type=F32).astype(BF16)


def _pack_w_uv(w_uv):
    eye = jnp.eye(2, dtype=w_uv.dtype)
    w = w_uv.reshape(ATTN_HEADS // 2, 2, KV_RANK, 1, HEAD_DIM) * eye[None, :, None, :, None]
    return w.reshape(ATTN_HEADS // 2, 2 * KV_RANK, 2 * HEAD_DIM).astype(BF16)


def _dsa(qt, qit, wt, ki, kv, kvt, tables, wuv):
    bsz, seq, _ = kv.shape
    nkt = seq // KEY_TILE
    k_sel = min(TOPK_MAX, seq // 4)
    slab_spec = pl.BlockSpec((1, 1, LANES, ATTN_HEADS * Q_BLOCK), lambda b, i: (b, i, 0, 0))
    seq_spec = pl.BlockSpec((1, seq, LANES), lambda b, i: (b, 0, 0))
    return pl.pallas_call(
        functools.partial(_dsa_kernel, k_sel=k_sel),
        grid=(bsz, seq // Q_BLOCK),
        in_specs=[slab_spec, slab_spec,
                  pl.BlockSpec((1, 1, IDX_HEADS, Q_BLOCK), lambda b, i: (b, i, 0, 0)),
                  seq_spec, seq_spec,
                  pl.BlockSpec((1, nkt, LANES, KEY_TILE), lambda b, i: (b, 0, 0, 0)),
                  _const_spec((3, ATTN_HEADS, KEY_TILE, Q_BLOCK)),
                  _const_spec((ATTN_HEADS // 2, 2 * KV_RANK, 2 * HEAD_DIM))],
        out_specs=pl.BlockSpec((1, Q_BLOCK, ATTN_HEADS * HEAD_DIM), lambda b, i: (b, i, 0)),
        out_shape=jax.ShapeDtypeStruct((bsz, seq, ATTN_HEADS * HEAD_DIM), BF16),
        scratch_shapes=[pltpu.VMEM((nkt, KEY_TILE, Q_BLOCK), F32),
                        pltpu.VMEM((nkt, ATTN_HEADS, KEY_TILE, Q_BLOCK), F32),
                        pltpu.VMEM((KV_RANK, ATTN_HEADS * Q_BLOCK), F32)],
        compiler_params=_cparams(("parallel", "arbitrary")),
        name="dsa",
    )(qt, qit, wt, ki, kv, kvt, tables, wuv)


def _pack3(v):
    lane = lax.broadcasted_iota(jnp.int32, v.shape, 1)
    v = jnp.where(lane < SSD_HEADS, v, 0.0)
    hi = v.astype(BF16).astype(F32)
    r = v - hi
    mid = r.astype(BF16).astype(F32)
    lo = r - mid
    return (hi + pltpu.roll(mid, SSD_HEADS, axis=1) + pltpu.roll(lo, 2 * SSD_HEADS, axis=1)).astype(BF16)


def _cumsum_rows(x):
    n = x.shape[0]
    r = lax.broadcasted_iota(jnp.int32, x.shape, 0)
    s = 1
    while s < n:
        x = x + jnp.where(r >= s, pltpu.roll(x, s, axis=0), 0.0)
        s *= 2
    return x


def _shift_rows(x, s):
    r = pltpu.roll(x, s, axis=1)
    prev = jnp.concatenate([r[-1:], r[:-1]], axis=0)
    sub = lax.broadcasted_iota(jnp.int32, x.shape, 1)
    return jnp.where(sub >= s, r, prev)


def _ssd_kernel(z_ref, xbc_ref, dt_ref, cw_ref, cb_ref, dtb_ref, alog_ref, dsk_ref, nw_ref, e_ref,
                y_ref, tail_ref, u_ref, g_ref, state_ref):
    nq = SSD_Q

    @pl.when(pl.program_id(1) == 0)
    def _():
        tail_ref[...] = jnp.zeros_like(tail_ref)
        state_ref[...] = jnp.zeros_like(state_ref)

    assert CONV_W == 4
    for blk in range(CONV_DIM // CONV_BLOCK):
        cols = slice(blk * CONV_BLOCK, (blk + 1) * CONV_BLOCK)
        ext = jnp.concatenate([tail_ref[:, cols], xbc_ref[0, :, cols].astype(F32)], axis=0)
        ext = ext.reshape(1 + nq // SUBLANES, SUBLANES, CONV_BLOCK)
        s1 = _shift_rows(ext, 1)
        a = cw_ref[3:4, cols] * ext + cw_ref[2:3, cols] * s1 + cb_ref[:, cols]
        b = cw_ref[1:2, cols] * ext + cw_ref[0:1, cols] * s1
        conv = (a + _shift_rows(b, 2))[1:].reshape(nq, CONV_BLOCK)
        u_ref[:, cols] = _silu(conv)
        tail_ref[:, cols] = ext[nq // SUBLANES]

    t = dt_ref[0] + dtb_ref[...]
    dt = jnp.maximum(t, 0.0) + jnp.log1p(jnp.exp(-jnp.abs(t)))
    a2 = _cumsum_rows(dt * (-jnp.exp(alog_ref[...]))) * LOG2E
    a2_t = a2.T
    dt_p = _pack3(dt)
    dec_p = _pack3(dt * jnp.exp2(a2[nq - 1:nq, :] - a2))
    expa_p = _pack3(jnp.exp2(a2))

    r = lax.broadcasted_iota(jnp.int32, (nq, nq), 0)
    c = lax.broadcasted_iota(jnp.int32, (nq, nq), 1)
    causal = r >= c
    lane = lax.broadcasted_iota(jnp.int32, (nq, LANES), 1)
    heads_per_group = SSD_HEADS // SSD_GROUPS
    gw = heads_per_group * SSD_HEADDIM
    b_col = D_INNER
    c_col = D_INNER + SSD_GROUPS * D_STATE
    ssq = jnp.zeros((nq, LANES), F32)

    for g in range(SSD_GROUPS):
        gcols = slice(g * gw, (g + 1) * gw)
        eg = e_ref[:, gcols]
        dt_e = jnp.dot(dt_p, eg, preferred_element_type=F32)
        dec_e = jnp.dot(dec_p, eg, preferred_element_type=F32)
        expa_e = jnp.dot(expa_p, eg, preferred_element_type=F32)
        xs = u_ref[:, gcols]
        xdt_b = (xs * dt_e).astype(BF16)
        xdec_b = (xs * dec_e).astype(BF16)
        cg = u_ref[:, c_col + g * D_STATE:c_col + (g + 1) * D_STATE].astype(BF16)
        bgt = u_ref[:, b_col + g * D_STATE:b_col + (g + 1) * D_STATE].T.astype(BF16)
        cb = jnp.dot(cg, bgt, preferred_element_type=F32)
        prev = state_ref[g]
        y_off = jnp.dot(cg, prev.astype(BF16), preferred_element_type=F32) * expa_e
        pairs = []
        for pp in range(heads_per_group // 2):
            xp = xdt_b[:, pp * LANES:(pp + 1) * LANES]
            yh = []
            for hh in range(2):
                h = heads_per_group * g + 2 * pp + hh
                seg = a2[:, h:h + 1] - a2_t[h:h + 1, :]
                m = (cb * jnp.where(causal, jnp.exp2(seg), 0.0)).astype(BF16)
                yh.append(jnp.dot(m, xp, preferred_element_type=F32))
            pairs.append(jnp.where(lane < SSD_HEADDIM, yh[0], yh[1]))
        y = jnp.concatenate(pairs, axis=1) + y_off + dsk_ref[:, gcols] * xs
        gated = y * _silu(z_ref[0, :, gcols].astype(F32))
        g_ref[:, gcols] = gated
        sq = gated * gated
        ssq = ssq + sq[:, :LANES] + sq[:, LANES:]
        new = jnp.dot(bgt, xdec_b, preferred_element_type=F32)
        state_ref[g] = prev * expa_e[nq - 1:nq, :] + new
    scale = lax.rsqrt(jnp.sum(ssq, axis=-1, keepdims=True) * (1.0 / D_INNER) + EPS)
    y_ref[0] = (g_ref[...] * scale * nw_ref[...]).astype(BF16)


def _ssd(head, wide, conv_w, conv_b, dt_bias, a_log, d_skip, ssd_norm):
    bsz, seq, _ = head.shape
    nq = SSD_Q

    def pad_heads(v):
        return jnp.concatenate([v, jnp.zeros((LANES - SSD_HEADS,), F32)]).reshape(1, LANES)

    sel = np.concatenate([np.eye(SSD_HEADS)] * 3 + [np.zeros((LANES - 3 * SSD_HEADS, SSD_HEADS))], axis=0)
    e = jnp.asarray(np.kron(sel, np.ones((1, SSD_HEADDIM))), BF16)
    return pl.pallas_call(
        _ssd_kernel,
        grid=(bsz, seq // nq),
        in_specs=[pl.BlockSpec((1, nq, D_INNER), lambda b, i: (b, i, WIDE_Z // D_INNER)),
                  pl.BlockSpec((1, nq, CONV_DIM), lambda b, i: (b, i, WIDE_XBC // CONV_DIM)),
                  pl.BlockSpec((1, nq, LANES), lambda b, i: (b, i, HEAD_DT // LANES)),
                  _const_spec((CONV_W, CONV_DIM)), _const_spec((1, CONV_DIM)),
                  _const_spec((1, LANES)), _const_spec((1, LANES)),
                  _const_spec((1, D_INNER)), _const_spec((1, D_INNER)),
                  _const_spec((LANES, D_INNER))],
        out_specs=pl.BlockSpec((1, nq, D_INNER), lambda b, i: (b, i, 0)),
        out_shape=jax.ShapeDtypeStruct((bsz, seq, D_INNER), BF16),
        scratch_shapes=[pltpu.VMEM((SUBLANES, CONV_DIM), F32),
                        pltpu.VMEM((nq, CONV_DIM), F32),
                        pltpu.VMEM((nq, D_INNER), F32),
                        pltpu.VMEM((SSD_GROUPS, D_STATE, 4 * SSD_HEADDIM), F32)],
        compiler_params=_cparams(("parallel", "arbitrary")),
        name="ssd",
    )(wide, wide, head, conv_w, conv_b.reshape(1, CONV_DIM), pad_heads(dt_bias), pad_heads(a_log),
      jnp.repeat(d_skip, SSD_HEADDIM).reshape(1, D_INNER), ssd_norm.reshape(1, D_INNER), e)


def _mix_kernel(ao_ref, sy_ref, ga_ref, gb_ref, x_ref, mod_ref, nw_ref, woa_ref, wos_ref, wout_ref, o_ref):
    ya = jnp.dot(ao_ref[0], woa_ref[...], preferred_element_type=F32)
    yb = jnp.dot(sy_ref[0], wos_ref[...], preferred_element_type=F32)
    mix = jax.nn.sigmoid(ga_ref[0].astype(F32)) * ya + jax.nn.sigmoid(gb_ref[0].astype(F32)) * yb
    m2 = jnp.dot(mix.astype(BF16), wout_ref[...], preferred_element_type=F32)
    o_ref[0] = x_ref[0] + mod_ref[0][2:3] * _rms(m2, nw_ref[...])


def _mix(attn_o, ssd_y, wide, x, mod3, post_norm, w_o_attn, w_o_ssd, w_out):
    bsz, seq, _ = x.shape
    tm = min(seq, 512)

    def rows(width, col_block=0):
        return pl.BlockSpec((1, tm, width), lambda b, i: (b, i, col_block))

    return pl.pallas_call(
        _mix_kernel,
        grid=(bsz, seq // tm),
        in_specs=[rows(D_MODEL), rows(D_INNER), rows(D_MODEL, WIDE_GA // D_MODEL), rows(D_MODEL, WIDE_GB // D_MODEL),
                  rows(D_MODEL), pl.BlockSpec((1, 6, D_MODEL), lambda b, i: (b, 0, 0)),
                  _const_spec((1, D_MODEL)), _const_spec((D_MODEL, D_MODEL)),
                  _const_spec((D_INNER, D_MODEL)), _const_spec((D_MODEL, D_MODEL))],
        out_specs=rows(D_MODEL),
        out_shape=jax.ShapeDtypeStruct((bsz, seq, D_MODEL), F32),
        compiler_params=_cparams(("parallel", "parallel")),
        name="mix",
    )(attn_o, ssd_y, wide, wide, x, mod3, post_norm.reshape(1, D_MODEL),
      w_o_attn.astype(BF16), w_o_ssd.astype(BF16), w_out.astype(BF16))


def _ffn_kernel(x_ref, mod_ref, nw1_ref, nw2_ref, wg_ref, wu_ref, wo_ref, o_ref):
    x = x_ref[0]
    m = mod_ref[0]
    h2 = (_rms(x, nw1_ref[...]) * (1.0 + m[4:5]) + m[3:4]).astype(BF16)
    ug = jnp.dot(h2, wg_ref[...], preferred_element_type=F32)
    uu = jnp.dot(h2, wu_ref[...], preferred_element_type=F32)
    f = jnp.dot((_silu(ug) * uu).astype(BF16), wo_ref[...], preferred_element_type=F32)
    o_ref[0] = x + m[5:6] * _rms(f, nw2_ref[...])


def _ffn(x, mod3, pre_norm, post_norm, w_ffn_in, w_ffn_out):
    bsz, seq, _ = x.shape
    tm = min(seq, 512)
    rows = pl.BlockSpec((1, tm, D_MODEL), lambda b, i: (b, i, 0))
    w_in = w_ffn_in.astype(BF16)

    def half(k):
        return pl.BlockSpec((D_MODEL, D_FF), lambda b, i: (0, k), pipeline_mode=pl.Buffered(1))

    return pl.pallas_call(
        _ffn_kernel,
        grid=(bsz, seq // tm),
        in_specs=[rows, pl.BlockSpec((1, 6, D_MODEL), lambda b, i: (b, 0, 0)),
                  _const_spec((1, D_MODEL)), _const_spec((1, D_MODEL)),
                  half(0), half(1), _const_spec((D_FF, D_MODEL))],
        out_specs=rows,
        out_shape=jax.ShapeDtypeStruct((bsz, seq, D_MODEL), F32),
        compiler_params=_cparams(("parallel", "parallel")),
        name="ffn",
    )(x, mod3, pre_norm.reshape(1, D_MODEL), post_norm.reshape(1, D_MODEL), w_in, w_in, w_ffn_out.astype(BF16))


def kernel(x, c, positions, ada_w, ada_b, pre_norm_mix, post_norm_mix, pre_norm_ffn, post_norm_ffn, w_in, q_norm, kv_norm, w_uq, w_uv, rel_bias, w_qidx, kidx_norm, conv_w, conv_b, dt_bias, a_log, d_skip, ssd_norm, w_o_attn, w_o_ssd, w_out, w_ffn_in, w_ffn_out):
    del positions
    bsz, seq, _ = x.shape
    assert seq % (2 * KEY_TILE) == 0 and x.shape[-1] == D_MODEL
    mod3 = _mod(c, ada_w, ada_b).reshape(bsz, 6, D_MODEL)
    head, wide = _inproj(x, mod3, pre_norm_mix, _pack_w_in(w_in))
    qt, qit, kv, kvt, ki, wt = _prep(head, q_norm, kv_norm, kidx_norm, w_uq, w_qidx)
    attn_o = _dsa(qt, qit, wt, ki, kv, kvt, _bias_tables(rel_bias), _pack_w_uv(w_uv))
    ssd_y = _ssd(head, wide, conv_w, conv_b, dt_bias, a_log, d_skip, ssd_norm)
    x1 = _mix(attn_o, ssd_y, wide, x, mod3, post_norm_mix, w_o_attn, w_o_ssd, w_out)
    return _ffn(x1, mod3, pre_norm_ffn, post_norm_ffn, w_ffn_in, w_ffn_out)
```

```python
import functools
import math

import numpy as np
import jax
import jax.numpy as jnp
from jax import lax
from jax.experimental import pallas as pl
from jax.experimental.pallas import tpu as pltpu

F32 = jnp.float32
BF16 = jnp.bfloat16

D_MODEL = 1024
CHUNK = 64
Q_BLOCK = 128
EPS = 1e-6
ATTN_HEADS = 16
HEAD_DIM = 64
Q_RANK = 256
KV_RANK = 128
IDX_HEADS = 16
IDX_DIM = 64
TOPK_MAX = 256
NUM_BUCKETS = 32
MAX_DISTANCE = 128
D_INNER = 2 * D_MODEL
SSD_HEADDIM = 64
SSD_HEADS = D_INNER // SSD_HEADDIM
SSD_GROUPS = 8
D_STATE = 128
CONV_W = 4
CONV_DIM = D_INNER + 2 * SSD_GROUPS * D_STATE
D_FF = -(-8 * D_MODEL // (3 * 256)) * 256

LANES = 128
SUBLANES = 8
KEY_TILE = 256
SSD_Q = 128
CONV_BLOCK = 512
VMEM_LIMIT = 56 * 1024 * 1024
BISECT_MAX_ITERS = 320
BISECT_UNROLL = 4
BISECT_COARSE_ITERS = 14
LOG2E = math.log2(math.e)

HEAD_COLS = 1024
HEAD_KV, HEAD_KIDX, HEAD_W, HEAD_DT = 256, 384, 512, 640
WIDE_XBC, WIDE_Z, WIDE_GA, WIDE_GB = 0, 4096, 6144, 7168
WIDE_COLS = 8192
INPROJ_TN = 1024


def _cparams(sem):
    return pltpu.CompilerParams(dimension_semantics=sem, vmem_limit_bytes=VMEM_LIMIT)


def _const_spec(shape):
    nd = len(shape)
    return pl.BlockSpec(shape, lambda *_: (0,) * nd, pipeline_mode=pl.Buffered(1))


def _rms(x, w, n=None):
    n = x.shape[-1] if n is None else n
    return x * lax.rsqrt(jnp.sum(x * x, axis=-1, keepdims=True) * (1.0 / n) + EPS) * w


def _silu(x):
    h = 0.5 * x
    return h + h * jnp.tanh(h)


def _mod_kernel(c_ref, w_ref, b_ref, o_ref):
    c = c_ref[...]
    s = _silu(c).astype(BF16)
    o_ref[...] = jnp.dot(s, w_ref[...].astype(BF16), preferred_element_type=F32) + b_ref[...]


def _mod(c, ada_w, ada_b):
    bsz = c.shape[0]
    return pl.pallas_call(
        _mod_kernel,
        grid=(6,),
        in_specs=[pl.BlockSpec((bsz, D_MODEL), lambda j: (0, 0)),
                  pl.BlockSpec((D_MODEL, D_MODEL), lambda j: (0, j)),
                  pl.BlockSpec((1, D_MODEL), lambda j: (0, j))],
        out_specs=pl.BlockSpec((bsz, D_MODEL), lambda j: (0, j)),
        out_shape=jax.ShapeDtypeStruct((bsz, 6 * D_MODEL), F32),
        compiler_params=_cparams(("parallel",)),
        name="mod",
    )(c, ada_w, ada_b.reshape(1, 6 * D_MODEL))


def _t5_bucket_np(rel):
    half = NUM_BUCKETS // 2
    max_exact = half // 2
    side = np.where(rel > 0, half, 0)
    n = np.abs(rel)
    large = max_exact + (np.log(np.maximum(n, max_exact).astype(np.float64) / max_exact)
                         / math.log(MAX_DISTANCE / max_exact) * (half - max_exact)).astype(np.int64)
    large = np.minimum(large, half - 1)
    return (side + np.where(n < max_exact, n, large)).astype(np.int32)


def _bias_kernel(idx_ref, rb_ref, o_ref):
    h = pl.program_id(0)
    idx = idx_ref[...]
    far = rb_ref[h, NUM_BUCKETS // 2 - 1]
    acc = jnp.zeros(idx.shape, F32)
    for b in range(NUM_BUCKETS):
        acc = jnp.where(idx == b, (rb_ref[h, b] - far) * LOG2E, acc)
    o_ref[0] = acc


def _bias_tables(rel_bias):
    kk = np.arange(2 * KEY_TILE)[:, None]
    ql = np.arange(Q_BLOCK)[None, :]
    idx = jnp.asarray(_t5_bucket_np(kk - KEY_TILE - ql))
    t = pl.pallas_call(
        _bias_kernel,
        grid=(ATTN_HEADS,),
        in_specs=[pl.BlockSpec((2 * KEY_TILE, Q_BLOCK), lambda h: (0, 0)),
                  pl.BlockSpec(memory_space=pltpu.SMEM)],
        out_specs=pl.BlockSpec((1, 2 * KEY_TILE, Q_BLOCK), lambda h: (h, 0, 0)),
        out_shape=jax.ShapeDtypeStruct((ATTN_HEADS, 2 * KEY_TILE, Q_BLOCK), F32),
        compiler_params=_cparams(("arbitrary",)),
        name="bias",
    )(idx, rel_bias.T)
    return jnp.stack([t[:, 256:512], t[:, 128:384], t[:, 0:256]])


def _inproj_kernel(x_ref, mod_ref, nw_ref, w_ref, head_ref, wide_ref):
    m = mod_ref[0]
    hn = (_rms(x_ref[0], nw_ref[...]) * (1.0 + m[1:2]) + m[0:1]).astype(BF16)
    head_ref[0] = _dot_nt(hn, w_ref[0:HEAD_COLS, :])
    for c in range(WIDE_COLS // INPROJ_TN):
        rows = slice(HEAD_COLS + c * INPROJ_TN, HEAD_COLS + (c + 1) * INPROJ_TN)
        wide_ref[0, :, c * INPROJ_TN:(c + 1) * INPROJ_TN] = _dot_nt(hn, w_ref[rows, :]).astype(BF16)


def _pack_w_in(w_in):
    sizes = [Q_RANK, KV_RANK, IDX_DIM, IDX_HEADS, D_INNER, CONV_DIM, SSD_HEADS, D_MODEL, D_MODEL]
    offs = np.cumsum([0] + sizes)
    wt = w_in.T
    q, kv, ki, wi, z, xbc, dt, ga, gb = [wt[offs[i]:offs[i + 1]] for i in range(9)]

    def zr(n):
        return jnp.zeros((n, D_MODEL), w_in.dtype)

    return jnp.concatenate([q, kv, ki, zr(LANES - IDX_DIM), wi, zr(LANES - IDX_HEADS),
                            dt, zr(LANES - SSD_HEADS), zr(2 * LANES), xbc, z, ga, gb], axis=0).astype(BF16)


def _dot_nt(a, b):
    return lax.dot_general(a, b, (((1,), (1,)), ((), ())), preferred_element_type=F32)


def _inproj(x, mod3, pre_norm, w_packed):
    bsz, seq, _ = x.shape
    tm = min(seq, 512)
    return pl.pallas_call(
        _inproj_kernel,
        grid=(bsz, seq // tm),
        in_specs=[pl.BlockSpec((1, tm, D_MODEL), lambda b, i: (b, i, 0)),
                  pl.BlockSpec((1, 6, D_MODEL), lambda b, i: (b, 0, 0)),
                  _const_spec((1, D_MODEL)),
                  _const_spec((HEAD_COLS + WIDE_COLS, D_MODEL))],
        out_specs=[pl.BlockSpec((1, tm, HEAD_COLS), lambda b, i: (b, i, 0)),
                   pl.BlockSpec((1, tm, WIDE_COLS), lambda b, i: (b, i, 0))],
        out_shape=[jax.ShapeDtypeStruct((bsz, seq, HEAD_COLS), F32),
                   jax.ShapeDtypeStruct((bsz, seq, WIDE_COLS), BF16)],
        compiler_params=_cparams(("parallel", "parallel")),
        name="inproj",
    )(x, mod3, pre_norm.reshape(1, D_MODEL), w_packed)


def _prep_kernel(p_ref, qn_ref, kvn_ref, kin_ref, wuqt_ref, wqit_ref,
                 qt_ref, qit_ref, kv_ref, kvt_ref, ki_ref, wt_ref, *, tc):
    p = p_ref[0]
    qnt = _rms(p[:, :Q_RANK], qn_ref[...]).T.astype(BF16)
    qt = (jnp.dot(wuqt_ref[...], qnt, preferred_element_type=F32) * (KV_RANK ** -0.5 * LOG2E)).astype(BF16)
    qit = jnp.dot(wqit_ref[...], qnt, preferred_element_type=F32).astype(BF16)
    wt = (p[:, HEAD_W:HEAD_W + LANES] * (IDX_HEADS ** -0.5 * IDX_DIM ** -0.5)).T
    for blk in range(tc // Q_BLOCK):
        cols = slice(blk * Q_BLOCK, (blk + 1) * Q_BLOCK)
        for h in range(ATTN_HEADS):
            qt_ref[0, blk, :, h * LANES:(h + 1) * LANES] = qt[h * LANES:(h + 1) * LANES, cols]
            qit_ref[0, blk, :, h * LANES:(h + 1) * LANES] = qit[h * LANES:(h + 1) * LANES, cols]
        wt_ref[0, blk] = wt[0:IDX_HEADS, cols]
    kv = _rms(p[:, HEAD_KV:HEAD_KV + KV_RANK], kvn_ref[...])
    kv_ref[0] = kv.astype(BF16)
    kvt = kv.T
    for c in range(tc // KEY_TILE):
        kvt_ref[0, c] = kvt[:, c * KEY_TILE:(c + 1) * KEY_TILE].astype(BF16)
    ki_ref[0] = _rms(p[:, HEAD_KIDX:HEAD_KIDX + LANES], kin_ref[...], n=IDX_DIM).astype(BF16)


def _prep(proj, q_norm, kv_norm, kidx_norm, w_uq, w_qidx):
    bsz, seq, _ = proj.shape
    tc = min(seq, 512)
    wqi = w_qidx.reshape(Q_RANK, IDX_HEADS, IDX_DIM)
    wqit = jnp.concatenate([wqi, jnp.zeros_like(wqi)], axis=-1).reshape(Q_RANK, IDX_HEADS * LANES).T.astype(BF16)
    kin = jnp.concatenate([kidx_norm, jnp.zeros((LANES - IDX_DIM,), F32)]).reshape(1, LANES)
    nb = seq // Q_BLOCK
    slab = jax.ShapeDtypeStruct((bsz, nb, LANES, ATTN_HEADS * Q_BLOCK), BF16)
    slab_spec = pl.BlockSpec((1, tc // Q_BLOCK, LANES, ATTN_HEADS * Q_BLOCK), lambda b, i: (b, i, 0, 0))
    row_spec = pl.BlockSpec((1, tc, LANES), lambda b, i: (b, i, 0))
    return pl.pallas_call(
        functools.partial(_prep_kernel, tc=tc),
        grid=(bsz, seq // tc),
        in_specs=[pl.BlockSpec((1, tc, 1024), lambda b, i: (b, i, 0)),
                  _const_spec((1, Q_RANK)), _const_spec((1, KV_RANK)), _const_spec((1, LANES)),
                  _const_spec((ATTN_HEADS * KV_RANK, Q_RANK)), _const_spec((IDX_HEADS * LANES, Q_RANK))],
        out_specs=[slab_spec, slab_spec, row_spec,
                   pl.BlockSpec((1, tc // KEY_TILE, LANES, KEY_TILE), lambda b, i: (b, i, 0, 0)),
                   row_spec,
                   pl.BlockSpec((1, tc // Q_BLOCK, IDX_HEADS, Q_BLOCK), lambda b, i: (b, i, 0, 0))],
        out_shape=[slab, slab, jax.ShapeDtypeStruct((bsz, seq, LANES), BF16),
                   jax.ShapeDtypeStruct((bsz, seq // KEY_TILE, LANES, KEY_TILE), BF16),
                   jax.ShapeDtypeStruct((bsz, seq, LANES), BF16),
                   jax.ShapeDtypeStruct((bsz, nb, IDX_HEADS, Q_BLOCK), F32)],
        compiler_params=_cparams(("parallel", "parallel")),
        name="prep",
    )(proj, q_norm.reshape(1, Q_RANK), kv_norm.reshape(1, KV_RANK), kin, w_uq.T.astype(BF16), wqit)


def _dsa_kernel(qt_ref, qit0_ref, wt0_ref, qitn_ref, wtn_ref, ki_ref, kv_ref, kvt_ref, tb_ref, wuv_ref, o_ref,
                iscbuf_ref, sbuf_ref, acc_ref, *, k_sel, n_blocks):
    i = pl.program_id(1)
    last = i // 2
    odd = i % 2
    n_tiles = last + 1
    isct_ref = iscbuf_ref.at[i % 2]
    qt = qt_ref[0, 0]
    row = lax.broadcasted_iota(jnp.int32, (KEY_TILE, Q_BLOCK), 0)
    col = lax.broadcasted_iota(jnp.int32, (KEY_TILE, Q_BLOCK), 1)

    def key_rows(j):
        return pl.ds(pl.multiple_of(j * KEY_TILE, KEY_TILE), KEY_TILE)

    n_pairs = ATTN_HEADS // 2

    def pair_cols(p):
        return slice(p * KEY_TILE, (p + 1) * KEY_TILE)

    def half(x, hh):
        return x[:, hh * Q_BLOCK:(hh + 1) * Q_BLOCK]

    def idx_tile(j, blk, qit_ref, wt_ref, dst_ref):
        keys = ki_ref[0, key_rows(j), :]
        wt = wt_ref[0, 0]
        acc = jnp.zeros((KEY_TILE, Q_BLOCK), F32)
        for p in range(n_pairs):
            s = jnp.dot(keys, qit_ref[0, 0, :, pair_cols(p)], preferred_element_type=F32)
            for hh in range(2):
                h = 2 * p + hh
                acc = acc + wt[h:h + 1, :] * jnp.maximum(half(s, hh), 0.0)
        key_limit = blk * Q_BLOCK + jnp.where(col < CHUNK, CHUNK, 2 * CHUNK)
        dst_ref[j] = jnp.where(j * KEY_TILE + row < key_limit, acc, -jnp.inf)

    def tile_loop(n, body, carry):
        def pair(jj, c):
            return body(2 * jj + 1, body(2 * jj, c))
        carry = lax.fori_loop(0, n // 2, pair, carry)
        return lax.cond(n % 2 == 1, lambda c: body(n - 1, c), lambda c: c, carry)

    @pl.when(i == 0)
    def _():
        idx_tile(0, 0, qit0_ref, wt0_ref, isct_ref)

    def rows_all(x, op):
        return jnp.broadcast_to(op(x, axis=0, keepdims=True), (SUBLANES, LANES))

    ACCS = 4

    def tile_rows(j):
        return isct_ref[j].reshape(KEY_TILE // (ACCS * SUBLANES), ACCS, SUBLANES, LANES)

    def count_where(pred):
        def body(j, c):
            x = tile_rows(j)
            for r in range(x.shape[0]):
                c = jnp.where(pred(x[r]), c + 1.0, c)
            return c
        c = lax.fori_loop(0, n_tiles, body, jnp.zeros((ACCS, SUBLANES, LANES), F32))
        return rows_all(jnp.sum(c, axis=0), jnp.sum)

    def minmax_body(j, c):
        lo, hi = c
        x = tile_rows(j)
        return (jnp.minimum(lo, jnp.min(jnp.where(x == -jnp.inf, jnp.inf, x), axis=0)),
                jnp.maximum(hi, jnp.max(x, axis=0)))

    lo, hi = lax.fori_loop(0, n_tiles, minmax_body,
                           (jnp.full((ACCS, SUBLANES, LANES), jnp.inf, F32),
                            jnp.full((ACCS, SUBLANES, LANES), -jnp.inf, F32)))
    lo = rows_all(jnp.min(lo, axis=0), jnp.min)
    hi = rows_all(jnp.max(hi, axis=0), jnp.max)
    kf = float(k_sel)
    cnt_lo = count_where(lambda x: x >= lo[0:1])
    cnt_hi = count_where(lambda x: x >= hi[0:1])
    at_max = cnt_hi >= kf
    lo = jnp.where(at_max, hi, lo)
    cnt = jnp.where(at_max, cnt_hi, cnt_lo)

    def is_open(cnt, stalled):
        return jnp.logical_and(cnt > kf, stalled == 0.0)

    def any_lane(mask):
        return jnp.max(jnp.where(mask, 1.0, 0.0)) > 0.0

    def bis_step(c):
        lo, hi, cnt, cnt_hi, stalled = c
        mid = 0.5 * lo + 0.5 * hi
        cm = count_where(lambda x: x >= mid[0:1])
        active = is_open(cnt, stalled)
        noprog = jnp.logical_or(mid <= lo, mid >= hi)
        move = jnp.logical_and(active, jnp.logical_not(noprog))
        up = jnp.logical_and(move, cm >= kf)
        down = jnp.logical_and(move, cm < kf)
        return (jnp.where(up, mid, lo), jnp.where(down, mid, hi), jnp.where(up, cm, cnt),
                jnp.where(down, cm, cnt_hi), jnp.where(jnp.logical_and(active, noprog), 1.0, stalled))

    def bisect(state, max_iters):
        def cond(c):
            return jnp.logical_and(c[5] < max_iters, any_lane(is_open(c[2], c[4])))

        def body(c):
            state = c[:5]
            for _ in range(BISECT_UNROLL):
                state = bis_step(state)
            return state + (c[5] + BISECT_UNROLL,)

        return lax.while_loop(cond, body, state + (jnp.int32(0),))[:5]

    def max_below(t):
        def body(j, c):
            x = tile_rows(j)
            return jnp.maximum(c, jnp.max(jnp.where(x < t, x, -jnp.inf), axis=0))
        c = lax.fori_loop(0, n_tiles, body, jnp.full((ACCS, SUBLANES, LANES), -jnp.inf, F32))
        return rows_all(jnp.max(c, axis=0), jnp.max)

    def walk(c):
        top, left = c
        return jnp.where(left > 0.0, max_below(top), top), jnp.maximum(left - 1.0, 0.0)

    def coarse(state):
        for _ in range(BISECT_COARSE_ITERS):
            state = bis_step(state)
        lo, hi, cnt, cnt_hi, stalled = state
        left = jnp.where(is_open(cnt, stalled), kf - cnt_hi, 0.0)
        return state + walk(walk((hi, left)))

    no_stall = jnp.zeros((SUBLANES, LANES), F32)
    state = (lo, hi, cnt, cnt_hi, no_stall)
    lo, hi, cnt, cnt_hi, stalled, top, left = lax.cond(
        any_lane(is_open(cnt, no_stall)), coarse, lambda s: s + (hi, no_stall), state)
    opened = is_open(cnt, stalled)
    top, _ = lax.while_loop(lambda c: any_lane(c[1] > 0.0), walk, (top, left))
    lo = jnp.where(opened, top, lo)
    cnt = count_where(lambda x: x >= lo[0:1])
    thr8, _, cnt, _, _ = bisect((lo, hi, cnt, cnt_hi, stalled), BISECT_MAX_ITERS)

    tied = cnt > kf

    @pl.when(jnp.max(jnp.where(tied, 1.0, 0.0)) > 0.0)
    def _():
        need = kf - count_where(lambda x: x > thr8[0:1])

        def body(j, seen):
            x = isct_ref[j]
            eq = jnp.where(x == thr8[0:1], 1.0, 0.0)
            inc = _cumsum_rows(eq)
            rank = inc - eq + seen[0:1]
            drop = jnp.logical_and(jnp.logical_and(tied[0:1], eq > 0.0), rank >= need[0:1])
            isct_ref[j] = jnp.where(drop, -jnp.inf, x)
            return seen + inc[KEY_TILE - 1:KEY_TILE]

        lax.fori_loop(0, n_tiles, body, jnp.zeros((SUBLANES, LANES), F32))

    thr = thr8[0:1]

    def fold_rows(x, op):
        x = x.reshape(KEY_TILE // (ACCS * SUBLANES), ACCS, SUBLANES, LANES)
        return op(op(x, axis=0), axis=0)

    def p1_tile(j, m, near):
        keys = kv_ref[0, key_rows(j), :]
        neg = jnp.where(isct_ref[j] >= thr, 0.0, -jnp.inf)
        table = jnp.where(j == last, odd, 2)
        out = []
        for p in range(n_pairs):
            s = jnp.dot(keys, qt[:, pair_cols(p)], preferred_element_type=F32)
            for hh in range(2):
                h = 2 * p + hh
                sh = half(s, hh) + neg
                if near:
                    sh = sh + tb_ref[table, h]
                sbuf_ref[j, h] = sh
                out.append(jnp.maximum(m[h], fold_rows(sh, jnp.max)))
        return jnp.stack(out)

    n_far = jnp.maximum(last - 1 + odd, 0)
    m = jnp.full((ATTN_HEADS, SUBLANES, LANES), -jnp.inf, F32)
    m = tile_loop(n_far, lambda j, m: p1_tile(j, m, False), m)
    m = lax.fori_loop(n_far, n_tiles, lambda j, m: p1_tile(j, m, True), m)
    m = jnp.max(m, axis=1, keepdims=True)

    acc_ref[...] = jnp.zeros_like(acc_ref)

    def p2_body(j, l):
        values_t = kvt_ref[0, j]
        out = []
        for p in range(n_pairs):
            probs = []
            for hh in range(2):
                h = 2 * p + hh
                e = jnp.exp2(sbuf_ref[j, h] - m[h])
                out.append(l[h] + fold_rows(e, jnp.sum))
                probs.append(e.astype(BF16))
            acc_ref[p] += jnp.dot(values_t, jnp.concatenate(probs, axis=1),
                                  preferred_element_type=F32)
        return jnp.stack(out)

    next_ref = iscbuf_ref.at[(i + 1) % 2]

    def p2_and_next_scores(j, l):
        l = p2_body(j, l)
        idx_tile(j, i + 1, qitn_ref, wtn_ref, next_ref)
        return l

    l = tile_loop(n_tiles, p2_and_next_scores, jnp.zeros((ATTN_HEADS, SUBLANES, LANES), F32))

    @pl.when(jnp.logical_and(odd == 1, i + 1 < n_blocks))
    def _():
        idx_tile(n_tiles, i + 1, qitn_ref, wtn_ref, next_ref)

    l = jnp.sum(l, axis=1, keepdims=True)

    outs = [(acc_ref[h // 2, :, (h % 2) * Q_BLOCK:(h % 2 + 1) * Q_BLOCK] / l[h]).T.astype(BF16)
            for h in range(ATTN_HEADS)]
    for p in range(ATTN_HEADS // 2):
        pair = jnp.concatenate(outs[2 * p:2 * p + 2], axis=1)
        o_ref[0, :, p * LANES:(p + 1) * LANES] = jnp.dot(
            pair, wuv_ref[p], preferred_element_type=F32).astype(BF16)


def _pack_w_uv(w_uv):
    eye = jnp.eye(2, dtype=w_uv.dtype)
    w = w_uv.reshape(ATTN_HEADS // 2, 2, KV_RANK, 1, HEAD_DIM) * eye[None, :, None, :, None]
    return w.reshape(ATTN_HEADS // 2, 2 * KV_RANK, 2 * HEAD_DIM).astype(BF16)


def _dsa(qt, qit, wt, ki, kv, kvt, tables, wuv):
    bsz, seq, _ = kv.shape
    nkt = seq // KEY_TILE
    k_sel = min(TOPK_MAX, seq // 4)
    nb = seq // Q_BLOCK
    slab = (1, 1, LANES, ATTN_HEADS * Q_BLOCK)
    wslab = (1, 1, IDX_HEADS, Q_BLOCK)

    def first(b, i):
        return (b, 0, 0, 0)

    def following(b, i):
        return (b, jnp.minimum(i + 1, nb - 1), 0, 0)

    seq_spec = pl.BlockSpec((1, seq, LANES), lambda b, i: (b, 0, 0))
    return pl.pallas_call(
        functools.partial(_dsa_kernel, k_sel=k_sel, n_blocks=nb),
        grid=(bsz, nb),
        in_specs=[pl.BlockSpec(slab, lambda b, i: (b, i, 0, 0)),
                  pl.BlockSpec(slab, first), pl.BlockSpec(wslab, first),
                  pl.BlockSpec(slab, following), pl.BlockSpec(wslab, following),
                  seq_spec, seq_spec,
                  pl.BlockSpec((1, nkt, LANES, KEY_TILE), lambda b, i: (b, 0, 0, 0)),
                  _const_spec((3, ATTN_HEADS, KEY_TILE, Q_BLOCK)),
                  _const_spec((ATTN_HEADS // 2, 2 * KV_RANK, 2 * HEAD_DIM))],
        out_specs=pl.BlockSpec((1, Q_BLOCK, ATTN_HEADS * HEAD_DIM), lambda b, i: (b, i, 0)),
        out_shape=jax.ShapeDtypeStruct((bsz, seq, ATTN_HEADS * HEAD_DIM), BF16),
        scratch_shapes=[pltpu.VMEM((2, nkt, KEY_TILE, Q_BLOCK), F32),
                        pltpu.VMEM((nkt, ATTN_HEADS, KEY_TILE, Q_BLOCK), F32),
                        pltpu.VMEM((ATTN_HEADS // 2, KV_RANK, 2 * Q_BLOCK), F32)],
        compiler_params=_cparams(("parallel", "arbitrary")),
        name="dsa",
    )(qt, qit, wt, qit, wt, ki, kv, kvt, tables, wuv)


def _pack3(v):
    lane = lax.broadcasted_iota(jnp.int32, v.shape, 1)
    v = jnp.where(lane < SSD_HEADS, v, 0.0)
    hi = v.astype(BF16).astype(F32)
    r = v - hi
    mid = r.astype(BF16).astype(F32)
    lo = r - mid
    return (hi + pltpu.roll(mid, SSD_HEADS, axis=1) + pltpu.roll(lo, 2 * SSD_HEADS, axis=1)).astype(BF16)


def _cumsum_rows(x):
    n = x.shape[0]
    r = lax.broadcasted_iota(jnp.int32, x.shape, 0)
    s = 1
    while s < n:
        x = x + jnp.where(r >= s, pltpu.roll(x, s, axis=0), 0.0)
        s *= 2
    return x


def _shift_rows(x, s):
    r = pltpu.roll(x, s, axis=1)
    prev = jnp.concatenate([r[-1:], r[:-1]], axis=0)
    sub = lax.broadcasted_iota(jnp.int32, x.shape, 1)
    return jnp.where(sub >= s, r, prev)


def _ssd_kernel(z_ref, xbc_ref, dt_ref, cw_ref, cb_ref, dtb_ref, alog_ref, dsk_ref, nw_ref, e_ref,
                y_ref, tail_ref, u_ref, g_ref, state_ref):
    nq = SSD_Q

    @pl.when(pl.program_id(1) == 0)
    def _():
        tail_ref[...] = jnp.zeros_like(tail_ref)
        state_ref[...] = jnp.zeros_like(state_ref)

    assert CONV_W == 4
    for blk in range(CONV_DIM // CONV_BLOCK):
        cols = slice(blk * CONV_BLOCK, (blk + 1) * CONV_BLOCK)
        ext = jnp.concatenate([tail_ref[:, cols], xbc_ref[0, :, cols].astype(F32)], axis=0)
        ext = ext.reshape(1 + nq // SUBLANES, SUBLANES, CONV_BLOCK)
        s1 = _shift_rows(ext, 1)
        a = cw_ref[3:4, cols] * ext + cw_ref[2:3, cols] * s1 + cb_ref[:, cols]
        b = cw_ref[1:2, cols] * ext + cw_ref[0:1, cols] * s1
        conv = (a + _shift_rows(b, 2))[1:].reshape(nq, CONV_BLOCK)
        u_ref[:, cols] = _silu(conv)
        tail_ref[:, cols] = ext[nq // SUBLANES]

    t = dt_ref[0] + dtb_ref[...]
    dt = jnp.maximum(t, 0.0) + jnp.log1p(jnp.exp(-jnp.abs(t)))
    a2 = _cumsum_rows(dt * (-jnp.exp(alog_ref[...]))) * LOG2E
    a2_t = a2.T
    dt_p = _pack3(dt)
    dec_p = _pack3(dt * jnp.exp2(a2[nq - 1:nq, :] - a2))
    expa_p = _pack3(jnp.exp2(a2))

    r = lax.broadcasted_iota(jnp.int32, (nq, nq), 0)
    c = lax.broadcasted_iota(jnp.int32, (nq, nq), 1)
    causal = r >= c
    lane = lax.broadcasted_iota(jnp.int32, (nq, LANES), 1)
    heads_per_group = SSD_HEADS // SSD_GROUPS
    gw = heads_per_group * SSD_HEADDIM
    b_col = D_INNER
    c_col = D_INNER + SSD_GROUPS * D_STATE
    ssq = jnp.zeros((nq, LANES), F32)

    for g in range(SSD_GROUPS):
        gcols = slice(g * gw, (g + 1) * gw)
        eg = e_ref[:, gcols]
        dt_e = jnp.dot(dt_p, eg, preferred_element_type=F32)
        dec_e = jnp.dot(dec_p, eg, preferred_element_type=F32)
        expa_e = jnp.dot(expa_p, eg, preferred_element_type=F32)
        xs = u_ref[:, gcols]
        xdt_b = (xs * dt_e).astype(BF16)
        xdec_b = (xs * dec_e).astype(BF16)
        cg = u_ref[:, c_col + g * D_STATE:c_col + (g + 1) * D_STATE].astype(BF16)
        bgt = u_ref[:, b_col + g * D_STATE:b_col + (g + 1) * D_STATE].T.astype(BF16)
        cb = jnp.dot(cg, bgt, preferred_element_type=F32)
        prev = state_ref[g]
        y_off = jnp.dot(cg, prev.astype(BF16), preferred_element_type=F32) * expa_e
        pairs = []
        for pp in range(heads_per_group // 2):
            xp = xdt_b[:, pp * LANES:(pp + 1) * LANES]
            yh = []
            for hh in range(2):
                h = heads_per_group * g + 2 * pp + hh
                seg = a2[:, h:h + 1] - a2_t[h:h + 1, :]
                m = (cb * jnp.where(causal, jnp.exp2(seg), 0.0)).astype(BF16)
                yh.append(jnp.dot(m, xp, preferred_element_type=F32))
            pairs.append(jnp.where(lane < SSD_HEADDIM, yh[0], yh[1]))
        y = jnp.concatenate(pairs, axis=1) + y_off + dsk_ref[:, gcols] * xs
        gated = y * _silu(z_ref[0, :, gcols].astype(F32))
        g_ref[:, gcols] = gated
        sq = gated * gated
        ssq = ssq + sq[:, :LANES] + sq[:, LANES:]
        new = jnp.dot(bgt, xdec_b, preferred_element_type=F32)
        state_ref[g] = prev * expa_e[nq - 1:nq, :] + new
    scale = lax.rsqrt(jnp.sum(ssq, axis=-1, keepdims=True) * (1.0 / D_INNER) + EPS)
    y_ref[0] = (g_ref[...] * scale * nw_ref[...]).astype(BF16)


def _ssd(head, wide, conv_w, conv_b, dt_bias, a_log, d_skip, ssd_norm):
    bsz, seq, _ = head.shape
    nq = SSD_Q

    def pad_heads(v):
        return jnp.concatenate([v, jnp.zeros((LANES - SSD_HEADS,), F32)]).reshape(1, LANES)

    sel = np.concatenate([np.eye(SSD_HEADS)] * 3 + [np.zeros((LANES - 3 * SSD_HEADS, SSD_HEADS))], axis=0)
    e = jnp.asarray(np.kron(sel, np.ones((1, SSD_HEADDIM))), BF16)
    return pl.pallas_call(
        _ssd_kernel,
        grid=(bsz, seq // nq),
        in_specs=[pl.BlockSpec((1, nq, D_INNER), lambda b, i: (b, i, WIDE_Z // D_INNER)),
                  pl.BlockSpec((1, nq, CONV_DIM), lambda b, i: (b, i, WIDE_XBC // CONV_DIM)),
                  pl.BlockSpec((1, nq, LANES), lambda b, i: (b, i, HEAD_DT // LANES)),
                  _const_spec((CONV_W, CONV_DIM)), _const_spec((1, CONV_DIM)),
                  _const_spec((1, LANES)), _const_spec((1, LANES)),
                  _const_spec((1, D_INNER)), _const_spec((1, D_INNER)),
                  _const_spec((LANES, D_INNER))],
        out_specs=pl.BlockSpec((1, nq, D_INNER), lambda b, i: (b, i, 0)),
        out_shape=jax.ShapeDtypeStruct((bsz, seq, D_INNER), BF16),
        scratch_shapes=[pltpu.VMEM((SUBLANES, CONV_DIM), F32),
                        pltpu.VMEM((nq, CONV_DIM), F32),
                        pltpu.VMEM((nq, D_INNER), F32),
                        pltpu.VMEM((SSD_GROUPS, D_STATE, 4 * SSD_HEADDIM), F32)],
        compiler_params=_cparams(("parallel", "arbitrary")),
        name="ssd",
    )(wide, wide, head, conv_w, conv_b.reshape(1, CONV_DIM), pad_heads(dt_bias), pad_heads(a_log),
      jnp.repeat(d_skip, SSD_HEADDIM).reshape(1, D_INNER), ssd_norm.reshape(1, D_INNER), e)


def _mix_kernel(ao_ref, sy_ref, ga_ref, gb_ref, x_ref, mod_ref, nw_ref, woa_ref, wos_ref, wout_ref, o_ref):
    ya = jnp.dot(ao_ref[0], woa_ref[...], preferred_element_type=F32)
    yb = jnp.dot(sy_ref[0], wos_ref[...], preferred_element_type=F32)
    mix = jax.nn.sigmoid(ga_ref[0].astype(F32)) * ya + jax.nn.sigmoid(gb_ref[0].astype(F32)) * yb
    m2 = jnp.dot(mix.astype(BF16), wout_ref[...], preferred_element_type=F32)
    o_ref[0] = x_ref[0] + mod_ref[0][2:3] * _rms(m2, nw_ref[...])


def _mix(attn_o, ssd_y, wide, x, mod3, post_norm, w_o_attn, w_o_ssd, w_out):
    bsz, seq, _ = x.shape
    tm = min(seq, 512)

    def rows(width, col_block=0):
        return pl.BlockSpec((1, tm, width), lambda b, i: (b, i, col_block))

    return pl.pallas_call(
        _mix_kernel,
        grid=(bsz, seq // tm),
        in_specs=[rows(D_MODEL), rows(D_INNER), rows(D_MODEL, WIDE_GA // D_MODEL), rows(D_MODEL, WIDE_GB // D_MODEL),
                  rows(D_MODEL), pl.BlockSpec((1, 6, D_MODEL), lambda b, i: (b, 0, 0)),
                  _const_spec((1, D_MODEL)), _const_spec((D_MODEL, D_MODEL)),
                  _const_spec((D_INNER, D_MODEL)), _const_spec((D_MODEL, D_MODEL))],
        out_specs=rows(D_MODEL),
        out_shape=jax.ShapeDtypeStruct((bsz, seq, D_MODEL), F32),
        compiler_params=_cparams(("parallel", "parallel")),
        name="mix",
    )(attn_o, ssd_y, wide, wide, x, mod3, post_norm.reshape(1, D_MODEL),
      w_o_attn.astype(BF16), w_o_ssd.astype(BF16), w_out.astype(BF16))


def _ffn_kernel(x_ref, mod_ref, nw1_ref, nw2_ref, wg_ref, wu_ref, wo_ref, o_ref):
    x = x_ref[0]
    m = mod_ref[0]
    h2 = (_rms(x, nw1_ref[...]) * (1.0 + m[4:5]) + m[3:4]).astype(BF16)
    ug = jnp.dot(h2, wg_ref[...], preferred_element_type=F32)
    uu = jnp.dot(h2, wu_ref[...], preferred_element_type=F32)
    f = jnp.dot((_silu(ug) * uu).astype(BF16), wo_ref[...], preferred_element_type=F32)
    o_ref[0] = x + m[5:6] * _rms(f, nw2_ref[...])


def _ffn(x, mod3, pre_norm, post_norm, w_ffn_in, w_ffn_out):
    bsz, seq, _ = x.shape
    tm = min(seq, 512)
    rows = pl.BlockSpec((1, tm, D_MODEL), lambda b, i: (b, i, 0))
    w_in = w_ffn_in.astype(BF16)

    def half(k):
        return pl.BlockSpec((D_MODEL, D_FF), lambda b, i: (0, k), pipeline_mode=pl.Buffered(1))

    return pl.pallas_call(
        _ffn_kernel,
        grid=(bsz, seq // tm),
        in_specs=[rows, pl.BlockSpec((1, 6, D_MODEL), lambda b, i: (b, 0, 0)),
                  _const_spec((1, D_MODEL)), _const_spec((1, D_MODEL)),
                  half(0), half(1), _const_spec((D_FF, D_MODEL))],
        out_specs=rows,
        out_shape=jax.ShapeDtypeStruct((bsz, seq, D_MODEL), F32),
        compiler_params=_cparams(("parallel", "parallel")),
        name="ffn",
    )(x, mod3, pre_norm.reshape(1, D_MODEL), post_norm.reshape(1, D_MODEL), w_in, w_in, w_ffn_out.astype(BF16))


def kernel(x, c, positions, ada_w, ada_b, pre_norm_mix, post_norm_mix, pre_norm_ffn, post_norm_ffn, w_in, q_norm, kv_norm, w_uq, w_uv, rel_bias, w_qidx, kidx_norm, conv_w, conv_b, dt_bias, a_log, d_skip, ssd_norm, w_o_attn, w_o_ssd, w_out, w_ffn_in, w_ffn_out):
    del positions
    bsz, seq, _ = x.shape
    assert seq % (2 * KEY_TILE) == 0 and x.shape[-1] == D_MODEL
    mod3 = _mod(c, ada_w, ada_b).reshape(bsz, 6, D_MODEL)
    head, wide = _inproj(x, mod3, pre_norm_mix, _pack_w_in(w_in))
    qt, qit, kv, kvt, ki, wt = _prep(head, q_norm, kv_norm, kidx_norm, w_uq, w_qidx)
    attn_o = _dsa(qt, qit, wt, ki, kv, kvt, _bias_tables(rel_bias), _pack_w_uv(w_uv))
    ssd_y = _ssd(head, wide, conv_w, conv_b, dt_bias, a_log, d_skip, ssd_norm)
    x1 = _mix(attn_o, ssd_y, wide, x, mod3, post_norm_mix, w_o_attn, w_o_ssd, w_out)
    return _ffn(x1, mod3, pre_norm_ffn, post_norm_ffn, w_ffn_in, w_ffn_out)
```

```python
import functools
import math

import numpy as np
import jax
import jax.numpy as jnp
from jax import lax
from jax.experimental import pallas as pl
from jax.experimental.pallas import tpu as pltpu

F32 = jnp.float32
BF16 = jnp.bfloat16

D_MODEL = 1024
CHUNK = 64
Q_BLOCK = 128
EPS = 1e-6
ATTN_HEADS = 16
HEAD_DIM = 64
Q_RANK = 256
KV_RANK = 128
IDX_HEADS = 16
IDX_DIM = 64
TOPK_MAX = 256
NUM_BUCKETS = 32
MAX_DISTANCE = 128
D_INNER = 2 * D_MODEL
SSD_HEADDIM = 64
SSD_HEADS = D_INNER // SSD_HEADDIM
SSD_GROUPS = 8
D_STATE = 128
CONV_W = 4
CONV_DIM = D_INNER + 2 * SSD_GROUPS * D_STATE
D_FF = -(-8 * D_MODEL // (3 * 256)) * 256

LANES = 128
SUBLANES = 8
KEY_TILE = 256
SSD_Q = 128
CONV_BLOCK = 512
VMEM_LIMIT = 56 * 1024 * 1024
BISECT_MAX_ITERS = 320
BISECT_UNROLL = 4
BISECT_COARSE_ITERS = 14
LOG2E = math.log2(math.e)
BOUND_SLACK = 1.0 + 2.0 ** -6
UNDERFLOW_GUARD = 2.0 ** -80

HEAD_COLS = 1024
HEAD_KV, HEAD_KIDX, HEAD_W, HEAD_DT = 256, 384, 512, 640
WIDE_XBC, WIDE_Z, WIDE_GA, WIDE_GB = 0, 4096, 6144, 7168
WIDE_COLS = 8192
INPROJ_TN = 1024


def _cparams(sem):
    return pltpu.CompilerParams(dimension_semantics=sem, vmem_limit_bytes=VMEM_LIMIT)


def _const_spec(shape):
    nd = len(shape)
    return pl.BlockSpec(shape, lambda *_: (0,) * nd, pipeline_mode=pl.Buffered(1))


def _rms(x, w, n=None):
    n = x.shape[-1] if n is None else n
    return x * lax.rsqrt(jnp.sum(x * x, axis=-1, keepdims=True) * (1.0 / n) + EPS) * w


def _silu(x):
    h = 0.5 * x
    return h + h * jnp.tanh(h)


def _mod_kernel(c_ref, w_ref, b_ref, o_ref):
    c = c_ref[...]
    s = _silu(c).astype(BF16)
    o_ref[...] = jnp.dot(s, w_ref[...].astype(BF16), preferred_element_type=F32) + b_ref[...]


def _mod(c, ada_w, ada_b):
    bsz = c.shape[0]
    return pl.pallas_call(
        _mod_kernel,
        grid=(6,),
        in_specs=[pl.BlockSpec((bsz, D_MODEL), lambda j: (0, 0)),
                  pl.BlockSpec((D_MODEL, D_MODEL), lambda j: (0, j)),
                  pl.BlockSpec((1, D_MODEL), lambda j: (0, j))],
        out_specs=pl.BlockSpec((bsz, D_MODEL), lambda j: (0, j)),
        out_shape=jax.ShapeDtypeStruct((bsz, 6 * D_MODEL), F32),
        compiler_params=_cparams(("parallel",)),
        name="mod",
    )(c, ada_w, ada_b.reshape(1, 6 * D_MODEL))


def _t5_bucket_np(rel):
    half = NUM_BUCKETS // 2
    max_exact = half // 2
    side = np.where(rel > 0, half, 0)
    n = np.abs(rel)
    large = max_exact + (np.log(np.maximum(n, max_exact).astype(np.float64) / max_exact)
                         / math.log(MAX_DISTANCE / max_exact) * (half - max_exact)).astype(np.int64)
    large = np.minimum(large, half - 1)
    return (side + np.where(n < max_exact, n, large)).astype(np.int32)


def _bias_kernel(idx_ref, rb_ref, o_ref):
    h = pl.program_id(0)
    idx = idx_ref[...]
    far = rb_ref[h, NUM_BUCKETS // 2 - 1]
    acc = jnp.zeros(idx.shape, F32)
    for b in range(NUM_BUCKETS):
        acc = jnp.where(idx == b, (rb_ref[h, b] - far) * LOG2E, acc)
    o_ref[0] = acc


def _bias_tables(rel_bias):
    kk = np.arange(2 * KEY_TILE)[:, None]
    ql = np.arange(Q_BLOCK)[None, :]
    idx = jnp.asarray(_t5_bucket_np(kk - KEY_TILE - ql))
    t = pl.pallas_call(
        _bias_kernel,
        grid=(ATTN_HEADS,),
        in_specs=[pl.BlockSpec((2 * KEY_TILE, Q_BLOCK), lambda h: (0, 0)),
                  pl.BlockSpec(memory_space=pltpu.SMEM)],
        out_specs=pl.BlockSpec((1, 2 * KEY_TILE, Q_BLOCK), lambda h: (h, 0, 0)),
        out_shape=jax.ShapeDtypeStruct((ATTN_HEADS, 2 * KEY_TILE, Q_BLOCK), F32),
        compiler_params=_cparams(("arbitrary",)),
        name="bias",
    )(idx, rel_bias.T)
    return jnp.stack([t[:, 256:512], t[:, 128:384], t[:, 0:256]])


def _inproj_kernel(x_ref, mod_ref, nw_ref, w_ref, head_ref, wide_ref):
    m = mod_ref[0]
    hn = (_rms(x_ref[0], nw_ref[...]) * (1.0 + m[1:2]) + m[0:1]).astype(BF16)
    head_ref[0] = _dot_nt(hn, w_ref[0:HEAD_COLS, :])
    for c in range(WIDE_COLS // INPROJ_TN):
        rows = slice(HEAD_COLS + c * INPROJ_TN, HEAD_COLS + (c + 1) * INPROJ_TN)
        wide_ref[0, :, c * INPROJ_TN:(c + 1) * INPROJ_TN] = _dot_nt(hn, w_ref[rows, :]).astype(BF16)


def _pack_w_in(w_in):
    sizes = [Q_RANK, KV_RANK, IDX_DIM, IDX_HEADS, D_INNER, CONV_DIM, SSD_HEADS, D_MODEL, D_MODEL]
    offs = np.cumsum([0] + sizes)
    wt = w_in.T
    q, kv, ki, wi, z, xbc, dt, ga, gb = [wt[offs[i]:offs[i + 1]] for i in range(9)]

    def zr(n):
        return jnp.zeros((n, D_MODEL), w_in.dtype)

    return jnp.concatenate([q, kv, ki, zr(LANES - IDX_DIM), wi, zr(LANES - IDX_HEADS),
                            dt, zr(LANES - SSD_HEADS), zr(2 * LANES), xbc, z, ga, gb], axis=0).astype(BF16)


def _dot_nt(a, b):
    return lax.dot_general(a, b, (((1,), (1,)), ((), ())), preferred_element_type=F32)


def _inproj(x, mod3, pre_norm, w_packed):
    bsz, seq, _ = x.shape
    tm = min(seq, 512)
    return pl.pallas_call(
        _inproj_kernel,
        grid=(bsz, seq // tm),
        in_specs=[pl.BlockSpec((1, tm, D_MODEL), lambda b, i: (b, i, 0)),
                  pl.BlockSpec((1, 6, D_MODEL), lambda b, i: (b, 0, 0)),
                  _const_spec((1, D_MODEL)),
                  _const_spec((HEAD_COLS + WIDE_COLS, D_MODEL))],
        out_specs=[pl.BlockSpec((1, tm, HEAD_COLS), lambda b, i: (b, i, 0)),
                   pl.BlockSpec((1, tm, WIDE_COLS), lambda b, i: (b, i, 0))],
        out_shape=[jax.ShapeDtypeStruct((bsz, seq, HEAD_COLS), F32),
                   jax.ShapeDtypeStruct((bsz, seq, WIDE_COLS), BF16)],
        compiler_params=_cparams(("parallel", "parallel")),
        name="inproj",
    )(x, mod3, pre_norm.reshape(1, D_MODEL), w_packed)


def _prep_kernel(p_ref, qn_ref, kvn_ref, kin_ref, wuqt_ref, wqit_ref,
                 qt_ref, qit_ref, kv_ref, kvt_ref, ki_ref, wt_ref, qnorm_ref, *, tc):
    p = p_ref[0]
    qnt = _rms(p[:, :Q_RANK], qn_ref[...]).T.astype(BF16)
    qt = (jnp.dot(wuqt_ref[...], qnt, preferred_element_type=F32) * (KV_RANK ** -0.5 * LOG2E)).astype(BF16)
    qf = qt.astype(F32).reshape(ATTN_HEADS, KV_RANK, tc)
    qnorm = jnp.sqrt(jnp.sum(qf * qf, axis=1))
    qit = jnp.dot(wqit_ref[...], qnt, preferred_element_type=F32).astype(BF16)
    wt = (p[:, HEAD_W:HEAD_W + LANES] * (IDX_HEADS ** -0.5 * IDX_DIM ** -0.5)).T
    for blk in range(tc // Q_BLOCK):
        cols = slice(blk * Q_BLOCK, (blk + 1) * Q_BLOCK)
        for h in range(ATTN_HEADS):
            qt_ref[0, blk, :, h * LANES:(h + 1) * LANES] = qt[h * LANES:(h + 1) * LANES, cols]
            qit_ref[0, blk, :, h * LANES:(h + 1) * LANES] = qit[h * LANES:(h + 1) * LANES, cols]
        wt_ref[0, blk] = wt[0:IDX_HEADS, cols]
        qnorm_ref[0, blk] = qnorm[:, cols]
    kv = _rms(p[:, HEAD_KV:HEAD_KV + KV_RANK], kvn_ref[...])
    kv_ref[0] = kv.astype(BF16)
    kvt = kv.T
    for c in range(tc // KEY_TILE):
        kvt_ref[0, c] = kvt[:, c * KEY_TILE:(c + 1) * KEY_TILE].astype(BF16)
    ki_ref[0] = _rms(p[:, HEAD_KIDX:HEAD_KIDX + LANES], kin_ref[...], n=IDX_DIM).astype(BF16)


def _prep(proj, q_norm, kv_norm, kidx_norm, w_uq, w_qidx):
    bsz, seq, _ = proj.shape
    tc = min(seq, 512)
    wqi = w_qidx.reshape(Q_RANK, IDX_HEADS, IDX_DIM)
    wqit = jnp.concatenate([wqi, jnp.zeros_like(wqi)], axis=-1).reshape(Q_RANK, IDX_HEADS * LANES).T.astype(BF16)
    kin = jnp.concatenate([kidx_norm, jnp.zeros((LANES - IDX_DIM,), F32)]).reshape(1, LANES)
    nb = seq // Q_BLOCK
    slab = jax.ShapeDtypeStruct((bsz, nb, LANES, ATTN_HEADS * Q_BLOCK), BF16)
    slab_spec = pl.BlockSpec((1, tc // Q_BLOCK, LANES, ATTN_HEADS * Q_BLOCK), lambda b, i: (b, i, 0, 0))
    row_spec = pl.BlockSpec((1, tc, LANES), lambda b, i: (b, i, 0))
    return pl.pallas_call(
        functools.partial(_prep_kernel, tc=tc),
        grid=(bsz, seq // tc),
        in_specs=[pl.BlockSpec((1, tc, 1024), lambda b, i: (b, i, 0)),
                  _const_spec((1, Q_RANK)), _const_spec((1, KV_RANK)), _const_spec((1, LANES)),
                  _const_spec((ATTN_HEADS * KV_RANK, Q_RANK)), _const_spec((IDX_HEADS * LANES, Q_RANK))],
        out_specs=[slab_spec, slab_spec, row_spec,
                   pl.BlockSpec((1, tc // KEY_TILE, LANES, KEY_TILE), lambda b, i: (b, i, 0, 0)),
                   row_spec,
                   pl.BlockSpec((1, tc // Q_BLOCK, IDX_HEADS, Q_BLOCK), lambda b, i: (b, i, 0, 0)),
                   pl.BlockSpec((1, tc // Q_BLOCK, ATTN_HEADS, Q_BLOCK), lambda b, i: (b, i, 0, 0))],
        out_shape=[slab, slab, jax.ShapeDtypeStruct((bsz, seq, LANES), BF16),
                   jax.ShapeDtypeStruct((bsz, seq // KEY_TILE, LANES, KEY_TILE), BF16),
                   jax.ShapeDtypeStruct((bsz, seq, LANES), BF16),
                   jax.ShapeDtypeStruct((bsz, nb, IDX_HEADS, Q_BLOCK), F32),
                   jax.ShapeDtypeStruct((bsz, nb, ATTN_HEADS, Q_BLOCK), F32)],
        compiler_params=_cparams(("parallel", "parallel")),
        name="prep",
    )(proj, q_norm.reshape(1, Q_RANK), kv_norm.reshape(1, KV_RANK), kin, w_uq.T.astype(BF16), wqit)


def _dsa_kernel(qt_ref, qn_ref, qit0_ref, wt0_ref, qitn_ref, wtn_ref, ki_ref, kv_ref, kvt_ref, tb_ref, bmax_ref,
                wuv_ref, o_ref, iscbuf_ref, sbuf_ref, acc_ref, kmax_ref, *, k_sel, n_blocks):
    i = pl.program_id(1)
    last = i // 2
    odd = i % 2
    n_tiles = last + 1
    isct_ref = iscbuf_ref.at[i % 2]
    qt = qt_ref[0, 0]
    row = lax.broadcasted_iota(jnp.int32, (KEY_TILE, Q_BLOCK), 0)
    col = lax.broadcasted_iota(jnp.int32, (KEY_TILE, Q_BLOCK), 1)

    def key_rows(j):
        return pl.ds(pl.multiple_of(j * KEY_TILE, KEY_TILE), KEY_TILE)

    n_pairs = ATTN_HEADS // 2

    def pair_cols(p):
        return slice(p * KEY_TILE, (p + 1) * KEY_TILE)

    def half(x, hh):
        return x[:, hh * Q_BLOCK:(hh + 1) * Q_BLOCK]

    def idx_tile(j, blk, qit_ref, wt_ref, dst_ref):
        keys = ki_ref[0, key_rows(j), :]
        wt = wt_ref[0, 0]
        acc = jnp.zeros((KEY_TILE, Q_BLOCK), F32)
        for p in range(n_pairs):
            s = jnp.dot(keys, qit_ref[0, 0, :, pair_cols(p)], preferred_element_type=F32)
            for hh in range(2):
                h = 2 * p + hh
                acc = acc + wt[h:h + 1, :] * jnp.maximum(half(s, hh), 0.0)
        key_limit = blk * Q_BLOCK + jnp.where(col < CHUNK, CHUNK, 2 * CHUNK)
        dst_ref[j] = jnp.where(j * KEY_TILE + row < key_limit, acc, -jnp.inf)

    def tile_loop(n, body, carry):
        def pair(jj, c):
            return body(2 * jj + 1, body(2 * jj, c))
        carry = lax.fori_loop(0, n // 2, pair, carry)
        return lax.cond(n % 2 == 1, lambda c: body(n - 1, c), lambda c: c, carry)

    @pl.when(i == 0)
    def _():
        idx_tile(0, 0, qit0_ref, wt0_ref, isct_ref)
        best = jnp.zeros((1, KEY_TILE), F32)
        for j in range(kvt_ref.shape[1]):
            x = kvt_ref[0, j].astype(F32)
            best = jnp.maximum(best, jnp.sum(x * x, axis=0, keepdims=True))
        kmax_ref[...] = jnp.broadcast_to(jnp.sqrt(jnp.max(best, axis=1, keepdims=True)), (SUBLANES, LANES))

    def rows_all(x, op):
        return jnp.broadcast_to(op(x, axis=0, keepdims=True), (SUBLANES, LANES))

    ACCS = 4

    def tile_rows(j):
        return isct_ref[j].reshape(KEY_TILE // (ACCS * SUBLANES), ACCS, SUBLANES, LANES)

    def count_where(pred):
        def body(j, c):
            x = tile_rows(j)
            for r in range(x.shape[0]):
                c = jnp.where(pred(x[r]), c + 1.0, c)
            return c
        c = lax.fori_loop(0, n_tiles, body, jnp.zeros((ACCS, SUBLANES, LANES), F32))
        return rows_all(jnp.sum(c, axis=0), jnp.sum)

    def minmax_body(j, c):
        lo, hi = c
        x = tile_rows(j)
        return (jnp.minimum(lo, jnp.min(jnp.where(x == -jnp.inf, jnp.inf, x), axis=0)),
                jnp.maximum(hi, jnp.max(x, axis=0)))

    lo, hi = lax.fori_loop(0, n_tiles, minmax_body,
                           (jnp.full((ACCS, SUBLANES, LANES), jnp.inf, F32),
                            jnp.full((ACCS, SUBLANES, LANES), -jnp.inf, F32)))
    lo = rows_all(jnp.min(lo, axis=0), jnp.min)
    hi = rows_all(jnp.max(hi, axis=0), jnp.max)
    kf = float(k_sel)
    cnt_lo = count_where(lambda x: x >= lo[0:1])
    cnt_hi = count_where(lambda x: x >= hi[0:1])
    at_max = cnt_hi >= kf
    lo = jnp.where(at_max, hi, lo)
    cnt = jnp.where(at_max, cnt_hi, cnt_lo)

    def is_open(cnt, stalled):
        return jnp.logical_and(cnt > kf, stalled == 0.0)

    def any_lane(mask):
        return jnp.max(jnp.where(mask, 1.0, 0.0)) > 0.0

    def bis_step(c):
        lo, hi, cnt, cnt_hi, stalled = c
        mid = 0.5 * lo + 0.5 * hi
        cm = count_where(lambda x: x >= mid[0:1])
        active = is_open(cnt, stalled)
        noprog = jnp.logical_or(mid <= lo, mid >= hi)
        move = jnp.logical_and(active, jnp.logical_not(noprog))
        up = jnp.logical_and(move, cm >= kf)
        down = jnp.logical_and(move, cm < kf)
        return (jnp.where(up, mid, lo), jnp.where(down, mid, hi), jnp.where(up, cm, cnt),
                jnp.where(down, cm, cnt_hi), jnp.where(jnp.logical_and(active, noprog), 1.0, stalled))

    def bisect(state, max_iters):
        def cond(c):
            return jnp.logical_and(c[5] < max_iters, any_lane(is_open(c[2], c[4])))

        def body(c):
            state = c[:5]
            for _ in range(BISECT_UNROLL):
                state = bis_step(state)
            return state + (c[5] + BISECT_UNROLL,)

        return lax.while_loop(cond, body, state + (jnp.int32(0),))[:5]

    def max_below(t):
        def body(j, c):
            x = tile_rows(j)
            return jnp.maximum(c, jnp.max(jnp.where(x < t, x, -jnp.inf), axis=0))
        c = lax.fori_loop(0, n_tiles, body, jnp.full((ACCS, SUBLANES, LANES), -jnp.inf, F32))
        return rows_all(jnp.max(c, axis=0), jnp.max)

    def walk(c):
        top, left = c
        return jnp.where(left > 0.0, max_below(top), top), jnp.maximum(left - 1.0, 0.0)

    def coarse(state):
        for _ in range(BISECT_COARSE_ITERS):
            state = bis_step(state)
        lo, hi, cnt, cnt_hi, stalled = state
        left = jnp.where(is_open(cnt, stalled), kf - cnt_hi, 0.0)
        return state + walk(walk((hi, left)))

    no_stall = jnp.zeros((SUBLANES, LANES), F32)
    state = (lo, hi, cnt, cnt_hi, no_stall)
    lo, hi, cnt, cnt_hi, stalled, top, left = lax.cond(
        any_lane(is_open(cnt, no_stall)), coarse, lambda s: s + (hi, no_stall), state)
    opened = is_open(cnt, stalled)
    top, _ = lax.while_loop(lambda c: any_lane(c[1] > 0.0), walk, (top, left))
    lo = jnp.where(opened, top, lo)
    cnt = count_where(lambda x: x >= lo[0:1])
    thr8, _, cnt, _, _ = bisect((lo, hi, cnt, cnt_hi, stalled), BISECT_MAX_ITERS)

    tied = cnt > kf

    @pl.when(jnp.max(jnp.where(tied, 1.0, 0.0)) > 0.0)
    def _():
        need = kf - count_where(lambda x: x > thr8[0:1])

        def body(j, seen):
            x = isct_ref[j]
            eq = jnp.where(x == thr8[0:1], 1.0, 0.0)
            inc = _cumsum_rows(eq)
            rank = inc - eq + seen[0:1]
            drop = jnp.logical_and(jnp.logical_and(tied[0:1], eq > 0.0), rank >= need[0:1])
            isct_ref[j] = jnp.where(drop, -jnp.inf, x)
            return seen + inc[KEY_TILE - 1:KEY_TILE]

        lax.fori_loop(0, n_tiles, body, jnp.zeros((SUBLANES, LANES), F32))

    thr = thr8[0:1]

    def fold_rows(x, op):
        x = x.reshape(KEY_TILE // (ACCS * SUBLANES), ACCS, SUBLANES, LANES)
        return op(op(x, axis=0), axis=0)

    n_far = jnp.maximum(last - 1 + odd, 0)
    next_ref = iscbuf_ref.at[(i + 1) % 2]

    zero_l = jnp.zeros((ATTN_HEADS, SUBLANES, LANES), F32)

    shift = [qn_ref[0, 0, h:h + 1, :] * kmax_ref[0:1, :] * BOUND_SLACK + (bmax_ref[h] + BOUND_SLACK)
             for h in range(ATTN_HEADS)]

    def sweep_tile(j, l, near):
        keys = kv_ref[0, key_rows(j), :]
        values_t = kvt_ref[0, j]
        neg = jnp.where(isct_ref[j] >= thr, 0.0, -jnp.inf)
        table = jnp.where(j == last, odd, 2)
        for p in range(n_pairs):
            s = jnp.dot(keys, qt[:, pair_cols(p)], preferred_element_type=F32)
            for hh in range(2):
                h = 2 * p + hh
                sh = half(s, hh) + neg
                sbuf_ref[j, h] = sh + tb_ref[table, h] if near else sh
        out = []
        for p in range(n_pairs):
            probs = []
            for hh in range(2):
                h = 2 * p + hh
                e = jnp.exp2(sbuf_ref[j, h] - shift[h])
                out.append(l[h] + fold_rows(e, jnp.sum))
                probs.append(e.astype(BF16))
            acc_ref[p] += jnp.dot(values_t, jnp.concatenate(probs, axis=1),
                                  preferred_element_type=F32)
        idx_tile(j, i + 1, qitn_ref, wtn_ref, next_ref)
        return jnp.stack(out)

    acc_ref[...] = jnp.zeros_like(acc_ref)
    l = tile_loop(n_far, lambda j, l: sweep_tile(j, l, False), zero_l)
    l = lax.fori_loop(n_far, n_tiles, lambda j, l: sweep_tile(j, l, True), l)
    l = jnp.sum(l, axis=1, keepdims=True)

    @pl.when(jnp.logical_and(odd == 1, i + 1 < n_blocks))
    def _():
        idx_tile(n_tiles, i + 1, qitn_ref, wtn_ref, next_ref)

    def p1_tile(j, m, near):
        keys = kv_ref[0, key_rows(j), :]
        neg = jnp.where(isct_ref[j] >= thr, 0.0, -jnp.inf)
        table = jnp.where(j == last, odd, 2)
        out = []
        for p in range(n_pairs):
            s = jnp.dot(keys, qt[:, pair_cols(p)], preferred_element_type=F32)
            for hh in range(2):
                h = 2 * p + hh
                sh = half(s, hh) + neg
                if near:
                    sh = sh + tb_ref[table, h]
                sbuf_ref[j, h] = sh
                out.append(jnp.maximum(m[h], fold_rows(sh, jnp.max)))
        return jnp.stack(out)

    def exact_softmax(_):
        m = jnp.full((ATTN_HEADS, SUBLANES, LANES), -jnp.inf, F32)
        m = tile_loop(n_far, lambda j, m: p1_tile(j, m, False), m)
        m = lax.fori_loop(n_far, n_tiles, lambda j, m: p1_tile(j, m, True), m)
        m = jnp.max(m, axis=1, keepdims=True)
        acc_ref[...] = jnp.zeros_like(acc_ref)

        def p2_body(j, l):
            values_t = kvt_ref[0, j]
            out = []
            for p in range(n_pairs):
                probs = []
                for hh in range(2):
                    h = 2 * p + hh
                    e = jnp.exp2(sbuf_ref[j, h] - m[h])
                    out.append(l[h] + fold_rows(e, jnp.sum))
                    probs.append(e.astype(BF16))
                acc_ref[p] += jnp.dot(values_t, jnp.concatenate(probs, axis=1),
                                      preferred_element_type=F32)
            return jnp.stack(out)

        return jnp.sum(tile_loop(n_tiles, p2_body, zero_l), axis=1, keepdims=True)

    underflowed = jnp.max(jnp.where(l > UNDERFLOW_GUARD, 0.0, 1.0)) > 0.0
    l = lax.cond(underflowed, exact_softmax, lambda l: l, l)

    outs = [(acc_ref[h // 2, :, (h % 2) * Q_BLOCK:(h % 2 + 1) * Q_BLOCK] / l[h]).T.astype(BF16)
            for h in range(ATTN_HEADS)]
    for p in range(ATTN_HEADS // 2):
        pair = jnp.concatenate(outs[2 * p:2 * p + 2], axis=1)
        o_ref[0, :, p * LANES:(p + 1) * LANES] = jnp.dot(
            pair, wuv_ref[p], preferred_element_type=F32).astype(BF16)


def _pack_w_uv(w_uv):
    eye = jnp.eye(2, dtype=w_uv.dtype)
    w = w_uv.reshape(ATTN_HEADS // 2, 2, KV_RANK, 1, HEAD_DIM) * eye[None, :, None, :, None]
    return w.reshape(ATTN_HEADS // 2, 2 * KV_RANK, 2 * HEAD_DIM).astype(BF16)


def _dsa(qt, qnorm, qit, wt, ki, kv, kvt, tables, bias_max, wuv):
    bsz, seq, _ = kv.shape
    nkt = seq // KEY_TILE
    k_sel = min(TOPK_MAX, seq // 4)
    nb = seq // Q_BLOCK
    slab = (1, 1, LANES, ATTN_HEADS * Q_BLOCK)
    wslab = (1, 1, IDX_HEADS, Q_BLOCK)

    def first(b, i):
        return (b, 0, 0, 0)

    def following(b, i):
        return (b, jnp.minimum(i + 1, nb - 1), 0, 0)

    seq_spec = pl.BlockSpec((1, seq, LANES), lambda b, i: (b, 0, 0))
    return pl.pallas_call(
        functools.partial(_dsa_kernel, k_sel=k_sel, n_blocks=nb),
        grid=(bsz, nb),
        in_specs=[pl.BlockSpec(slab, lambda b, i: (b, i, 0, 0)),
                  pl.BlockSpec(wslab, lambda b, i: (b, i, 0, 0)),
                  pl.BlockSpec(slab, first), pl.BlockSpec(wslab, first),
                  pl.BlockSpec(slab, following), pl.BlockSpec(wslab, following),
                  seq_spec, seq_spec,
                  pl.BlockSpec((1, nkt, LANES, KEY_TILE), lambda b, i: (b, 0, 0, 0)),
                  _const_spec((3, ATTN_HEADS, KEY_TILE, Q_BLOCK)),
                  pl.BlockSpec(memory_space=pltpu.SMEM),
                  _const_spec((ATTN_HEADS // 2, 2 * KV_RANK, 2 * HEAD_DIM))],
        out_specs=pl.BlockSpec((1, Q_BLOCK, ATTN_HEADS * HEAD_DIM), lambda b, i: (b, i, 0)),
        out_shape=jax.ShapeDtypeStruct((bsz, seq, ATTN_HEADS * HEAD_DIM), BF16),
        scratch_shapes=[pltpu.VMEM((2, nkt, KEY_TILE, Q_BLOCK), F32),
                        pltpu.VMEM((nkt, ATTN_HEADS, KEY_TILE, Q_BLOCK), F32),
                        pltpu.VMEM((ATTN_HEADS // 2, KV_RANK, 2 * Q_BLOCK), F32),
                        pltpu.VMEM((SUBLANES, LANES), F32)],
        compiler_params=_cparams(("parallel", "arbitrary")),
        name="dsa",
    )(qt, qnorm, qit, wt, qit, wt, ki, kv, kvt, tables, bias_max, wuv)


def _pack3(v):
    lane = lax.broadcasted_iota(jnp.int32, v.shape, 1)
    v = jnp.where(lane < SSD_HEADS, v, 0.0)
    hi = v.astype(BF16).astype(F32)
    r = v - hi
    mid = r.astype(BF16).astype(F32)
    lo = r - mid
    return (hi + pltpu.roll(mid, SSD_HEADS, axis=1) + pltpu.roll(lo, 2 * SSD_HEADS, axis=1)).astype(BF16)


def _cumsum_rows(x):
    n = x.shape[0]
    r = lax.broadcasted_iota(jnp.int32, x.shape, 0)
    s = 1
    while s < n:
        x = x + jnp.where(r >= s, pltpu.roll(x, s, axis=0), 0.0)
        s *= 2
    return x


def _shift_rows(x, s):
    r = pltpu.roll(x, s, axis=1)
    prev = jnp.concatenate([r[-1:], r[:-1]], axis=0)
    sub = lax.broadcasted_iota(jnp.int32, x.shape, 1)
    return jnp.where(sub >= s, r, prev)


def _ssd_kernel(z_ref, xbc_ref, dt_ref, cw_ref, cb_ref, dtb_ref, alog_ref, dsk_ref, nw_ref, e_ref,
                y_ref, tail_ref, u_ref, g_ref, state_ref):
    nq = SSD_Q

    @pl.when(pl.program_id(1) == 0)
    def _():
        tail_ref[...] = jnp.zeros_like(tail_ref)
        state_ref[...] = jnp.zeros_like(state_ref)

    assert CONV_W == 4
    for blk in range(CONV_DIM // CONV_BLOCK):
        cols = slice(blk * CONV_BLOCK, (blk + 1) * CONV_BLOCK)
        ext = jnp.concatenate([tail_ref[:, cols], xbc_ref[0, :, cols].astype(F32)], axis=0)
        ext = ext.reshape(1 + nq // SUBLANES, SUBLANES, CONV_BLOCK)
        s1 = _shift_rows(ext, 1)
        a = cw_ref[3:4, cols] * ext + cw_ref[2:3, cols] * s1 + cb_ref[:, cols]
        b = cw_ref[1:2, cols] * ext + cw_ref[0:1, cols] * s1
        conv = (a + _shift_rows(b, 2))[1:].reshape(nq, CONV_BLOCK)
        u_ref[:, cols] = _silu(conv)
        tail_ref[:, cols] = ext[nq // SUBLANES]

    t = dt_ref[0] + dtb_ref[...]
    dt = jnp.maximum(t, 0.0) + jnp.log1p(jnp.exp(-jnp.abs(t)))
    a2 = _cumsum_rows(dt * (-jnp.exp(alog_ref[...]))) * LOG2E
    a2_t = a2.T
    dt_p = _pack3(dt)
    dec_p = _pack3(dt * jnp.exp2(a2[nq - 1:nq, :] - a2))
    expa_p = _pack3(jnp.exp2(a2))

    r = lax.broadcasted_iota(jnp.int32, (nq, nq), 0)
    c = lax.broadcasted_iota(jnp.int32, (nq, nq), 1)
    causal = r >= c
    lane = lax.broadcasted_iota(jnp.int32, (nq, LANES), 1)
    heads_per_group = SSD_HEADS // SSD_GROUPS
    gw = heads_per_group * SSD_HEADDIM
    b_col = D_INNER
    c_col = D_INNER + SSD_GROUPS * D_STATE
    ssq = jnp.zeros((nq, LANES), F32)

    for g in range(SSD_GROUPS):
        gcols = slice(g * gw, (g + 1) * gw)
        eg = e_ref[:, gcols]
        dt_e = jnp.dot(dt_p, eg, preferred_element_type=F32)
        dec_e = jnp.dot(dec_p, eg, preferred_element_type=F32)
        expa_e = jnp.dot(expa_p, eg, preferred_element_type=F32)
        xs = u_ref[:, gcols]
        xdt_b = (xs * dt_e).astype(BF16)
        xdec_b = (xs * dec_e).astype(BF16)
        cg = u_ref[:, c_col + g * D_STATE:c_col + (g + 1) * D_STATE].astype(BF16)
        bgt = u_ref[:, b_col + g * D_STATE:b_col + (g + 1) * D_STATE].T.astype(BF16)
        cb = jnp.dot(cg, bgt, preferred_element_type=F32)
        prev = state_ref[g]
        y_off = jnp.dot(cg, prev.astype(BF16), preferred_element_type=F32) * expa_e
        pairs = []
        for pp in range(heads_per_group // 2):
            xp = xdt_b[:, pp * LANES:(pp + 1) * LANES]
            yh = []
            for hh in range(2):
                h = heads_per_group * g + 2 * pp + hh
                seg = a2[:, h:h + 1] - a2_t[h:h + 1, :]
                m = (cb * jnp.where(causal, jnp.exp2(seg), 0.0)).astype(BF16)
                yh.append(jnp.dot(m, xp, preferred_element_type=F32))
            pairs.append(jnp.where(lane < SSD_HEADDIM, yh[0], yh[1]))
        y = jnp.concatenate(pairs, axis=1) + y_off + dsk_ref[:, gcols] * xs
        gated = y * _silu(z_ref[0, :, gcols].astype(F32))
        g_ref[:, gcols] = gated
        sq = gated * gated
        ssq = ssq + sq[:, :LANES] + sq[:, LANES:]
        new = jnp.dot(bgt, xdec_b, preferred_element_type=F32)
        state_ref[g] = prev * expa_e[nq - 1:nq, :] + new
    scale = lax.rsqrt(jnp.sum(ssq, axis=-1, keepdims=True) * (1.0 / D_INNER) + EPS)
    y_ref[0] = (g_ref[...] * scale * nw_ref[...]).astype(BF16)


def _ssd(head, wide, conv_w, conv_b, dt_bias, a_log, d_skip, ssd_norm):
    bsz, seq, _ = head.shape
    nq = SSD_Q

    def pad_heads(v):
        return jnp.concatenate([v, jnp.zeros((LANES - SSD_HEADS,), F32)]).reshape(1, LANES)

    sel = np.concatenate([np.eye(SSD_HEADS)] * 3 + [np.zeros((LANES - 3 * SSD_HEADS, SSD_HEADS))], axis=0)
    e = jnp.asarray(np.kron(sel, np.ones((1, SSD_HEADDIM))), BF16)
    return pl.pallas_call(
        _ssd_kernel,
        grid=(bsz, seq // nq),
        in_specs=[pl.BlockSpec((1, nq, D_INNER), lambda b, i: (b, i, WIDE_Z // D_INNER)),
                  pl.BlockSpec((1, nq, CONV_DIM), lambda b, i: (b, i, WIDE_XBC // CONV_DIM)),
                  pl.BlockSpec((1, nq, LANES), lambda b, i: (b, i, HEAD_DT // LANES)),
                  _const_spec((CONV_W, CONV_DIM)), _const_spec((1, CONV_DIM)),
                  _const_spec((1, LANES)), _const_spec((1, LANES)),
                  _const_spec((1, D_INNER)), _const_spec((1, D_INNER)),
                  _const_spec((LANES, D_INNER))],
        out_specs=pl.BlockSpec((1, nq, D_INNER), lambda b, i: (b, i, 0)),
        out_shape=jax.ShapeDtypeStruct((bsz, seq, D_INNER), BF16),
        scratch_shapes=[pltpu.VMEM((SUBLANES, CONV_DIM), F32),
                        pltpu.VMEM((nq, CONV_DIM), F32),
                        pltpu.VMEM((nq, D_INNER), F32),
                        pltpu.VMEM((SSD_GROUPS, D_STATE, 4 * SSD_HEADDIM), F32)],
        compiler_params=_cparams(("parallel", "arbitrary")),
        name="ssd",
    )(wide, wide, head, conv_w, conv_b.reshape(1, CONV_DIM), pad_heads(dt_bias), pad_heads(a_log),
      jnp.repeat(d_skip, SSD_HEADDIM).reshape(1, D_INNER), ssd_norm.reshape(1, D_INNER), e)


def _mix_kernel(ao_ref, sy_ref, ga_ref, gb_ref, x_ref, mod_ref, nw_ref, woa_ref, wos_ref, wout_ref, o_ref):
    ya = jnp.dot(ao_ref[0], woa_ref[...], preferred_element_type=F32)
    yb = jnp.dot(sy_ref[0], wos_ref[...], preferred_element_type=F32)
    mix = jax.nn.sigmoid(ga_ref[0].astype(F32)) * ya + jax.nn.sigmoid(gb_ref[0].astype(F32)) * yb
    m2 = jnp.dot(mix.astype(BF16), wout_ref[...], preferred_element_type=F32)
    o_ref[0] = x_ref[0] + mod_ref[0][2:3] * _rms(m2, nw_ref[...])


def _mix(attn_o, ssd_y, wide, x, mod3, post_norm, w_o_attn, w_o_ssd, w_out):
    bsz, seq, _ = x.shape
    tm = min(seq, 512)

    def rows(width, col_block=0):
        return pl.BlockSpec((1, tm, width), lambda b, i: (b, i, col_block))

    return pl.pallas_call(
        _mix_kernel,
        grid=(bsz, seq // tm),
        in_specs=[rows(D_MODEL), rows(D_INNER), rows(D_MODEL, WIDE_GA // D_MODEL), rows(D_MODEL, WIDE_GB // D_MODEL),
                  rows(D_MODEL), pl.BlockSpec((1, 6, D_MODEL), lambda b, i: (b, 0, 0)),
                  _const_spec((1, D_MODEL)), _const_spec((D_MODEL, D_MODEL)),
                  _const_spec((D_INNER, D_MODEL)), _const_spec((D_MODEL, D_MODEL))],
        out_specs=rows(D_MODEL),
        out_shape=jax.ShapeDtypeStruct((bsz, seq, D_MODEL), F32),
        compiler_params=_cparams(("parallel", "parallel")),
        name="mix",
    )(attn_o, ssd_y, wide, wide, x, mod3, post_norm.reshape(1, D_MODEL),
      w_o_attn.astype(BF16), w_o_ssd.astype(BF16), w_out.astype(BF16))


def _ffn_kernel(x_ref, mod_ref, nw1_ref, nw2_ref, wg_ref, wu_ref, wo_ref, o_ref):
    x = x_ref[0]
    m = mod_ref[0]
    h2 = (_rms(x, nw1_ref[...]) * (1.0 + m[4:5]) + m[3:4]).astype(BF16)
    ug = jnp.dot(h2, wg_ref[...], preferred_element_type=F32)
    uu = jnp.dot(h2, wu_ref[...], preferred_element_type=F32)
    f = jnp.dot((_silu(ug) * uu).astype(BF16), wo_ref[...], preferred_element_type=F32)
    o_ref[0] = x + m[5:6] * _rms(f, nw2_ref[...])


def _ffn(x, mod3, pre_norm, post_norm, w_ffn_in, w_ffn_out):
    bsz, seq, _ = x.shape
    tm = min(seq, 512)
    rows = pl.BlockSpec((1, tm, D_MODEL), lambda b, i: (b, i, 0))
    w_in = w_ffn_in.astype(BF16)

    def half(k):
        return pl.BlockSpec((D_MODEL, D_FF), lambda b, i: (0, k), pipeline_mode=pl.Buffered(1))

    return pl.pallas_call(
        _ffn_kernel,
        grid=(bsz, seq // tm),
        in_specs=[rows, pl.BlockSpec((1, 6, D_MODEL), lambda b, i: (b, 0, 0)),
                  _const_spec((1, D_MODEL)), _const_spec((1, D_MODEL)),
                  half(0), half(1), _const_spec((D_FF, D_MODEL))],
        out_specs=rows,
        out_shape=jax.ShapeDtypeStruct((bsz, seq, D_MODEL), F32),
        compiler_params=_cparams(("parallel", "parallel")),
        name="ffn",
    )(x, mod3, pre_norm.reshape(1, D_MODEL), post_norm.reshape(1, D_MODEL), w_in, w_in, w_ffn_out.astype(BF16))


def kernel(x, c, positions, ada_w, ada_b, pre_norm_mix, post_norm_mix, pre_norm_ffn, post_norm_ffn, w_in, q_norm, kv_norm, w_uq, w_uv, rel_bias, w_qidx, kidx_norm, conv_w, conv_b, dt_bias, a_log, d_skip, ssd_norm, w_o_attn, w_o_ssd, w_out, w_ffn_in, w_ffn_out):
    del positions
    bsz, seq, _ = x.shape
    assert seq % (2 * KEY_TILE) == 0 and x.shape[-1] == D_MODEL
    mod3 = _mod(c, ada_w, ada_b).reshape(bsz, 6, D_MODEL)
    head, wide = _inproj(x, mod3, pre_norm_mix, _pack_w_in(w_in))
    qt, qit, kv, kvt, ki, wt, qnorm = _prep(head, q_norm, kv_norm, kidx_norm, w_uq, w_qidx)
    bias_max = jnp.max(rel_bias - rel_bias[NUM_BUCKETS // 2 - 1], axis=0) * LOG2E
    attn_o = _dsa(qt, qnorm, qit, wt, ki, kv, kvt, _bias_tables(rel_bias), bias_max, _pack_w_uv(w_uv))
    ssd_y = _ssd(head, wide, conv_w, conv_b, dt_bias, a_log, d_skip, ssd_norm)
    x1 = _mix(attn_o, ssd_y, wide, x, mod3, post_norm_mix, w_o_attn, w_o_ssd, w_out)
    return _ffn(x1, mod3, pre_norm_ffn, post_norm_ffn, w_ffn_in, w_ffn_out)
```

```python
import functools
import math
import statistics

import numpy as np
import jax
import jax.numpy as jnp
from jax import lax
from jax.experimental import pallas as pl
from jax.experimental.pallas import tpu as pltpu

F32 = jnp.float32
BF16 = jnp.bfloat16

D_MODEL = 1024
CHUNK = 64
Q_BLOCK = 128
EPS = 1e-6
ATTN_HEADS = 16
HEAD_DIM = 64
Q_RANK = 256
KV_RANK = 128
IDX_HEADS = 16
IDX_DIM = 64
TOPK_MAX = 256
NUM_BUCKETS = 32
MAX_DISTANCE = 128
D_INNER = 2 * D_MODEL
SSD_HEADDIM = 64
SSD_HEADS = D_INNER // SSD_HEADDIM
SSD_GROUPS = 8
D_STATE = 128
CONV_W = 4
CONV_DIM = D_INNER + 2 * SSD_GROUPS * D_STATE
D_FF = -(-8 * D_MODEL // (3 * 256)) * 256

LANES = 128
SUBLANES = 8
KEY_TILE = 256
SSD_Q = 128
CONV_BLOCK = 512
VMEM_LIMIT = 56 * 1024 * 1024
BISECT_MAX_ITERS = 320
BISECT_UNROLL = 4
BISECT_COARSE_ITERS = 11
BRACKET_HALF_WIDTH = 0.3
FLOAT_BIG = 3.0e38
LOG2E = math.log2(math.e)
BOUND_SLACK = 1.0 + 2.0 ** -6
UNDERFLOW_GUARD = 2.0 ** -80

HEAD_COLS = 1024
HEAD_KV, HEAD_KIDX, HEAD_W, HEAD_DT = 256, 384, 512, 640
WIDE_XBC, WIDE_Z, WIDE_GA, WIDE_GB = 0, 4096, 6144, 7168
WIDE_COLS = 8192
INPROJ_TN = 1024


def _cparams(sem):
    return pltpu.CompilerParams(dimension_semantics=sem, vmem_limit_bytes=VMEM_LIMIT)


def _const_spec(shape):
    nd = len(shape)
    return pl.BlockSpec(shape, lambda *_: (0,) * nd, pipeline_mode=pl.Buffered(1))


def _rms(x, w, n=None):
    n = x.shape[-1] if n is None else n
    return x * lax.rsqrt(jnp.sum(x * x, axis=-1, keepdims=True) * (1.0 / n) + EPS) * w


def _silu(x):
    h = 0.5 * x
    return h + h * jnp.tanh(h)


def _mod_kernel(c_ref, w_ref, b_ref, o_ref):
    c = c_ref[...]
    s = _silu(c).astype(BF16)
    o_ref[...] = jnp.dot(s, w_ref[...].astype(BF16), preferred_element_type=F32) + b_ref[...]


def _mod(c, ada_w, ada_b):
    bsz = c.shape[0]
    return pl.pallas_call(
        _mod_kernel,
        grid=(6,),
        in_specs=[pl.BlockSpec((bsz, D_MODEL), lambda j: (0, 0)),
                  pl.BlockSpec((D_MODEL, D_MODEL), lambda j: (0, j)),
                  pl.BlockSpec((1, D_MODEL), lambda j: (0, j))],
        out_specs=pl.BlockSpec((bsz, D_MODEL), lambda j: (0, j)),
        out_shape=jax.ShapeDtypeStruct((bsz, 6 * D_MODEL), F32),
        compiler_params=_cparams(("parallel",)),
        name="mod",
    )(c, ada_w, ada_b.reshape(1, 6 * D_MODEL))


def _t5_bucket_np(rel):
    half = NUM_BUCKETS // 2
    max_exact = half // 2
    side = np.where(rel > 0, half, 0)
    n = np.abs(rel)
    large = max_exact + (np.log(np.maximum(n, max_exact).astype(np.float64) / max_exact)
                         / math.log(MAX_DISTANCE / max_exact) * (half - max_exact)).astype(np.int64)
    large = np.minimum(large, half - 1)
    return (side + np.where(n < max_exact, n, large)).astype(np.int32)


def _bias_kernel(idx_ref, rb_ref, o_ref):
    h = pl.program_id(0)
    idx = idx_ref[...]
    far = rb_ref[h, NUM_BUCKETS // 2 - 1]
    acc = jnp.zeros(idx.shape, F32)
    for b in range(NUM_BUCKETS):
        acc = jnp.where(idx == b, (rb_ref[h, b] - far) * LOG2E, acc)
    o_ref[0] = acc


def _bias_tables(rel_bias):
    kk = np.arange(2 * KEY_TILE)[:, None]
    ql = np.arange(Q_BLOCK)[None, :]
    idx = jnp.asarray(_t5_bucket_np(kk - KEY_TILE - ql))
    t = pl.pallas_call(
        _bias_kernel,
        grid=(ATTN_HEADS,),
        in_specs=[pl.BlockSpec((2 * KEY_TILE, Q_BLOCK), lambda h: (0, 0)),
                  pl.BlockSpec(memory_space=pltpu.SMEM)],
        out_specs=pl.BlockSpec((1, 2 * KEY_TILE, Q_BLOCK), lambda h: (h, 0, 0)),
        out_shape=jax.ShapeDtypeStruct((ATTN_HEADS, 2 * KEY_TILE, Q_BLOCK), F32),
        compiler_params=_cparams(("arbitrary",)),
        name="bias",
    )(idx, rel_bias.T)
    return jnp.stack([t[:, 256:512], t[:, 128:384], t[:, 0:256]])


def _inproj_kernel(x_ref, mod_ref, nw_ref, w_ref, head_ref, wide_ref):
    m = mod_ref[0]
    hn = (_rms(x_ref[0], nw_ref[...]) * (1.0 + m[1:2]) + m[0:1]).astype(BF16)
    head_ref[0] = _dot_nt(hn, w_ref[0:HEAD_COLS, :])
    for c in range(WIDE_COLS // INPROJ_TN):
        rows = slice(HEAD_COLS + c * INPROJ_TN, HEAD_COLS + (c + 1) * INPROJ_TN)
        wide_ref[0, :, c * INPROJ_TN:(c + 1) * INPROJ_TN] = _dot_nt(hn, w_ref[rows, :]).astype(BF16)


def _pack_w_in(w_in):
    sizes = [Q_RANK, KV_RANK, IDX_DIM, IDX_HEADS, D_INNER, CONV_DIM, SSD_HEADS, D_MODEL, D_MODEL]
    offs = np.cumsum([0] + sizes)
    wt = w_in.T
    q, kv, ki, wi, z, xbc, dt, ga, gb = [wt[offs[i]:offs[i + 1]] for i in range(9)]

    def zr(n):
        return jnp.zeros((n, D_MODEL), w_in.dtype)

    return jnp.concatenate([q, kv, ki, zr(LANES - IDX_DIM), wi, zr(LANES - IDX_HEADS),
                            dt, zr(LANES - SSD_HEADS), zr(2 * LANES), xbc, z, ga, gb], axis=0).astype(BF16)


def _dot_nt(a, b):
    return lax.dot_general(a, b, (((1,), (1,)), ((), ())), preferred_element_type=F32)


def _inproj(x, mod3, pre_norm, w_packed):
    bsz, seq, _ = x.shape
    tm = min(seq, 512)
    return pl.pallas_call(
        _inproj_kernel,
        grid=(bsz, seq // tm),
        in_specs=[pl.BlockSpec((1, tm, D_MODEL), lambda b, i: (b, i, 0)),
                  pl.BlockSpec((1, 6, D_MODEL), lambda b, i: (b, 0, 0)),
                  _const_spec((1, D_MODEL)),
                  _const_spec((HEAD_COLS + WIDE_COLS, D_MODEL))],
        out_specs=[pl.BlockSpec((1, tm, HEAD_COLS), lambda b, i: (b, i, 0)),
                   pl.BlockSpec((1, tm, WIDE_COLS), lambda b, i: (b, i, 0))],
        out_shape=[jax.ShapeDtypeStruct((bsz, seq, HEAD_COLS), F32),
                   jax.ShapeDtypeStruct((bsz, seq, WIDE_COLS), BF16)],
        compiler_params=_cparams(("parallel", "parallel")),
        name="inproj",
    )(x, mod3, pre_norm.reshape(1, D_MODEL), w_packed)


def _prep_kernel(p_ref, qn_ref, kvn_ref, kin_ref, wuqt_ref, wqit_ref,
                 qt_ref, qit_ref, kv_ref, kvt_ref, ki_ref, wt_ref, qnorm_ref, *, tc):
    p = p_ref[0]
    qnt = _rms(p[:, :Q_RANK], qn_ref[...]).T.astype(BF16)
    qt = (jnp.dot(wuqt_ref[...], qnt, preferred_element_type=F32) * (KV_RANK ** -0.5 * LOG2E)).astype(BF16)
    qf = qt.astype(F32).reshape(ATTN_HEADS, KV_RANK, tc)
    qnorm = jnp.sqrt(jnp.sum(qf * qf, axis=1))
    qit = jnp.dot(wqit_ref[...], qnt, preferred_element_type=F32).astype(BF16)
    wt = (p[:, HEAD_W:HEAD_W + LANES] * (IDX_HEADS ** -0.5 * IDX_DIM ** -0.5)).T
    for blk in range(tc // Q_BLOCK):
        cols = slice(blk * Q_BLOCK, (blk + 1) * Q_BLOCK)
        for h in range(ATTN_HEADS):
            qt_ref[0, blk, :, h * LANES:(h + 1) * LANES] = qt[h * LANES:(h + 1) * LANES, cols]
            qit_ref[0, blk, :, h * LANES:(h + 1) * LANES] = qit[h * LANES:(h + 1) * LANES, cols]
        wt_ref[0, blk] = wt[0:IDX_HEADS, cols]
        qnorm_ref[0, blk] = qnorm[:, cols]
    kv = _rms(p[:, HEAD_KV:HEAD_KV + KV_RANK], kvn_ref[...])
    kv_ref[0] = kv.astype(BF16)
    kvt = kv.T
    for c in range(tc // KEY_TILE):
        kvt_ref[0, c] = kvt[:, c * KEY_TILE:(c + 1) * KEY_TILE].astype(BF16)
    ki_ref[0] = _rms(p[:, HEAD_KIDX:HEAD_KIDX + LANES], kin_ref[...], n=IDX_DIM).astype(BF16)


def _prep(proj, q_norm, kv_norm, kidx_norm, w_uq, w_qidx):
    bsz, seq, _ = proj.shape
    tc = min(seq, 512)
    wqi = w_qidx.reshape(Q_RANK, IDX_HEADS, IDX_DIM)
    wqit = jnp.concatenate([wqi, jnp.zeros_like(wqi)], axis=-1).reshape(Q_RANK, IDX_HEADS * LANES).T.astype(BF16)
    kin = jnp.concatenate([kidx_norm, jnp.zeros((LANES - IDX_DIM,), F32)]).reshape(1, LANES)
    nb = seq // Q_BLOCK
    slab = jax.ShapeDtypeStruct((bsz, nb, LANES, ATTN_HEADS * Q_BLOCK), BF16)
    slab_spec = pl.BlockSpec((1, tc // Q_BLOCK, LANES, ATTN_HEADS * Q_BLOCK), lambda b, i: (b, i, 0, 0))
    row_spec = pl.BlockSpec((1, tc, LANES), lambda b, i: (b, i, 0))
    return pl.pallas_call(
        functools.partial(_prep_kernel, tc=tc),
        grid=(bsz, seq // tc),
        in_specs=[pl.BlockSpec((1, tc, 1024), lambda b, i: (b, i, 0)),
                  _const_spec((1, Q_RANK)), _const_spec((1, KV_RANK)), _const_spec((1, LANES)),
                  _const_spec((ATTN_HEADS * KV_RANK, Q_RANK)), _const_spec((IDX_HEADS * LANES, Q_RANK))],
        out_specs=[slab_spec, slab_spec, row_spec,
                   pl.BlockSpec((1, tc // KEY_TILE, LANES, KEY_TILE), lambda b, i: (b, i, 0, 0)),
                   row_spec,
                   pl.BlockSpec((1, tc // Q_BLOCK, IDX_HEADS, Q_BLOCK), lambda b, i: (b, i, 0, 0)),
                   pl.BlockSpec((1, tc // Q_BLOCK, ATTN_HEADS, Q_BLOCK), lambda b, i: (b, i, 0, 0))],
        out_shape=[slab, slab, jax.ShapeDtypeStruct((bsz, seq, LANES), BF16),
                   jax.ShapeDtypeStruct((bsz, seq // KEY_TILE, LANES, KEY_TILE), BF16),
                   jax.ShapeDtypeStruct((bsz, seq, LANES), BF16),
                   jax.ShapeDtypeStruct((bsz, nb, IDX_HEADS, Q_BLOCK), F32),
                   jax.ShapeDtypeStruct((bsz, nb, ATTN_HEADS, Q_BLOCK), F32)],
        compiler_params=_cparams(("parallel", "parallel")),
        name="prep",
    )(proj, q_norm.reshape(1, Q_RANK), kv_norm.reshape(1, KV_RANK), kin, w_uq.T.astype(BF16), wqit)


def _dsa_kernel(qt_ref, qn_ref, zq_ref, qit0_ref, wt0_ref, qitn_ref, wtn_ref, ki_ref, kv_ref, kvt_ref, tb_ref,
                bmax_ref, wuv_ref, o_ref, iscbuf_ref, mom_ref, sbuf_ref, acc_ref, kmax_ref, *, k_sel, n_blocks):
    i = pl.program_id(1)
    last = i // 2
    odd = i % 2
    n_tiles = last + 1
    isct_ref = iscbuf_ref.at[i % 2]
    qt = qt_ref[0, 0]
    row = lax.broadcasted_iota(jnp.int32, (KEY_TILE, Q_BLOCK), 0)
    col = lax.broadcasted_iota(jnp.int32, (KEY_TILE, Q_BLOCK), 1)

    def key_rows(j):
        return pl.ds(pl.multiple_of(j * KEY_TILE, KEY_TILE), KEY_TILE)

    n_pairs = ATTN_HEADS // 2

    def pair_cols(p):
        return slice(p * KEY_TILE, (p + 1) * KEY_TILE)

    def half(x, hh):
        return x[:, hh * Q_BLOCK:(hh + 1) * Q_BLOCK]

    ACCS = 4

    def fold_rows(x, op):
        x = x.reshape(KEY_TILE // (ACCS * SUBLANES), ACCS, SUBLANES, LANES)
        return op(op(x, axis=0), axis=0)

    def idx_tile(j, blk, qit_ref, wt_ref, slot):
        dst_ref = iscbuf_ref.at[slot]
        keys = ki_ref[0, key_rows(j), :]
        wt = wt_ref[0, 0]
        acc = jnp.zeros((KEY_TILE, Q_BLOCK), F32)
        for p in range(n_pairs):
            s = jnp.dot(keys, qit_ref[0, 0, :, pair_cols(p)], preferred_element_type=F32)
            for hh in range(2):
                h = 2 * p + hh
                acc = acc + wt[h:h + 1, :] * jnp.maximum(half(s, hh), 0.0)
        key_limit = blk * Q_BLOCK + jnp.where(col < CHUNK, CHUNK, 2 * CHUNK)
        visible = j * KEY_TILE + row < key_limit
        dst_ref[j] = jnp.where(visible, acc, -jnp.inf)
        seen = jnp.where(visible, acc, 0.0)
        old = jnp.where(j == 0, 0.0, mom_ref[slot])
        mom_ref[slot] = old + jnp.stack([fold_rows(seen, jnp.sum), fold_rows(seen * seen, jnp.sum)])

    def tile_loop(n, body, carry):
        def pair(jj, c):
            return body(2 * jj + 1, body(2 * jj, c))
        carry = lax.fori_loop(0, n // 2, pair, carry)
        return lax.cond(n % 2 == 1, lambda c: body(n - 1, c), lambda c: c, carry)

    @pl.when(i == 0)
    def _():
        idx_tile(0, 0, qit0_ref, wt0_ref, 0)
        best = jnp.zeros((1, KEY_TILE), F32)
        for j in range(kvt_ref.shape[1]):
            x = kvt_ref[0, j].astype(F32)
            best = jnp.maximum(best, jnp.sum(x * x, axis=0, keepdims=True))
        kmax_ref[...] = jnp.broadcast_to(jnp.sqrt(jnp.max(best, axis=1, keepdims=True)), (SUBLANES, LANES))

    def rows_all(x, op):
        return jnp.broadcast_to(op(x, axis=0, keepdims=True), (SUBLANES, LANES))

    def tile_rows(j):
        return isct_ref[j].reshape(KEY_TILE // (ACCS * SUBLANES), ACCS, SUBLANES, LANES)

    def count_where(pred):
        def body(j, c):
            x = tile_rows(j)
            for r in range(x.shape[0]):
                c = jnp.where(pred(x[r]), c + 1.0, c)
            return c
        c = lax.fori_loop(0, n_tiles, body, jnp.zeros((ACCS, SUBLANES, LANES), F32))
        return rows_all(jnp.sum(c, axis=0), jnp.sum)

    kf = float(k_sel)

    def is_open(cnt, stalled):
        return jnp.logical_and(cnt > kf, stalled == 0.0)

    def any_lane(mask):
        return jnp.max(jnp.where(mask, 1.0, 0.0)) > 0.0

    lane8 = lax.broadcasted_iota(jnp.int32, (SUBLANES, LANES), 1)
    n_vis = (i * Q_BLOCK + jnp.where(lane8 < CHUNK, CHUNK, 2 * CHUNK)).astype(F32)
    mom = mom_ref[i % 2]
    mean = rows_all(mom[0], jnp.sum) / n_vis
    std = jnp.sqrt(jnp.maximum(rows_all(mom[1], jnp.sum) / n_vis - mean * mean, 0.0))
    z = zq_ref[0]
    takes_all = n_vis <= kf
    lo = jnp.where(takes_all, -FLOAT_BIG, mean + (z - BRACKET_HALF_WIDTH) * std)
    hi = jnp.where(takes_all, FLOAT_BIG, mean + (z + BRACKET_HALF_WIDTH) * std)
    cnt = count_where(lambda x: x >= lo[0:1])
    cnt_hi = count_where(lambda x: x >= hi[0:1])
    low_ok = cnt >= kf
    high_ok = cnt_hi < kf
    lo = jnp.where(low_ok, lo, -FLOAT_BIG)
    cnt = jnp.where(low_ok, cnt, n_vis)
    hi = jnp.where(high_ok, hi, FLOAT_BIG)
    cnt_hi = jnp.where(high_ok, cnt_hi, 0.0)

    def bis_step(c):
        lo, hi, cnt, cnt_hi, stalled = c
        mid = 0.5 * lo + 0.5 * hi
        cm = count_where(lambda x: x >= mid[0:1])
        active = is_open(cnt, stalled)
        noprog = jnp.logical_or(mid <= lo, mid >= hi)
        move = jnp.logical_and(active, jnp.logical_not(noprog))
        up = jnp.logical_and(move, cm >= kf)
        down = jnp.logical_and(move, cm < kf)
        return (jnp.where(up, mid, lo), jnp.where(down, mid, hi), jnp.where(up, cm, cnt),
                jnp.where(down, cm, cnt_hi), jnp.where(jnp.logical_and(active, noprog), 1.0, stalled))

    def bisect(state, max_iters):
        def cond(c):
            return jnp.logical_and(c[5] < max_iters, any_lane(is_open(c[2], c[4])))

        def body(c):
            state = c[:5]
            for _ in range(BISECT_UNROLL):
                state = bis_step(state)
            return state + (c[5] + BISECT_UNROLL,)

        return lax.while_loop(cond, body, state + (jnp.int32(0),))[:5]

    def max_below(t):
        def body(j, c):
            x = tile_rows(j)
            return jnp.maximum(c, jnp.max(jnp.where(x < t, x, -jnp.inf), axis=0))
        c = lax.fori_loop(0, n_tiles, body, jnp.full((ACCS, SUBLANES, LANES), -jnp.inf, F32))
        return rows_all(jnp.max(c, axis=0), jnp.max)

    def walk(c):
        top, left = c
        return jnp.where(left > 0.0, max_below(top), top), jnp.maximum(left - 1.0, 0.0)

    state = (lo, hi, cnt, cnt_hi, jnp.zeros((SUBLANES, LANES), F32))
    for _ in range(BISECT_COARSE_ITERS):
        state = bis_step(state)
    lo, hi, cnt, cnt_hi, stalled = state
    walked = is_open(cnt, stalled)
    top, left = walk(walk((hi, jnp.where(walked, kf - cnt_hi, 0.0))))
    arrived = jnp.logical_and(walked, left == 0.0)
    lo = jnp.where(arrived, top, lo)
    hi = jnp.where(jnp.logical_and(walked, left > 0.0), top, hi)
    cnt = count_where(lambda x: x >= lo[0:1])

    def finish(state):
        thr8, _, cnt, _, _ = bisect(state, BISECT_MAX_ITERS)
        tied = cnt > kf

        @pl.when(any_lane(tied))
        def _():
            need = kf - count_where(lambda x: x > thr8[0:1])

            def body(j, seen):
                x = isct_ref[j]
                eq = jnp.where(x == thr8[0:1], 1.0, 0.0)
                inc = _cumsum_rows(eq)
                rank = inc - eq + seen[0:1]
                drop = jnp.logical_and(jnp.logical_and(tied[0:1], eq > 0.0), rank >= need[0:1])
                isct_ref[j] = jnp.where(drop, -jnp.inf, x)
                return seen + inc[KEY_TILE - 1:KEY_TILE]

            lax.fori_loop(0, n_tiles, body, jnp.zeros((SUBLANES, LANES), F32))

        return thr8

    thr8 = lax.cond(any_lane(cnt > kf), finish, lambda s: s[0], (lo, hi, cnt, cnt_hi, stalled))
    thr = thr8[0:1]

    n_far = jnp.maximum(last - 1 + odd, 0)
    next_slot = (i + 1) % 2

    zero_l = jnp.zeros((ATTN_HEADS, SUBLANES, LANES), F32)

    shift = [qn_ref[0, 0, h:h + 1, :] * kmax_ref[0:1, :] * BOUND_SLACK + (bmax_ref[h] + BOUND_SLACK)
             for h in range(ATTN_HEADS)]

    def sweep_tile(j, l, near):
        keys = kv_ref[0, key_rows(j), :]
        values_t = kvt_ref[0, j]
        neg = jnp.where(isct_ref[j] >= thr, 0.0, -jnp.inf)
        table = jnp.where(j == last, odd, 2)
        for p in range(n_pairs):
            s = jnp.dot(keys, qt[:, pair_cols(p)], preferred_element_type=F32)
            for hh in range(2):
                h = 2 * p + hh
                sh = half(s, hh) + neg
                sbuf_ref[j, h] = sh + tb_ref[table, h] if near else sh
        out = []
        for p in range(n_pairs):
            probs = []
            for hh in range(2):
                h = 2 * p + hh
                e = jnp.exp2(sbuf_ref[j, h] - shift[h])
                out.append(l[h] + fold_rows(e, jnp.sum))
                probs.append(e.astype(BF16))
            acc_ref[p] += jnp.dot(values_t, jnp.concatenate(probs, axis=1),
                                  preferred_element_type=F32)
        idx_tile(j, i + 1, qitn_ref, wtn_ref, next_slot)
        return jnp.stack(out)

    acc_ref[...] = jnp.zeros_like(acc_ref)
    l = tile_loop(n_far, lambda j, l: sweep_tile(j, l, False), zero_l)
    l = lax.fori_loop(n_far, n_tiles, lambda j, l: sweep_tile(j, l, True), l)
    l = jnp.sum(l, axis=1, keepdims=True)

    @pl.when(jnp.logical_and(odd == 1, i + 1 < n_blocks))
    def _():
        idx_tile(n_tiles, i + 1, qitn_ref, wtn_ref, next_slot)

    def p1_tile(j, m, near):
        keys = kv_ref[0, key_rows(j), :]
        neg = jnp.where(isct_ref[j] >= thr, 0.0, -jnp.inf)
        table = jnp.where(j == last, odd, 2)
        out = []
        for p in range(n_pairs):
            s = jnp.dot(keys, qt[:, pair_cols(p)], preferred_element_type=F32)
            for hh in range(2):
                h = 2 * p + hh
                sh = half(s, hh) + neg
                if near:
                    sh = sh + tb_ref[table, h]
                sbuf_ref[j, h] = sh
                out.append(jnp.maximum(m[h], fold_rows(sh, jnp.max)))
        return jnp.stack(out)

    def exact_softmax(_):
        m = jnp.full((ATTN_HEADS, SUBLANES, LANES), -jnp.inf, F32)
        m = tile_loop(n_far, lambda j, m: p1_tile(j, m, False), m)
        m = lax.fori_loop(n_far, n_tiles, lambda j, m: p1_tile(j, m, True), m)
        m = jnp.max(m, axis=1, keepdims=True)
        acc_ref[...] = jnp.zeros_like(acc_ref)

        def p2_body(j, l):
            values_t = kvt_ref[0, j]
            out = []
            for p in range(n_pairs):
                probs = []
                for hh in range(2):
                    h = 2 * p + hh
                    e = jnp.exp2(sbuf_ref[j, h] - m[h])
                    out.append(l[h] + fold_rows(e, jnp.sum))
                    probs.append(e.astype(BF16))
                acc_ref[p] += jnp.dot(values_t, jnp.concatenate(probs, axis=1),
                                      preferred_element_type=F32)
            return jnp.stack(out)

        return jnp.sum(tile_loop(n_tiles, p2_body, zero_l), axis=1, keepdims=True)

    underflowed = jnp.max(jnp.where(l > UNDERFLOW_GUARD, 0.0, 1.0)) > 0.0
    l = lax.cond(underflowed, exact_softmax, lambda l: l, l)

    outs = [(acc_ref[h // 2, :, (h % 2) * Q_BLOCK:(h % 2 + 1) * Q_BLOCK] / l[h]).T.astype(BF16)
            for h in range(ATTN_HEADS)]
    for p in range(ATTN_HEADS // 2):
        pair = jnp.concatenate(outs[2 * p:2 * p + 2], axis=1)
        o_ref[0, :, p * LANES:(p + 1) * LANES] = jnp.dot(
            pair, wuv_ref[p], preferred_element_type=F32).astype(BF16)


def _pack_w_uv(w_uv):
    eye = jnp.eye(2, dtype=w_uv.dtype)
    w = w_uv.reshape(ATTN_HEADS // 2, 2, KV_RANK, 1, HEAD_DIM) * eye[None, :, None, :, None]
    return w.reshape(ATTN_HEADS // 2, 2 * KV_RANK, 2 * HEAD_DIM).astype(BF16)


def _dsa(qt, qnorm, qit, wt, ki, kv, kvt, tables, bias_max, wuv):
    bsz, seq, _ = kv.shape
    nkt = seq // KEY_TILE
    k_sel = min(TOPK_MAX, seq // 4)
    nb = seq // Q_BLOCK
    slab = (1, 1, LANES, ATTN_HEADS * Q_BLOCK)
    wslab = (1, 1, IDX_HEADS, Q_BLOCK)

    def first(b, i):
        return (b, 0, 0, 0)

    def following(b, i):
        return (b, jnp.minimum(i + 1, nb - 1), 0, 0)

    seq_spec = pl.BlockSpec((1, seq, LANES), lambda b, i: (b, 0, 0))
    n_vis = (np.arange(nb)[:, None] * Q_BLOCK + np.where(np.arange(Q_BLOCK) < CHUNK, CHUNK, 2 * CHUNK)[None, :])
    quantile = np.array([[statistics.NormalDist().inv_cdf(1.0 - k_sel / n) if n > k_sel else 0.0 for n in r]
                         for r in n_vis], np.float32)
    zq = jnp.asarray(np.broadcast_to(quantile[:, None, :], (nb, SUBLANES, Q_BLOCK)))
    return pl.pallas_call(
        functools.partial(_dsa_kernel, k_sel=k_sel, n_blocks=nb),
        grid=(bsz, nb),
        in_specs=[pl.BlockSpec(slab, lambda b, i: (b, i, 0, 0)),
                  pl.BlockSpec(wslab, lambda b, i: (b, i, 0, 0)),
                  pl.BlockSpec((1, SUBLANES, Q_BLOCK), lambda b, i: (i, 0, 0)),
                  pl.BlockSpec(slab, first), pl.BlockSpec(wslab, first),
                  pl.BlockSpec(slab, following), pl.BlockSpec(wslab, following),
                  seq_spec, seq_spec,
                  pl.BlockSpec((1, nkt, LANES, KEY_TILE), lambda b, i: (b, 0, 0, 0)),
                  _const_spec((3, ATTN_HEADS, KEY_TILE, Q_BLOCK)),
                  pl.BlockSpec(memory_space=pltpu.SMEM),
                  _const_spec((ATTN_HEADS // 2, 2 * KV_RANK, 2 * HEAD_DIM))],
        out_specs=pl.BlockSpec((1, Q_BLOCK, ATTN_HEADS * HEAD_DIM), lambda b, i: (b, i, 0)),
        out_shape=jax.ShapeDtypeStruct((bsz, seq, ATTN_HEADS * HEAD_DIM), BF16),
        scratch_shapes=[pltpu.VMEM((2, nkt, KEY_TILE, Q_BLOCK), F32),
                        pltpu.VMEM((2, 2, SUBLANES, LANES), F32),
                        pltpu.VMEM((nkt, ATTN_HEADS, KEY_TILE, Q_BLOCK), F32),
                        pltpu.VMEM((ATTN_HEADS // 2, KV_RANK, 2 * Q_BLOCK), F32),
                        pltpu.VMEM((SUBLANES, LANES), F32)],
        compiler_params=_cparams(("parallel", "arbitrary")),
        name="dsa",
    )(qt, qnorm, zq, qit, wt, qit, wt, ki, kv, kvt, tables, bias_max, wuv)


def _pack3(v):
    lane = lax.broadcasted_iota(jnp.int32, v.shape, 1)
    v = jnp.where(lane < SSD_HEADS, v, 0.0)
    hi = v.astype(BF16).astype(F32)
    r = v - hi
    mid = r.astype(BF16).astype(F32)
    lo = r - mid
    return (hi + pltpu.roll(mid, SSD_HEADS, axis=1) + pltpu.roll(lo, 2 * SSD_HEADS, axis=1)).astype(BF16)


def _cumsum_rows(x):
    n = x.shape[0]
    r = lax.broadcasted_iota(jnp.int32, x.shape, 0)
    s = 1
    while s < n:
        x = x + jnp.where(r >= s, pltpu.roll(x, s, axis=0), 0.0)
        s *= 2
    return x


def _shift_rows(x, s):
    r = pltpu.roll(x, s, axis=1)
    prev = jnp.concatenate([r[-1:], r[:-1]], axis=0)
    sub = lax.broadcasted_iota(jnp.int32, x.shape, 1)
    return jnp.where(sub >= s, r, prev)


def _ssd_kernel(z_ref, xbc_ref, dt_ref, cw_ref, cb_ref, dtb_ref, alog_ref, dsk_ref, nw_ref, e_ref,
                y_ref, tail_ref, u_ref, g_ref, state_ref):
    nq = SSD_Q

    @pl.when(pl.program_id(1) == 0)
    def _():
        tail_ref[...] = jnp.zeros_like(tail_ref)
        state_ref[...] = jnp.zeros_like(state_ref)

    assert CONV_W == 4
    for blk in range(CONV_DIM // CONV_BLOCK):
        cols = slice(blk * CONV_BLOCK, (blk + 1) * CONV_BLOCK)
        ext = jnp.concatenate([tail_ref[:, cols], xbc_ref[0, :, cols].astype(F32)], axis=0)
        ext = ext.reshape(1 + nq // SUBLANES, SUBLANES, CONV_BLOCK)
        s1 = _shift_rows(ext, 1)
        a = cw_ref[3:4, cols] * ext + cw_ref[2:3, cols] * s1 + cb_ref[:, cols]
        b = cw_ref[1:2, cols] * ext + cw_ref[0:1, cols] * s1
        conv = (a + _shift_rows(b, 2))[1:].reshape(nq, CONV_BLOCK)
        u_ref[:, cols] = _silu(conv)
        tail_ref[:, cols] = ext[nq // SUBLANES]

    t = dt_ref[0] + dtb_ref[...]
    dt = jnp.maximum(t, 0.0) + jnp.log1p(jnp.exp(-jnp.abs(t)))
    a2 = _cumsum_rows(dt * (-jnp.exp(alog_ref[...]))) * LOG2E
    a2_t = a2.T
    dt_p = _pack3(dt)
    dec_p = _pack3(dt * jnp.exp2(a2[nq - 1:nq, :] - a2))
    expa_p = _pack3(jnp.exp2(a2))

    r = lax.broadcasted_iota(jnp.int32, (nq, nq), 0)
    c = lax.broadcasted_iota(jnp.int32, (nq, nq), 1)
    causal = r >= c
    lane = lax.broadcasted_iota(jnp.int32, (nq, LANES), 1)
    heads_per_group = SSD_HEADS // SSD_GROUPS
    gw = heads_per_group * SSD_HEADDIM
    b_col = D_INNER
    c_col = D_INNER + SSD_GROUPS * D_STATE
    ssq = jnp.zeros((nq, LANES), F32)

    for g in range(SSD_GROUPS):
        gcols = slice(g * gw, (g + 1) * gw)
        eg = e_ref[:, gcols]
        dt_e = jnp.dot(dt_p, eg, preferred_element_type=F32)
        dec_e = jnp.dot(dec_p, eg, preferred_element_type=F32)
        expa_e = jnp.dot(expa_p, eg, preferred_element_type=F32)
        xs = u_ref[:, gcols]
        xdt_b = (xs * dt_e).astype(BF16)
        xdec_b = (xs * dec_e).astype(BF16)
        cg = u_ref[:, c_col + g * D_STATE:c_col + (g + 1) * D_STATE].astype(BF16)
        bgt = u_ref[:, b_col + g * D_STATE:b_col + (g + 1) * D_STATE].T.astype(BF16)
        cb = jnp.dot(cg, bgt, preferred_element_type=F32)
        prev = state_ref[g]
        y_off = jnp.dot(cg, prev.astype(BF16), preferred_element_type=F32) * expa_e
        pairs = []
        for pp in range(heads_per_group // 2):
            xp = xdt_b[:, pp * LANES:(pp + 1) * LANES]
            yh = []
            for hh in range(2):
                h = heads_per_group * g + 2 * pp + hh
                seg = a2[:, h:h + 1] - a2_t[h:h + 1, :]
                m = (cb * jnp.where(causal, jnp.exp2(seg), 0.0)).astype(BF16)
                yh.append(jnp.dot(m, xp, preferred_element_type=F32))
            pairs.append(jnp.where(lane < SSD_HEADDIM, yh[0], yh[1]))
        y = jnp.concatenate(pairs, axis=1) + y_off + dsk_ref[:, gcols] * xs
        gated = y * _silu(z_ref[0, :, gcols].astype(F32))
        g_ref[:, gcols] = gated
        sq = gated * gated
        ssq = ssq + sq[:, :LANES] + sq[:, LANES:]
        new = jnp.dot(bgt, xdec_b, preferred_element_type=F32)
        state_ref[g] = prev * expa_e[nq - 1:nq, :] + new
    scale = lax.rsqrt(jnp.sum(ssq, axis=-1, keepdims=True) * (1.0 / D_INNER) + EPS)
    y_ref[0] = (g_ref[...] * scale * nw_ref[...]).astype(BF16)


def _ssd(head, wide, conv_w, conv_b, dt_bias, a_log, d_skip, ssd_norm):
    bsz, seq, _ = head.shape
    nq = SSD_Q

    def pad_heads(v):
        return jnp.concatenate([v, jnp.zeros((LANES - SSD_HEADS,), F32)]).reshape(1, LANES)

    sel = np.concatenate([np.eye(SSD_HEADS)] * 3 + [np.zeros((LANES - 3 * SSD_HEADS, SSD_HEADS))], axis=0)
    e = jnp.asarray(np.kron(sel, np.ones((1, SSD_HEADDIM))), BF16)
    return pl.pallas_call(
        _ssd_kernel,
        grid=(bsz, seq // nq),
        in_specs=[pl.BlockSpec((1, nq, D_INNER), lambda b, i: (b, i, WIDE_Z // D_INNER)),
                  pl.BlockSpec((1, nq, CONV_DIM), lambda b, i: (b, i, WIDE_XBC // CONV_DIM)),
                  pl.BlockSpec((1, nq, LANES), lambda b, i: (b, i, HEAD_DT // LANES)),
                  _const_spec((CONV_W, CONV_DIM)), _const_spec((1, CONV_DIM)),
                  _const_spec((1, LANES)), _const_spec((1, LANES)),
                  _const_spec((1, D_INNER)), _const_spec((1, D_INNER)),
                  _const_spec((LANES, D_INNER))],
        out_specs=pl.BlockSpec((1, nq, D_INNER), lambda b, i: (b, i, 0)),
        out_shape=jax.ShapeDtypeStruct((bsz, seq, D_INNER), BF16),
        scratch_shapes=[pltpu.VMEM((SUBLANES, CONV_DIM), F32),
                        pltpu.VMEM((nq, CONV_DIM), F32),
                        pltpu.VMEM((nq, D_INNER), F32),
                        pltpu.VMEM((SSD_GROUPS, D_STATE, 4 * SSD_HEADDIM), F32)],
        compiler_params=_cparams(("parallel", "arbitrary")),
        name="ssd",
    )(wide, wide, head, conv_w, conv_b.reshape(1, CONV_DIM), pad_heads(dt_bias), pad_heads(a_log),
      jnp.repeat(d_skip, SSD_HEADDIM).reshape(1, D_INNER), ssd_norm.reshape(1, D_INNER), e)


def _mix_kernel(ao_ref, sy_ref, ga_ref, gb_ref, x_ref, mod_ref, nw_ref, woa_ref, wos_ref, wout_ref, o_ref):
    ya = jnp.dot(ao_ref[0], woa_ref[...], preferred_element_type=F32)
    yb = jnp.dot(sy_ref[0], wos_ref[...], preferred_element_type=F32)
    mix = jax.nn.sigmoid(ga_ref[0].astype(F32)) * ya + jax.nn.sigmoid(gb_ref[0].astype(F32)) * yb
    m2 = jnp.dot(mix.astype(BF16), wout_ref[...], preferred_element_type=F32)
    o_ref[0] = x_ref[0] + mod_ref[0][2:3] * _rms(m2, nw_ref[...])


def _mix(attn_o, ssd_y, wide, x, mod3, post_norm, w_o_attn, w_o_ssd, w_out):
    bsz, seq, _ = x.shape
    tm = min(seq, 512)

    def rows(width, col_block=0):
        return pl.BlockSpec((1, tm, width), lambda b, i: (b, i, col_block))

    return pl.pallas_call(
        _mix_kernel,
        grid=(bsz, seq // tm),
        in_specs=[rows(D_MODEL), rows(D_INNER), rows(D_MODEL, WIDE_GA // D_MODEL), rows(D_MODEL, WIDE_GB // D_MODEL),
                  rows(D_MODEL), pl.BlockSpec((1, 6, D_MODEL), lambda b, i: (b, 0, 0)),
                  _const_spec((1, D_MODEL)), _const_spec((D_MODEL, D_MODEL)),
                  _const_spec((D_INNER, D_MODEL)), _const_spec((D_MODEL, D_MODEL))],
        out_specs=rows(D_MODEL),
        out_shape=jax.ShapeDtypeStruct((bsz, seq, D_MODEL), F32),
        compiler_params=_cparams(("parallel", "parallel")),
        name="mix",
    )(attn_o, ssd_y, wide, wide, x, mod3, post_norm.reshape(1, D_MODEL),
      w_o_attn.astype(BF16), w_o_ssd.astype(BF16), w_out.astype(BF16))


def _ffn_kernel(x_ref, mod_ref, nw1_ref, nw2_ref, wg_ref, wu_ref, wo_ref, o_ref):
    x = x_ref[0]
    m = mod_ref[0]
    h2 = (_rms(x, nw1_ref[...]) * (1.0 + m[4:5]) + m[3:4]).astype(BF16)
    ug = jnp.dot(h2, wg_ref[...], preferred_element_type=F32)
    uu = jnp.dot(h2, wu_ref[...], preferred_element_type=F32)
    f = jnp.dot((_silu(ug) * uu).astype(BF16), wo_ref[...], preferred_element_type=F32)
    o_ref[0] = x + m[5:6] * _rms(f, nw2_ref[...])


def _ffn(x, mod3, pre_norm, post_norm, w_ffn_in, w_ffn_out):
    bsz, seq, _ = x.shape
    tm = min(seq, 512)
    rows = pl.BlockSpec((1, tm, D_MODEL), lambda b, i: (b, i, 0))
    w_in = w_ffn_in.astype(BF16)

    def half(k):
        return pl.BlockSpec((D_MODEL, D_FF), lambda b, i: (0, k), pipeline_mode=pl.Buffered(1))

    return pl.pallas_call(
        _ffn_kernel,
        grid=(bsz, seq // tm),
        in_specs=[rows, pl.BlockSpec((1, 6, D_MODEL), lambda b, i: (b, 0, 0)),
                  _const_spec((1, D_MODEL)), _const_spec((1, D_MODEL)),
                  half(0), half(1), _const_spec((D_FF, D_MODEL))],
        out_specs=rows,
        out_shape=jax.ShapeDtypeStruct((bsz, seq, D_MODEL), F32),
        compiler_params=_cparams(("parallel", "parallel")),
        name="ffn",
    )(x, mod3, pre_norm.reshape(1, D_MODEL), post_norm.reshape(1, D_MODEL), w_in, w_in, w_ffn_out.astype(BF16))


def kernel(x, c, positions, ada_w, ada_b, pre_norm_mix, post_norm_mix, pre_norm_ffn, post_norm_ffn, w_in, q_norm, kv_norm, w_uq, w_uv, rel_bias, w_qidx, kidx_norm, conv_w, conv_b, dt_bias, a_log, d_skip, ssd_norm, w_o_attn, w_o_ssd, w_out, w_ffn_in, w_ffn_out):
    del positions
    bsz, seq, _ = x.shape
    assert seq % (2 * KEY_TILE) == 0 and x.shape[-1] == D_MODEL
    mod3 = _mod(c, ada_w, ada_b).reshape(bsz, 6, D_MODEL)
    head, wide = _inproj(x, mod3, pre_norm_mix, _pack_w_in(w_in))
    qt, qit, kv, kvt, ki, wt, qnorm = _prep(head, q_norm, kv_norm, kidx_norm, w_uq, w_qidx)
    bias_max = jnp.max(rel_bias - rel_bias[NUM_BUCKETS // 2 - 1], axis=0) * LOG2E
    attn_o = _dsa(qt, qnorm, qit, wt, ki, kv, kvt, _bias_tables(rel_bias), bias_max, _pack_w_uv(w_uv))
    ssd_y = _ssd(head, wide, conv_w, conv_b, dt_bias, a_log, d_skip, ssd_norm)
    x1 = _mix(attn_o, ssd_y, wide, x, mod3, post_norm_mix, w_o_attn, w_o_ssd, w_out)
    return _ffn(x1, mod3, pre_norm_ffn, post_norm_ffn, w_ffn_in, w_ffn_out)
```

```python
import functools
import math
import statistics

import numpy as np
import jax
import jax.numpy as jnp
from jax import lax
from jax.experimental import pallas as pl
from jax.experimental.pallas import tpu as pltpu

F32 = jnp.float32
BF16 = jnp.bfloat16

D_MODEL = 1024
CHUNK = 64
Q_BLOCK = 128
EPS = 1e-6
ATTN_HEADS = 16
HEAD_DIM = 64
Q_RANK = 256
KV_RANK = 128
IDX_HEADS = 16
IDX_DIM = 64
TOPK_MAX = 256
NUM_BUCKETS = 32
MAX_DISTANCE = 128
D_INNER = 2 * D_MODEL
SSD_HEADDIM = 64
SSD_HEADS = D_INNER // SSD_HEADDIM
SSD_GROUPS = 8
D_STATE = 128
CONV_W = 4
CONV_DIM = D_INNER + 2 * SSD_GROUPS * D_STATE
D_FF = -(-8 * D_MODEL // (3 * 256)) * 256

LANES = 128
SUBLANES = 8
KEY_TILE = 256
SSD_Q = 128
CONV_BLOCK = 512
VMEM_LIMIT = 56 * 1024 * 1024
BISECT_MAX_ITERS = 320
BISECT_UNROLL = 4
BISECT_COARSE_ITERS = 10
BRACKET_HALF_WIDTH = 0.3
FLOAT_BIG = 3.0e38
LOG2E = math.log2(math.e)
BOUND_SLACK = 1.0 + 2.0 ** -6
UNDERFLOW_GUARD = 2.0 ** -80

HEAD_COLS = 1024
HEAD_KV, HEAD_KIDX, HEAD_W, HEAD_DT = 256, 384, 512, 640
WIDE_XBC, WIDE_Z, WIDE_GA, WIDE_GB = 0, 4096, 6144, 7168
WIDE_COLS = 8192
INPROJ_TN = 1024


def _cparams(sem):
    return pltpu.CompilerParams(dimension_semantics=sem, vmem_limit_bytes=VMEM_LIMIT)


def _const_spec(shape):
    nd = len(shape)
    return pl.BlockSpec(shape, lambda *_: (0,) * nd, pipeline_mode=pl.Buffered(1))


def _rms(x, w, n=None):
    n = x.shape[-1] if n is None else n
    return x * lax.rsqrt(jnp.sum(x * x, axis=-1, keepdims=True) * (1.0 / n) + EPS) * w


def _silu(x):
    h = 0.5 * x
    return h + h * jnp.tanh(h)


def _mod_kernel(c_ref, w_ref, b_ref, o_ref):
    c = c_ref[...]
    s = _silu(c).astype(BF16)
    o_ref[...] = jnp.dot(s, w_ref[...].astype(BF16), preferred_element_type=F32) + b_ref[...]


def _mod(c, ada_w, ada_b):
    bsz = c.shape[0]
    return pl.pallas_call(
        _mod_kernel,
        grid=(6,),
        in_specs=[pl.BlockSpec((bsz, D_MODEL), lambda j: (0, 0)),
                  pl.BlockSpec((D_MODEL, D_MODEL), lambda j: (0, j)),
                  pl.BlockSpec((1, D_MODEL), lambda j: (0, j))],
        out_specs=pl.BlockSpec((bsz, D_MODEL), lambda j: (0, j)),
        out_shape=jax.ShapeDtypeStruct((bsz, 6 * D_MODEL), F32),
        compiler_params=_cparams(("parallel",)),
        name="mod",
    )(c, ada_w, ada_b.reshape(1, 6 * D_MODEL))


def _t5_bucket_np(rel):
    half = NUM_BUCKETS // 2
    max_exact = half // 2
    side = np.where(rel > 0, half, 0)
    n = np.abs(rel)
    large = max_exact + (np.log(np.maximum(n, max_exact).astype(np.float64) / max_exact)
                         / math.log(MAX_DISTANCE / max_exact) * (half - max_exact)).astype(np.int64)
    large = np.minimum(large, half - 1)
    return (side + np.where(n < max_exact, n, large)).astype(np.int32)


def _bias_kernel(idx_ref, rb_ref, o_ref):
    h = pl.program_id(0)
    idx = idx_ref[...]
    far = rb_ref[h, NUM_BUCKETS // 2 - 1]
    acc = jnp.zeros(idx.shape, F32)
    for b in range(NUM_BUCKETS):
        acc = jnp.where(idx == b, (rb_ref[h, b] - far) * LOG2E, acc)
    o_ref[0] = acc


def _bias_tables(rel_bias):
    kk = np.arange(2 * KEY_TILE)[:, None]
    ql = np.arange(Q_BLOCK)[None, :]
    idx = jnp.asarray(_t5_bucket_np(kk - KEY_TILE - ql))
    t = pl.pallas_call(
        _bias_kernel,
        grid=(ATTN_HEADS,),
        in_specs=[pl.BlockSpec((2 * KEY_TILE, Q_BLOCK), lambda h: (0, 0)),
                  pl.BlockSpec(memory_space=pltpu.SMEM)],
        out_specs=pl.BlockSpec((1, 2 * KEY_TILE, Q_BLOCK), lambda h: (h, 0, 0)),
        out_shape=jax.ShapeDtypeStruct((ATTN_HEADS, 2 * KEY_TILE, Q_BLOCK), F32),
        compiler_params=_cparams(("arbitrary",)),
        name="bias",
    )(idx, rel_bias.T)
    return jnp.stack([t[:, 256:512], t[:, 128:384], t[:, 0:256]])


def _inproj_kernel(x_ref, mod_ref, nw_ref, w_ref, head_ref, wide_ref):
    m = mod_ref[0]
    hn = (_rms(x_ref[0], nw_ref[...]) * (1.0 + m[1:2]) + m[0:1]).astype(BF16)
    head_ref[0] = _dot_nt(hn, w_ref[0:HEAD_COLS, :])
    for c in range(WIDE_COLS // INPROJ_TN):
        rows = slice(HEAD_COLS + c * INPROJ_TN, HEAD_COLS + (c + 1) * INPROJ_TN)
        wide_ref[0, :, c * INPROJ_TN:(c + 1) * INPROJ_TN] = _dot_nt(hn, w_ref[rows, :]).astype(BF16)


def _pack_w_in(w_in):
    sizes = [Q_RANK, KV_RANK, IDX_DIM, IDX_HEADS, D_INNER, CONV_DIM, SSD_HEADS, D_MODEL, D_MODEL]
    offs = np.cumsum([0] + sizes)
    wt = w_in.T
    q, kv, ki, wi, z, xbc, dt, ga, gb = [wt[offs[i]:offs[i + 1]] for i in range(9)]

    def zr(n):
        return jnp.zeros((n, D_MODEL), w_in.dtype)

    return jnp.concatenate([q, kv, ki, zr(LANES - IDX_DIM), wi, zr(LANES - IDX_HEADS),
                            dt, zr(LANES - SSD_HEADS), zr(2 * LANES), xbc, z, ga, gb], axis=0).astype(BF16)


def _dot_nt(a, b):
    return lax.dot_general(a, b, (((1,), (1,)), ((), ())), preferred_element_type=F32)


def _inproj(x, mod3, pre_norm, w_packed):
    bsz, seq, _ = x.shape
    tm = min(seq, 512)
    return pl.pallas_call(
        _inproj_kernel,
        grid=(bsz, seq // tm),
        in_specs=[pl.BlockSpec((1, tm, D_MODEL), lambda b, i: (b, i, 0)),
                  pl.BlockSpec((1, 6, D_MODEL), lambda b, i: (b, 0, 0)),
                  _const_spec((1, D_MODEL)),
                  _const_spec((HEAD_COLS + WIDE_COLS, D_MODEL))],
        out_specs=[pl.BlockSpec((1, tm, HEAD_COLS), lambda b, i: (b, i, 0)),
                   pl.BlockSpec((1, tm, WIDE_COLS), lambda b, i: (b, i, 0))],
        out_shape=[jax.ShapeDtypeStruct((bsz, seq, HEAD_COLS), F32),
                   jax.ShapeDtypeStruct((bsz, seq, WIDE_COLS), BF16)],
        compiler_params=_cparams(("parallel", "parallel")),
        name="inproj",
    )(x, mod3, pre_norm.reshape(1, D_MODEL), w_packed)


def _prep_kernel(p_ref, qn_ref, kvn_ref, kin_ref, wuqt_ref, wqit_ref,
                 qt_ref, qit_ref, kv_ref, kvt_ref, ki_ref, wt_ref, qnorm_ref, *, tc):
    p = p_ref[0]
    qnt = _rms(p[:, :Q_RANK], qn_ref[...]).T.astype(BF16)
    qf = jnp.dot(wuqt_ref[...], qnt, preferred_element_type=F32) * (KV_RANK ** -0.5 * LOG2E)
    qt = qf.astype(BF16)
    qnorm = jnp.sqrt(jnp.sum((qf * qf).reshape(ATTN_HEADS, KV_RANK, tc), axis=1))
    qit = jnp.dot(wqit_ref[...], qnt, preferred_element_type=F32).astype(BF16)
    wt = (p[:, HEAD_W:HEAD_W + LANES] * (IDX_HEADS ** -0.5 * IDX_DIM ** -0.5)).T
    for blk in range(tc // Q_BLOCK):
        cols = slice(blk * Q_BLOCK, (blk + 1) * Q_BLOCK)
        for h in range(ATTN_HEADS):
            qt_ref[0, blk, :, h * LANES:(h + 1) * LANES] = qt[h * LANES:(h + 1) * LANES, cols]
            qit_ref[0, blk, :, h * LANES:(h + 1) * LANES] = qit[h * LANES:(h + 1) * LANES, cols]
        wt_ref[0, blk] = wt[0:IDX_HEADS, cols]
        qnorm_ref[0, blk] = qnorm[:, cols]
    kv = _rms(p[:, HEAD_KV:HEAD_KV + KV_RANK], kvn_ref[...])
    kv_ref[0] = kv.astype(BF16)
    kvt = kv.T
    for c in range(tc // KEY_TILE):
        kvt_ref[0, c] = kvt[:, c * KEY_TILE:(c + 1) * KEY_TILE].astype(BF16)
    ki_ref[0] = _rms(p[:, HEAD_KIDX:HEAD_KIDX + LANES], kin_ref[...], n=IDX_DIM).astype(BF16)


def _prep(proj, q_norm, kv_norm, kidx_norm, w_uq, w_qidx):
    bsz, seq, _ = proj.shape
    tc = min(seq, 512)
    wqi = w_qidx.reshape(Q_RANK, IDX_HEADS, IDX_DIM)
    wqit = jnp.concatenate([wqi, jnp.zeros_like(wqi)], axis=-1).reshape(Q_RANK, IDX_HEADS * LANES).T.astype(BF16)
    kin = jnp.concatenate([kidx_norm, jnp.zeros((LANES - IDX_DIM,), F32)]).reshape(1, LANES)
    nb = seq // Q_BLOCK
    slab = jax.ShapeDtypeStruct((bsz, nb, LANES, ATTN_HEADS * Q_BLOCK), BF16)
    slab_spec = pl.BlockSpec((1, tc // Q_BLOCK, LANES, ATTN_HEADS * Q_BLOCK), lambda b, i: (b, i, 0, 0))
    row_spec = pl.BlockSpec((1, tc, LANES), lambda b, i: (b, i, 0))
    return pl.pallas_call(
        functools.partial(_prep_kernel, tc=tc),
        grid=(bsz, seq // tc),
        in_specs=[pl.BlockSpec((1, tc, 1024), lambda b, i: (b, i, 0)),
                  _const_spec((1, Q_RANK)), _const_spec((1, KV_RANK)), _const_spec((1, LANES)),
                  _const_spec((ATTN_HEADS * KV_RANK, Q_RANK)), _const_spec((IDX_HEADS * LANES, Q_RANK))],
        out_specs=[slab_spec, slab_spec, row_spec,
                   pl.BlockSpec((1, tc // KEY_TILE, LANES, KEY_TILE), lambda b, i: (b, i, 0, 0)),
                   row_spec,
                   pl.BlockSpec((1, tc // Q_BLOCK, IDX_HEADS, Q_BLOCK), lambda b, i: (b, i, 0, 0)),
                   pl.BlockSpec((1, tc // Q_BLOCK, ATTN_HEADS, Q_BLOCK), lambda b, i: (b, i, 0, 0))],
        out_shape=[slab, slab, jax.ShapeDtypeStruct((bsz, seq, LANES), BF16),
                   jax.ShapeDtypeStruct((bsz, seq // KEY_TILE, LANES, KEY_TILE), BF16),
                   jax.ShapeDtypeStruct((bsz, seq, LANES), BF16),
                   jax.ShapeDtypeStruct((bsz, nb, IDX_HEADS, Q_BLOCK), F32),
                   jax.ShapeDtypeStruct((bsz, nb, ATTN_HEADS, Q_BLOCK), F32)],
        compiler_params=_cparams(("parallel", "parallel")),
        name="prep",
    )(proj, q_norm.reshape(1, Q_RANK), kv_norm.reshape(1, KV_RANK), kin, w_uq.T.astype(BF16), wqit)


def _dsa_kernel(qt_ref, qn_ref, zq_ref, qit0_ref, wt0_ref, qitn_ref, wtn_ref, ki_ref, kv_ref, kvt_ref, tb_ref,
                bmax_ref, wuv_ref, o_ref, iscbuf_ref, mom_ref, sbuf_ref, acc_ref, kmax_ref, *, k_sel, n_blocks):
    i = pl.program_id(1)
    last = i // 2
    odd = i % 2
    n_tiles = last + 1
    isct_ref = iscbuf_ref.at[i % 2]
    qt = qt_ref[0, 0]
    row = lax.broadcasted_iota(jnp.int32, (KEY_TILE, Q_BLOCK), 0)
    col = lax.broadcasted_iota(jnp.int32, (KEY_TILE, Q_BLOCK), 1)

    def key_rows(j):
        return pl.ds(pl.multiple_of(j * KEY_TILE, KEY_TILE), KEY_TILE)

    n_pairs = ATTN_HEADS // 2

    def pair_cols(p):
        return slice(p * KEY_TILE, (p + 1) * KEY_TILE)

    def half(x, hh):
        return x[:, hh * Q_BLOCK:(hh + 1) * Q_BLOCK]

    ACCS = 4

    def fold_rows(x, op):
        x = x.reshape(KEY_TILE // (ACCS * SUBLANES), ACCS, SUBLANES, LANES)
        return op(op(x, axis=0), axis=0)

    def idx_tile(j, blk, qit_ref, wt_ref, slot):
        dst_ref = iscbuf_ref.at[slot]
        keys = ki_ref[0, key_rows(j), :]
        wt = wt_ref[0, 0]
        acc = jnp.zeros((KEY_TILE, Q_BLOCK), F32)
        for p in range(n_pairs):
            s = jnp.dot(keys, qit_ref[0, 0, :, pair_cols(p)], preferred_element_type=F32)
            for hh in range(2):
                h = 2 * p + hh
                acc = acc + wt[h:h + 1, :] * jnp.maximum(half(s, hh), 0.0)
        key_limit = blk * Q_BLOCK + jnp.where(col < CHUNK, CHUNK, 2 * CHUNK)
        visible = j * KEY_TILE + row < key_limit
        dst_ref[j] = jnp.where(visible, acc, -jnp.inf)
        seen = jnp.where(visible, acc, 0.0)
        old = jnp.where(j == 0, 0.0, mom_ref[slot])
        mom_ref[slot] = old + jnp.stack([fold_rows(seen, jnp.sum), fold_rows(seen * seen, jnp.sum)])

    def tile_loop(n, body, carry):
        def pair(jj, c):
            return body(2 * jj + 1, body(2 * jj, c))
        carry = lax.fori_loop(0, n // 2, pair, carry)
        return lax.cond(n % 2 == 1, lambda c: body(n - 1, c), lambda c: c, carry)

    @pl.when(i == 0)
    def _():
        idx_tile(0, 0, qit0_ref, wt0_ref, 0)
        best = jnp.zeros((1, KEY_TILE), F32)
        for j in range(kvt_ref.shape[1]):
            x = kvt_ref[0, j].astype(F32)
            best = jnp.maximum(best, jnp.sum(x * x, axis=0, keepdims=True))
        kmax_ref[...] = jnp.broadcast_to(jnp.sqrt(jnp.max(best, axis=1, keepdims=True)), (SUBLANES, LANES))

    def rows_all(x, op):
        return jnp.broadcast_to(op(x, axis=0, keepdims=True), (SUBLANES, LANES))

    def tile_rows(j):
        return isct_ref[j].reshape(KEY_TILE // (ACCS * SUBLANES), ACCS, SUBLANES, LANES)

    def count_where(pred):
        def body(j, c):
            x = tile_rows(j)
            for r in range(x.shape[0]):
                c = jnp.where(pred(x[r]), c + 1.0, c)
            return c
        c = lax.fori_loop(0, n_tiles, body, jnp.zeros((ACCS, SUBLANES, LANES), F32))
        return rows_all(jnp.sum(c, axis=0), jnp.sum)

    kf = float(k_sel)

    def is_open(cnt, stalled):
        return jnp.logical_and(cnt > kf, stalled == 0.0)

    def any_lane(mask):
        return jnp.max(jnp.where(mask, 1.0, 0.0)) > 0.0

    lane8 = lax.broadcasted_iota(jnp.int32, (SUBLANES, LANES), 1)
    n_vis = (i * Q_BLOCK + jnp.where(lane8 < CHUNK, CHUNK, 2 * CHUNK)).astype(F32)
    mom = mom_ref[i % 2]
    mean = rows_all(mom[0], jnp.sum) / n_vis
    std = jnp.sqrt(jnp.maximum(rows_all(mom[1], jnp.sum) / n_vis - mean * mean, 0.0))
    z = zq_ref[0]
    takes_all = n_vis <= kf
    lo = jnp.where(takes_all, -FLOAT_BIG, mean + (z - BRACKET_HALF_WIDTH) * std)
    hi = jnp.where(takes_all, FLOAT_BIG, mean + (z + BRACKET_HALF_WIDTH) * std)
    cnt = count_where(lambda x: x >= lo[0:1])
    cnt_hi = count_where(lambda x: x >= hi[0:1])
    low_ok = cnt >= kf
    high_ok = cnt_hi < kf
    lo = jnp.where(low_ok, lo, -FLOAT_BIG)
    cnt = jnp.where(low_ok, cnt, n_vis)
    hi = jnp.where(high_ok, hi, FLOAT_BIG)
    cnt_hi = jnp.where(high_ok, cnt_hi, 0.0)

    def bis_step(c):
        lo, hi, cnt, cnt_hi, stalled = c
        mid = 0.5 * lo + 0.5 * hi
        cm = count_where(lambda x: x >= mid[0:1])
        active = is_open(cnt, stalled)
        noprog = jnp.logical_or(mid <= lo, mid >= hi)
        move = jnp.logical_and(active, jnp.logical_not(noprog))
        up = jnp.logical_and(move, cm >= kf)
        down = jnp.logical_and(move, cm < kf)
        return (jnp.where(up, mid, lo), jnp.where(down, mid, hi), jnp.where(up, cm, cnt),
                jnp.where(down, cm, cnt_hi), jnp.where(jnp.logical_and(active, noprog), 1.0, stalled))

    def bisect(state, max_iters):
        def cond(c):
            return jnp.logical_and(c[5] < max_iters, any_lane(is_open(c[2], c[4])))

        def body(c):
            state = c[:5]
            for _ in range(BISECT_UNROLL):
                state = bis_step(state)
            return state + (c[5] + BISECT_UNROLL,)

        return lax.while_loop(cond, body, state + (jnp.int32(0),))[:5]

    def max_below(t):
        def body(j, c):
            x = tile_rows(j)
            return jnp.maximum(c, jnp.max(jnp.where(x < t, x, -jnp.inf), axis=0))
        c = lax.fori_loop(0, n_tiles, body, jnp.full((ACCS, SUBLANES, LANES), -jnp.inf, F32))
        return rows_all(jnp.max(c, axis=0), jnp.max)

    def walk(c):
        top, left = c
        return jnp.where(left > 0.0, max_below(top), top), jnp.maximum(left - 1.0, 0.0)

    state = (lo, hi, cnt, cnt_hi, jnp.zeros((SUBLANES, LANES), F32))
    for _ in range(BISECT_COARSE_ITERS):
        state = bis_step(state)
    lo, hi, cnt, cnt_hi, stalled = state
    walked = is_open(cnt, stalled)
    top, left = walk(walk((hi, jnp.where(walked, kf - cnt_hi, 0.0))))
    arrived = jnp.logical_and(walked, left == 0.0)
    lo = jnp.where(arrived, top, lo)
    hi = jnp.where(jnp.logical_and(walked, left > 0.0), top, hi)
    cnt = count_where(lambda x: x >= lo[0:1])

    def finish(state):
        thr8, _, cnt, _, _ = bisect(state, BISECT_MAX_ITERS)
        tied = cnt > kf

        @pl.when(any_lane(tied))
        def _():
            need = kf - count_where(lambda x: x > thr8[0:1])

            def body(j, seen):
                x = isct_ref[j]
                eq = jnp.where(x == thr8[0:1], 1.0, 0.0)
                inc = _cumsum_rows(eq)
                rank = inc - eq + seen[0:1]
                drop = jnp.logical_and(jnp.logical_and(tied[0:1], eq > 0.0), rank >= need[0:1])
                isct_ref[j] = jnp.where(drop, -jnp.inf, x)
                return seen + inc[KEY_TILE - 1:KEY_TILE]

            lax.fori_loop(0, n_tiles, body, jnp.zeros((SUBLANES, LANES), F32))

        return thr8

    thr8 = lax.cond(any_lane(cnt > kf), finish, lambda s: s[0], (lo, hi, cnt, cnt_hi, stalled))
    thr = thr8[0:1]

    n_far = jnp.maximum(last - 1 + odd, 0)
    next_slot = (i + 1) % 2

    zero_l = jnp.zeros((ATTN_HEADS, SUBLANES, LANES), F32)

    shift = [qn_ref[0, 0, h:h + 1, :] * kmax_ref[0:1, :] * BOUND_SLACK + (bmax_ref[h] + BOUND_SLACK)
             for h in range(ATTN_HEADS)]

    def sweep_tile(j, l, near):
        keys = kv_ref[0, key_rows(j), :]
        values_t = kvt_ref[0, j]
        neg = jnp.where(isct_ref[j] >= thr, 0.0, -jnp.inf)
        table = jnp.where(j == last, odd, 2)
        for p in range(n_pairs):
            s = jnp.dot(keys, qt[:, pair_cols(p)], preferred_element_type=F32)
            for hh in range(2):
                h = 2 * p + hh
                sh = half(s, hh) + neg
                sbuf_ref[j, h] = sh + tb_ref[table, h] if near else sh
        out = []
        for p in range(n_pairs):
            probs = []
            for hh in range(2):
                h = 2 * p + hh
                e = jnp.exp2(sbuf_ref[j, h] - shift[h])
                out.append(l[h] + fold_rows(e, jnp.sum))
                probs.append(e.astype(BF16))
            acc_ref[p] += jnp.dot(values_t, jnp.concatenate(probs, axis=1),
                                  preferred_element_type=F32)
        idx_tile(j, i + 1, qitn_ref, wtn_ref, next_slot)
        return jnp.stack(out)

    acc_ref[...] = jnp.zeros_like(acc_ref)
    l = tile_loop(n_far, lambda j, l: sweep_tile(j, l, False), zero_l)
    l = lax.fori_loop(n_far, n_tiles, lambda j, l: sweep_tile(j, l, True), l)
    l = jnp.sum(l, axis=1, keepdims=True)

    @pl.when(jnp.logical_and(odd == 1, i + 1 < n_blocks))
    def _():
        idx_tile(n_tiles, i + 1, qitn_ref, wtn_ref, next_slot)

    def p1_tile(j, m, near):
        keys = kv_ref[0, key_rows(j), :]
        neg = jnp.where(isct_ref[j] >= thr, 0.0, -jnp.inf)
        table = jnp.where(j == last, odd, 2)
        out = []
        for p in range(n_pairs):
            s = jnp.dot(keys, qt[:, pair_cols(p)], preferred_element_type=F32)
            for hh in range(2):
                h = 2 * p + hh
                sh = half(s, hh) + neg
                if near:
                    sh = sh + tb_ref[table, h]
                sbuf_ref[j, h] = sh
                out.append(jnp.maximum(m[h], fold_rows(sh, jnp.max)))
        return jnp.stack(out)

    def exact_softmax(_):
        m = jnp.full((ATTN_HEADS, SUBLANES, LANES), -jnp.inf, F32)
        m = tile_loop(n_far, lambda j, m: p1_tile(j, m, False), m)
        m = lax.fori_loop(n_far, n_tiles, lambda j, m: p1_tile(j, m, True), m)
        m = jnp.max(m, axis=1, keepdims=True)
        acc_ref[...] = jnp.zeros_like(acc_ref)

        def p2_body(j, l):
            values_t = kvt_ref[0, j]
            out = []
            for p in range(n_pairs):
                probs = []
                for hh in range(2):
                    h = 2 * p + hh
                    e = jnp.exp2(sbuf_ref[j, h] - m[h])
                    out.append(l[h] + fold_rows(e, jnp.sum))
                    probs.append(e.astype(BF16))
                acc_ref[p] += jnp.dot(values_t, jnp.concatenate(probs, axis=1),
                                      preferred_element_type=F32)
            return jnp.stack(out)

        return jnp.sum(tile_loop(n_tiles, p2_body, zero_l), axis=1, keepdims=True)

    underflowed = jnp.max(jnp.where(l > UNDERFLOW_GUARD, 0.0, 1.0)) > 0.0
    l = lax.cond(underflowed, exact_softmax, lambda l: l, l)

    outs = [(acc_ref[h // 2, :, (h % 2) * Q_BLOCK:(h % 2 + 1) * Q_BLOCK] / l[h]).T.astype(BF16)
            for h in range(ATTN_HEADS)]
    for p in range(ATTN_HEADS // 2):
        pair = jnp.concatenate(outs[2 * p:2 * p + 2], axis=1)
        o_ref[0, :, p * LANES:(p + 1) * LANES] = jnp.dot(
            pair, wuv_ref[p], preferred_element_type=F32).astype(BF16)


def _pack_w_uv(w_uv):
    eye = jnp.eye(2, dtype=w_uv.dtype)
    w = w_uv.reshape(ATTN_HEADS // 2, 2, KV_RANK, 1, HEAD_DIM) * eye[None, :, None, :, None]
    return w.reshape(ATTN_HEADS // 2, 2 * KV_RANK, 2 * HEAD_DIM).astype(BF16)


def _dsa(qt, qnorm, qit, wt, ki, kv, kvt, tables, bias_max, wuv):
    bsz, seq, _ = kv.shape
    nkt = seq // KEY_TILE
    k_sel = min(TOPK_MAX, seq // 4)
    nb = seq // Q_BLOCK
    slab = (1, 1, LANES, ATTN_HEADS * Q_BLOCK)
    wslab = (1, 1, IDX_HEADS, Q_BLOCK)

    def first(b, i):
        return (b, 0, 0, 0)

    def following(b, i):
        return (b, jnp.minimum(i + 1, nb - 1), 0, 0)

    seq_spec = pl.BlockSpec((1, seq, LANES), lambda b, i: (b, 0, 0))
    n_vis = (np.arange(nb)[:, None] * Q_BLOCK + np.where(np.arange(Q_BLOCK) < CHUNK, CHUNK, 2 * CHUNK)[None, :])
    quantile = np.array([[statistics.NormalDist().inv_cdf(1.0 - k_sel / n) if n > k_sel else 0.0 for n in r]
                         for r in n_vis], np.float32)
    zq = jnp.asarray(np.broadcast_to(quantile[:, None, :], (nb, SUBLANES, Q_BLOCK)))
    return pl.pallas_call(
        functools.partial(_dsa_kernel, k_sel=k_sel, n_blocks=nb),
        grid=(bsz, nb),
        in_specs=[pl.BlockSpec(slab, lambda b, i: (b, i, 0, 0)),
                  pl.BlockSpec(wslab, lambda b, i: (b, i, 0, 0)),
                  pl.BlockSpec((1, SUBLANES, Q_BLOCK), lambda b, i: (i, 0, 0)),
                  pl.BlockSpec(slab, first), pl.BlockSpec(wslab, first),
                  pl.BlockSpec(slab, following), pl.BlockSpec(wslab, following),
                  seq_spec, seq_spec,
                  pl.BlockSpec((1, nkt, LANES, KEY_TILE), lambda b, i: (b, 0, 0, 0)),
                  _const_spec((3, ATTN_HEADS, KEY_TILE, Q_BLOCK)),
                  pl.BlockSpec(memory_space=pltpu.SMEM),
                  _const_spec((ATTN_HEADS // 2, 2 * KV_RANK, 2 * HEAD_DIM))],
        out_specs=pl.BlockSpec((1, Q_BLOCK, ATTN_HEADS * HEAD_DIM), lambda b, i: (b, i, 0)),
        out_shape=jax.ShapeDtypeStruct((bsz, seq, ATTN_HEADS * HEAD_DIM), BF16),
        scratch_shapes=[pltpu.VMEM((2, nkt, KEY_TILE, Q_BLOCK), F32),
                        pltpu.VMEM((2, 2, SUBLANES, LANES), F32),
                        pltpu.VMEM((nkt, ATTN_HEADS, KEY_TILE, Q_BLOCK), F32),
                        pltpu.VMEM((ATTN_HEADS // 2, KV_RANK, 2 * Q_BLOCK), F32),
                        pltpu.VMEM((SUBLANES, LANES), F32)],
        compiler_params=_cparams(("parallel", "arbitrary")),
        name="dsa",
    )(qt, qnorm, zq, qit, wt, qit, wt, ki, kv, kvt, tables, bias_max, wuv)


def _pack3(v):
    lane = lax.broadcasted_iota(jnp.int32, v.shape, 1)
    v = jnp.where(lane < SSD_HEADS, v, 0.0)
    hi = v.astype(BF16).astype(F32)
    r = v - hi
    mid = r.astype(BF16).astype(F32)
    lo = r - mid
    return (hi + pltpu.roll(mid, SSD_HEADS, axis=1) + pltpu.roll(lo, 2 * SSD_HEADS, axis=1)).astype(BF16)


def _cumsum_rows(x):
    n = x.shape[0]
    r = lax.broadcasted_iota(jnp.int32, x.shape, 0)
    s = 1
    while s < n:
        x = x + jnp.where(r >= s, pltpu.roll(x, s, axis=0), 0.0)
        s *= 2
    return x


def _shift_rows(x, s):
    r = pltpu.roll(x, s, axis=1)
    prev = jnp.concatenate([r[-1:], r[:-1]], axis=0)
    sub = lax.broadcasted_iota(jnp.int32, x.shape, 1)
    return jnp.where(sub >= s, r, prev)


def _ssd_kernel(z_ref, xbc_ref, dt_ref, cw_ref, cb_ref, dtb_ref, alog_ref, dsk_ref, nw_ref, e_ref,
                y_ref, tail_ref, u_ref, g_ref, state_ref):
    nq = SSD_Q

    @pl.when(pl.program_id(1) == 0)
    def _():
        tail_ref[...] = jnp.zeros_like(tail_ref)
        state_ref[...] = jnp.zeros_like(state_ref)

    assert CONV_W == 4
    for blk in range(CONV_DIM // CONV_BLOCK):
        cols = slice(blk * CONV_BLOCK, (blk + 1) * CONV_BLOCK)
        ext = jnp.concatenate([tail_ref[:, cols], xbc_ref[0, :, cols].astype(F32)], axis=0)
        ext = ext.reshape(1 + nq // SUBLANES, SUBLANES, CONV_BLOCK)
        s1 = _shift_rows(ext, 1)
        a = cw_ref[3:4, cols] * ext + cw_ref[2:3, cols] * s1 + cb_ref[:, cols]
        b = cw_ref[1:2, cols] * ext + cw_ref[0:1, cols] * s1
        conv = (a + _shift_rows(b, 2))[1:].reshape(nq, CONV_BLOCK)
        u_ref[:, cols] = _silu(conv)
        tail_ref[:, cols] = ext[nq // SUBLANES]

    t = dt_ref[0] + dtb_ref[...]
    dt = jnp.maximum(t, 0.0) + jnp.log1p(jnp.exp(-jnp.abs(t)))
    a2 = _cumsum_rows(dt * (-jnp.exp(alog_ref[...]))) * LOG2E
    a2_t = a2.T
    dt_p = _pack3(dt)
    dec_p = _pack3(dt * jnp.exp2(a2[nq - 1:nq, :] - a2))
    expa_p = _pack3(jnp.exp2(a2))

    r = lax.broadcasted_iota(jnp.int32, (nq, nq), 0)
    c = lax.broadcasted_iota(jnp.int32, (nq, nq), 1)
    causal = r >= c
    lane = lax.broadcasted_iota(jnp.int32, (nq, LANES), 1)
    heads_per_group = SSD_HEADS // SSD_GROUPS
    gw = heads_per_group * SSD_HEADDIM
    b_col = D_INNER
    c_col = D_INNER + SSD_GROUPS * D_STATE
    ssq = jnp.zeros((nq, LANES), F32)

    for g in range(SSD_GROUPS):
        gcols = slice(g * gw, (g + 1) * gw)
        eg = e_ref[:, gcols]
        dt_e = jnp.dot(dt_p, eg, preferred_element_type=F32)
        dec_e = jnp.dot(dec_p, eg, preferred_element_type=F32)
        expa_e = jnp.dot(expa_p, eg, preferred_element_type=F32)
        xs = u_ref[:, gcols]
        xdt_b = (xs * dt_e).astype(BF16)
        xdec_b = (xs * dec_e).astype(BF16)
        cg = u_ref[:, c_col + g * D_STATE:c_col + (g + 1) * D_STATE].astype(BF16)
        bgt = u_ref[:, b_col + g * D_STATE:b_col + (g + 1) * D_STATE].T.astype(BF16)
        cb = jnp.dot(cg, bgt, preferred_element_type=F32)
        prev = state_ref[g]
        y_off = jnp.dot(cg, prev.astype(BF16), preferred_element_type=F32) * expa_e
        pairs = []
        for pp in range(heads_per_group // 2):
            xp = xdt_b[:, pp * LANES:(pp + 1) * LANES]
            yh = []
            for hh in range(2):
                h = heads_per_group * g + 2 * pp + hh
                seg = a2[:, h:h + 1] - a2_t[h:h + 1, :]
                m = (cb * jnp.where(causal, jnp.exp2(seg), 0.0)).astype(BF16)
                yh.append(jnp.dot(m, xp, preferred_element_type=F32))
            pairs.append(jnp.where(lane < SSD_HEADDIM, yh[0], yh[1]))
        y = jnp.concatenate(pairs, axis=1) + y_off + dsk_ref[:, gcols] * xs
        gated = y * _silu(z_ref[0, :, gcols].astype(F32))
        g_ref[:, gcols] = gated
        sq = gated * gated
        ssq = ssq + sq[:, :LANES] + sq[:, LANES:]
        new = jnp.dot(bgt, xdec_b, preferred_element_type=F32)
        state_ref[g] = prev * expa_e[nq - 1:nq, :] + new
    scale = lax.rsqrt(jnp.sum(ssq, axis=-1, keepdims=True) * (1.0 / D_INNER) + EPS)
    y_ref[0] = (g_ref[...] * scale * nw_ref[...]).astype(BF16)


def _ssd(head, wide, conv_w, conv_b, dt_bias, a_log, d_skip, ssd_norm):
    bsz, seq, _ = head.shape
    nq = SSD_Q

    def pad_heads(v):
        return jnp.concatenate([v, jnp.zeros((LANES - SSD_HEADS,), F32)]).reshape(1, LANES)

    sel = np.concatenate([np.eye(SSD_HEADS)] * 3 + [np.zeros((LANES - 3 * SSD_HEADS, SSD_HEADS))], axis=0)
    e = jnp.asarray(np.kron(sel, np.ones((1, SSD_HEADDIM))), BF16)
    return pl.pallas_call(
        _ssd_kernel,
        grid=(bsz, seq // nq),
        in_specs=[pl.BlockSpec((1, nq, D_INNER), lambda b, i: (b, i, WIDE_Z // D_INNER)),
                  pl.BlockSpec((1, nq, CONV_DIM), lambda b, i: (b, i, WIDE_XBC // CONV_DIM)),
                  pl.BlockSpec((1, nq, LANES), lambda b, i: (b, i, HEAD_DT // LANES)),
                  _const_spec((CONV_W, CONV_DIM)), _const_spec((1, CONV_DIM)),
                  _const_spec((1, LANES)), _const_spec((1, LANES)),
                  _const_spec((1, D_INNER)), _const_spec((1, D_INNER)),
                  _const_spec((LANES, D_INNER))],
        out_specs=pl.BlockSpec((1, nq, D_INNER), lambda b, i: (b, i, 0)),
        out_shape=jax.ShapeDtypeStruct((bsz, seq, D_INNER), BF16),
        scratch_shapes=[pltpu.VMEM((SUBLANES, CONV_DIM), F32),
                        pltpu.VMEM((nq, CONV_DIM), F32),
                        pltpu.VMEM((nq, D_INNER), F32),
                        pltpu.VMEM((SSD_GROUPS, D_STATE, 4 * SSD_HEADDIM), F32)],
        compiler_params=_cparams(("parallel", "arbitrary")),
        name="ssd",
    )(wide, wide, head, conv_w, conv_b.reshape(1, CONV_DIM), pad_heads(dt_bias), pad_heads(a_log),
      jnp.repeat(d_skip, SSD_HEADDIM).reshape(1, D_INNER), ssd_norm.reshape(1, D_INNER), e)


def _mix_kernel(ao_ref, sy_ref, ga_ref, gb_ref, x_ref, mod_ref, nw_ref, woa_ref, wos_ref, wout_ref, o_ref):
    ya = jnp.dot(ao_ref[0], woa_ref[...], preferred_element_type=F32)
    yb = jnp.dot(sy_ref[0], wos_ref[...], preferred_element_type=F32)
    mix = jax.nn.sigmoid(ga_ref[0].astype(F32)) * ya + jax.nn.sigmoid(gb_ref[0].astype(F32)) * yb
    m2 = jnp.dot(mix.astype(BF16), wout_ref[...], preferred_element_type=F32)
    o_ref[0] = x_ref[0] + mod_ref[0][2:3] * _rms(m2, nw_ref[...])


def _mix(attn_o, ssd_y, wide, x, mod3, post_norm, w_o_attn, w_o_ssd, w_out):
    bsz, seq, _ = x.shape
    tm = min(seq, 512)

    def rows(width, col_block=0):
        return pl.BlockSpec((1, tm, width), lambda b, i: (b, i, col_block))

    return pl.pallas_call(
        _mix_kernel,
        grid=(bsz, seq // tm),
        in_specs=[rows(D_MODEL), rows(D_INNER), rows(D_MODEL, WIDE_GA // D_MODEL), rows(D_MODEL, WIDE_GB // D_MODEL),
                  rows(D_MODEL), pl.BlockSpec((1, 6, D_MODEL), lambda b, i: (b, 0, 0)),
                  _const_spec((1, D_MODEL)), _const_spec((D_MODEL, D_MODEL)),
                  _const_spec((D_INNER, D_MODEL)), _const_spec((D_MODEL, D_MODEL))],
        out_specs=rows(D_MODEL),
        out_shape=jax.ShapeDtypeStruct((bsz, seq, D_MODEL), F32),
        compiler_params=_cparams(("parallel", "parallel")),
        name="mix",
    )(attn_o, ssd_y, wide, wide, x, mod3, post_norm.reshape(1, D_MODEL),
      w_o_attn.astype(BF16), w_o_ssd.astype(BF16), w_out.astype(BF16))


def _ffn_kernel(x_ref, mod_ref, nw1_ref, nw2_ref, wg_ref, wu_ref, wo_ref, o_ref):
    x = x_ref[0]
    m = mod_ref[0]
    h2 = (_rms(x, nw1_ref[...]) * (1.0 + m[4:5]) + m[3:4]).astype(BF16)
    ug = jnp.dot(h2, wg_ref[...], preferred_element_type=F32)
    uu = jnp.dot(h2, wu_ref[...], preferred_element_type=F32)
    f = jnp.dot((_silu(ug) * uu).astype(BF16), wo_ref[...], preferred_element_type=F32)
    o_ref[0] = x + m[5:6] * _rms(f, nw2_ref[...])


def _ffn(x, mod3, pre_norm, post_norm, w_ffn_in, w_ffn_out):
    bsz, seq, _ = x.shape
    tm = min(seq, 512)
    rows = pl.BlockSpec((1, tm, D_MODEL), lambda b, i: (b, i, 0))
    w_in = w_ffn_in.astype(BF16)

    def half(k):
        return pl.BlockSpec((D_MODEL, D_FF), lambda b, i: (0, k), pipeline_mode=pl.Buffered(1))

    return pl.pallas_call(
        _ffn_kernel,
        grid=(bsz, seq // tm),
        in_specs=[rows, pl.BlockSpec((1, 6, D_MODEL), lambda b, i: (b, 0, 0)),
                  _const_spec((1, D_MODEL)), _const_spec((1, D_MODEL)),
                  half(0), half(1), _const_spec((D_FF, D_MODEL))],
        out_specs=rows,
        out_shape=jax.ShapeDtypeStruct((bsz, seq, D_MODEL), F32),
        compiler_params=_cparams(("parallel", "parallel")),
        name="ffn",
    )(x, mod3, pre_norm.reshape(1, D_MODEL), post_norm.reshape(1, D_MODEL), w_in, w_in, w_ffn_out.astype(BF16))


def kernel(x, c, positions, ada_w, ada_b, pre_norm_mix, post_norm_mix, pre_norm_ffn, post_norm_ffn, w_in, q_norm, kv_norm, w_uq, w_uv, rel_bias, w_qidx, kidx_norm, conv_w, conv_b, dt_bias, a_log, d_skip, ssd_norm, w_o_attn, w_o_ssd, w_out, w_ffn_in, w_ffn_out):
    del positions
    bsz, seq, _ = x.shape
    assert seq % (2 * KEY_TILE) == 0 and x.shape[-1] == D_MODEL
    mod3 = _mod(c, ada_w, ada_b).reshape(bsz, 6, D_MODEL)
    head, wide = _inproj(x, mod3, pre_norm_mix, _pack_w_in(w_in))
    qt, qit, kv, kvt, ki, wt, qnorm = _prep(head, q_norm, kv_norm, kidx_norm, w_uq, w_qidx)
    bias_max = jnp.max(rel_bias - rel_bias[NUM_BUCKETS // 2 - 1], axis=0) * LOG2E
    attn_o = _dsa(qt, qnorm, qit, wt, ki, kv, kvt, _bias_tables(rel_bias), bias_max, _pack_w_uv(w_uv))
    ssd_y = _ssd(head, wide, conv_w, conv_b, dt_bias, a_log, d_skip, ssd_norm)
    x1 = _mix(attn_o, ssd_y, wide, x, mod3, post_norm_mix, w_o_attn, w_o_ssd, w_out)
    return _ffn(x1, mod3, pre_norm_ffn, post_norm_ffn, w_ffn_in, w_ffn_out)
```

```python
import functools
import math
import statistics

import numpy as np
import jax
import jax.numpy as jnp
from jax import lax
from jax.experimental import pallas as pl
from jax.experimental.pallas import tpu as pltpu

F32 = jnp.float32
BF16 = jnp.bfloat16

D_MODEL = 1024
CHUNK = 64
Q_BLOCK = 128
EPS = 1e-6
ATTN_HEADS = 16
HEAD_DIM = 64
Q_RANK = 256
KV_RANK = 128
IDX_HEADS = 16
IDX_DIM = 64
TOPK_MAX = 256
NUM_BUCKETS = 32
MAX_DISTANCE = 128
D_INNER = 2 * D_MODEL
SSD_HEADDIM = 64
SSD_HEADS = D_INNER // SSD_HEADDIM
SSD_GROUPS = 8
D_STATE = 128
CONV_W = 4
CONV_DIM = D_INNER + 2 * SSD_GROUPS * D_STATE
D_FF = -(-8 * D_MODEL // (3 * 256)) * 256

LANES = 128
SUBLANES = 8
KEY_TILE = 256
SSD_Q = 128
CONV_BLOCK = 512
VMEM_LIMIT = 56 * 1024 * 1024
BISECT_MAX_ITERS = 320
BISECT_UNROLL = 4
BISECT_COARSE_ITERS = 10
BRACKET_HALF_WIDTH = 0.3
FLOAT_BIG = 3.0e38
LOG2E = math.log2(math.e)
BOUND_SLACK = 1.0 + 2.0 ** -6
UNDERFLOW_GUARD = 2.0 ** -80

HEAD_COLS = 1024
HEAD_KV, HEAD_KIDX, HEAD_W, HEAD_DT = 256, 384, 512, 640
WIDE_XBC, WIDE_Z, WIDE_GA, WIDE_GB = 0, 4096, 6144, 7168
WIDE_COLS = 8192
INPROJ_TN = 1024


def _cparams(sem):
    return pltpu.CompilerParams(dimension_semantics=sem, vmem_limit_bytes=VMEM_LIMIT)


def _const_spec(shape):
    nd = len(shape)
    return pl.BlockSpec(shape, lambda *_: (0,) * nd, pipeline_mode=pl.Buffered(1))


def _rms(x, w, n=None):
    n = x.shape[-1] if n is None else n
    return x * lax.rsqrt(jnp.sum(x * x, axis=-1, keepdims=True) * (1.0 / n) + EPS) * w


def _silu_of_half(h):
    return h + h * jnp.tanh(h)


def _silu(x):
    return _silu_of_half(0.5 * x)


def _mod_kernel(c_ref, w_ref, b_ref, o_ref):
    c = c_ref[...]
    s = _silu(c).astype(BF16)
    o_ref[...] = jnp.dot(s, w_ref[...].astype(BF16), preferred_element_type=F32) + b_ref[...]


def _mod(c, ada_w, ada_b):
    bsz = c.shape[0]
    return pl.pallas_call(
        _mod_kernel,
        grid=(6,),
        in_specs=[pl.BlockSpec((bsz, D_MODEL), lambda j: (0, 0)),
                  pl.BlockSpec((D_MODEL, D_MODEL), lambda j: (0, j)),
                  pl.BlockSpec((1, D_MODEL), lambda j: (0, j))],
        out_specs=pl.BlockSpec((bsz, D_MODEL), lambda j: (0, j)),
        out_shape=jax.ShapeDtypeStruct((bsz, 6 * D_MODEL), F32),
        compiler_params=_cparams(("parallel",)),
        name="mod",
    )(c, ada_w, ada_b.reshape(1, 6 * D_MODEL))


def _t5_bucket_np(rel):
    half = NUM_BUCKETS // 2
    max_exact = half // 2
    side = np.where(rel > 0, half, 0)
    n = np.abs(rel)
    large = max_exact + (np.log(np.maximum(n, max_exact).astype(np.float64) / max_exact)
                         / math.log(MAX_DISTANCE / max_exact) * (half - max_exact)).astype(np.int64)
    large = np.minimum(large, half - 1)
    return (side + np.where(n < max_exact, n, large)).astype(np.int32)


def _bias_kernel(idx_ref, rb_ref, o_ref):
    h = pl.program_id(0)
    idx = idx_ref[...]
    far = rb_ref[h, NUM_BUCKETS // 2 - 1]
    acc = jnp.zeros(idx.shape, F32)
    for b in range(NUM_BUCKETS):
        acc = jnp.where(idx == b, (rb_ref[h, b] - far) * LOG2E, acc)
    o_ref[0] = acc


def _bias_tables(rel_bias):
    kk = np.arange(2 * KEY_TILE)[:, None]
    ql = np.arange(Q_BLOCK)[None, :]
    idx = jnp.asarray(_t5_bucket_np(kk - KEY_TILE - ql))
    t = pl.pallas_call(
        _bias_kernel,
        grid=(ATTN_HEADS,),
        in_specs=[pl.BlockSpec((2 * KEY_TILE, Q_BLOCK), lambda h: (0, 0)),
                  pl.BlockSpec(memory_space=pltpu.SMEM)],
        out_specs=pl.BlockSpec((1, 2 * KEY_TILE, Q_BLOCK), lambda h: (h, 0, 0)),
        out_shape=jax.ShapeDtypeStruct((ATTN_HEADS, 2 * KEY_TILE, Q_BLOCK), F32),
        compiler_params=_cparams(("arbitrary",)),
        name="bias",
    )(idx, rel_bias.T)
    return jnp.stack([t[:, 256:512], t[:, 128:384], t[:, 0:256]])


def _inproj_kernel(x_ref, mod_ref, nw_ref, w_ref, head_ref, wide_ref):
    m = mod_ref[0]
    hn = (_rms(x_ref[0], nw_ref[...]) * (1.0 + m[1:2]) + m[0:1]).astype(BF16)
    head_ref[0] = _dot_nt(hn, w_ref[0:HEAD_COLS, :])
    for c in range(WIDE_COLS // INPROJ_TN):
        rows = slice(HEAD_COLS + c * INPROJ_TN, HEAD_COLS + (c + 1) * INPROJ_TN)
        wide_ref[0, :, c * INPROJ_TN:(c + 1) * INPROJ_TN] = _dot_nt(hn, w_ref[rows, :]).astype(BF16)


def _pack_w_in(w_in):
    sizes = [Q_RANK, KV_RANK, IDX_DIM, IDX_HEADS, D_INNER, CONV_DIM, SSD_HEADS, D_MODEL, D_MODEL]
    offs = np.cumsum([0] + sizes)
    wt = w_in.T
    q, kv, ki, wi, z, xbc, dt, ga, gb = [wt[offs[i]:offs[i + 1]] for i in range(9)]

    def zr(n):
        return jnp.zeros((n, D_MODEL), w_in.dtype)

    return jnp.concatenate([q, kv, ki, zr(LANES - IDX_DIM), wi, zr(LANES - IDX_HEADS),
                            dt, zr(LANES - SSD_HEADS), zr(2 * LANES), xbc, 0.5 * z, ga, gb], axis=0).astype(BF16)


def _dot_nt(a, b):
    return lax.dot_general(a, b, (((1,), (1,)), ((), ())), preferred_element_type=F32)


def _inproj(x, mod3, pre_norm, w_packed):
    bsz, seq, _ = x.shape
    tm = min(seq, 512)
    return pl.pallas_call(
        _inproj_kernel,
        grid=(bsz, seq // tm),
        in_specs=[pl.BlockSpec((1, tm, D_MODEL), lambda b, i: (b, i, 0)),
                  pl.BlockSpec((1, 6, D_MODEL), lambda b, i: (b, 0, 0)),
                  _const_spec((1, D_MODEL)),
                  _const_spec((HEAD_COLS + WIDE_COLS, D_MODEL))],
        out_specs=[pl.BlockSpec((1, tm, HEAD_COLS), lambda b, i: (b, i, 0)),
                   pl.BlockSpec((1, tm, WIDE_COLS), lambda b, i: (b, i, 0))],
        out_shape=[jax.ShapeDtypeStruct((bsz, seq, HEAD_COLS), F32),
                   jax.ShapeDtypeStruct((bsz, seq, WIDE_COLS), BF16)],
        compiler_params=_cparams(("parallel", "parallel")),
        name="inproj",
    )(x, mod3, pre_norm.reshape(1, D_MODEL), w_packed)


def _prep_kernel(p_ref, qn_ref, kvn_ref, kin_ref, wuqt_ref, wqit_ref,
                 qt_ref, qit_ref, kv_ref, kvt_ref, ki_ref, wt_ref, qnorm_ref, *, tc):
    p = p_ref[0]
    qnt = _rms(p[:, :Q_RANK], qn_ref[...]).T.astype(BF16)
    qf = jnp.dot(wuqt_ref[...], qnt, preferred_element_type=F32) * (KV_RANK ** -0.5 * LOG2E)
    qt = qf.astype(BF16)
    qnorm = jnp.sqrt(jnp.sum((qf * qf).reshape(ATTN_HEADS, KV_RANK, tc), axis=1))
    qit = jnp.dot(wqit_ref[...], qnt, preferred_element_type=F32).astype(BF16)
    wt = (p[:, HEAD_W:HEAD_W + LANES] * (IDX_HEADS ** -0.5 * IDX_DIM ** -0.5)).T
    for blk in range(tc // Q_BLOCK):
        cols = slice(blk * Q_BLOCK, (blk + 1) * Q_BLOCK)
        for h in range(ATTN_HEADS):
            qt_ref[0, blk, :, h * LANES:(h + 1) * LANES] = qt[h * LANES:(h + 1) * LANES, cols]
            qit_ref[0, blk, :, h * LANES:(h + 1) * LANES] = qit[h * LANES:(h + 1) * LANES, cols]
        wt_ref[0, blk] = wt[0:IDX_HEADS, cols]
        qnorm_ref[0, blk] = qnorm[:, cols]
    kv = _rms(p[:, HEAD_KV:HEAD_KV + KV_RANK], kvn_ref[...])
    kv_ref[0] = kv.astype(BF16)
    kvt = kv.T
    for c in range(tc // KEY_TILE):
        kvt_ref[0, c] = kvt[:, c * KEY_TILE:(c + 1) * KEY_TILE].astype(BF16)
    ki_ref[0] = _rms(p[:, HEAD_KIDX:HEAD_KIDX + LANES], kin_ref[...], n=IDX_DIM).astype(BF16)


def _prep(proj, q_norm, kv_norm, kidx_norm, w_uq, w_qidx):
    bsz, seq, _ = proj.shape
    tc = min(seq, 512)
    wqi = w_qidx.reshape(Q_RANK, IDX_HEADS, IDX_DIM)
    wqit = jnp.concatenate([wqi, jnp.zeros_like(wqi)], axis=-1).reshape(Q_RANK, IDX_HEADS * LANES).T.astype(BF16)
    kin = jnp.concatenate([kidx_norm, jnp.zeros((LANES - IDX_DIM,), F32)]).reshape(1, LANES)
    nb = seq // Q_BLOCK
    slab = jax.ShapeDtypeStruct((bsz, nb, LANES, ATTN_HEADS * Q_BLOCK), BF16)
    slab_spec = pl.BlockSpec((1, tc // Q_BLOCK, LANES, ATTN_HEADS * Q_BLOCK), lambda b, i: (b, i, 0, 0))
    row_spec = pl.BlockSpec((1, tc, LANES), lambda b, i: (b, i, 0))
    return pl.pallas_call(
        functools.partial(_prep_kernel, tc=tc),
        grid=(bsz, seq // tc),
        in_specs=[pl.BlockSpec((1, tc, 1024), lambda b, i: (b, i, 0)),
                  _const_spec((1, Q_RANK)), _const_spec((1, KV_RANK)), _const_spec((1, LANES)),
                  _const_spec((ATTN_HEADS * KV_RANK, Q_RANK)), _const_spec((IDX_HEADS * LANES, Q_RANK))],
        out_specs=[slab_spec, slab_spec, row_spec,
                   pl.BlockSpec((1, tc // KEY_TILE, LANES, KEY_TILE), lambda b, i: (b, i, 0, 0)),
                   row_spec,
                   pl.BlockSpec((1, tc // Q_BLOCK, IDX_HEADS, Q_BLOCK), lambda b, i: (b, i, 0, 0)),
                   pl.BlockSpec((1, tc // Q_BLOCK, ATTN_HEADS, Q_BLOCK), lambda b, i: (b, i, 0, 0))],
        out_shape=[slab, slab, jax.ShapeDtypeStruct((bsz, seq, LANES), BF16),
                   jax.ShapeDtypeStruct((bsz, seq // KEY_TILE, LANES, KEY_TILE), BF16),
                   jax.ShapeDtypeStruct((bsz, seq, LANES), BF16),
                   jax.ShapeDtypeStruct((bsz, nb, IDX_HEADS, Q_BLOCK), F32),
                   jax.ShapeDtypeStruct((bsz, nb, ATTN_HEADS, Q_BLOCK), F32)],
        compiler_params=_cparams(("parallel", "parallel")),
        name="prep",
    )(proj, q_norm.reshape(1, Q_RANK), kv_norm.reshape(1, KV_RANK), kin, w_uq.T.astype(BF16), wqit)


def _dsa_kernel(qt_ref, qn_ref, zq_ref, qit0_ref, wt0_ref, qitn_ref, wtn_ref, ki_ref, kv_ref, kvt_ref, tb_ref,
                bmax_ref, wuv_ref, o_ref, iscbuf_ref, mom_ref, sbuf_ref, acc_ref, kmax_ref, *, k_sel, n_blocks):
    i = pl.program_id(1)
    last = i // 2
    odd = i % 2
    n_tiles = last + 1
    isct_ref = iscbuf_ref.at[i % 2]
    qt = qt_ref[0, 0]
    row = lax.broadcasted_iota(jnp.int32, (KEY_TILE, Q_BLOCK), 0)
    col = lax.broadcasted_iota(jnp.int32, (KEY_TILE, Q_BLOCK), 1)

    def key_rows(j):
        return pl.ds(pl.multiple_of(j * KEY_TILE, KEY_TILE), KEY_TILE)

    n_pairs = ATTN_HEADS // 2

    def pair_cols(p):
        return slice(p * KEY_TILE, (p + 1) * KEY_TILE)

    def half(x, hh):
        return x[:, hh * Q_BLOCK:(hh + 1) * Q_BLOCK]

    ACCS = 4

    def fold_rows(x, op):
        x = x.reshape(KEY_TILE // (ACCS * SUBLANES), ACCS, SUBLANES, LANES)
        return op(op(x, axis=0), axis=0)

    def idx_tile(j, blk, qit_ref, wt_ref, slot):
        dst_ref = iscbuf_ref.at[slot]
        keys = ki_ref[0, key_rows(j), :]
        wt = wt_ref[0, 0]
        acc = jnp.zeros((KEY_TILE, Q_BLOCK), F32)
        for p in range(n_pairs):
            s = jnp.dot(keys, qit_ref[0, 0, :, pair_cols(p)], preferred_element_type=F32)
            for hh in range(2):
                h = 2 * p + hh
                acc = acc + wt[h:h + 1, :] * jnp.maximum(half(s, hh), 0.0)
        key_limit = blk * Q_BLOCK + jnp.where(col < CHUNK, CHUNK, 2 * CHUNK)
        visible = j * KEY_TILE + row < key_limit
        dst_ref[j] = jnp.where(visible, acc, -jnp.inf)
        seen = jnp.where(visible, acc, 0.0)
        old = jnp.where(j == 0, 0.0, mom_ref[slot])
        mom_ref[slot] = old + jnp.stack([fold_rows(seen, jnp.sum), fold_rows(seen * seen, jnp.sum)])

    def tile_loop(n, body, carry):
        def pair(jj, c):
            return body(2 * jj + 1, body(2 * jj, c))
        carry = lax.fori_loop(0, n // 2, pair, carry)
        return lax.cond(n % 2 == 1, lambda c: body(n - 1, c), lambda c: c, carry)

    @pl.when(i == 0)
    def _():
        idx_tile(0, 0, qit0_ref, wt0_ref, 0)
        best = jnp.zeros((1, KEY_TILE), F32)
        for j in range(kvt_ref.shape[1]):
            x = kvt_ref[0, j].astype(F32)
            best = jnp.maximum(best, jnp.sum(x * x, axis=0, keepdims=True))
        kmax_ref[...] = jnp.broadcast_to(jnp.sqrt(jnp.max(best, axis=1, keepdims=True)), (SUBLANES, LANES))

    def rows_all(x, op):
        return jnp.broadcast_to(op(x, axis=0, keepdims=True), (SUBLANES, LANES))

    def tile_rows(j):
        return isct_ref[j].reshape(KEY_TILE // (ACCS * SUBLANES), ACCS, SUBLANES, LANES)

    def count_where(pred):
        def body(j, c):
            x = tile_rows(j)
            for r in range(x.shape[0]):
                c = jnp.where(pred(x[r]), c + 1.0, c)
            return c
        c = lax.fori_loop(0, n_tiles, body, jnp.zeros((ACCS, SUBLANES, LANES), F32))
        return rows_all(jnp.sum(c, axis=0), jnp.sum)

    kf = float(k_sel)

    def is_open(cnt, stalled):
        return jnp.logical_and(cnt > kf, stalled == 0.0)

    def any_lane(mask):
        return jnp.max(jnp.where(mask, 1.0, 0.0)) > 0.0

    lane8 = lax.broadcasted_iota(jnp.int32, (SUBLANES, LANES), 1)
    n_vis = (i * Q_BLOCK + jnp.where(lane8 < CHUNK, CHUNK, 2 * CHUNK)).astype(F32)
    mom = mom_ref[i % 2]
    mean = rows_all(mom[0], jnp.sum) / n_vis
    std = jnp.sqrt(jnp.maximum(rows_all(mom[1], jnp.sum) / n_vis - mean * mean, 0.0))
    z = zq_ref[0]
    takes_all = n_vis <= kf
    lo = jnp.where(takes_all, -FLOAT_BIG, mean + (z - BRACKET_HALF_WIDTH) * std)
    hi = jnp.where(takes_all, FLOAT_BIG, mean + (z + BRACKET_HALF_WIDTH) * std)
    cnt = count_where(lambda x: x >= lo[0:1])
    cnt_hi = count_where(lambda x: x >= hi[0:1])
    low_ok = cnt >= kf
    high_ok = cnt_hi < kf
    lo = jnp.where(low_ok, lo, -FLOAT_BIG)
    cnt = jnp.where(low_ok, cnt, n_vis)
    hi = jnp.where(high_ok, hi, FLOAT_BIG)
    cnt_hi = jnp.where(high_ok, cnt_hi, 0.0)

    def bis_step(c):
        lo, hi, cnt, cnt_hi, stalled = c
        mid = 0.5 * lo + 0.5 * hi
        cm = count_where(lambda x: x >= mid[0:1])
        active = is_open(cnt, stalled)
        noprog = jnp.logical_or(mid <= lo, mid >= hi)
        move = jnp.logical_and(active, jnp.logical_not(noprog))
        up = jnp.logical_and(move, cm >= kf)
        down = jnp.logical_and(move, cm < kf)
        return (jnp.where(up, mid, lo), jnp.where(down, mid, hi), jnp.where(up, cm, cnt),
                jnp.where(down, cm, cnt_hi), jnp.where(jnp.logical_and(active, noprog), 1.0, stalled))

    def bisect(state, max_iters):
        def cond(c):
            return jnp.logical_and(c[5] < max_iters, any_lane(is_open(c[2], c[4])))

        def body(c):
            state = c[:5]
            for _ in range(BISECT_UNROLL):
                state = bis_step(state)
            return state + (c[5] + BISECT_UNROLL,)

        return lax.while_loop(cond, body, state + (jnp.int32(0),))[:5]

    def max_below(t):
        def body(j, c):
            x = tile_rows(j)
            return jnp.maximum(c, jnp.max(jnp.where(x < t, x, -jnp.inf), axis=0))
        c = lax.fori_loop(0, n_tiles, body, jnp.full((ACCS, SUBLANES, LANES), -jnp.inf, F32))
        return rows_all(jnp.max(c, axis=0), jnp.max)

    def walk(c):
        top, left = c
        return jnp.where(left > 0.0, max_below(top), top), jnp.maximum(left - 1.0, 0.0)

    state = (lo, hi, cnt, cnt_hi, jnp.zeros((SUBLANES, LANES), F32))
    for _ in range(BISECT_COARSE_ITERS):
        state = bis_step(state)
    lo, hi, cnt, cnt_hi, stalled = state
    walked = is_open(cnt, stalled)
    top, left = walk(walk((hi, jnp.where(walked, kf - cnt_hi, 0.0))))
    arrived = jnp.logical_and(walked, left == 0.0)
    lo = jnp.where(arrived, top, lo)
    hi = jnp.where(jnp.logical_and(walked, left > 0.0), top, hi)
    cnt = count_where(lambda x: x >= lo[0:1])

    def finish(state):
        thr8, _, cnt, _, _ = bisect(state, BISECT_MAX_ITERS)
        tied = cnt > kf

        @pl.when(any_lane(tied))
        def _():
            need = kf - count_where(lambda x: x > thr8[0:1])

            def body(j, seen):
                x = isct_ref[j]
                eq = jnp.where(x == thr8[0:1], 1.0, 0.0)
                inc = _cumsum_rows(eq)
                rank = inc - eq + seen[0:1]
                drop = jnp.logical_and(jnp.logical_and(tied[0:1], eq > 0.0), rank >= need[0:1])
                isct_ref[j] = jnp.where(drop, -jnp.inf, x)
                return seen + inc[KEY_TILE - 1:KEY_TILE]

            lax.fori_loop(0, n_tiles, body, jnp.zeros((SUBLANES, LANES), F32))

        return thr8

    thr8 = lax.cond(any_lane(cnt > kf), finish, lambda s: s[0], (lo, hi, cnt, cnt_hi, stalled))
    thr = thr8[0:1]

    n_far = jnp.maximum(last - 1 + odd, 0)
    next_slot = (i + 1) % 2

    zero_l = jnp.zeros((ATTN_HEADS, SUBLANES, LANES), F32)

    shift = [qn_ref[0, 0, h:h + 1, :] * kmax_ref[0:1, :] * BOUND_SLACK + (bmax_ref[h] + BOUND_SLACK)
             for h in range(ATTN_HEADS)]

    def sweep_tile(j, l, near):
        keys = kv_ref[0, key_rows(j), :]
        values_t = kvt_ref[0, j]
        neg = jnp.where(isct_ref[j] >= thr, 0.0, -jnp.inf)
        table = jnp.where(j == last, odd, 2)
        for p in range(n_pairs):
            s = jnp.dot(keys, qt[:, pair_cols(p)], preferred_element_type=F32)
            for hh in range(2):
                h = 2 * p + hh
                sh = half(s, hh) + neg
                sbuf_ref[j, h] = sh + tb_ref[table, h] if near else sh
        out = []
        for p in range(n_pairs):
            probs = []
            for hh in range(2):
                h = 2 * p + hh
                e = jnp.exp2(sbuf_ref[j, h] - shift[h])
                out.append(l[h] + fold_rows(e, jnp.sum))
                probs.append(e.astype(BF16))
            acc_ref[p] += jnp.dot(values_t, jnp.concatenate(probs, axis=1),
                                  preferred_element_type=F32)
        idx_tile(j, i + 1, qitn_ref, wtn_ref, next_slot)
        return jnp.stack(out)

    acc_ref[...] = jnp.zeros_like(acc_ref)
    l = tile_loop(n_far, lambda j, l: sweep_tile(j, l, False), zero_l)
    l = lax.fori_loop(n_far, n_tiles, lambda j, l: sweep_tile(j, l, True), l)
    l = jnp.sum(l, axis=1, keepdims=True)

    @pl.when(jnp.logical_and(odd == 1, i + 1 < n_blocks))
    def _():
        idx_tile(n_tiles, i + 1, qitn_ref, wtn_ref, next_slot)

    def p1_tile(j, m, near):
        keys = kv_ref[0, key_rows(j), :]
        neg = jnp.where(isct_ref[j] >= thr, 0.0, -jnp.inf)
        table = jnp.where(j == last, odd, 2)
        out = []
        for p in range(n_pairs):
            s = jnp.dot(keys, qt[:, pair_cols(p)], preferred_element_type=F32)
            for hh in range(2):
                h = 2 * p + hh
                sh = half(s, hh) + neg
                if near:
                    sh = sh + tb_ref[table, h]
                sbuf_ref[j, h] = sh
                out.append(jnp.maximum(m[h], fold_rows(sh, jnp.max)))
        return jnp.stack(out)

    def exact_softmax(_):
        m = jnp.full((ATTN_HEADS, SUBLANES, LANES), -jnp.inf, F32)
        m = tile_loop(n_far, lambda j, m: p1_tile(j, m, False), m)
        m = lax.fori_loop(n_far, n_tiles, lambda j, m: p1_tile(j, m, True), m)
        m = jnp.max(m, axis=1, keepdims=True)
        acc_ref[...] = jnp.zeros_like(acc_ref)

        def p2_body(j, l):
            values_t = kvt_ref[0, j]
            out = []
            for p in range(n_pairs):
                probs = []
                for hh in range(2):
                    h = 2 * p + hh
                    e = jnp.exp2(sbuf_ref[j, h] - m[h])
                    out.append(l[h] + fold_rows(e, jnp.sum))
                    probs.append(e.astype(BF16))
                acc_ref[p] += jnp.dot(values_t, jnp.concatenate(probs, axis=1),
                                      preferred_element_type=F32)
            return jnp.stack(out)

        return jnp.sum(tile_loop(n_tiles, p2_body, zero_l), axis=1, keepdims=True)

    underflowed = jnp.max(jnp.where(l > UNDERFLOW_GUARD, 0.0, 1.0)) > 0.0
    l = lax.cond(underflowed, exact_softmax, lambda l: l, l)

    outs = [(acc_ref[h // 2, :, (h % 2) * Q_BLOCK:(h % 2 + 1) * Q_BLOCK] / l[h]).T.astype(BF16)
            for h in range(ATTN_HEADS)]
    for p in range(ATTN_HEADS // 2):
        pair = jnp.concatenate(outs[2 * p:2 * p + 2], axis=1)
        o_ref[0, :, p * LANES:(p + 1) * LANES] = jnp.dot(
            pair, wuv_ref[p], preferred_element_type=F32).astype(BF16)


def _pack_w_uv(w_uv):
    eye = jnp.eye(2, dtype=w_uv.dtype)
    w = w_uv.reshape(ATTN_HEADS // 2, 2, KV_RANK, 1, HEAD_DIM) * eye[None, :, None, :, None]
    return w.reshape(ATTN_HEADS // 2, 2 * KV_RANK, 2 * HEAD_DIM).astype(BF16)


def _dsa(qt, qnorm, qit, wt, ki, kv, kvt, tables, bias_max, wuv):
    bsz, seq, _ = kv.shape
    nkt = seq // KEY_TILE
    k_sel = min(TOPK_MAX, seq // 4)
    nb = seq // Q_BLOCK
    slab = (1, 1, LANES, ATTN_HEADS * Q_BLOCK)
    wslab = (1, 1, IDX_HEADS, Q_BLOCK)

    def first(b, i):
        return (b, 0, 0, 0)

    def following(b, i):
        return (b, jnp.minimum(i + 1, nb - 1), 0, 0)

    seq_spec = pl.BlockSpec((1, seq, LANES), lambda b, i: (b, 0, 0))
    n_vis = (np.arange(nb)[:, None] * Q_BLOCK + np.where(np.arange(Q_BLOCK) < CHUNK, CHUNK, 2 * CHUNK)[None, :])
    quantile = np.array([[statistics.NormalDist().inv_cdf(1.0 - k_sel / n) if n > k_sel else 0.0 for n in r]
                         for r in n_vis], np.float32)
    zq = jnp.asarray(np.broadcast_to(quantile[:, None, :], (nb, SUBLANES, Q_BLOCK)))
    return pl.pallas_call(
        functools.partial(_dsa_kernel, k_sel=k_sel, n_blocks=nb),
        grid=(bsz, nb),
        in_specs=[pl.BlockSpec(slab, lambda b, i: (b, i, 0, 0)),
                  pl.BlockSpec(wslab, lambda b, i: (b, i, 0, 0)),
                  pl.BlockSpec((1, SUBLANES, Q_BLOCK), lambda b, i: (i, 0, 0)),
                  pl.BlockSpec(slab, first), pl.BlockSpec(wslab, first),
                  pl.BlockSpec(slab, following), pl.BlockSpec(wslab, following),
                  seq_spec, seq_spec,
                  pl.BlockSpec((1, nkt, LANES, KEY_TILE), lambda b, i: (b, 0, 0, 0)),
                  _const_spec((3, ATTN_HEADS, KEY_TILE, Q_BLOCK)),
                  pl.BlockSpec(memory_space=pltpu.SMEM),
                  _const_spec((ATTN_HEADS // 2, 2 * KV_RANK, 2 * HEAD_DIM))],
        out_specs=pl.BlockSpec((1, Q_BLOCK, ATTN_HEADS * HEAD_DIM), lambda b, i: (b, i, 0)),
        out_shape=jax.ShapeDtypeStruct((bsz, seq, ATTN_HEADS * HEAD_DIM), BF16),
        scratch_shapes=[pltpu.VMEM((2, nkt, KEY_TILE, Q_BLOCK), F32),
                        pltpu.VMEM((2, 2, SUBLANES, LANES), F32),
                        pltpu.VMEM((nkt, ATTN_HEADS, KEY_TILE, Q_BLOCK), F32),
                        pltpu.VMEM((ATTN_HEADS // 2, KV_RANK, 2 * Q_BLOCK), F32),
                        pltpu.VMEM((SUBLANES, LANES), F32)],
        compiler_params=_cparams(("parallel", "arbitrary")),
        name="dsa",
    )(qt, qnorm, zq, qit, wt, qit, wt, ki, kv, kvt, tables, bias_max, wuv)


def _pack3(v):
    lane = lax.broadcasted_iota(jnp.int32, v.shape, 1)
    v = jnp.where(lane < SSD_HEADS, v, 0.0)
    hi = v.astype(BF16).astype(F32)
    r = v - hi
    mid = r.astype(BF16).astype(F32)
    lo = r - mid
    return (hi + pltpu.roll(mid, SSD_HEADS, axis=1) + pltpu.roll(lo, 2 * SSD_HEADS, axis=1)).astype(BF16)


def _cumsum_rows(x):
    n = x.shape[0]
    r = lax.broadcasted_iota(jnp.int32, x.shape, 0)
    s = 1
    while s < n:
        x = x + jnp.where(r >= s, pltpu.roll(x, s, axis=0), 0.0)
        s *= 2
    return x


def _shift_rows(x, s):
    r = pltpu.roll(x, s, axis=1)
    prev = jnp.concatenate([r[-1:], r[:-1]], axis=0)
    sub = lax.broadcasted_iota(jnp.int32, x.shape, 1)
    return jnp.where(sub >= s, r, prev)


def _ssd_kernel(z_ref, xbc_ref, dt_ref, cw_ref, cb_ref, dtb_ref, alog_ref, dsk_ref, nw_ref, e_ref,
                y_ref, tail_ref, u_ref, g_ref, state_ref):
    nq = SSD_Q

    @pl.when(pl.program_id(1) == 0)
    def _():
        tail_ref[...] = jnp.zeros_like(tail_ref)
        state_ref[...] = jnp.zeros_like(state_ref)

    assert CONV_W == 4
    for blk in range(CONV_DIM // CONV_BLOCK):
        cols = slice(blk * CONV_BLOCK, (blk + 1) * CONV_BLOCK)
        ext = jnp.concatenate([tail_ref[:, cols], xbc_ref[0, :, cols].astype(F32)], axis=0)
        ext = ext.reshape(1 + nq // SUBLANES, SUBLANES, CONV_BLOCK)
        s1 = _shift_rows(ext, 1)
        a = cw_ref[3:4, cols] * ext + cw_ref[2:3, cols] * s1 + cb_ref[:, cols]
        b = cw_ref[1:2, cols] * ext + cw_ref[0:1, cols] * s1
        conv = (a + _shift_rows(b, 2))[1:].reshape(nq, CONV_BLOCK)
        u_ref[:, cols] = _silu_of_half(conv)
        tail_ref[:, cols] = ext[nq // SUBLANES]

    t = dt_ref[0] + dtb_ref[...]
    dt = jnp.maximum(t, 0.0) + jnp.log1p(jnp.exp(-jnp.abs(t)))
    a2 = _cumsum_rows(dt * (-jnp.exp(alog_ref[...]))) * LOG2E
    a2_t = a2.T
    dt_p = _pack3(dt)
    dec_p = _pack3(dt * jnp.exp2(a2[nq - 1:nq, :] - a2))
    expa_p = _pack3(jnp.exp2(a2))

    r = lax.broadcasted_iota(jnp.int32, (nq, nq), 0)
    c = lax.broadcasted_iota(jnp.int32, (nq, nq), 1)
    causal = r >= c
    lane = lax.broadcasted_iota(jnp.int32, (nq, LANES), 1)
    heads_per_group = SSD_HEADS // SSD_GROUPS
    gw = heads_per_group * SSD_HEADDIM
    b_col = D_INNER
    c_col = D_INNER + SSD_GROUPS * D_STATE
    ssq = jnp.zeros((nq, LANES), F32)

    for g in range(SSD_GROUPS):
        gcols = slice(g * gw, (g + 1) * gw)
        eg = e_ref[:, gcols]
        dt_e = jnp.dot(dt_p, eg, preferred_element_type=F32)
        dec_e = jnp.dot(dec_p, eg, preferred_element_type=F32)
        expa_e = jnp.dot(expa_p, eg, preferred_element_type=F32)
        xs = u_ref[:, gcols]
        xdt_b = (xs * dt_e).astype(BF16)
        xdec_b = (xs * dec_e).astype(BF16)
        cg = u_ref[:, c_col + g * D_STATE:c_col + (g + 1) * D_STATE].astype(BF16)
        bgt = u_ref[:, b_col + g * D_STATE:b_col + (g + 1) * D_STATE].T.astype(BF16)
        cb = jnp.dot(cg, bgt, preferred_element_type=F32)
        prev = state_ref[g]
        y_off = jnp.dot(cg, prev.astype(BF16), preferred_element_type=F32) * expa_e
        pairs = []
        for pp in range(heads_per_group // 2):
            xp = xdt_b[:, pp * LANES:(pp + 1) * LANES]
            yh = []
            for hh in range(2):
                h = heads_per_group * g + 2 * pp + hh
                seg = a2[:, h:h + 1] - a2_t[h:h + 1, :]
                m = (cb * jnp.where(causal, jnp.exp2(seg), 0.0)).astype(BF16)
                yh.append(jnp.dot(m, xp, preferred_element_type=F32))
            pairs.append(jnp.where(lane < SSD_HEADDIM, yh[0], yh[1]))
        y = jnp.concatenate(pairs, axis=1) + y_off + dsk_ref[:, gcols] * xs
        gated = y * _silu_of_half(z_ref[0, :, gcols].astype(F32))
        g_ref[:, gcols] = gated
        sq = gated * gated
        ssq = ssq + sq[:, :LANES] + sq[:, LANES:]
        new = jnp.dot(bgt, xdec_b, preferred_element_type=F32)
        state_ref[g] = prev * expa_e[nq - 1:nq, :] + new
    scale = lax.rsqrt(jnp.sum(ssq, axis=-1, keepdims=True) * (1.0 / D_INNER) + EPS)
    y_ref[0] = (g_ref[...] * scale * nw_ref[...]).astype(BF16)


def _ssd(head, wide, conv_w, conv_b, dt_bias, a_log, d_skip, ssd_norm):
    bsz, seq, _ = head.shape
    nq = SSD_Q

    def pad_heads(v):
        return jnp.concatenate([v, jnp.zeros((LANES - SSD_HEADS,), F32)]).reshape(1, LANES)

    sel = np.concatenate([np.eye(SSD_HEADS)] * 3 + [np.zeros((LANES - 3 * SSD_HEADS, SSD_HEADS))], axis=0)
    e = jnp.asarray(np.kron(sel, np.ones((1, SSD_HEADDIM))), BF16)
    return pl.pallas_call(
        _ssd_kernel,
        grid=(bsz, seq // nq),
        in_specs=[pl.BlockSpec((1, nq, D_INNER), lambda b, i: (b, i, WIDE_Z // D_INNER)),
                  pl.BlockSpec((1, nq, CONV_DIM), lambda b, i: (b, i, WIDE_XBC // CONV_DIM)),
                  pl.BlockSpec((1, nq, LANES), lambda b, i: (b, i, HEAD_DT // LANES)),
                  _const_spec((CONV_W, CONV_DIM)), _const_spec((1, CONV_DIM)),
                  _const_spec((1, LANES)), _const_spec((1, LANES)),
                  _const_spec((1, D_INNER)), _const_spec((1, D_INNER)),
                  _const_spec((LANES, D_INNER))],
        out_specs=pl.BlockSpec((1, nq, D_INNER), lambda b, i: (b, i, 0)),
        out_shape=jax.ShapeDtypeStruct((bsz, seq, D_INNER), BF16),
        scratch_shapes=[pltpu.VMEM((SUBLANES, CONV_DIM), F32),
                        pltpu.VMEM((nq, CONV_DIM), F32),
                        pltpu.VMEM((nq, D_INNER), F32),
                        pltpu.VMEM((SSD_GROUPS, D_STATE, 4 * SSD_HEADDIM), F32)],
        compiler_params=_cparams(("parallel", "arbitrary")),
        name="ssd",
    )(wide, wide, head, 0.5 * conv_w, 0.5 * conv_b.reshape(1, CONV_DIM), pad_heads(dt_bias), pad_heads(a_log),
      jnp.repeat(d_skip, SSD_HEADDIM).reshape(1, D_INNER), ssd_norm.reshape(1, D_INNER), e)


def _mix_kernel(ao_ref, sy_ref, ga_ref, gb_ref, x_ref, mod_ref, nw_ref, woa_ref, wos_ref, wout_ref, o_ref):
    ya = jnp.dot(ao_ref[0], woa_ref[...], preferred_element_type=F32)
    yb = jnp.dot(sy_ref[0], wos_ref[...], preferred_element_type=F32)
    mix = jax.nn.sigmoid(ga_ref[0].astype(F32)) * ya + jax.nn.sigmoid(gb_ref[0].astype(F32)) * yb
    m2 = jnp.dot(mix.astype(BF16), wout_ref[...], preferred_element_type=F32)
    o_ref[0] = x_ref[0] + mod_ref[0][2:3] * _rms(m2, nw_ref[...])


def _mix(attn_o, ssd_y, wide, x, mod3, post_norm, w_o_attn, w_o_ssd, w_out):
    bsz, seq, _ = x.shape
    tm = min(seq, 512)

    def rows(width, col_block=0):
        return pl.BlockSpec((1, tm, width), lambda b, i: (b, i, col_block))

    return pl.pallas_call(
        _mix_kernel,
        grid=(bsz, seq // tm),
        in_specs=[rows(D_MODEL), rows(D_INNER), rows(D_MODEL, WIDE_GA // D_MODEL), rows(D_MODEL, WIDE_GB // D_MODEL),
                  rows(D_MODEL), pl.BlockSpec((1, 6, D_MODEL), lambda b, i: (b, 0, 0)),
                  _const_spec((1, D_MODEL)), _const_spec((D_MODEL, D_MODEL)),
                  _const_spec((D_INNER, D_MODEL)), _const_spec((D_MODEL, D_MODEL))],
        out_specs=rows(D_MODEL),
        out_shape=jax.ShapeDtypeStruct((bsz, seq, D_MODEL), F32),
        compiler_params=_cparams(("parallel", "parallel")),
        name="mix",
    )(attn_o, ssd_y, wide, wide, x, mod3, post_norm.reshape(1, D_MODEL),
      w_o_attn.astype(BF16), w_o_ssd.astype(BF16), w_out.astype(BF16))


def _ffn_kernel(x_ref, mod_ref, nw1_ref, nw2_ref, wg_ref, wu_ref, wo_ref, o_ref):
    x = x_ref[0]
    m = mod_ref[0]
    h2 = (_rms(x, nw1_ref[...]) * (1.0 + m[4:5]) + m[3:4]).astype(BF16)
    ug = jnp.dot(h2, wg_ref[...], preferred_element_type=F32)
    uu = jnp.dot(h2, wu_ref[...], preferred_element_type=F32)
    f = jnp.dot((_silu(ug) * uu).astype(BF16), wo_ref[...], preferred_element_type=F32)
    o_ref[0] = x + m[5:6] * _rms(f, nw2_ref[...])


def _ffn(x, mod3, pre_norm, post_norm, w_ffn_in, w_ffn_out):
    bsz, seq, _ = x.shape
    tm = min(seq, 512)
    rows = pl.BlockSpec((1, tm, D_MODEL), lambda b, i: (b, i, 0))
    w_in = w_ffn_in.astype(BF16)

    def half(k):
        return pl.BlockSpec((D_MODEL, D_FF), lambda b, i: (0, k), pipeline_mode=pl.Buffered(1))

    return pl.pallas_call(
        _ffn_kernel,
        grid=(bsz, seq // tm),
        in_specs=[rows, pl.BlockSpec((1, 6, D_MODEL), lambda b, i: (b, 0, 0)),
                  _const_spec((1, D_MODEL)), _const_spec((1, D_MODEL)),
                  half(0), half(1), _const_spec((D_FF, D_MODEL))],
        out_specs=rows,
        out_shape=jax.ShapeDtypeStruct((bsz, seq, D_MODEL), F32),
        compiler_params=_cparams(("parallel", "parallel")),
        name="ffn",
    )(x, mod3, pre_norm.reshape(1, D_MODEL), post_norm.reshape(1, D_MODEL), w_in, w_in, w_ffn_out.astype(BF16))


def kernel(x, c, positions, ada_w, ada_b, pre_norm_mix, post_norm_mix, pre_norm_ffn, post_norm_ffn, w_in, q_norm, kv_norm, w_uq, w_uv, rel_bias, w_qidx, kidx_norm, conv_w, conv_b, dt_bias, a_log, d_skip, ssd_norm, w_o_attn, w_o_ssd, w_out, w_ffn_in, w_ffn_out):
    del positions
    bsz, seq, _ = x.shape
    assert seq % (2 * KEY_TILE) == 0 and x.shape[-1] == D_MODEL
    mod3 = _mod(c, ada_w, ada_b).reshape(bsz, 6, D_MODEL)
    head, wide = _inproj(x, mod3, pre_norm_mix, _pack_w_in(w_in))
    qt, qit, kv, kvt, ki, wt, qnorm = _prep(head, q_norm, kv_norm, kidx_norm, w_uq, w_qidx)
    bias_max = jnp.max(rel_bias - rel_bias[NUM_BUCKETS // 2 - 1], axis=0) * LOG2E
    attn_o = _dsa(qt, qnorm, qit, wt, ki, kv, kvt, _bias_tables(rel_bias), bias_max, _pack_w_uv(w_uv))
    ssd_y = _ssd(head, wide, conv_w, conv_b, dt_bias, a_log, d_skip, ssd_norm)
    x1 = _mix(attn_o, ssd_y, wide, x, mod3, post_norm_mix, w_o_attn, w_o_ssd, w_out)
    return _ffn(x1, mod3, pre_norm_ffn, post_norm_ffn, w_ffn_in, w_ffn_out)
```

```python
import functools
import math
import statistics

import numpy as np
import jax
import jax.numpy as jnp
from jax import lax
from jax.experimental import pallas as pl
from jax.experimental.pallas import tpu as pltpu

F32 = jnp.float32
BF16 = jnp.bfloat16

D_MODEL = 1024
CHUNK = 64
Q_BLOCK = 128
EPS = 1e-6
ATTN_HEADS = 16
HEAD_DIM = 64
Q_RANK = 256
KV_RANK = 128
IDX_HEADS = 16
IDX_DIM = 64
TOPK_MAX = 256
NUM_BUCKETS = 32
MAX_DISTANCE = 128
D_INNER = 2 * D_MODEL
SSD_HEADDIM = 64
SSD_HEADS = D_INNER // SSD_HEADDIM
SSD_GROUPS = 8
D_STATE = 128
CONV_W = 4
CONV_DIM = D_INNER + 2 * SSD_GROUPS * D_STATE
D_FF = -(-8 * D_MODEL // (3 * 256)) * 256

LANES = 128
SUBLANES = 8
KEY_TILE = 256
SSD_Q = 128
CONV_BLOCK = 512
VMEM_LIMIT = 56 * 1024 * 1024
BISECT_MAX_ITERS = 320
BISECT_UNROLL = 4
BISECT_COARSE_ITERS = 10
BRACKET_HALF_WIDTH = 0.3
FLOAT_BIG = 3.0e38
LOG2E = math.log2(math.e)
BOUND_SLACK = 1.0 + 2.0 ** -6
UNDERFLOW_GUARD = 2.0 ** -80

HEAD_COLS = 768
HEAD_KV, HEAD_KIDX, HEAD_W, HEAD_DT = 256, 384, 512, 640
WIDE_XBC, WIDE_Z, WIDE_GA, WIDE_GB = 0, 4096, 6144, 7168
WIDE_COLS = 8192
INPROJ_TN = 1024


def _cparams(sem):
    return pltpu.CompilerParams(dimension_semantics=sem, vmem_limit_bytes=VMEM_LIMIT)


def _const_spec(shape):
    nd = len(shape)
    return pl.BlockSpec(shape, lambda *_: (0,) * nd, pipeline_mode=pl.Buffered(1))


def _rms(x, w, n=None):
    n = x.shape[-1] if n is None else n
    return x * lax.rsqrt(jnp.sum(x * x, axis=-1, keepdims=True) * (1.0 / n) + EPS) * w


def _silu_of_half(h):
    return h + h * jnp.tanh(h)


def _silu(x):
    return _silu_of_half(0.5 * x)


def _mod_kernel(c_ref, w_ref, b_ref, o_ref):
    c = c_ref[...]
    s = _silu(c).astype(BF16)
    o_ref[...] = jnp.dot(s, w_ref[...].astype(BF16), preferred_element_type=F32) + b_ref[...]


def _mod(c, ada_w, ada_b):
    bsz = c.shape[0]
    return pl.pallas_call(
        _mod_kernel,
        grid=(6,),
        in_specs=[pl.BlockSpec((bsz, D_MODEL), lambda j: (0, 0)),
                  pl.BlockSpec((D_MODEL, D_MODEL), lambda j: (0, j)),
                  pl.BlockSpec((1, D_MODEL), lambda j: (0, j))],
        out_specs=pl.BlockSpec((bsz, D_MODEL), lambda j: (0, j)),
        out_shape=jax.ShapeDtypeStruct((bsz, 6 * D_MODEL), F32),
        compiler_params=_cparams(("parallel",)),
        name="mod",
    )(c, ada_w, ada_b.reshape(1, 6 * D_MODEL))


def _t5_bucket_np(rel):
    half = NUM_BUCKETS // 2
    max_exact = half // 2
    side = np.where(rel > 0, half, 0)
    n = np.abs(rel)
    large = max_exact + (np.log(np.maximum(n, max_exact).astype(np.float64) / max_exact)
                         / math.log(MAX_DISTANCE / max_exact) * (half - max_exact)).astype(np.int64)
    large = np.minimum(large, half - 1)
    return (side + np.where(n < max_exact, n, large)).astype(np.int32)


def _bias_kernel(idx_ref, rb_ref, o_ref):
    h = pl.program_id(0)
    idx = idx_ref[...]
    far = rb_ref[h, NUM_BUCKETS // 2 - 1]
    acc = jnp.zeros(idx.shape, F32)
    for b in range(NUM_BUCKETS):
        acc = jnp.where(idx == b, (rb_ref[h, b] - far) * LOG2E, acc)
    o_ref[0] = acc


def _bias_tables(rel_bias):
    kk = np.arange(2 * KEY_TILE)[:, None]
    ql = np.arange(Q_BLOCK)[None, :]
    idx = jnp.asarray(_t5_bucket_np(kk - KEY_TILE - ql))
    t = pl.pallas_call(
        _bias_kernel,
        grid=(ATTN_HEADS,),
        in_specs=[pl.BlockSpec((2 * KEY_TILE, Q_BLOCK), lambda h: (0, 0)),
                  pl.BlockSpec(memory_space=pltpu.SMEM)],
        out_specs=pl.BlockSpec((1, 2 * KEY_TILE, Q_BLOCK), lambda h: (h, 0, 0)),
        out_shape=jax.ShapeDtypeStruct((ATTN_HEADS, 2 * KEY_TILE, Q_BLOCK), F32),
        compiler_params=_cparams(("arbitrary",)),
        name="bias",
    )(idx, rel_bias.T)
    return jnp.stack([t[:, 256:512], t[:, 128:384], t[:, 0:256]])


def _inproj_kernel(x_ref, mod_ref, nw_ref, w_ref, head_ref, wide_ref):
    m = mod_ref[0]
    hn = (_rms(x_ref[0], nw_ref[...]) * (1.0 + m[1:2]) + m[0:1]).astype(BF16)
    head_ref[0] = _dot_nt(hn, w_ref[0:HEAD_COLS, :])
    for c in range(WIDE_COLS // INPROJ_TN):
        rows = slice(HEAD_COLS + c * INPROJ_TN, HEAD_COLS + (c + 1) * INPROJ_TN)
        wide_ref[0, :, c * INPROJ_TN:(c + 1) * INPROJ_TN] = _dot_nt(hn, w_ref[rows, :]).astype(BF16)


def _pack_w_in(w_in):
    sizes = [Q_RANK, KV_RANK, IDX_DIM, IDX_HEADS, D_INNER, CONV_DIM, SSD_HEADS, D_MODEL, D_MODEL]
    offs = np.cumsum([0] + sizes)
    wt = w_in.T
    q, kv, ki, wi, z, xbc, dt, ga, gb = [wt[offs[i]:offs[i + 1]] for i in range(9)]

    def zr(n):
        return jnp.zeros((n, D_MODEL), w_in.dtype)

    return jnp.concatenate([q, kv, ki, zr(LANES - IDX_DIM), wi, zr(LANES - IDX_HEADS),
                            dt, zr(LANES - SSD_HEADS), xbc, 0.5 * z, ga, gb], axis=0).astype(BF16)


def _dot_nt(a, b):
    return lax.dot_general(a, b, (((1,), (1,)), ((), ())), preferred_element_type=F32)


def _inproj(x, mod3, pre_norm, w_packed):
    bsz, seq, _ = x.shape
    tm = min(seq, 512)
    return pl.pallas_call(
        _inproj_kernel,
        grid=(bsz, seq // tm),
        in_specs=[pl.BlockSpec((1, tm, D_MODEL), lambda b, i: (b, i, 0)),
                  pl.BlockSpec((1, 6, D_MODEL), lambda b, i: (b, 0, 0)),
                  _const_spec((1, D_MODEL)),
                  _const_spec((HEAD_COLS + WIDE_COLS, D_MODEL))],
        out_specs=[pl.BlockSpec((1, tm, HEAD_COLS), lambda b, i: (b, i, 0)),
                   pl.BlockSpec((1, tm, WIDE_COLS), lambda b, i: (b, i, 0))],
        out_shape=[jax.ShapeDtypeStruct((bsz, seq, HEAD_COLS), F32),
                   jax.ShapeDtypeStruct((bsz, seq, WIDE_COLS), BF16)],
        compiler_params=_cparams(("parallel", "parallel")),
        name="inproj",
    )(x, mod3, pre_norm.reshape(1, D_MODEL), w_packed)


def _prep_kernel(p_ref, qn_ref, kvn_ref, kin_ref, wuqt_ref, wqit_ref,
                 qt_ref, qit_ref, kv_ref, kvt_ref, ki_ref, wt_ref, qnorm_ref, *, tc):
    p = p_ref[0]
    qnt = _rms(p[:, :Q_RANK], qn_ref[...]).T.astype(BF16)
    qf = jnp.dot(wuqt_ref[...], qnt, preferred_element_type=F32) * (KV_RANK ** -0.5 * LOG2E)
    qt = qf.astype(BF16)
    qnorm = jnp.sqrt(jnp.sum((qf * qf).reshape(ATTN_HEADS, KV_RANK, tc), axis=1))
    qit = jnp.dot(wqit_ref[...], qnt, preferred_element_type=F32).astype(BF16)
    wt = (p[:, HEAD_W:HEAD_W + LANES] * (IDX_HEADS ** -0.5 * IDX_DIM ** -0.5)).T
    for blk in range(tc // Q_BLOCK):
        cols = slice(blk * Q_BLOCK, (blk + 1) * Q_BLOCK)
        for h in range(ATTN_HEADS):
            qt_ref[0, blk, :, h * LANES:(h + 1) * LANES] = qt[h * LANES:(h + 1) * LANES, cols]
            qit_ref[0, blk, :, h * LANES:(h + 1) * LANES] = qit[h * IDX_DIM:(h + 1) * IDX_DIM, cols]
        wt_ref[0, blk] = wt[0:IDX_HEADS, cols]
        qnorm_ref[0, blk] = qnorm[:, cols]
    kv = _rms(p[:, HEAD_KV:HEAD_KV + KV_RANK], kvn_ref[...])
    kv_ref[0] = kv.astype(BF16)
    kvt = kv.T
    for c in range(tc // KEY_TILE):
        kvt_ref[0, c] = kvt[:, c * KEY_TILE:(c + 1) * KEY_TILE].astype(BF16)
    ki_ref[0] = _rms(p[:, HEAD_KIDX:HEAD_KIDX + LANES], kin_ref[...], n=IDX_DIM).astype(BF16)


def _prep(proj, q_norm, kv_norm, kidx_norm, w_uq, w_qidx):
    bsz, seq, _ = proj.shape
    tc = min(seq, 512)
    kin = jnp.concatenate([kidx_norm, jnp.zeros((LANES - IDX_DIM,), F32)]).reshape(1, LANES)
    nb = seq // Q_BLOCK

    def slab(rows):
        return (jax.ShapeDtypeStruct((bsz, nb, rows, ATTN_HEADS * Q_BLOCK), BF16),
                pl.BlockSpec((1, tc // Q_BLOCK, rows, ATTN_HEADS * Q_BLOCK), lambda b, i: (b, i, 0, 0)))

    (q_slab, q_spec), (qi_slab, qi_spec) = slab(KV_RANK), slab(IDX_DIM)
    row_spec = pl.BlockSpec((1, tc, LANES), lambda b, i: (b, i, 0))
    return pl.pallas_call(
        functools.partial(_prep_kernel, tc=tc),
        grid=(bsz, seq // tc),
        in_specs=[pl.BlockSpec((1, tc, HEAD_COLS), lambda b, i: (b, i, 0)),
                  _const_spec((1, Q_RANK)), _const_spec((1, KV_RANK)), _const_spec((1, LANES)),
                  _const_spec((ATTN_HEADS * KV_RANK, Q_RANK)), _const_spec((IDX_HEADS * IDX_DIM, Q_RANK))],
        out_specs=[q_spec, qi_spec, row_spec,
                   pl.BlockSpec((1, tc // KEY_TILE, LANES, KEY_TILE), lambda b, i: (b, i, 0, 0)),
                   row_spec,
                   pl.BlockSpec((1, tc // Q_BLOCK, IDX_HEADS, Q_BLOCK), lambda b, i: (b, i, 0, 0)),
                   pl.BlockSpec((1, tc // Q_BLOCK, ATTN_HEADS, Q_BLOCK), lambda b, i: (b, i, 0, 0))],
        out_shape=[q_slab, qi_slab, jax.ShapeDtypeStruct((bsz, seq, LANES), BF16),
                   jax.ShapeDtypeStruct((bsz, seq // KEY_TILE, LANES, KEY_TILE), BF16),
                   jax.ShapeDtypeStruct((bsz, seq, LANES), BF16),
                   jax.ShapeDtypeStruct((bsz, nb, IDX_HEADS, Q_BLOCK), F32),
                   jax.ShapeDtypeStruct((bsz, nb, ATTN_HEADS, Q_BLOCK), F32)],
        compiler_params=_cparams(("parallel", "parallel")),
        name="prep",
    )(proj, q_norm.reshape(1, Q_RANK), kv_norm.reshape(1, KV_RANK), kin, w_uq.T.astype(BF16),
      w_qidx.T.astype(BF16))


def _dsa_kernel(qt_ref, qn_ref, zq_ref, qit0_ref, wt0_ref, qitn_ref, wtn_ref, ki_ref, kv_ref, kvt_ref, tb_ref,
                bmax_ref, wuv_ref, o_ref, iscbuf_ref, mom_ref, sbuf_ref, acc_ref, kmax_ref, *, k_sel, n_blocks):
    i = pl.program_id(1)
    last = i // 2
    odd = i % 2
    n_tiles = last + 1
    isct_ref = iscbuf_ref.at[i % 2]
    qt = qt_ref[0, 0]
    row = lax.broadcasted_iota(jnp.int32, (KEY_TILE, Q_BLOCK), 0)
    col = lax.broadcasted_iota(jnp.int32, (KEY_TILE, Q_BLOCK), 1)

    def key_rows(j):
        return pl.ds(pl.multiple_of(j * KEY_TILE, KEY_TILE), KEY_TILE)

    n_pairs = ATTN_HEADS // 2

    def pair_cols(p):
        return slice(p * KEY_TILE, (p + 1) * KEY_TILE)

    def half(x, hh):
        return x[:, hh * Q_BLOCK:(hh + 1) * Q_BLOCK]

    ACCS = 4

    def fold_rows(x, op):
        x = x.reshape(KEY_TILE // (ACCS * SUBLANES), ACCS, SUBLANES, LANES)
        return op(op(x, axis=0), axis=0)

    def idx_tile(j, blk, qit_ref, wt_ref, slot):
        dst_ref = iscbuf_ref.at[slot]
        keys = ki_ref[0, key_rows(j), :]
        wt = wt_ref[0, 0]
        acc = jnp.zeros((KEY_TILE, Q_BLOCK), F32)
        pad = jnp.zeros((LANES - IDX_DIM, KEY_TILE), BF16)
        for p in range(n_pairs):
            qi = jnp.concatenate([qit_ref[0, 0, :, pair_cols(p)], pad], axis=0)
            s = jnp.dot(keys, qi, preferred_element_type=F32)
            for hh in range(2):
                h = 2 * p + hh
                acc = acc + wt[h:h + 1, :] * jnp.maximum(half(s, hh), 0.0)
        key_limit = blk * Q_BLOCK + jnp.where(col < CHUNK, CHUNK, 2 * CHUNK)
        visible = j * KEY_TILE + row < key_limit
        dst_ref[j] = jnp.where(visible, acc, -jnp.inf)
        seen = jnp.where(visible, acc, 0.0)
        old = jnp.where(j == 0, 0.0, mom_ref[slot])
        mom_ref[slot] = old + jnp.stack([fold_rows(seen, jnp.sum), fold_rows(seen * seen, jnp.sum)])

    def tile_loop(n, body, carry):
        def pair(jj, c):
            return body(2 * jj + 1, body(2 * jj, c))
        carry = lax.fori_loop(0, n // 2, pair, carry)
        return lax.cond(n % 2 == 1, lambda c: body(n - 1, c), lambda c: c, carry)

    @pl.when(i == 0)
    def _():
        idx_tile(0, 0, qit0_ref, wt0_ref, 0)
        best = jnp.zeros((1, KEY_TILE), F32)
        for j in range(kvt_ref.shape[1]):
            x = kvt_ref[0, j].astype(F32)
            best = jnp.maximum(best, jnp.sum(x * x, axis=0, keepdims=True))
        kmax_ref[...] = jnp.broadcast_to(jnp.sqrt(jnp.max(best, axis=1, keepdims=True)), (SUBLANES, LANES))

    def rows_all(x, op):
        return jnp.broadcast_to(op(x, axis=0, keepdims=True), (SUBLANES, LANES))

    def tile_rows(j):
        return isct_ref[j].reshape(KEY_TILE // (ACCS * SUBLANES), ACCS, SUBLANES, LANES)

    def count_where(pred):
        def body(j, c):
            x = tile_rows(j)
            for r in range(x.shape[0]):
                c = jnp.where(pred(x[r]), c + 1.0, c)
            return c
        c = lax.fori_loop(0, n_tiles, body, jnp.zeros((ACCS, SUBLANES, LANES), F32))
        return rows_all(jnp.sum(c, axis=0), jnp.sum)

    kf = float(k_sel)

    def is_open(cnt, stalled):
        return jnp.logical_and(cnt > kf, stalled == 0.0)

    def any_lane(mask):
        return jnp.max(jnp.where(mask, 1.0, 0.0)) > 0.0

    lane8 = lax.broadcasted_iota(jnp.int32, (SUBLANES, LANES), 1)
    n_vis = (i * Q_BLOCK + jnp.where(lane8 < CHUNK, CHUNK, 2 * CHUNK)).astype(F32)
    mom = mom_ref[i % 2]
    mean = rows_all(mom[0], jnp.sum) / n_vis
    std = jnp.sqrt(jnp.maximum(rows_all(mom[1], jnp.sum) / n_vis - mean * mean, 0.0))
    z = zq_ref[0]
    takes_all = n_vis <= kf
    lo = jnp.where(takes_all, -FLOAT_BIG, mean + (z - BRACKET_HALF_WIDTH) * std)
    hi = jnp.where(takes_all, FLOAT_BIG, mean + (z + BRACKET_HALF_WIDTH) * std)
    cnt = count_where(lambda x: x >= lo[0:1])
    cnt_hi = count_where(lambda x: x >= hi[0:1])
    low_ok = cnt >= kf
    high_ok = cnt_hi < kf
    lo = jnp.where(low_ok, lo, -FLOAT_BIG)
    cnt = jnp.where(low_ok, cnt, n_vis)
    hi = jnp.where(high_ok, hi, FLOAT_BIG)
    cnt_hi = jnp.where(high_ok, cnt_hi, 0.0)

    def bis_step(c):
        lo, hi, cnt, cnt_hi, stalled = c
        mid = 0.5 * lo + 0.5 * hi
        cm = count_where(lambda x: x >= mid[0:1])
        active = is_open(cnt, stalled)
        noprog = jnp.logical_or(mid <= lo, mid >= hi)
        move = jnp.logical_and(active, jnp.logical_not(noprog))
        up = jnp.logical_and(move, cm >= kf)
        down = jnp.logical_and(move, cm < kf)
        return (jnp.where(up, mid, lo), jnp.where(down, mid, hi), jnp.where(up, cm, cnt),
                jnp.where(down, cm, cnt_hi), jnp.where(jnp.logical_and(active, noprog), 1.0, stalled))

    def bisect(state, max_iters):
        def cond(c):
            return jnp.logical_and(c[5] < max_iters, any_lane(is_open(c[2], c[4])))

        def body(c):
            state = c[:5]
            for _ in range(BISECT_UNROLL):
                state = bis_step(state)
            return state + (c[5] + BISECT_UNROLL,)

        return lax.while_loop(cond, body, state + (jnp.int32(0),))[:5]

    def max_below(t):
        def body(j, c):
            x = tile_rows(j)
            return jnp.maximum(c, jnp.max(jnp.where(x < t, x, -jnp.inf), axis=0))
        c = lax.fori_loop(0, n_tiles, body, jnp.full((ACCS, SUBLANES, LANES), -jnp.inf, F32))
        return rows_all(jnp.max(c, axis=0), jnp.max)

    def walk(c):
        top, left = c
        return jnp.where(left > 0.0, max_below(top), top), jnp.maximum(left - 1.0, 0.0)

    state = (lo, hi, cnt, cnt_hi, jnp.zeros((SUBLANES, LANES), F32))
    for _ in range(BISECT_COARSE_ITERS):
        state = bis_step(state)
    lo, hi, cnt, cnt_hi, stalled = state
    walked = is_open(cnt, stalled)
    top, left = walk(walk((hi, jnp.where(walked, kf - cnt_hi, 0.0))))
    arrived = jnp.logical_and(walked, left == 0.0)
    lo = jnp.where(arrived, top, lo)
    hi = jnp.where(jnp.logical_and(walked, left > 0.0), top, hi)
    cnt = count_where(lambda x: x >= lo[0:1])

    def finish(state):
        thr8, _, cnt, _, _ = bisect(state, BISECT_MAX_ITERS)
        tied = cnt > kf

        @pl.when(any_lane(tied))
        def _():
            need = kf - count_where(lambda x: x > thr8[0:1])

            def body(j, seen):
                x = isct_ref[j]
                eq = jnp.where(x == thr8[0:1], 1.0, 0.0)
                inc = _cumsum_rows(eq)
                rank = inc - eq + seen[0:1]
                drop = jnp.logical_and(jnp.logical_and(tied[0:1], eq > 0.0), rank >= need[0:1])
                isct_ref[j] = jnp.where(drop, -jnp.inf, x)
                return seen + inc[KEY_TILE - 1:KEY_TILE]

            lax.fori_loop(0, n_tiles, body, jnp.zeros((SUBLANES, LANES), F32))

        return thr8

    thr8 = lax.cond(any_lane(cnt > kf), finish, lambda s: s[0], (lo, hi, cnt, cnt_hi, stalled))
    thr = thr8[0:1]

    n_far = jnp.maximum(last - 1 + odd, 0)
    next_slot = (i + 1) % 2

    zero_l = jnp.zeros((ATTN_HEADS, SUBLANES, LANES), F32)

    shift = [qn_ref[0, 0, h:h + 1, :] * kmax_ref[0:1, :] * BOUND_SLACK + (bmax_ref[h] + BOUND_SLACK)
             for h in range(ATTN_HEADS)]

    def sweep_tile(j, l, near):
        keys = kv_ref[0, key_rows(j), :]
        values_t = kvt_ref[0, j]
        neg = jnp.where(isct_ref[j] >= thr, 0.0, -jnp.inf)
        table = jnp.where(j == last, odd, 2)
        for p in range(n_pairs):
            s = jnp.dot(keys, qt[:, pair_cols(p)], preferred_element_type=F32)
            for hh in range(2):
                h = 2 * p + hh
                sh = half(s, hh) + neg
                sbuf_ref[j, h] = sh + tb_ref[table, h] if near else sh
        out = []
        for p in range(n_pairs):
            probs = []
            for hh in range(2):
                h = 2 * p + hh
                e = jnp.exp2(sbuf_ref[j, h] - shift[h])
                out.append(l[h] + fold_rows(e, jnp.sum))
                probs.append(e.astype(BF16))
            acc_ref[p] += jnp.dot(values_t, jnp.concatenate(probs, axis=1),
                                  preferred_element_type=F32)
        idx_tile(j, i + 1, qitn_ref, wtn_ref, next_slot)
        return jnp.stack(out)

    acc_ref[...] = jnp.zeros_like(acc_ref)
    l = tile_loop(n_far, lambda j, l: sweep_tile(j, l, False), zero_l)
    l = lax.fori_loop(n_far, n_tiles, lambda j, l: sweep_tile(j, l, True), l)
    l = jnp.sum(l, axis=1, keepdims=True)

    @pl.when(jnp.logical_and(odd == 1, i + 1 < n_blocks))
    def _():
        idx_tile(n_tiles, i + 1, qitn_ref, wtn_ref, next_slot)

    def p1_tile(j, m, near):
        keys = kv_ref[0, key_rows(j), :]
        neg = jnp.where(isct_ref[j] >= thr, 0.0, -jnp.inf)
        table = jnp.where(j == last, odd, 2)
        out = []
        for p in range(n_pairs):
            s = jnp.dot(keys, qt[:, pair_cols(p)], preferred_element_type=F32)
            for hh in range(2):
                h = 2 * p + hh
                sh = half(s, hh) + neg
                if near:
                    sh = sh + tb_ref[table, h]
                sbuf_ref[j, h] = sh
                out.append(jnp.maximum(m[h], fold_rows(sh, jnp.max)))
        return jnp.stack(out)

    def exact_softmax(_):
        m = jnp.full((ATTN_HEADS, SUBLANES, LANES), -jnp.inf, F32)
        m = tile_loop(n_far, lambda j, m: p1_tile(j, m, False), m)
        m = lax.fori_loop(n_far, n_tiles, lambda j, m: p1_tile(j, m, True), m)
        m = jnp.max(m, axis=1, keepdims=True)
        acc_ref[...] = jnp.zeros_like(acc_ref)

        def p2_body(j, l):
            values_t = kvt_ref[0, j]
            out = []
            for p in range(n_pairs):
                probs = []
                for hh in range(2):
                    h = 2 * p + hh
                    e = jnp.exp2(sbuf_ref[j, h] - m[h])
                    out.append(l[h] + fold_rows(e, jnp.sum))
                    probs.append(e.astype(BF16))
                acc_ref[p] += jnp.dot(values_t, jnp.concatenate(probs, axis=1),
                                      preferred_element_type=F32)
            return jnp.stack(out)

        return jnp.sum(tile_loop(n_tiles, p2_body, zero_l), axis=1, keepdims=True)

    underflowed = jnp.max(jnp.where(l > UNDERFLOW_GUARD, 0.0, 1.0)) > 0.0
    l = lax.cond(underflowed, exact_softmax, lambda l: l, l)

    outs = [(acc_ref[h // 2, :, (h % 2) * Q_BLOCK:(h % 2 + 1) * Q_BLOCK] / l[h]).T.astype(BF16)
            for h in range(ATTN_HEADS)]
    for p in range(ATTN_HEADS // 2):
        pair = jnp.concatenate(outs[2 * p:2 * p + 2], axis=1)
        o_ref[0, :, p * LANES:(p + 1) * LANES] = jnp.dot(
            pair, wuv_ref[p], preferred_element_type=F32).astype(BF16)


def _pack_w_uv(w_uv):
    eye = jnp.eye(2, dtype=w_uv.dtype)
    w = w_uv.reshape(ATTN_HEADS // 2, 2, KV_RANK, 1, HEAD_DIM) * eye[None, :, None, :, None]
    return w.reshape(ATTN_HEADS // 2, 2 * KV_RANK, 2 * HEAD_DIM).astype(BF16)


def _dsa(qt, qnorm, qit, wt, ki, kv, kvt, tables, bias_max, wuv):
    bsz, seq, _ = kv.shape
    nkt = seq // KEY_TILE
    k_sel = min(TOPK_MAX, seq // 4)
    nb = seq // Q_BLOCK
    slab = (1, 1, KV_RANK, ATTN_HEADS * Q_BLOCK)
    islab = (1, 1, IDX_DIM, IDX_HEADS * Q_BLOCK)
    wslab = (1, 1, IDX_HEADS, Q_BLOCK)

    def first(b, i):
        return (b, 0, 0, 0)

    def following(b, i):
        return (b, jnp.minimum(i + 1, nb - 1), 0, 0)

    seq_spec = pl.BlockSpec((1, seq, LANES), lambda b, i: (b, 0, 0))
    n_vis = (np.arange(nb)[:, None] * Q_BLOCK + np.where(np.arange(Q_BLOCK) < CHUNK, CHUNK, 2 * CHUNK)[None, :])
    quantile = np.array([[statistics.NormalDist().inv_cdf(1.0 - k_sel / n) if n > k_sel else 0.0 for n in r]
                         for r in n_vis], np.float32)
    zq = jnp.asarray(np.broadcast_to(quantile[:, None, :], (nb, SUBLANES, Q_BLOCK)))
    return pl.pallas_call(
        functools.partial(_dsa_kernel, k_sel=k_sel, n_blocks=nb),
        grid=(bsz, nb),
        in_specs=[pl.BlockSpec(slab, lambda b, i: (b, i, 0, 0)),
                  pl.BlockSpec(wslab, lambda b, i: (b, i, 0, 0)),
                  pl.BlockSpec((1, SUBLANES, Q_BLOCK), lambda b, i: (i, 0, 0)),
                  pl.BlockSpec(islab, first), pl.BlockSpec(wslab, first),
                  pl.BlockSpec(islab, following), pl.BlockSpec(wslab, following),
                  seq_spec, seq_spec,
                  pl.BlockSpec((1, nkt, LANES, KEY_TILE), lambda b, i: (b, 0, 0, 0)),
                  _const_spec((3, ATTN_HEADS, KEY_TILE, Q_BLOCK)),
                  pl.BlockSpec(memory_space=pltpu.SMEM),
                  _const_spec((ATTN_HEADS // 2, 2 * KV_RANK, 2 * HEAD_DIM))],
        out_specs=pl.BlockSpec((1, Q_BLOCK, ATTN_HEADS * HEAD_DIM), lambda b, i: (b, i, 0)),
        out_shape=jax.ShapeDtypeStruct((bsz, seq, ATTN_HEADS * HEAD_DIM), BF16),
        scratch_shapes=[pltpu.VMEM((2, nkt, KEY_TILE, Q_BLOCK), F32),
                        pltpu.VMEM((2, 2, SUBLANES, LANES), F32),
                        pltpu.VMEM((nkt, ATTN_HEADS, KEY_TILE, Q_BLOCK), F32),
                        pltpu.VMEM((ATTN_HEADS // 2, KV_RANK, 2 * Q_BLOCK), F32),
                        pltpu.VMEM((SUBLANES, LANES), F32)],
        compiler_params=_cparams(("parallel", "arbitrary")),
        name="dsa",
    )(qt, qnorm, zq, qit, wt, qit, wt, ki, kv, kvt, tables, bias_max, wuv)


def _pack3(v):
    lane = lax.broadcasted_iota(jnp.int32, v.shape, 1)
    v = jnp.where(lane < SSD_HEADS, v, 0.0)
    hi = v.astype(BF16).astype(F32)
    r = v - hi
    mid = r.astype(BF16).astype(F32)
    lo = r - mid
    return (hi + pltpu.roll(mid, SSD_HEADS, axis=1) + pltpu.roll(lo, 2 * SSD_HEADS, axis=1)).astype(BF16)


def _cumsum_rows(x):
    n = x.shape[0]
    r = lax.broadcasted_iota(jnp.int32, x.shape, 0)
    s = 1
    while s < n:
        x = x + jnp.where(r >= s, pltpu.roll(x, s, axis=0), 0.0)
        s *= 2
    return x


def _shift_rows(x, s):
    r = pltpu.roll(x, s, axis=1)
    prev = jnp.concatenate([r[-1:], r[:-1]], axis=0)
    sub = lax.broadcasted_iota(jnp.int32, x.shape, 1)
    return jnp.where(sub >= s, r, prev)


def _ssd_kernel(z_ref, xbc_ref, dt_ref, cw_ref, cb_ref, dtb_ref, alog_ref, dsk_ref, nw_ref, e_ref,
                y_ref, tail_ref, u_ref, g_ref, state_ref):
    nq = SSD_Q

    @pl.when(pl.program_id(1) == 0)
    def _():
        tail_ref[...] = jnp.zeros_like(tail_ref)
        state_ref[...] = jnp.zeros_like(state_ref)

    assert CONV_W == 4
    for blk in range(CONV_DIM // CONV_BLOCK):
        cols = slice(blk * CONV_BLOCK, (blk + 1) * CONV_BLOCK)
        ext = jnp.concatenate([tail_ref[:, cols], xbc_ref[0, :, cols].astype(F32)], axis=0)
        ext = ext.reshape(1 + nq // SUBLANES, SUBLANES, CONV_BLOCK)
        s1 = _shift_rows(ext, 1)
        a = cw_ref[3:4, cols] * ext + cw_ref[2:3, cols] * s1 + cb_ref[:, cols]
        b = cw_ref[1:2, cols] * ext + cw_ref[0:1, cols] * s1
        conv = (a + _shift_rows(b, 2))[1:].reshape(nq, CONV_BLOCK)
        u_ref[:, cols] = _silu_of_half(conv)
        tail_ref[:, cols] = ext[nq // SUBLANES]

    t = dt_ref[0] + dtb_ref[...]
    dt = jnp.maximum(t, 0.0) + jnp.log1p(jnp.exp(-jnp.abs(t)))
    a2 = _cumsum_rows(dt * (-jnp.exp(alog_ref[...]))) * LOG2E
    a2_t = a2.T
    dt_p = _pack3(dt)
    dec_p = _pack3(dt * jnp.exp2(a2[nq - 1:nq, :] - a2))
    expa_p = _pack3(jnp.exp2(a2))

    r = lax.broadcasted_iota(jnp.int32, (nq, nq), 0)
    c = lax.broadcasted_iota(jnp.int32, (nq, nq), 1)
    causal = r >= c
    lane = lax.broadcasted_iota(jnp.int32, (nq, LANES), 1)
    heads_per_group = SSD_HEADS // SSD_GROUPS
    gw = heads_per_group * SSD_HEADDIM
    b_col = D_INNER
    c_col = D_INNER + SSD_GROUPS * D_STATE
    ssq = jnp.zeros((nq, LANES), F32)

    for g in range(SSD_GROUPS):
        gcols = slice(g * gw, (g + 1) * gw)
        eg = e_ref[:, gcols]
        dt_e = jnp.dot(dt_p, eg, preferred_element_type=F32)
        dec_e = jnp.dot(dec_p, eg, preferred_element_type=F32)
        expa_e = jnp.dot(expa_p, eg, preferred_element_type=F32)
        xs = u_ref[:, gcols]
        xdt_b = (xs * dt_e).astype(BF16)
        xdec_b = (xs * dec_e).astype(BF16)
        cg = u_ref[:, c_col + g * D_STATE:c_col + (g + 1) * D_STATE].astype(BF16)
        bgt = u_ref[:, b_col + g * D_STATE:b_col + (g + 1) * D_STATE].T.astype(BF16)
        cb = jnp.dot(cg, bgt, preferred_element_type=F32)
        prev = state_ref[g]
        y_off = jnp.dot(cg, prev.astype(BF16), preferred_element_type=F32) * expa_e
        pairs = []
        for pp in range(heads_per_group // 2):
            xp = xdt_b[:, pp * LANES:(pp + 1) * LANES]
            yh = []
            for hh in range(2):
                h = heads_per_group * g + 2 * pp + hh
                seg = a2[:, h:h + 1] - a2_t[h:h + 1, :]
                m = (cb * jnp.where(causal, jnp.exp2(seg), 0.0)).astype(BF16)
                yh.append(jnp.dot(m, xp, preferred_element_type=F32))
            pairs.append(jnp.where(lane < SSD_HEADDIM, yh[0], yh[1]))
        y = jnp.concatenate(pairs, axis=1) + y_off + dsk_ref[:, gcols] * xs
        gated = y * _silu_of_half(z_ref[0, :, gcols].astype(F32))
        g_ref[:, gcols] = gated
        sq = gated * gated
        ssq = ssq + sq[:, :LANES] + sq[:, LANES:]
        new = jnp.dot(bgt, xdec_b, preferred_element_type=F32)
        state_ref[g] = prev * expa_e[nq - 1:nq, :] + new
    scale = lax.rsqrt(jnp.sum(ssq, axis=-1, keepdims=True) * (1.0 / D_INNER) + EPS)
    y_ref[0] = (g_ref[...] * scale * nw_ref[...]).astype(BF16)


def _ssd(head, wide, conv_w, conv_b, dt_bias, a_log, d_skip, ssd_norm):
    bsz, seq, _ = head.shape
    nq = SSD_Q

    def pad_heads(v):
        return jnp.concatenate([v, jnp.zeros((LANES - SSD_HEADS,), F32)]).reshape(1, LANES)

    sel = np.concatenate([np.eye(SSD_HEADS)] * 3 + [np.zeros((LANES - 3 * SSD_HEADS, SSD_HEADS))], axis=0)
    e = jnp.asarray(np.kron(sel, np.ones((1, SSD_HEADDIM))), BF16)
    return pl.pallas_call(
        _ssd_kernel,
        grid=(bsz, seq // nq),
        in_specs=[pl.BlockSpec((1, nq, D_INNER), lambda b, i: (b, i, WIDE_Z // D_INNER)),
                  pl.BlockSpec((1, nq, CONV_DIM), lambda b, i: (b, i, WIDE_XBC // CONV_DIM)),
                  pl.BlockSpec((1, nq, LANES), lambda b, i: (b, i, HEAD_DT // LANES)),
                  _const_spec((CONV_W, CONV_DIM)), _const_spec((1, CONV_DIM)),
                  _const_spec((1, LANES)), _const_spec((1, LANES)),
                  _const_spec((1, D_INNER)), _const_spec((1, D_INNER)),
                  _const_spec((LANES, D_INNER))],
        out_specs=pl.BlockSpec((1, nq, D_INNER), lambda b, i: (b, i, 0)),
        out_shape=jax.ShapeDtypeStruct((bsz, seq, D_INNER), BF16),
        scratch_shapes=[pltpu.VMEM((SUBLANES, CONV_DIM), F32),
                        pltpu.VMEM((nq, CONV_DIM), F32),
                        pltpu.VMEM((nq, D_INNER), F32),
                        pltpu.VMEM((SSD_GROUPS, D_STATE, 4 * SSD_HEADDIM), F32)],
        compiler_params=_cparams(("parallel", "arbitrary")),
        name="ssd",
    )(wide, wide, head, 0.5 * conv_w, 0.5 * conv_b.reshape(1, CONV_DIM), pad_heads(dt_bias), pad_heads(a_log),
      jnp.repeat(d_skip, SSD_HEADDIM).reshape(1, D_INNER), ssd_norm.reshape(1, D_INNER), e)


def _mix_kernel(ao_ref, sy_ref, ga_ref, gb_ref, x_ref, mod_ref, nw_ref, woa_ref, wos_ref, wout_ref, o_ref):
    ya = jnp.dot(ao_ref[0], woa_ref[...], preferred_element_type=F32)
    yb = jnp.dot(sy_ref[0], wos_ref[...], preferred_element_type=F32)
    mix = jax.nn.sigmoid(ga_ref[0].astype(F32)) * ya + jax.nn.sigmoid(gb_ref[0].astype(F32)) * yb
    m2 = jnp.dot(mix.astype(BF16), wout_ref[...], preferred_element_type=F32)
    o_ref[0] = x_ref[0] + mod_ref[0][2:3] * _rms(m2, nw_ref[...])


def _mix(attn_o, ssd_y, wide, x, mod3, post_norm, w_o_attn, w_o_ssd, w_out):
    bsz, seq, _ = x.shape
    tm = min(seq, 512)

    def rows(width, col_block=0):
        return pl.BlockSpec((1, tm, width), lambda b, i: (b, i, col_block))

    return pl.pallas_call(
        _mix_kernel,
        grid=(bsz, seq // tm),
        in_specs=[rows(D_MODEL), rows(D_INNER), rows(D_MODEL, WIDE_GA // D_MODEL), rows(D_MODEL, WIDE_GB // D_MODEL),
                  rows(D_MODEL), pl.BlockSpec((1, 6, D_MODEL), lambda b, i: (b, 0, 0)),
                  _const_spec((1, D_MODEL)), _const_spec((D_MODEL, D_MODEL)),
                  _const_spec((D_INNER, D_MODEL)), _const_spec((D_MODEL, D_MODEL))],
        out_specs=rows(D_MODEL),
        out_shape=jax.ShapeDtypeStruct((bsz, seq, D_MODEL), F32),
        compiler_params=_cparams(("parallel", "parallel")),
        name="mix",
    )(attn_o, ssd_y, wide, wide, x, mod3, post_norm.reshape(1, D_MODEL),
      w_o_attn.astype(BF16), w_o_ssd.astype(BF16), w_out.astype(BF16))


def _ffn_kernel(x_ref, mod_ref, nw1_ref, nw2_ref, wg_ref, wu_ref, wo_ref, o_ref):
    x = x_ref[0]
    m = mod_ref[0]
    h2 = (_rms(x, nw1_ref[...]) * (1.0 + m[4:5]) + m[3:4]).astype(BF16)
    ug = jnp.dot(h2, wg_ref[...], preferred_element_type=F32)
    uu = jnp.dot(h2, wu_ref[...], preferred_element_type=F32)
    f = jnp.dot((_silu(ug) * uu).astype(BF16), wo_ref[...], preferred_element_type=F32)
    o_ref[0] = x + m[5:6] * _rms(f, nw2_ref[...])


def _ffn(x, mod3, pre_norm, post_norm, w_ffn_in, w_ffn_out):
    bsz, seq, _ = x.shape
    tm = min(seq, 512)
    rows = pl.BlockSpec((1, tm, D_MODEL), lambda b, i: (b, i, 0))
    w_in = w_ffn_in.astype(BF16)

    def half(k):
        return pl.BlockSpec((D_MODEL, D_FF), lambda b, i: (0, k), pipeline_mode=pl.Buffered(1))

    return pl.pallas_call(
        _ffn_kernel,
        grid=(bsz, seq // tm),
        in_specs=[rows, pl.BlockSpec((1, 6, D_MODEL), lambda b, i: (b, 0, 0)),
                  _const_spec((1, D_MODEL)), _const_spec((1, D_MODEL)),
                  half(0), half(1), _const_spec((D_FF, D_MODEL))],
        out_specs=rows,
        out_shape=jax.ShapeDtypeStruct((bsz, seq, D_MODEL), F32),
        compiler_params=_cparams(("parallel", "parallel")),
        name="ffn",
    )(x, mod3, pre_norm.reshape(1, D_MODEL), post_norm.reshape(1, D_MODEL), w_in, w_in, w_ffn_out.astype(BF16))


def kernel(x, c, positions, ada_w, ada_b, pre_norm_mix, post_norm_mix, pre_norm_ffn, post_norm_ffn, w_in, q_norm, kv_norm, w_uq, w_uv, rel_bias, w_qidx, kidx_norm, conv_w, conv_b, dt_bias, a_log, d_skip, ssd_norm, w_o_attn, w_o_ssd, w_out, w_ffn_in, w_ffn_out):
    del positions
    bsz, seq, _ = x.shape
    assert seq % (2 * KEY_TILE) == 0 and x.shape[-1] == D_MODEL
    mod3 = _mod(c, ada_w, ada_b).reshape(bsz, 6, D_MODEL)
    head, wide = _inproj(x, mod3, pre_norm_mix, _pack_w_in(w_in))
    qt, qit, kv, kvt, ki, wt, qnorm = _prep(head, q_norm, kv_norm, kidx_norm, w_uq, w_qidx)
    bias_max = jnp.max(rel_bias - rel_bias[NUM_BUCKETS // 2 - 1], axis=0) * LOG2E
    attn_o = _dsa(qt, qnorm, qit, wt, ki, kv, kvt, _bias_tables(rel_bias), bias_max, _pack_w_uv(w_uv))
    ssd_y = _ssd(head, wide, conv_w, conv_b, dt_bias, a_log, d_skip, ssd_norm)
    x1 = _mix(attn_o, ssd_y, wide, x, mod3, post_norm_mix, w_o_attn, w_o_ssd, w_out)
    return _ffn(x1, mod3, pre_norm_ffn, post_norm_ffn, w_ffn_in, w_ffn_out)
```

```python
import functools
import math
import statistics

import numpy as np
import jax
import jax.numpy as jnp
from jax import lax
from jax.experimental import pallas as pl
from jax.experimental.pallas import tpu as pltpu

F32 = jnp.float32
BF16 = jnp.bfloat16

D_MODEL = 1024
CHUNK = 64
Q_BLOCK = 128
EPS = 1e-6
ATTN_HEADS = 16
HEAD_DIM = 64
Q_RANK = 256
KV_RANK = 128
IDX_HEADS = 16
IDX_DIM = 64
TOPK_MAX = 256
NUM_BUCKETS = 32
MAX_DISTANCE = 128
D_INNER = 2 * D_MODEL
SSD_HEADDIM = 64
SSD_HEADS = D_INNER // SSD_HEADDIM
SSD_GROUPS = 8
D_STATE = 128
CONV_W = 4
CONV_DIM = D_INNER + 2 * SSD_GROUPS * D_STATE
D_FF = -(-8 * D_MODEL // (3 * 256)) * 256

LANES = 128
SUBLANES = 8
KEY_TILE = 256
SSD_Q = 128
CONV_BLOCK = 512
VMEM_LIMIT = 56 * 1024 * 1024
BISECT_MAX_ITERS = 320
BISECT_UNROLL = 4
BISECT_COARSE_ITERS = 10
BRACKET_HALF_WIDTH = 0.3
FLOAT_BIG = 3.0e38
LOG2E = math.log2(math.e)
BOUND_SLACK = 1.0 + 2.0 ** -6
UNDERFLOW_GUARD = 2.0 ** -80

HEAD_COLS = 768
HEAD_KV, HEAD_KIDX, HEAD_W, HEAD_DT = 256, 384, 512, 640
WIDE_XBC, WIDE_Z, WIDE_GA, WIDE_GB = 0, 4096, 6144, 7168
WIDE_COLS = 8192
INPROJ_TN = 1024


def _cparams(sem):
    return pltpu.CompilerParams(dimension_semantics=sem, vmem_limit_bytes=VMEM_LIMIT)


def _const_spec(shape):
    nd = len(shape)
    return pl.BlockSpec(shape, lambda *_: (0,) * nd, pipeline_mode=pl.Buffered(1))


def _rms(x, w, n=None):
    n = x.shape[-1] if n is None else n
    return x * lax.rsqrt(jnp.sum(x * x, axis=-1, keepdims=True) * (1.0 / n) + EPS) * w


def _silu_of_half(h):
    return h + h * jnp.tanh(h)


def _silu(x):
    return _silu_of_half(0.5 * x)


def _mod_kernel(c_ref, w_ref, b_ref, o_ref):
    c = c_ref[...]
    s = _silu(c).astype(BF16)
    o_ref[...] = jnp.dot(s, w_ref[...].astype(BF16), preferred_element_type=F32) + b_ref[...]


def _mod(c, ada_w, ada_b):
    bsz = c.shape[0]
    return pl.pallas_call(
        _mod_kernel,
        grid=(6,),
        in_specs=[pl.BlockSpec((bsz, D_MODEL), lambda j: (0, 0)),
                  pl.BlockSpec((D_MODEL, D_MODEL), lambda j: (0, j)),
                  pl.BlockSpec((1, D_MODEL), lambda j: (0, j))],
        out_specs=pl.BlockSpec((bsz, D_MODEL), lambda j: (0, j)),
        out_shape=jax.ShapeDtypeStruct((bsz, 6 * D_MODEL), F32),
        compiler_params=_cparams(("parallel",)),
        name="mod",
    )(c, ada_w, ada_b.reshape(1, 6 * D_MODEL))


def _t5_bucket_np(rel):
    half = NUM_BUCKETS // 2
    max_exact = half // 2
    side = np.where(rel > 0, half, 0)
    n = np.abs(rel)
    large = max_exact + (np.log(np.maximum(n, max_exact).astype(np.float64) / max_exact)
                         / math.log(MAX_DISTANCE / max_exact) * (half - max_exact)).astype(np.int64)
    large = np.minimum(large, half - 1)
    return (side + np.where(n < max_exact, n, large)).astype(np.int32)


def _bias_kernel(idx_ref, rb_ref, o_ref):
    h = pl.program_id(0)
    idx = idx_ref[...]
    far = rb_ref[h, NUM_BUCKETS // 2 - 1]
    acc = jnp.zeros(idx.shape, F32)
    for b in range(NUM_BUCKETS):
        acc = jnp.where(idx == b, (rb_ref[h, b] - far) * LOG2E, acc)
    o_ref[0] = acc


def _bias_tables(rel_bias):
    kk = np.arange(2 * KEY_TILE)[:, None]
    ql = np.arange(Q_BLOCK)[None, :]
    idx = jnp.asarray(_t5_bucket_np(kk - KEY_TILE - ql))
    t = pl.pallas_call(
        _bias_kernel,
        grid=(ATTN_HEADS,),
        in_specs=[pl.BlockSpec((2 * KEY_TILE, Q_BLOCK), lambda h: (0, 0)),
                  pl.BlockSpec(memory_space=pltpu.SMEM)],
        out_specs=pl.BlockSpec((1, 2 * KEY_TILE, Q_BLOCK), lambda h: (h, 0, 0)),
        out_shape=jax.ShapeDtypeStruct((ATTN_HEADS, 2 * KEY_TILE, Q_BLOCK), F32),
        compiler_params=_cparams(("arbitrary",)),
        name="bias",
    )(idx, rel_bias.T)
    return jnp.stack([t[:, 256:512], t[:, 128:384], t[:, 0:256]])


def _inproj_kernel(x_ref, mod_ref, nw_ref, w_ref, head_ref, wide_ref):
    m = mod_ref[0]
    hn = (_rms(x_ref[0], nw_ref[...]) * (1.0 + m[1:2]) + m[0:1]).astype(BF16)
    head_ref[0] = _dot_nt(hn, w_ref[0:HEAD_COLS, :])
    for c in range(WIDE_COLS // INPROJ_TN):
        rows = slice(HEAD_COLS + c * INPROJ_TN, HEAD_COLS + (c + 1) * INPROJ_TN)
        wide_ref[0, :, c * INPROJ_TN:(c + 1) * INPROJ_TN] = _dot_nt(hn, w_ref[rows, :]).astype(BF16)


def _pack_w_in(w_in):
    sizes = [Q_RANK, KV_RANK, IDX_DIM, IDX_HEADS, D_INNER, CONV_DIM, SSD_HEADS, D_MODEL, D_MODEL]
    offs = np.cumsum([0] + sizes)
    wt = w_in.T
    q, kv, ki, wi, z, xbc, dt, ga, gb = [wt[offs[i]:offs[i + 1]] for i in range(9)]

    def zr(n):
        return jnp.zeros((n, D_MODEL), w_in.dtype)

    return jnp.concatenate([q, kv, ki, zr(LANES - IDX_DIM), wi, zr(LANES - IDX_HEADS),
                            dt, zr(LANES - SSD_HEADS), xbc, 0.5 * z, ga, gb], axis=0).astype(BF16)


def _dot_nt(a, b):
    return lax.dot_general(a, b, (((1,), (1,)), ((), ())), preferred_element_type=F32)


def _inproj(x, mod3, pre_norm, w_packed):
    bsz, seq, _ = x.shape
    tm = min(seq, 512)
    return pl.pallas_call(
        _inproj_kernel,
        grid=(bsz, seq // tm),
        in_specs=[pl.BlockSpec((1, tm, D_MODEL), lambda b, i: (b, i, 0)),
                  pl.BlockSpec((1, 6, D_MODEL), lambda b, i: (b, 0, 0)),
                  _const_spec((1, D_MODEL)),
                  _const_spec((HEAD_COLS + WIDE_COLS, D_MODEL))],
        out_specs=[pl.BlockSpec((1, tm, HEAD_COLS), lambda b, i: (b, i, 0)),
                   pl.BlockSpec((1, tm, WIDE_COLS), lambda b, i: (b, i, 0))],
        out_shape=[jax.ShapeDtypeStruct((bsz, seq, HEAD_COLS), F32),
                   jax.ShapeDtypeStruct((bsz, seq, WIDE_COLS), BF16)],
        compiler_params=_cparams(("parallel", "parallel")),
        name="inproj",
    )(x, mod3, pre_norm.reshape(1, D_MODEL), w_packed)


def _prep_kernel(p_ref, qn_ref, kvn_ref, kin_ref, wuqt_ref, wqit_ref,
                 qt_ref, qit_ref, kv_ref, kvt_ref, ki_ref, wt_ref, qnorm_ref, *, tc):
    p = p_ref[0]
    qnt = _rms(p[:, :Q_RANK], qn_ref[...]).T.astype(BF16)
    qf = jnp.dot(wuqt_ref[...], qnt, preferred_element_type=F32) * (KV_RANK ** -0.5 * LOG2E)
    qt = qf.astype(BF16)
    qnorm = jnp.sqrt(jnp.sum((qf * qf).reshape(ATTN_HEADS, KV_RANK, tc), axis=1))
    qit = jnp.dot(wqit_ref[...], qnt, preferred_element_type=F32).astype(BF16)
    wt = (p[:, HEAD_W:HEAD_W + LANES] * (IDX_HEADS ** -0.5 * IDX_DIM ** -0.5)).T
    for blk in range(tc // Q_BLOCK):
        cols = slice(blk * Q_BLOCK, (blk + 1) * Q_BLOCK)
        for h in range(ATTN_HEADS):
            qt_ref[0, blk, :, h * LANES:(h + 1) * LANES] = qt[h * LANES:(h + 1) * LANES, cols]
            qit_ref[0, blk, :, h * LANES:(h + 1) * LANES] = qit[h * IDX_DIM:(h + 1) * IDX_DIM, cols]
        wt_ref[0, blk] = wt[0:IDX_HEADS, cols]
        qnorm_ref[0, blk] = qnorm[:, cols]
    kv = _rms(p[:, HEAD_KV:HEAD_KV + KV_RANK], kvn_ref[...])
    kv_ref[0] = kv.astype(BF16)
    kvt = kv.T
    for c in range(tc // KEY_TILE):
        kvt_ref[0, c] = kvt[:, c * KEY_TILE:(c + 1) * KEY_TILE].astype(BF16)
    ki_ref[0] = _rms(p[:, HEAD_KIDX:HEAD_KIDX + LANES], kin_ref[...], n=IDX_DIM).astype(BF16)


def _prep(proj, q_norm, kv_norm, kidx_norm, w_uq, w_qidx):
    bsz, seq, _ = proj.shape
    tc = min(seq, 1024)
    kin = jnp.concatenate([kidx_norm, jnp.zeros((LANES - IDX_DIM,), F32)]).reshape(1, LANES)
    nb = seq // Q_BLOCK

    def slab(rows):
        return (jax.ShapeDtypeStruct((bsz, nb, rows, ATTN_HEADS * Q_BLOCK), BF16),
                pl.BlockSpec((1, tc // Q_BLOCK, rows, ATTN_HEADS * Q_BLOCK), lambda b, i: (b, i, 0, 0)))

    (q_slab, q_spec), (qi_slab, qi_spec) = slab(KV_RANK), slab(IDX_DIM)
    row_spec = pl.BlockSpec((1, tc, LANES), lambda b, i: (b, i, 0))
    return pl.pallas_call(
        functools.partial(_prep_kernel, tc=tc),
        grid=(bsz, seq // tc),
        in_specs=[pl.BlockSpec((1, tc, HEAD_COLS), lambda b, i: (b, i, 0)),
                  _const_spec((1, Q_RANK)), _const_spec((1, KV_RANK)), _const_spec((1, LANES)),
                  _const_spec((ATTN_HEADS * KV_RANK, Q_RANK)), _const_spec((IDX_HEADS * IDX_DIM, Q_RANK))],
        out_specs=[q_spec, qi_spec, row_spec,
                   pl.BlockSpec((1, tc // KEY_TILE, LANES, KEY_TILE), lambda b, i: (b, i, 0, 0)),
                   row_spec,
                   pl.BlockSpec((1, tc // Q_BLOCK, IDX_HEADS, Q_BLOCK), lambda b, i: (b, i, 0, 0)),
                   pl.BlockSpec((1, tc // Q_BLOCK, ATTN_HEADS, Q_BLOCK), lambda b, i: (b, i, 0, 0))],
        out_shape=[q_slab, qi_slab, jax.ShapeDtypeStruct((bsz, seq, LANES), BF16),
                   jax.ShapeDtypeStruct((bsz, seq // KEY_TILE, LANES, KEY_TILE), BF16),
                   jax.ShapeDtypeStruct((bsz, seq, LANES), BF16),
                   jax.ShapeDtypeStruct((bsz, nb, IDX_HEADS, Q_BLOCK), F32),
                   jax.ShapeDtypeStruct((bsz, nb, ATTN_HEADS, Q_BLOCK), F32)],
        compiler_params=_cparams(("parallel", "parallel")),
        name="prep",
    )(proj, q_norm.reshape(1, Q_RANK), kv_norm.reshape(1, KV_RANK), kin, w_uq.T.astype(BF16),
      w_qidx.T.astype(BF16))


def _dsa_kernel(qt_ref, qn_ref, zq_ref, qit0_ref, wt0_ref, qitn_ref, wtn_ref, ki_ref, kv_ref, kvt_ref, tb_ref,
                bmax_ref, wuv_ref, o_ref, iscbuf_ref, mom_ref, sbuf_ref, acc_ref, kmax_ref, *, k_sel, n_blocks):
    i = pl.program_id(1)
    last = i // 2
    odd = i % 2
    n_tiles = last + 1
    isct_ref = iscbuf_ref.at[i % 2]
    qt = qt_ref[0, 0]
    row = lax.broadcasted_iota(jnp.int32, (KEY_TILE, Q_BLOCK), 0)
    col = lax.broadcasted_iota(jnp.int32, (KEY_TILE, Q_BLOCK), 1)

    def key_rows(j):
        return pl.ds(pl.multiple_of(j * KEY_TILE, KEY_TILE), KEY_TILE)

    n_pairs = ATTN_HEADS // 2

    def pair_cols(p):
        return slice(p * KEY_TILE, (p + 1) * KEY_TILE)

    def half(x, hh):
        return x[:, hh * Q_BLOCK:(hh + 1) * Q_BLOCK]

    ACCS = 4

    def fold_rows(x, op):
        x = x.reshape(KEY_TILE // (ACCS * SUBLANES), ACCS, SUBLANES, LANES)
        return op(op(x, axis=0), axis=0)

    def idx_tile(j, blk, qit_ref, wt_ref, slot):
        dst_ref = iscbuf_ref.at[slot]
        keys = ki_ref[0, key_rows(j), :]
        wt = wt_ref[0, 0]
        acc = jnp.zeros((KEY_TILE, Q_BLOCK), F32)
        pad = jnp.zeros((LANES - IDX_DIM, KEY_TILE), BF16)
        for p in range(n_pairs):
            qi = jnp.concatenate([qit_ref[0, 0, :, pair_cols(p)], pad], axis=0)
            s = jnp.dot(keys, qi, preferred_element_type=F32)
            for hh in range(2):
                h = 2 * p + hh
                acc = acc + wt[h:h + 1, :] * jnp.maximum(half(s, hh), 0.0)
        key_limit = blk * Q_BLOCK + jnp.where(col < CHUNK, CHUNK, 2 * CHUNK)
        visible = j * KEY_TILE + row < key_limit
        dst_ref[j] = jnp.where(visible, acc, -jnp.inf)
        seen = jnp.where(visible, acc, 0.0)
        old = jnp.where(j == 0, 0.0, mom_ref[slot])
        mom_ref[slot] = old + jnp.stack([fold_rows(seen, jnp.sum), fold_rows(seen * seen, jnp.sum)])

    def tile_loop(n, body, carry):
        def pair(jj, c):
            return body(2 * jj + 1, body(2 * jj, c))
        carry = lax.fori_loop(0, n // 2, pair, carry)
        return lax.cond(n % 2 == 1, lambda c: body(n - 1, c), lambda c: c, carry)

    @pl.when(i == 0)
    def _():
        idx_tile(0, 0, qit0_ref, wt0_ref, 0)
        best = jnp.zeros((1, KEY_TILE), F32)
        for j in range(kvt_ref.shape[1]):
            x = kvt_ref[0, j].astype(F32)
            best = jnp.maximum(best, jnp.sum(x * x, axis=0, keepdims=True))
        kmax_ref[...] = jnp.broadcast_to(jnp.sqrt(jnp.max(best, axis=1, keepdims=True)), (SUBLANES, LANES))

    def rows_all(x, op):
        return jnp.broadcast_to(op(x, axis=0, keepdims=True), (SUBLANES, LANES))

    def tile_rows(j):
        return isct_ref[j].reshape(KEY_TILE // (ACCS * SUBLANES), ACCS, SUBLANES, LANES)

    def count_where(pred):
        def body(j, c):
            x = tile_rows(j)
            for r in range(x.shape[0]):
                c = jnp.where(pred(x[r]), c + 1.0, c)
            return c
        c = lax.fori_loop(0, n_tiles, body, jnp.zeros((ACCS, SUBLANES, LANES), F32))
        return rows_all(jnp.sum(c, axis=0), jnp.sum)

    kf = float(k_sel)

    def is_open(cnt, stalled):
        return jnp.logical_and(cnt > kf, stalled == 0.0)

    def any_lane(mask):
        return jnp.max(jnp.where(mask, 1.0, 0.0)) > 0.0

    lane8 = lax.broadcasted_iota(jnp.int32, (SUBLANES, LANES), 1)
    n_vis = (i * Q_BLOCK + jnp.where(lane8 < CHUNK, CHUNK, 2 * CHUNK)).astype(F32)
    mom = mom_ref[i % 2]
    mean = rows_all(mom[0], jnp.sum) / n_vis
    std = jnp.sqrt(jnp.maximum(rows_all(mom[1], jnp.sum) / n_vis - mean * mean, 0.0))
    z = zq_ref[0]
    takes_all = n_vis <= kf
    lo = jnp.where(takes_all, -FLOAT_BIG, mean + (z - BRACKET_HALF_WIDTH) * std)
    hi = jnp.where(takes_all, FLOAT_BIG, mean + (z + BRACKET_HALF_WIDTH) * std)
    cnt = count_where(lambda x: x >= lo[0:1])
    cnt_hi = count_where(lambda x: x >= hi[0:1])
    low_ok = cnt >= kf
    high_ok = cnt_hi < kf
    lo = jnp.where(low_ok, lo, -FLOAT_BIG)
    cnt = jnp.where(low_ok, cnt, n_vis)
    hi = jnp.where(high_ok, hi, FLOAT_BIG)
    cnt_hi = jnp.where(high_ok, cnt_hi, 0.0)

    def bis_step(c):
        lo, hi, cnt, cnt_hi, stalled = c
        mid = 0.5 * lo + 0.5 * hi
        cm = count_where(lambda x: x >= mid[0:1])
        active = is_open(cnt, stalled)
        noprog = jnp.logical_or(mid <= lo, mid >= hi)
        move = jnp.logical_and(active, jnp.logical_not(noprog))
        up = jnp.logical_and(move, cm >= kf)
        down = jnp.logical_and(move, cm < kf)
        return (jnp.where(up, mid, lo), jnp.where(down, mid, hi), jnp.where(up, cm, cnt),
                jnp.where(down, cm, cnt_hi), jnp.where(jnp.logical_and(active, noprog), 1.0, stalled))

    def bisect(state, max_iters):
        def cond(c):
            return jnp.logical_and(c[5] < max_iters, any_lane(is_open(c[2], c[4])))

        def body(c):
            state = c[:5]
            for _ in range(BISECT_UNROLL):
                state = bis_step(state)
            return state + (c[5] + BISECT_UNROLL,)

        return lax.while_loop(cond, body, state + (jnp.int32(0),))[:5]

    def max_below(t):
        def body(j, c):
            x = tile_rows(j)
            return jnp.maximum(c, jnp.max(jnp.where(x < t, x, -jnp.inf), axis=0))
        c = lax.fori_loop(0, n_tiles, body, jnp.full((ACCS, SUBLANES, LANES), -jnp.inf, F32))
        return rows_all(jnp.max(c, axis=0), jnp.max)

    def walk(c):
        top, left = c
        return jnp.where(left > 0.0, max_below(top), top), jnp.maximum(left - 1.0, 0.0)

    state = (lo, hi, cnt, cnt_hi, jnp.zeros((SUBLANES, LANES), F32))
    for _ in range(BISECT_COARSE_ITERS):
        state = bis_step(state)
    lo, hi, cnt, cnt_hi, stalled = state
    walked = is_open(cnt, stalled)
    top, left = walk(walk((hi, jnp.where(walked, kf - cnt_hi, 0.0))))
    arrived = jnp.logical_and(walked, left == 0.0)
    lo = jnp.where(arrived, top, lo)
    hi = jnp.where(jnp.logical_and(walked, left > 0.0), top, hi)
    cnt = count_where(lambda x: x >= lo[0:1])

    def finish(state):
        thr8, _, cnt, _, _ = bisect(state, BISECT_MAX_ITERS)
        tied = cnt > kf

        @pl.when(any_lane(tied))
        def _():
            need = kf - count_where(lambda x: x > thr8[0:1])

            def body(j, seen):
                x = isct_ref[j]
                eq = jnp.where(x == thr8[0:1], 1.0, 0.0)
                inc = _cumsum_rows(eq)
                rank = inc - eq + seen[0:1]
                drop = jnp.logical_and(jnp.logical_and(tied[0:1], eq > 0.0), rank >= need[0:1])
                isct_ref[j] = jnp.where(drop, -jnp.inf, x)
                return seen + inc[KEY_TILE - 1:KEY_TILE]

            lax.fori_loop(0, n_tiles, body, jnp.zeros((SUBLANES, LANES), F32))

        return thr8

    thr8 = lax.cond(any_lane(cnt > kf), finish, lambda s: s[0], (lo, hi, cnt, cnt_hi, stalled))
    thr = thr8[0:1]

    n_far = jnp.maximum(last - 1 + odd, 0)
    next_slot = (i + 1) % 2

    zero_l = jnp.zeros((ATTN_HEADS, SUBLANES, LANES), F32)

    shift = [qn_ref[0, 0, h:h + 1, :] * kmax_ref[0:1, :] * BOUND_SLACK + (bmax_ref[h] + BOUND_SLACK)
             for h in range(ATTN_HEADS)]

    def sweep_tile(j, l, near, has_next):
        keys = kv_ref[0, key_rows(j), :]
        values_t = kvt_ref[0, j]
        neg = jnp.where(isct_ref[j] >= thr, 0.0, -jnp.inf)
        table = jnp.where(j == last, odd, 2)
        for p in range(n_pairs):
            s = jnp.dot(keys, qt[:, pair_cols(p)], preferred_element_type=F32)
            for hh in range(2):
                h = 2 * p + hh
                sh = half(s, hh) + neg
                sbuf_ref[j, h] = sh + tb_ref[table, h] if near else sh
        out = []
        for p in range(n_pairs):
            probs = []
            for hh in range(2):
                h = 2 * p + hh
                e = jnp.exp2(sbuf_ref[j, h] - shift[h])
                out.append(l[h] + fold_rows(e, jnp.sum))
                probs.append(e.astype(BF16))
            acc_ref[p] += jnp.dot(values_t, jnp.concatenate(probs, axis=1),
                                  preferred_element_type=F32)
        if has_next:
            idx_tile(j, i + 1, qitn_ref, wtn_ref, next_slot)
        return jnp.stack(out)

    def sweep(has_next):
        def run(l):
            l = tile_loop(n_far, lambda j, l: sweep_tile(j, l, False, has_next), l)
            return lax.cond(
                n_tiles - n_far == 2,
                lambda l: sweep_tile(last, sweep_tile(last - 1, l, True, has_next), True, has_next),
                lambda l: sweep_tile(last, l, True, has_next), l)
        return run

    acc_ref[...] = jnp.zeros_like(acc_ref)
    l = lax.cond(i + 1 < n_blocks, sweep(True), sweep(False), zero_l)
    l = jnp.sum(l, axis=1, keepdims=True)

    @pl.when(jnp.logical_and(odd == 1, i + 1 < n_blocks))
    def _():
        idx_tile(n_tiles, i + 1, qitn_ref, wtn_ref, next_slot)

    def p1_tile(j, m, near):
        keys = kv_ref[0, key_rows(j), :]
        neg = jnp.where(isct_ref[j] >= thr, 0.0, -jnp.inf)
        table = jnp.where(j == last, odd, 2)
        out = []
        for p in range(n_pairs):
            s = jnp.dot(keys, qt[:, pair_cols(p)], preferred_element_type=F32)
            for hh in range(2):
                h = 2 * p + hh
                sh = half(s, hh) + neg
                if near:
                    sh = sh + tb_ref[table, h]
                sbuf_ref[j, h] = sh
                out.append(jnp.maximum(m[h], fold_rows(sh, jnp.max)))
        return jnp.stack(out)

    def exact_softmax(_):
        m = jnp.full((ATTN_HEADS, SUBLANES, LANES), -jnp.inf, F32)
        m = tile_loop(n_far, lambda j, m: p1_tile(j, m, False), m)
        m = lax.fori_loop(n_far, n_tiles, lambda j, m: p1_tile(j, m, True), m)
        m = jnp.max(m, axis=1, keepdims=True)
        acc_ref[...] = jnp.zeros_like(acc_ref)

        def p2_body(j, l):
            values_t = kvt_ref[0, j]
            out = []
            for p in range(n_pairs):
                probs = []
                for hh in range(2):
                    h = 2 * p + hh
                    e = jnp.exp2(sbuf_ref[j, h] - m[h])
                    out.append(l[h] + fold_rows(e, jnp.sum))
                    probs.append(e.astype(BF16))
                acc_ref[p] += jnp.dot(values_t, jnp.concatenate(probs, axis=1),
                                      preferred_element_type=F32)
            return jnp.stack(out)

        return jnp.sum(tile_loop(n_tiles, p2_body, zero_l), axis=1, keepdims=True)

    underflowed = jnp.max(jnp.where(l > UNDERFLOW_GUARD, 0.0, 1.0)) > 0.0
    l = lax.cond(underflowed, exact_softmax, lambda l: l, l)

    outs = [(acc_ref[h // 2, :, (h % 2) * Q_BLOCK:(h % 2 + 1) * Q_BLOCK] / l[h]).T.astype(BF16)
            for h in range(ATTN_HEADS)]
    for p in range(ATTN_HEADS // 2):
        pair = jnp.concatenate(outs[2 * p:2 * p + 2], axis=1)
        o_ref[0, :, p * LANES:(p + 1) * LANES] = jnp.dot(
            pair, wuv_ref[p], preferred_element_type=F32).astype(BF16)


def _pack_w_uv(w_uv):
    eye = jnp.eye(2, dtype=w_uv.dtype)
    w = w_uv.reshape(ATTN_HEADS // 2, 2, KV_RANK, 1, HEAD_DIM) * eye[None, :, None, :, None]
    return w.reshape(ATTN_HEADS // 2, 2 * KV_RANK, 2 * HEAD_DIM).astype(BF16)


def _dsa(qt, qnorm, qit, wt, ki, kv, kvt, tables, bias_max, wuv):
    bsz, seq, _ = kv.shape
    nkt = seq // KEY_TILE
    k_sel = min(TOPK_MAX, seq // 4)
    nb = seq // Q_BLOCK
    slab = (1, 1, KV_RANK, ATTN_HEADS * Q_BLOCK)
    islab = (1, 1, IDX_DIM, IDX_HEADS * Q_BLOCK)
    wslab = (1, 1, IDX_HEADS, Q_BLOCK)

    def first(b, i):
        return (b, 0, 0, 0)

    def following(b, i):
        return (b, jnp.minimum(i + 1, nb - 1), 0, 0)

    seq_spec = pl.BlockSpec((1, seq, LANES), lambda b, i: (b, 0, 0))
    n_vis = (np.arange(nb)[:, None] * Q_BLOCK + np.where(np.arange(Q_BLOCK) < CHUNK, CHUNK, 2 * CHUNK)[None, :])
    quantile = np.array([[statistics.NormalDist().inv_cdf(1.0 - k_sel / n) if n > k_sel else 0.0 for n in r]
                         for r in n_vis], np.float32)
    zq = jnp.asarray(np.broadcast_to(quantile[:, None, :], (nb, SUBLANES, Q_BLOCK)))
    return pl.pallas_call(
        functools.partial(_dsa_kernel, k_sel=k_sel, n_blocks=nb),
        grid=(bsz, nb),
        in_specs=[pl.BlockSpec(slab, lambda b, i: (b, i, 0, 0)),
                  pl.BlockSpec(wslab, lambda b, i: (b, i, 0, 0)),
                  pl.BlockSpec((1, SUBLANES, Q_BLOCK), lambda b, i: (i, 0, 0)),
                  pl.BlockSpec(islab, first), pl.BlockSpec(wslab, first),
                  pl.BlockSpec(islab, following), pl.BlockSpec(wslab, following),
                  seq_spec, seq_spec,
                  pl.BlockSpec((1, nkt, LANES, KEY_TILE), lambda b, i: (b, 0, 0, 0)),
                  _const_spec((3, ATTN_HEADS, KEY_TILE, Q_BLOCK)),
                  pl.BlockSpec(memory_space=pltpu.SMEM),
                  _const_spec((ATTN_HEADS // 2, 2 * KV_RANK, 2 * HEAD_DIM))],
        out_specs=pl.BlockSpec((1, Q_BLOCK, ATTN_HEADS * HEAD_DIM), lambda b, i: (b, i, 0)),
        out_shape=jax.ShapeDtypeStruct((bsz, seq, ATTN_HEADS * HEAD_DIM), BF16),
        scratch_shapes=[pltpu.VMEM((2, nkt, KEY_TILE, Q_BLOCK), F32),
                        pltpu.VMEM((2, 2, SUBLANES, LANES), F32),
                        pltpu.VMEM((nkt, ATTN_HEADS, KEY_TILE, Q_BLOCK), F32),
                        pltpu.VMEM((ATTN_HEADS // 2, KV_RANK, 2 * Q_BLOCK), F32),
                        pltpu.VMEM((SUBLANES, LANES), F32)],
        compiler_params=_cparams(("parallel", "arbitrary")),
        name="dsa",
    )(qt, qnorm, zq, qit, wt, qit, wt, ki, kv, kvt, tables, bias_max, wuv)


def _pack3(v):
    lane = lax.broadcasted_iota(jnp.int32, v.shape, 1)
    v = jnp.where(lane < SSD_HEADS, v, 0.0)
    hi = v.astype(BF16).astype(F32)
    r = v - hi
    mid = r.astype(BF16).astype(F32)
    lo = r - mid
    return (hi + pltpu.roll(mid, SSD_HEADS, axis=1) + pltpu.roll(lo, 2 * SSD_HEADS, axis=1)).astype(BF16)


def _cumsum_rows(x):
    n = x.shape[0]
    r = lax.broadcasted_iota(jnp.int32, x.shape, 0)
    s = 1
    while s < n:
        x = x + jnp.where(r >= s, pltpu.roll(x, s, axis=0), 0.0)
        s *= 2
    return x


def _shift_rows(x, s):
    r = pltpu.roll(x, s, axis=1)
    prev = jnp.concatenate([r[-1:], r[:-1]], axis=0)
    sub = lax.broadcasted_iota(jnp.int32, x.shape, 1)
    return jnp.where(sub >= s, r, prev)


def _ssd_kernel(z_ref, xbc_ref, dt_ref, cw_ref, cb_ref, dtb_ref, alog_ref, dsk_ref, nw_ref, e_ref,
                y_ref, tail_ref, u_ref, g_ref, state_ref):
    nq = SSD_Q

    @pl.when(pl.program_id(1) == 0)
    def _():
        tail_ref[...] = jnp.zeros_like(tail_ref)
        state_ref[...] = jnp.zeros_like(state_ref)

    assert CONV_W == 4
    for blk in range(CONV_DIM // CONV_BLOCK):
        cols = slice(blk * CONV_BLOCK, (blk + 1) * CONV_BLOCK)
        ext = jnp.concatenate([tail_ref[:, cols], xbc_ref[0, :, cols].astype(F32)], axis=0)
        ext = ext.reshape(1 + nq // SUBLANES, SUBLANES, CONV_BLOCK)
        s1 = _shift_rows(ext, 1)
        a = cw_ref[3:4, cols] * ext + cw_ref[2:3, cols] * s1 + cb_ref[:, cols]
        b = cw_ref[1:2, cols] * ext + cw_ref[0:1, cols] * s1
        conv = (a + _shift_rows(b, 2))[1:].reshape(nq, CONV_BLOCK)
        u_ref[:, cols] = _silu_of_half(conv)
        tail_ref[:, cols] = ext[nq // SUBLANES]

    t = dt_ref[0] + dtb_ref[...]
    dt = jnp.maximum(t, 0.0) + jnp.log1p(jnp.exp(-jnp.abs(t)))
    a2 = _cumsum_rows(dt * (-jnp.exp(alog_ref[...]))) * LOG2E
    a2_t = a2.T
    dt_p = _pack3(dt)
    dec_p = _pack3(dt * jnp.exp2(a2[nq - 1:nq, :] - a2))
    expa_p = _pack3(jnp.exp2(a2))

    r = lax.broadcasted_iota(jnp.int32, (nq, nq), 0)
    c = lax.broadcasted_iota(jnp.int32, (nq, nq), 1)
    causal = r >= c
    lane = lax.broadcasted_iota(jnp.int32, (nq, LANES), 1)
    heads_per_group = SSD_HEADS // SSD_GROUPS
    gw = heads_per_group * SSD_HEADDIM
    b_col = D_INNER
    c_col = D_INNER + SSD_GROUPS * D_STATE
    ssq = jnp.zeros((nq, LANES), F32)

    for g in range(SSD_GROUPS):
        gcols = slice(g * gw, (g + 1) * gw)
        eg = e_ref[:, gcols]
        dt_e = jnp.dot(dt_p, eg, preferred_element_type=F32)
        dec_e = jnp.dot(dec_p, eg, preferred_element_type=F32)
        expa_e = jnp.dot(expa_p, eg, preferred_element_type=F32)
        xs = u_ref[:, gcols]
        xdt_b = (xs * dt_e).astype(BF16)
        xdec_b = (xs * dec_e).astype(BF16)
        cg = u_ref[:, c_col + g * D_STATE:c_col + (g + 1) * D_STATE].astype(BF16)
        bgt = u_ref[:, b_col + g * D_STATE:b_col + (g + 1) * D_STATE].T.astype(BF16)
        cb = jnp.dot(cg, bgt, preferred_element_type=F32)
        prev = state_ref[g]
        y_off = jnp.dot(cg, prev.astype(BF16), preferred_element_type=F32) * expa_e
        pairs = []
        for pp in range(heads_per_group // 2):
            xp = xdt_b[:, pp * LANES:(pp + 1) * LANES]
            yh = []
            for hh in range(2):
                h = heads_per_group * g + 2 * pp + hh
                seg = a2[:, h:h + 1] - a2_t[h:h + 1, :]
                m = (cb * jnp.where(causal, jnp.exp2(seg), 0.0)).astype(BF16)
                yh.append(jnp.dot(m, xp, preferred_element_type=F32))
            pairs.append(jnp.where(lane < SSD_HEADDIM, yh[0], yh[1]))
        y = jnp.concatenate(pairs, axis=1) + y_off + dsk_ref[:, gcols] * xs
        gated = y * _silu_of_half(z_ref[0, :, gcols].astype(F32))
        g_ref[:, gcols] = gated
        sq = gated * gated
        ssq = ssq + sq[:, :LANES] + sq[:, LANES:]
        new = jnp.dot(bgt, xdec_b, preferred_element_type=F32)
        state_ref[g] = prev * expa_e[nq - 1:nq, :] + new
    scale = lax.rsqrt(jnp.sum(ssq, axis=-1, keepdims=True) * (1.0 / D_INNER) + EPS)
    y_ref[0] = (g_ref[...] * scale * nw_ref[...]).astype(BF16)


def _ssd(head, wide, conv_w, conv_b, dt_bias, a_log, d_skip, ssd_norm):
    bsz, seq, _ = head.shape
    nq = SSD_Q

    def pad_heads(v):
        return jnp.concatenate([v, jnp.zeros((LANES - SSD_HEADS,), F32)]).reshape(1, LANES)

    sel = np.concatenate([np.eye(SSD_HEADS)] * 3 + [np.zeros((LANES - 3 * SSD_HEADS, SSD_HEADS))], axis=0)
    e = jnp.asarray(np.kron(sel, np.ones((1, SSD_HEADDIM))), BF16)
    return pl.pallas_call(
        _ssd_kernel,
        grid=(bsz, seq // nq),
        in_specs=[pl.BlockSpec((1, nq, D_INNER), lambda b, i: (b, i, WIDE_Z // D_INNER)),
                  pl.BlockSpec((1, nq, CONV_DIM), lambda b, i: (b, i, WIDE_XBC // CONV_DIM)),
                  pl.BlockSpec((1, nq, LANES), lambda b, i: (b, i, HEAD_DT // LANES)),
                  _const_spec((CONV_W, CONV_DIM)), _const_spec((1, CONV_DIM)),
                  _const_spec((1, LANES)), _const_spec((1, LANES)),
                  _const_spec((1, D_INNER)), _const_spec((1, D_INNER)),
                  _const_spec((LANES, D_INNER))],
        out_specs=pl.BlockSpec((1, nq, D_INNER), lambda b, i: (b, i, 0)),
        out_shape=jax.ShapeDtypeStruct((bsz, seq, D_INNER), BF16),
        scratch_shapes=[pltpu.VMEM((SUBLANES, CONV_DIM), F32),
                        pltpu.VMEM((nq, CONV_DIM), F32),
                        pltpu.VMEM((nq, D_INNER), F32),
                        pltpu.VMEM((SSD_GROUPS, D_STATE, 4 * SSD_HEADDIM), F32)],
        compiler_params=_cparams(("parallel", "arbitrary")),
        name="ssd",
    )(wide, wide, head, 0.5 * conv_w, 0.5 * conv_b.reshape(1, CONV_DIM), pad_heads(dt_bias), pad_heads(a_log),
      jnp.repeat(d_skip, SSD_HEADDIM).reshape(1, D_INNER), ssd_norm.reshape(1, D_INNER), e)


def _mix_kernel(ao_ref, sy_ref, ga_ref, gb_ref, x_ref, mod_ref, nw_ref, woa_ref, wos_ref, wout_ref, o_ref):
    ya = jnp.dot(ao_ref[0], woa_ref[...], preferred_element_type=F32)
    yb = jnp.dot(sy_ref[0], wos_ref[...], preferred_element_type=F32)
    mix = jax.nn.sigmoid(ga_ref[0].astype(F32)) * ya + jax.nn.sigmoid(gb_ref[0].astype(F32)) * yb
    m2 = jnp.dot(mix.astype(BF16), wout_ref[...], preferred_element_type=F32)
    o_ref[0] = x_ref[0] + mod_ref[0][2:3] * _rms(m2, nw_ref[...])


def _mix(attn_o, ssd_y, wide, x, mod3, post_norm, w_o_attn, w_o_ssd, w_out):
    bsz, seq, _ = x.shape
    tm = min(seq, 512)

    def rows(width, col_block=0):
        return pl.BlockSpec((1, tm, width), lambda b, i: (b, i, col_block))

    return pl.pallas_call(
        _mix_kernel,
        grid=(bsz, seq // tm),
        in_specs=[rows(D_MODEL), rows(D_INNER), rows(D_MODEL, WIDE_GA // D_MODEL), rows(D_MODEL, WIDE_GB // D_MODEL),
                  rows(D_MODEL), pl.BlockSpec((1, 6, D_MODEL), lambda b, i: (b, 0, 0)),
                  _const_spec((1, D_MODEL)), _const_spec((D_MODEL, D_MODEL)),
                  _const_spec((D_INNER, D_MODEL)), _const_spec((D_MODEL, D_MODEL))],
        out_specs=rows(D_MODEL),
        out_shape=jax.ShapeDtypeStruct((bsz, seq, D_MODEL), F32),
        compiler_params=_cparams(("parallel", "parallel")),
        name="mix",
    )(attn_o, ssd_y, wide, wide, x, mod3, post_norm.reshape(1, D_MODEL),
      w_o_attn.astype(BF16), w_o_ssd.astype(BF16), w_out.astype(BF16))


def _ffn_kernel(x_ref, mod_ref, nw1_ref, nw2_ref, wg_ref, wu_ref, wo_ref, o_ref):
    x = x_ref[0]
    m = mod_ref[0]
    h2 = (_rms(x, nw1_ref[...]) * (1.0 + m[4:5]) + m[3:4]).astype(BF16)
    ug = jnp.dot(h2, wg_ref[...], preferred_element_type=F32)
    uu = jnp.dot(h2, wu_ref[...], preferred_element_type=F32)
    f = jnp.dot((_silu(ug) * uu).astype(BF16), wo_ref[...], preferred_element_type=F32)
    o_ref[0] = x + m[5:6] * _rms(f, nw2_ref[...])


def _ffn(x, mod3, pre_norm, post_norm, w_ffn_in, w_ffn_out):
    bsz, seq, _ = x.shape
    tm = min(seq, 512)
    rows = pl.BlockSpec((1, tm, D_MODEL), lambda b, i: (b, i, 0))
    w_in = w_ffn_in.astype(BF16)

    def half(k):
        return pl.BlockSpec((D_MODEL, D_FF), lambda b, i: (0, k), pipeline_mode=pl.Buffered(1))

    return pl.pallas_call(
        _ffn_kernel,
        grid=(bsz, seq // tm),
        in_specs=[rows, pl.BlockSpec((1, 6, D_MODEL), lambda b, i: (b, 0, 0)),
                  _const_spec((1, D_MODEL)), _const_spec((1, D_MODEL)),
                  half(0), half(1), _const_spec((D_FF, D_MODEL))],
        out_specs=rows,
        out_shape=jax.ShapeDtypeStruct((bsz, seq, D_MODEL), F32),
        compiler_params=_cparams(("parallel", "parallel")),
        name="ffn",
    )(x, mod3, pre_norm.reshape(1, D_MODEL), post_norm.reshape(1, D_MODEL), w_in, w_in, w_ffn_out.astype(BF16))


def kernel(x, c, positions, ada_w, ada_b, pre_norm_mix, post_norm_mix, pre_norm_ffn, post_norm_ffn, w_in, q_norm, kv_norm, w_uq, w_uv, rel_bias, w_qidx, kidx_norm, conv_w, conv_b, dt_bias, a_log, d_skip, ssd_norm, w_o_attn, w_o_ssd, w_out, w_ffn_in, w_ffn_out):
    del positions
    bsz, seq, _ = x.shape
    assert seq % (2 * KEY_TILE) == 0 and x.shape[-1] == D_MODEL
    mod3 = _mod(c, ada_w, ada_b).reshape(bsz, 6, D_MODEL)
    head, wide = _inproj(x, mod3, pre_norm_mix, _pack_w_in(w_in))
    qt, qit, kv, kvt, ki, wt, qnorm = _prep(head, q_norm, kv_norm, kidx_norm, w_uq, w_qidx)
    bias_max = jnp.max(rel_bias - rel_bias[NUM_BUCKETS // 2 - 1], axis=0) * LOG2E
    attn_o = _dsa(qt, qnorm, qit, wt, ki, kv, kvt, _bias_tables(rel_bias), bias_max, _pack_w_uv(w_uv))
    ssd_y = _ssd(head, wide, conv_w, conv_b, dt_bias, a_log, d_skip, ssd_norm)
    x1 = _mix(attn_o, ssd_y, wide, x, mod3, post_norm_mix, w_o_attn, w_o_ssd, w_out)
    return _ffn(x1, mod3, pre_norm_ffn, post_norm_ffn, w_ffn_in, w_ffn_out)
```

```python
import functools
import math
import statistics

import numpy as np
import jax
import jax.numpy as jnp
from jax import lax
from jax.experimental import pallas as pl
from jax.experimental.pallas import tpu as pltpu

F32 = jnp.float32
BF16 = jnp.bfloat16

D_MODEL = 1024
CHUNK = 64
Q_BLOCK = 128
EPS = 1e-6
ATTN_HEADS = 16
HEAD_DIM = 64
Q_RANK = 256
KV_RANK = 128
IDX_HEADS = 16
IDX_DIM = 64
TOPK_MAX = 256
NUM_BUCKETS = 32
MAX_DISTANCE = 128
D_INNER = 2 * D_MODEL
SSD_HEADDIM = 64
SSD_HEADS = D_INNER // SSD_HEADDIM
SSD_GROUPS = 8
D_STATE = 128
CONV_W = 4
CONV_DIM = D_INNER + 2 * SSD_GROUPS * D_STATE
D_FF = -(-8 * D_MODEL // (3 * 256)) * 256

LANES = 128
SUBLANES = 8
KEY_TILE = 256
SSD_Q = 128
CONV_BLOCK = 512
VMEM_LIMIT = 56 * 1024 * 1024
BISECT_MAX_ITERS = 320
BISECT_UNROLL = 4
BISECT_COARSE_ITERS = 10
BRACKET_HALF_WIDTH = 0.3
FLOAT_BIG = 3.0e38
LOG2E = math.log2(math.e)
BOUND_SLACK = 1.0 + 2.0 ** -6
UNDERFLOW_GUARD = 2.0 ** -80
SUM_ROWS = 16

HEAD_COLS = 768
HEAD_KV, HEAD_KIDX, HEAD_W, HEAD_DT = 256, 384, 512, 640
WIDE_XBC, WIDE_Z, WIDE_GA, WIDE_GB = 0, 4096, 6144, 7168
WIDE_COLS = 8192
INPROJ_TN = 1024


def _cparams(sem):
    return pltpu.CompilerParams(dimension_semantics=sem, vmem_limit_bytes=VMEM_LIMIT)


def _const_spec(shape):
    nd = len(shape)
    return pl.BlockSpec(shape, lambda *_: (0,) * nd, pipeline_mode=pl.Buffered(1))


def _rms(x, w, n=None):
    n = x.shape[-1] if n is None else n
    return x * lax.rsqrt(jnp.sum(x * x, axis=-1, keepdims=True) * (1.0 / n) + EPS) * w


def _silu_of_half(h):
    return h + h * jnp.tanh(h)


def _silu(x):
    return _silu_of_half(0.5 * x)


def _mod_kernel(c_ref, w_ref, b_ref, o_ref):
    c = c_ref[...]
    s = _silu(c).astype(BF16)
    o_ref[...] = jnp.dot(s, w_ref[...].astype(BF16), preferred_element_type=F32) + b_ref[...]


def _mod(c, ada_w, ada_b):
    bsz = c.shape[0]
    return pl.pallas_call(
        _mod_kernel,
        grid=(6,),
        in_specs=[pl.BlockSpec((bsz, D_MODEL), lambda j: (0, 0)),
                  pl.BlockSpec((D_MODEL, D_MODEL), lambda j: (0, j)),
                  pl.BlockSpec((1, D_MODEL), lambda j: (0, j))],
        out_specs=pl.BlockSpec((bsz, D_MODEL), lambda j: (0, j)),
        out_shape=jax.ShapeDtypeStruct((bsz, 6 * D_MODEL), F32),
        compiler_params=_cparams(("parallel",)),
        name="mod",
    )(c, ada_w, ada_b.reshape(1, 6 * D_MODEL))


def _t5_bucket_np(rel):
    half = NUM_BUCKETS // 2
    max_exact = half // 2
    side = np.where(rel > 0, half, 0)
    n = np.abs(rel)
    large = max_exact + (np.log(np.maximum(n, max_exact).astype(np.float64) / max_exact)
                         / math.log(MAX_DISTANCE / max_exact) * (half - max_exact)).astype(np.int64)
    large = np.minimum(large, half - 1)
    return (side + np.where(n < max_exact, n, large)).astype(np.int32)


def _bias_kernel(idx_ref, rb_ref, o_ref):
    h = pl.program_id(0)
    idx = idx_ref[...]
    far = rb_ref[h, NUM_BUCKETS // 2 - 1]
    acc = jnp.zeros(idx.shape, F32)
    for b in range(NUM_BUCKETS):
        acc = jnp.where(idx == b, (rb_ref[h, b] - far) * LOG2E, acc)
    o_ref[0] = acc


def _bias_tables(rel_bias):
    kk = np.arange(2 * KEY_TILE)[:, None]
    ql = np.arange(Q_BLOCK)[None, :]
    idx = jnp.asarray(_t5_bucket_np(kk - KEY_TILE - ql))
    t = pl.pallas_call(
        _bias_kernel,
        grid=(ATTN_HEADS,),
        in_specs=[pl.BlockSpec((2 * KEY_TILE, Q_BLOCK), lambda h: (0, 0)),
                  pl.BlockSpec(memory_space=pltpu.SMEM)],
        out_specs=pl.BlockSpec((1, 2 * KEY_TILE, Q_BLOCK), lambda h: (h, 0, 0)),
        out_shape=jax.ShapeDtypeStruct((ATTN_HEADS, 2 * KEY_TILE, Q_BLOCK), F32),
        compiler_params=_cparams(("arbitrary",)),
        name="bias",
    )(idx, rel_bias.T)
    return jnp.stack([t[:, 256:512], t[:, 128:384], t[:, 0:256]])


def _inproj_kernel(x_ref, mod_ref, nw_ref, w_ref, head_ref, wide_ref):
    m = mod_ref[0]
    hn = (_rms(x_ref[0], nw_ref[...]) * (1.0 + m[1:2]) + m[0:1]).astype(BF16)
    head_ref[0] = _dot_nt(hn, w_ref[0:HEAD_COLS, :])
    for c in range(WIDE_COLS // INPROJ_TN):
        rows = slice(HEAD_COLS + c * INPROJ_TN, HEAD_COLS + (c + 1) * INPROJ_TN)
        wide_ref[0, :, c * INPROJ_TN:(c + 1) * INPROJ_TN] = _dot_nt(hn, w_ref[rows, :]).astype(BF16)


def _pack_w_in(w_in):
    sizes = [Q_RANK, KV_RANK, IDX_DIM, IDX_HEADS, D_INNER, CONV_DIM, SSD_HEADS, D_MODEL, D_MODEL]
    offs = np.cumsum([0] + sizes)
    wt = w_in.T
    q, kv, ki, wi, z, xbc, dt, ga, gb = [wt[offs[i]:offs[i + 1]] for i in range(9)]

    def zr(n):
        return jnp.zeros((n, D_MODEL), w_in.dtype)

    return jnp.concatenate([q, kv, ki, zr(LANES - IDX_DIM), wi, zr(LANES - IDX_HEADS),
                            dt, zr(LANES - SSD_HEADS), xbc, 0.5 * z, ga, gb], axis=0).astype(BF16)


def _dot_nt(a, b):
    return lax.dot_general(a, b, (((1,), (1,)), ((), ())), preferred_element_type=F32)


def _inproj(x, mod3, pre_norm, w_packed):
    bsz, seq, _ = x.shape
    tm = min(seq, 512)
    return pl.pallas_call(
        _inproj_kernel,
        grid=(bsz, seq // tm),
        in_specs=[pl.BlockSpec((1, tm, D_MODEL), lambda b, i: (b, i, 0)),
                  pl.BlockSpec((1, 6, D_MODEL), lambda b, i: (b, 0, 0)),
                  _const_spec((1, D_MODEL)),
                  _const_spec((HEAD_COLS + WIDE_COLS, D_MODEL))],
        out_specs=[pl.BlockSpec((1, tm, HEAD_COLS), lambda b, i: (b, i, 0)),
                   pl.BlockSpec((1, tm, WIDE_COLS), lambda b, i: (b, i, 0))],
        out_shape=[jax.ShapeDtypeStruct((bsz, seq, HEAD_COLS), F32),
                   jax.ShapeDtypeStruct((bsz, seq, WIDE_COLS), BF16)],
        compiler_params=_cparams(("parallel", "parallel")),
        name="inproj",
    )(x, mod3, pre_norm.reshape(1, D_MODEL), w_packed)


def _prep_kernel(p_ref, qn_ref, kvn_ref, kin_ref, wuqt_ref, wqit_ref,
                 qt_ref, qit_ref, kv_ref, kvt_ref, ki_ref, wt_ref, qnorm_ref, *, tc):
    p = p_ref[0]
    qnt = _rms(p[:, :Q_RANK], qn_ref[...]).T.astype(BF16)
    qf = jnp.dot(wuqt_ref[...], qnt, preferred_element_type=F32) * (KV_RANK ** -0.5 * LOG2E)
    qt = qf.astype(BF16)
    qnorm = jnp.sqrt(jnp.sum((qf * qf).reshape(ATTN_HEADS, KV_RANK, tc), axis=1))
    qit = jnp.dot(wqit_ref[...], qnt, preferred_element_type=F32).astype(BF16)
    wt = (p[:, HEAD_W:HEAD_W + LANES] * (IDX_HEADS ** -0.5 * IDX_DIM ** -0.5)).T
    for blk in range(tc // Q_BLOCK):
        cols = slice(blk * Q_BLOCK, (blk + 1) * Q_BLOCK)
        for h in range(ATTN_HEADS):
            qt_ref[0, blk, :, h * LANES:(h + 1) * LANES] = qt[h * LANES:(h + 1) * LANES, cols]
            qit_ref[0, blk, :, h * LANES:(h + 1) * LANES] = qit[h * IDX_DIM:(h + 1) * IDX_DIM, cols]
        wt_ref[0, blk] = wt[0:IDX_HEADS, cols]
        qnorm_ref[0, blk] = qnorm[:, cols]
    kv = _rms(p[:, HEAD_KV:HEAD_KV + KV_RANK], kvn_ref[...])
    kv_ref[0] = kv.astype(BF16)
    kvt = kv.T
    for c in range(tc // KEY_TILE):
        kvt_ref[0, c] = kvt[:, c * KEY_TILE:(c + 1) * KEY_TILE].astype(BF16)
    ki_ref[0] = _rms(p[:, HEAD_KIDX:HEAD_KIDX + LANES], kin_ref[...], n=IDX_DIM).astype(BF16)


def _prep(proj, q_norm, kv_norm, kidx_norm, w_uq, w_qidx):
    bsz, seq, _ = proj.shape
    tc = min(seq, 1024)
    kin = jnp.concatenate([kidx_norm, jnp.zeros((LANES - IDX_DIM,), F32)]).reshape(1, LANES)
    nb = seq // Q_BLOCK

    def slab(rows):
        return (jax.ShapeDtypeStruct((bsz, nb, rows, ATTN_HEADS * Q_BLOCK), BF16),
                pl.BlockSpec((1, tc // Q_BLOCK, rows, ATTN_HEADS * Q_BLOCK), lambda b, i: (b, i, 0, 0)))

    (q_slab, q_spec), (qi_slab, qi_spec) = slab(KV_RANK), slab(IDX_DIM)
    row_spec = pl.BlockSpec((1, tc, LANES), lambda b, i: (b, i, 0))
    return pl.pallas_call(
        functools.partial(_prep_kernel, tc=tc),
        grid=(bsz, seq // tc),
        in_specs=[pl.BlockSpec((1, tc, HEAD_COLS), lambda b, i: (b, i, 0)),
                  _const_spec((1, Q_RANK)), _const_spec((1, KV_RANK)), _const_spec((1, LANES)),
                  _const_spec((ATTN_HEADS * KV_RANK, Q_RANK)), _const_spec((IDX_HEADS * IDX_DIM, Q_RANK))],
        out_specs=[q_spec, qi_spec, row_spec,
                   pl.BlockSpec((1, tc // KEY_TILE, LANES, KEY_TILE), lambda b, i: (b, i, 0, 0)),
                   row_spec,
                   pl.BlockSpec((1, tc // Q_BLOCK, IDX_HEADS, Q_BLOCK), lambda b, i: (b, i, 0, 0)),
                   pl.BlockSpec((1, tc // Q_BLOCK, ATTN_HEADS, Q_BLOCK), lambda b, i: (b, i, 0, 0))],
        out_shape=[q_slab, qi_slab, jax.ShapeDtypeStruct((bsz, seq, LANES), BF16),
                   jax.ShapeDtypeStruct((bsz, seq // KEY_TILE, LANES, KEY_TILE), BF16),
                   jax.ShapeDtypeStruct((bsz, seq, LANES), BF16),
                   jax.ShapeDtypeStruct((bsz, nb, IDX_HEADS, Q_BLOCK), F32),
                   jax.ShapeDtypeStruct((bsz, nb, ATTN_HEADS, Q_BLOCK), F32)],
        compiler_params=_cparams(("parallel", "parallel")),
        name="prep",
    )(proj, q_norm.reshape(1, Q_RANK), kv_norm.reshape(1, KV_RANK), kin, w_uq.T.astype(BF16),
      w_qidx.T.astype(BF16))


def _dsa_kernel(qt_ref, qn_ref, zq_ref, qit0_ref, wt0_ref, qitn_ref, wtn_ref, ki_ref, kv_ref, kvt_ref, tb_ref,
                bmax_ref, wuv_ref, o_ref, iscbuf_ref, mom_ref, sbuf_ref, acc_ref, kmax_ref, *, k_sel, n_blocks):
    i = pl.program_id(1)
    last = i // 2
    odd = i % 2
    n_tiles = last + 1
    isct_ref = iscbuf_ref.at[i % 2]
    qt = qt_ref[0, 0]
    row = lax.broadcasted_iota(jnp.int32, (KEY_TILE, Q_BLOCK), 0)
    col = lax.broadcasted_iota(jnp.int32, (KEY_TILE, Q_BLOCK), 1)

    def key_rows(j):
        return pl.ds(pl.multiple_of(j * KEY_TILE, KEY_TILE), KEY_TILE)

    n_pairs = ATTN_HEADS // 2

    def pair_cols(p):
        return slice(p * KEY_TILE, (p + 1) * KEY_TILE)

    def half(x, hh):
        return x[:, hh * Q_BLOCK:(hh + 1) * Q_BLOCK]

    ACCS = 4

    def fold_rows(x, op):
        x = x.reshape(KEY_TILE // (ACCS * SUBLANES), ACCS, SUBLANES, LANES)
        return op(op(x, axis=0), axis=0)

    def idx_tile(j, blk, qit_ref, wt_ref, slot):
        dst_ref = iscbuf_ref.at[slot]
        keys = ki_ref[0, key_rows(j), :]
        wt = wt_ref[0, 0]
        acc = jnp.zeros((KEY_TILE, Q_BLOCK), F32)
        pad = jnp.zeros((LANES - IDX_DIM, KEY_TILE), BF16)
        for p in range(n_pairs):
            qi = jnp.concatenate([qit_ref[0, 0, :, pair_cols(p)], pad], axis=0)
            s = jnp.dot(keys, qi, preferred_element_type=F32)
            for hh in range(2):
                h = 2 * p + hh
                acc = acc + wt[h:h + 1, :] * jnp.maximum(half(s, hh), 0.0)
        key_limit = blk * Q_BLOCK + jnp.where(col < CHUNK, CHUNK, 2 * CHUNK)
        visible = j * KEY_TILE + row < key_limit
        dst_ref[j] = jnp.where(visible, acc, -jnp.inf)
        seen = jnp.where(visible, acc, 0.0)
        old = jnp.where(j == 0, 0.0, mom_ref[slot])
        mom_ref[slot] = old + jnp.stack([fold_rows(seen, jnp.sum), fold_rows(seen * seen, jnp.sum)])

    def tile_loop(n, body, carry):
        def pair(jj, c):
            return body(2 * jj + 1, body(2 * jj, c))
        carry = lax.fori_loop(0, n // 2, pair, carry)
        return lax.cond(n % 2 == 1, lambda c: body(n - 1, c), lambda c: c, carry)

    @pl.when(i == 0)
    def _():
        idx_tile(0, 0, qit0_ref, wt0_ref, 0)
        best = jnp.zeros((1, KEY_TILE), F32)
        for j in range(kvt_ref.shape[1]):
            x = kvt_ref[0, j].astype(F32)
            best = jnp.maximum(best, jnp.sum(x * x, axis=0, keepdims=True))
        kmax_ref[...] = jnp.broadcast_to(jnp.sqrt(jnp.max(best, axis=1, keepdims=True)), (SUBLANES, LANES))

    def rows_all(x, op):
        return jnp.broadcast_to(op(x, axis=0, keepdims=True), (SUBLANES, LANES))

    def tile_rows(j):
        return isct_ref[j].reshape(KEY_TILE // (ACCS * SUBLANES), ACCS, SUBLANES, LANES)

    def count_where(pred):
        def body(j, c):
            x = tile_rows(j)
            for r in range(x.shape[0]):
                c = jnp.where(pred(x[r]), c + 1.0, c)
            return c
        c = lax.fori_loop(0, n_tiles, body, jnp.zeros((ACCS, SUBLANES, LANES), F32))
        return rows_all(jnp.sum(c, axis=0), jnp.sum)

    kf = float(k_sel)

    def is_open(cnt, stalled):
        return jnp.logical_and(cnt > kf, stalled == 0.0)

    def any_lane(mask):
        return jnp.max(jnp.where(mask, 1.0, 0.0)) > 0.0

    lane8 = lax.broadcasted_iota(jnp.int32, (SUBLANES, LANES), 1)
    n_vis = (i * Q_BLOCK + jnp.where(lane8 < CHUNK, CHUNK, 2 * CHUNK)).astype(F32)
    mom = mom_ref[i % 2]
    mean = rows_all(mom[0], jnp.sum) / n_vis
    std = jnp.sqrt(jnp.maximum(rows_all(mom[1], jnp.sum) / n_vis - mean * mean, 0.0))
    z = zq_ref[0]
    takes_all = n_vis <= kf
    lo = jnp.where(takes_all, -FLOAT_BIG, mean + (z - BRACKET_HALF_WIDTH) * std)
    hi = jnp.where(takes_all, FLOAT_BIG, mean + (z + BRACKET_HALF_WIDTH) * std)
    cnt = count_where(lambda x: x >= lo[0:1])
    cnt_hi = count_where(lambda x: x >= hi[0:1])
    low_ok = cnt >= kf
    high_ok = cnt_hi < kf
    lo = jnp.where(low_ok, lo, -FLOAT_BIG)
    cnt = jnp.where(low_ok, cnt, n_vis)
    hi = jnp.where(high_ok, hi, FLOAT_BIG)
    cnt_hi = jnp.where(high_ok, cnt_hi, 0.0)

    def bis_step(c):
        lo, hi, cnt, cnt_hi, stalled = c
        mid = 0.5 * lo + 0.5 * hi
        cm = count_where(lambda x: x >= mid[0:1])
        active = is_open(cnt, stalled)
        noprog = jnp.logical_or(mid <= lo, mid >= hi)
        move = jnp.logical_and(active, jnp.logical_not(noprog))
        up = jnp.logical_and(move, cm >= kf)
        down = jnp.logical_and(move, cm < kf)
        return (jnp.where(up, mid, lo), jnp.where(down, mid, hi), jnp.where(up, cm, cnt),
                jnp.where(down, cm, cnt_hi), jnp.where(jnp.logical_and(active, noprog), 1.0, stalled))

    def bisect(state, max_iters):
        def cond(c):
            return jnp.logical_and(c[5] < max_iters, any_lane(is_open(c[2], c[4])))

        def body(c):
            state = c[:5]
            for _ in range(BISECT_UNROLL):
                state = bis_step(state)
            return state + (c[5] + BISECT_UNROLL,)

        return lax.while_loop(cond, body, state + (jnp.int32(0),))[:5]

    def max_below(t):
        def body(j, c):
            x = tile_rows(j)
            return jnp.maximum(c, jnp.max(jnp.where(x < t, x, -jnp.inf), axis=0))
        c = lax.fori_loop(0, n_tiles, body, jnp.full((ACCS, SUBLANES, LANES), -jnp.inf, F32))
        return rows_all(jnp.max(c, axis=0), jnp.max)

    def walk(c):
        top, left = c
        return jnp.where(left > 0.0, max_below(top), top), jnp.maximum(left - 1.0, 0.0)

    state = (lo, hi, cnt, cnt_hi, jnp.zeros((SUBLANES, LANES), F32))
    for _ in range(BISECT_COARSE_ITERS):
        state = bis_step(state)
    lo, hi, cnt, cnt_hi, stalled = state
    walked = is_open(cnt, stalled)
    top, left = walk(walk((hi, jnp.where(walked, kf - cnt_hi, 0.0))))
    arrived = jnp.logical_and(walked, left == 0.0)
    lo = jnp.where(arrived, top, lo)
    hi = jnp.where(jnp.logical_and(walked, left > 0.0), top, hi)
    cnt = count_where(lambda x: x >= lo[0:1])

    def finish(state):
        thr8, _, cnt, _, _ = bisect(state, BISECT_MAX_ITERS)
        tied = cnt > kf

        @pl.when(any_lane(tied))
        def _():
            need = kf - count_where(lambda x: x > thr8[0:1])

            def body(j, seen):
                x = isct_ref[j]
                eq = jnp.where(x == thr8[0:1], 1.0, 0.0)
                inc = _cumsum_rows(eq)
                rank = inc - eq + seen[0:1]
                drop = jnp.logical_and(jnp.logical_and(tied[0:1], eq > 0.0), rank >= need[0:1])
                isct_ref[j] = jnp.where(drop, -jnp.inf, x)
                return seen + inc[KEY_TILE - 1:KEY_TILE]

            lax.fori_loop(0, n_tiles, body, jnp.zeros((SUBLANES, LANES), F32))

        return thr8

    thr8 = lax.cond(any_lane(cnt > kf), finish, lambda s: s[0], (lo, hi, cnt, cnt_hi, stalled))
    thr = thr8[0:1]

    n_far = jnp.maximum(last - 1 + odd, 0)
    next_slot = (i + 1) % 2

    zero_l = jnp.zeros((ATTN_HEADS, SUBLANES, LANES), F32)

    shift = [qn_ref[0, 0, h:h + 1, :] * kmax_ref[0:1, :] * BOUND_SLACK + (bmax_ref[h] + BOUND_SLACK)
             for h in range(ATTN_HEADS)]

    ones_rows = jnp.ones((SUM_ROWS, KEY_TILE), BF16)

    def sweep_tile(j, c, near, has_next):
        keys = kv_ref[0, key_rows(j), :]
        values_t = jnp.concatenate([kvt_ref[0, j], ones_rows], axis=0)
        keep = jnp.where(isct_ref[j] >= thr, 1.0, 0.0).astype(BF16)
        table = jnp.where(j == last, odd, 2)
        for p in range(n_pairs):
            s = jnp.dot(keys, qt[:, pair_cols(p)], preferred_element_type=F32)
            for hh in range(2):
                h = 2 * p + hh
                sbuf_ref[j, h] = half(s, hh) + tb_ref[table, h] if near else half(s, hh)
        for p in range(n_pairs):
            probs = [jnp.exp2(sbuf_ref[j, 2 * p + hh] - shift[2 * p + hh]).astype(BF16) * keep for hh in range(2)]
            acc_ref[p] += jnp.dot(values_t, jnp.concatenate(probs, axis=1),
                                  preferred_element_type=F32)
        if has_next:
            idx_tile(j, i + 1, qitn_ref, wtn_ref, next_slot)
        return c

    def sweep(has_next):
        def run(c):
            c = tile_loop(n_far, lambda j, c: sweep_tile(j, c, False, has_next), c)
            return lax.cond(
                n_tiles - n_far == 2,
                lambda c: sweep_tile(last, sweep_tile(last - 1, c, True, has_next), True, has_next),
                lambda c: sweep_tile(last, c, True, has_next), c)
        return run

    acc_ref[...] = jnp.zeros_like(acc_ref)
    lax.cond(i + 1 < n_blocks, sweep(True), sweep(False), jnp.int32(0))
    l = jnp.stack([acc_ref[h // 2, KV_RANK:KV_RANK + 1, (h % 2) * Q_BLOCK:(h % 2 + 1) * Q_BLOCK]
                   for h in range(ATTN_HEADS)])

    @pl.when(jnp.logical_and(odd == 1, i + 1 < n_blocks))
    def _():
        idx_tile(n_tiles, i + 1, qitn_ref, wtn_ref, next_slot)

    def p1_tile(j, m, near):
        keys = kv_ref[0, key_rows(j), :]
        neg = jnp.where(isct_ref[j] >= thr, 0.0, -jnp.inf)
        table = jnp.where(j == last, odd, 2)
        out = []
        for p in range(n_pairs):
            s = jnp.dot(keys, qt[:, pair_cols(p)], preferred_element_type=F32)
            for hh in range(2):
                h = 2 * p + hh
                sh = half(s, hh) + neg
                if near:
                    sh = sh + tb_ref[table, h]
                sbuf_ref[j, h] = sh
                out.append(jnp.maximum(m[h], fold_rows(sh, jnp.max)))
        return jnp.stack(out)

    def exact_softmax(_):
        m = jnp.full((ATTN_HEADS, SUBLANES, LANES), -jnp.inf, F32)
        m = tile_loop(n_far, lambda j, m: p1_tile(j, m, False), m)
        m = lax.fori_loop(n_far, n_tiles, lambda j, m: p1_tile(j, m, True), m)
        m = jnp.max(m, axis=1, keepdims=True)
        acc_ref[...] = jnp.zeros_like(acc_ref)

        def p2_body(j, l):
            values_t = kvt_ref[0, j]
            out = []
            for p in range(n_pairs):
                probs = []
                for hh in range(2):
                    h = 2 * p + hh
                    e = jnp.exp2(sbuf_ref[j, h] - m[h])
                    out.append(l[h] + fold_rows(e, jnp.sum))
                    probs.append(e.astype(BF16))
                acc_ref[p, 0:KV_RANK] += jnp.dot(values_t, jnp.concatenate(probs, axis=1),
                                                 preferred_element_type=F32)
            return jnp.stack(out)

        return jnp.sum(tile_loop(n_tiles, p2_body, zero_l), axis=1, keepdims=True)

    underflowed = jnp.max(jnp.where(l > UNDERFLOW_GUARD, 0.0, 1.0)) > 0.0
    l = lax.cond(underflowed, exact_softmax, lambda l: l, l)

    outs = [(acc_ref[h // 2, 0:KV_RANK, (h % 2) * Q_BLOCK:(h % 2 + 1) * Q_BLOCK] / l[h]).T.astype(BF16)
            for h in range(ATTN_HEADS)]
    for p in range(ATTN_HEADS // 2):
        pair = jnp.concatenate(outs[2 * p:2 * p + 2], axis=1)
        o_ref[0, :, p * LANES:(p + 1) * LANES] = jnp.dot(
            pair, wuv_ref[p], preferred_element_type=F32).astype(BF16)


def _pack_w_uv(w_uv):
    eye = jnp.eye(2, dtype=w_uv.dtype)
    w = w_uv.reshape(ATTN_HEADS // 2, 2, KV_RANK, 1, HEAD_DIM) * eye[None, :, None, :, None]
    return w.reshape(ATTN_HEADS // 2, 2 * KV_RANK, 2 * HEAD_DIM).astype(BF16)


def _dsa(qt, qnorm, qit, wt, ki, kv, kvt, tables, bias_max, wuv):
    bsz, seq, _ = kv.shape
    nkt = seq // KEY_TILE
    k_sel = min(TOPK_MAX, seq // 4)
    nb = seq // Q_BLOCK
    slab = (1, 1, KV_RANK, ATTN_HEADS * Q_BLOCK)
    islab = (1, 1, IDX_DIM, IDX_HEADS * Q_BLOCK)
    wslab = (1, 1, IDX_HEADS, Q_BLOCK)

    def first(b, i):
        return (b, 0, 0, 0)

    def following(b, i):
        return (b, jnp.minimum(i + 1, nb - 1), 0, 0)

    seq_spec = pl.BlockSpec((1, seq, LANES), lambda b, i: (b, 0, 0))
    n_vis = (np.arange(nb)[:, None] * Q_BLOCK + np.where(np.arange(Q_BLOCK) < CHUNK, CHUNK, 2 * CHUNK)[None, :])
    quantile = np.array([[statistics.NormalDist().inv_cdf(1.0 - k_sel / n) if n > k_sel else 0.0 for n in r]
                         for r in n_vis], np.float32)
    zq = jnp.asarray(np.broadcast_to(quantile[:, None, :], (nb, SUBLANES, Q_BLOCK)))
    return pl.pallas_call(
        functools.partial(_dsa_kernel, k_sel=k_sel, n_blocks=nb),
        grid=(bsz, nb),
        in_specs=[pl.BlockSpec(slab, lambda b, i: (b, i, 0, 0)),
                  pl.BlockSpec(wslab, lambda b, i: (b, i, 0, 0)),
                  pl.BlockSpec((1, SUBLANES, Q_BLOCK), lambda b, i: (i, 0, 0)),
                  pl.BlockSpec(islab, first), pl.BlockSpec(wslab, first),
                  pl.BlockSpec(islab, following), pl.BlockSpec(wslab, following),
                  seq_spec, seq_spec,
                  pl.BlockSpec((1, nkt, LANES, KEY_TILE), lambda b, i: (b, 0, 0, 0)),
                  _const_spec((3, ATTN_HEADS, KEY_TILE, Q_BLOCK)),
                  pl.BlockSpec(memory_space=pltpu.SMEM),
                  _const_spec((ATTN_HEADS // 2, 2 * KV_RANK, 2 * HEAD_DIM))],
        out_specs=pl.BlockSpec((1, Q_BLOCK, ATTN_HEADS * HEAD_DIM), lambda b, i: (b, i, 0)),
        out_shape=jax.ShapeDtypeStruct((bsz, seq, ATTN_HEADS * HEAD_DIM), BF16),
        scratch_shapes=[pltpu.VMEM((2, nkt, KEY_TILE, Q_BLOCK), F32),
                        pltpu.VMEM((2, 2, SUBLANES, LANES), F32),
                        pltpu.VMEM((nkt, ATTN_HEADS, KEY_TILE, Q_BLOCK), F32),
                        pltpu.VMEM((ATTN_HEADS // 2, KV_RANK + SUM_ROWS, 2 * Q_BLOCK), F32),
                        pltpu.VMEM((SUBLANES, LANES), F32)],
        compiler_params=_cparams(("parallel", "arbitrary")),
        name="dsa",
    )(qt, qnorm, zq, qit, wt, qit, wt, ki, kv, kvt, tables, bias_max, wuv)


def _pack3(v):
    lane = lax.broadcasted_iota(jnp.int32, v.shape, 1)
    v = jnp.where(lane < SSD_HEADS, v, 0.0)
    hi = v.astype(BF16).astype(F32)
    r = v - hi
    mid = r.astype(BF16).astype(F32)
    lo = r - mid
    return (hi + pltpu.roll(mid, SSD_HEADS, axis=1) + pltpu.roll(lo, 2 * SSD_HEADS, axis=1)).astype(BF16)


def _cumsum_rows(x):
    n = x.shape[0]
    r = lax.broadcasted_iota(jnp.int32, x.shape, 0)
    s = 1
    while s < n:
        x = x + jnp.where(r >= s, pltpu.roll(x, s, axis=0), 0.0)
        s *= 2
    return x


def _shift_rows(x, s):
    r = pltpu.roll(x, s, axis=1)
    prev = jnp.concatenate([r[-1:], r[:-1]], axis=0)
    sub = lax.broadcasted_iota(jnp.int32, x.shape, 1)
    return jnp.where(sub >= s, r, prev)


def _ssd_kernel(z_ref, xbc_ref, dt_ref, cw_ref, cb_ref, dtb_ref, alog_ref, dsk_ref, nw_ref, e_ref,
                y_ref, tail_ref, u_ref, g_ref, state_ref):
    nq = SSD_Q

    @pl.when(pl.program_id(1) == 0)
    def _():
        tail_ref[...] = jnp.zeros_like(tail_ref)
        state_ref[...] = jnp.zeros_like(state_ref)

    assert CONV_W == 4
    for blk in range(CONV_DIM // CONV_BLOCK):
        cols = slice(blk * CONV_BLOCK, (blk + 1) * CONV_BLOCK)
        ext = jnp.concatenate([tail_ref[:, cols], xbc_ref[0, :, cols].astype(F32)], axis=0)
        ext = ext.reshape(1 + nq // SUBLANES, SUBLANES, CONV_BLOCK)
        s1 = _shift_rows(ext, 1)
        a = cw_ref[3:4, cols] * ext + cw_ref[2:3, cols] * s1 + cb_ref[:, cols]
        b = cw_ref[1:2, cols] * ext + cw_ref[0:1, cols] * s1
        conv = (a + _shift_rows(b, 2))[1:].reshape(nq, CONV_BLOCK)
        u_ref[:, cols] = _silu_of_half(conv)
        tail_ref[:, cols] = ext[nq // SUBLANES]

    t = dt_ref[0] + dtb_ref[...]
    dt = jnp.maximum(t, 0.0) + jnp.log1p(jnp.exp(-jnp.abs(t)))
    a2 = _cumsum_rows(dt * (-jnp.exp(alog_ref[...]))) * LOG2E
    a2_t = a2.T
    dt_p = _pack3(dt)
    dec_p = _pack3(dt * jnp.exp2(a2[nq - 1:nq, :] - a2))
    expa_p = _pack3(jnp.exp2(a2))

    r = lax.broadcasted_iota(jnp.int32, (nq, nq), 0)
    c = lax.broadcasted_iota(jnp.int32, (nq, nq), 1)
    causal = r >= c
    lane = lax.broadcasted_iota(jnp.int32, (nq, LANES), 1)
    heads_per_group = SSD_HEADS // SSD_GROUPS
    gw = heads_per_group * SSD_HEADDIM
    b_col = D_INNER
    c_col = D_INNER + SSD_GROUPS * D_STATE
    ssq = jnp.zeros((nq, LANES), F32)

    for g in range(SSD_GROUPS):
        gcols = slice(g * gw, (g + 1) * gw)
        eg = e_ref[:, gcols]
        dt_e = jnp.dot(dt_p, eg, preferred_element_type=F32)
        dec_e = jnp.dot(dec_p, eg, preferred_element_type=F32)
        expa_e = jnp.dot(expa_p, eg, preferred_element_type=F32)
        xs = u_ref[:, gcols]
        xdt_b = (xs * dt_e).astype(BF16)
        xdec_b = (xs * dec_e).astype(BF16)
        cg = u_ref[:, c_col + g * D_STATE:c_col + (g + 1) * D_STATE].astype(BF16)
        bgt = u_ref[:, b_col + g * D_STATE:b_col + (g + 1) * D_STATE].T.astype(BF16)
        cb = jnp.dot(cg, bgt, preferred_element_type=F32)
        prev = state_ref[g]
        y_off = jnp.dot(cg, prev.astype(BF16), preferred_element_type=F32) * expa_e
        pairs = []
        for pp in range(heads_per_group // 2):
            xp = xdt_b[:, pp * LANES:(pp + 1) * LANES]
            yh = []
            for hh in range(2):
                h = heads_per_group * g + 2 * pp + hh
                seg = a2[:, h:h + 1] - a2_t[h:h + 1, :]
                m = (cb * jnp.where(causal, jnp.exp2(seg), 0.0)).astype(BF16)
                yh.append(jnp.dot(m, xp, preferred_element_type=F32))
            pairs.append(jnp.where(lane < SSD_HEADDIM, yh[0], yh[1]))
        y = jnp.concatenate(pairs, axis=1) + y_off + dsk_ref[:, gcols] * xs
        gated = y * _silu_of_half(z_ref[0, :, gcols].astype(F32))
        g_ref[:, gcols] = gated
        sq = gated * gated
        ssq = ssq + sq[:, :LANES] + sq[:, LANES:]
        new = jnp.dot(bgt, xdec_b, preferred_element_type=F32)
        state_ref[g] = prev * expa_e[nq - 1:nq, :] + new
    scale = lax.rsqrt(jnp.sum(ssq, axis=-1, keepdims=True) * (1.0 / D_INNER) + EPS)
    y_ref[0] = (g_ref[...] * scale * nw_ref[...]).astype(BF16)


def _ssd(head, wide, conv_w, conv_b, dt_bias, a_log, d_skip, ssd_norm):
    bsz, seq, _ = head.shape
    nq = SSD_Q

    def pad_heads(v):
        return jnp.concatenate([v, jnp.zeros((LANES - SSD_HEADS,), F32)]).reshape(1, LANES)

    sel = np.concatenate([np.eye(SSD_HEADS)] * 3 + [np.zeros((LANES - 3 * SSD_HEADS, SSD_HEADS))], axis=0)
    e = jnp.asarray(np.kron(sel, np.ones((1, SSD_HEADDIM))), BF16)
    return pl.pallas_call(
        _ssd_kernel,
        grid=(bsz, seq // nq),
        in_specs=[pl.BlockSpec((1, nq, D_INNER), lambda b, i: (b, i, WIDE_Z // D_INNER)),
                  pl.BlockSpec((1, nq, CONV_DIM), lambda b, i: (b, i, WIDE_XBC // CONV_DIM)),
                  pl.BlockSpec((1, nq, LANES), lambda b, i: (b, i, HEAD_DT // LANES)),
                  _const_spec((CONV_W, CONV_DIM)), _const_spec((1, CONV_DIM)),
                  _const_spec((1, LANES)), _const_spec((1, LANES)),
                  _const_spec((1, D_INNER)), _const_spec((1, D_INNER)),
                  _const_spec((LANES, D_INNER))],
        out_specs=pl.BlockSpec((1, nq, D_INNER), lambda b, i: (b, i, 0)),
        out_shape=jax.ShapeDtypeStruct((bsz, seq, D_INNER), BF16),
        scratch_shapes=[pltpu.VMEM((SUBLANES, CONV_DIM), F32),
                        pltpu.VMEM((nq, CONV_DIM), F32),
                        pltpu.VMEM((nq, D_INNER), F32),
                        pltpu.VMEM((SSD_GROUPS, D_STATE, 4 * SSD_HEADDIM), F32)],
        compiler_params=_cparams(("parallel", "arbitrary")),
        name="ssd",
    )(wide, wide, head, 0.5 * conv_w, 0.5 * conv_b.reshape(1, CONV_DIM), pad_heads(dt_bias), pad_heads(a_log),
      jnp.repeat(d_skip, SSD_HEADDIM).reshape(1, D_INNER), ssd_norm.reshape(1, D_INNER), e)


def _mix_kernel(ao_ref, sy_ref, ga_ref, gb_ref, x_ref, mod_ref, nw_ref, woa_ref, wos_ref, wout_ref, o_ref):
    ya = jnp.dot(ao_ref[0], woa_ref[...], preferred_element_type=F32)
    yb = jnp.dot(sy_ref[0], wos_ref[...], preferred_element_type=F32)
    mix = jax.nn.sigmoid(ga_ref[0].astype(F32)) * ya + jax.nn.sigmoid(gb_ref[0].astype(F32)) * yb
    m2 = jnp.dot(mix.astype(BF16), wout_ref[...], preferred_element_type=F32)
    o_ref[0] = x_ref[0] + mod_ref[0][2:3] * _rms(m2, nw_ref[...])


def _mix(attn_o, ssd_y, wide, x, mod3, post_norm, w_o_attn, w_o_ssd, w_out):
    bsz, seq, _ = x.shape
    tm = min(seq, 512)

    def rows(width, col_block=0):
        return pl.BlockSpec((1, tm, width), lambda b, i: (b, i, col_block))

    return pl.pallas_call(
        _mix_kernel,
        grid=(bsz, seq // tm),
        in_specs=[rows(D_MODEL), rows(D_INNER), rows(D_MODEL, WIDE_GA // D_MODEL), rows(D_MODEL, WIDE_GB // D_MODEL),
                  rows(D_MODEL), pl.BlockSpec((1, 6, D_MODEL), lambda b, i: (b, 0, 0)),
                  _const_spec((1, D_MODEL)), _const_spec((D_MODEL, D_MODEL)),
                  _const_spec((D_INNER, D_MODEL)), _const_spec((D_MODEL, D_MODEL))],
        out_specs=rows(D_MODEL),
        out_shape=jax.ShapeDtypeStruct((bsz, seq, D_MODEL), F32),
        compiler_params=_cparams(("parallel", "parallel")),
        name="mix",
    )(attn_o, ssd_y, wide, wide, x, mod3, post_norm.reshape(1, D_MODEL),
      w_o_attn.astype(BF16), w_o_ssd.astype(BF16), w_out.astype(BF16))


def _ffn_kernel(x_ref, mod_ref, nw1_ref, nw2_ref, wg_ref, wu_ref, wo_ref, o_ref):
    x = x_ref[0]
    m = mod_ref[0]
    h2 = (_rms(x, nw1_ref[...]) * (1.0 + m[4:5]) + m[3:4]).astype(BF16)
    ug = jnp.dot(h2, wg_ref[...], preferred_element_type=F32)
    uu = jnp.dot(h2, wu_ref[...], preferred_element_type=F32)
    f = jnp.dot((_silu(ug) * uu).astype(BF16), wo_ref[...], preferred_element_type=F32)
    o_ref[0] = x + m[5:6] * _rms(f, nw2_ref[...])


def _ffn(x, mod3, pre_norm, post_norm, w_ffn_in, w_ffn_out):
    bsz, seq, _ = x.shape
    tm = min(seq, 512)
    rows = pl.BlockSpec((1, tm, D_MODEL), lambda b, i: (b, i, 0))
    w_in = w_ffn_in.astype(BF16)

    def half(k):
        return pl.BlockSpec((D_MODEL, D_FF), lambda b, i: (0, k), pipeline_mode=pl.Buffered(1))

    return pl.pallas_call(
        _ffn_kernel,
        grid=(bsz, seq // tm),
        in_specs=[rows, pl.BlockSpec((1, 6, D_MODEL), lambda b, i: (b, 0, 0)),
                  _const_spec((1, D_MODEL)), _const_spec((1, D_MODEL)),
                  half(0), half(1), _const_spec((D_FF, D_MODEL))],
        out_specs=rows,
        out_shape=jax.ShapeDtypeStruct((bsz, seq, D_MODEL), F32),
        compiler_params=_cparams(("parallel", "parallel")),
        name="ffn",
    )(x, mod3, pre_norm.reshape(1, D_MODEL), post_norm.reshape(1, D_MODEL), w_in, w_in, w_ffn_out.astype(BF16))


def kernel(x, c, positions, ada_w, ada_b, pre_norm_mix, post_norm_mix, pre_norm_ffn, post_norm_ffn, w_in, q_norm, kv_norm, w_uq, w_uv, rel_bias, w_qidx, kidx_norm, conv_w, conv_b, dt_bias, a_log, d_skip, ssd_norm, w_o_attn, w_o_ssd, w_out, w_ffn_in, w_ffn_out):
    del positions
    bsz, seq, _ = x.shape
    assert seq % (2 * KEY_TILE) == 0 and x.shape[-1] == D_MODEL
    mod3 = _mod(c, ada_w, ada_b).reshape(bsz, 6, D_MODEL)
    head, wide = _inproj(x, mod3, pre_norm_mix, _pack_w_in(w_in))
    qt, qit, kv, kvt, ki, wt, qnorm = _prep(head, q_norm, kv_norm, kidx_norm, w_uq, w_qidx)
    bias_max = jnp.max(rel_bias - rel_bias[NUM_BUCKETS // 2 - 1], axis=0) * LOG2E
    attn_o = _dsa(qt, qnorm, qit, wt, ki, kv, kvt, _bias_tables(rel_bias), bias_max, _pack_w_uv(w_uv))
    ssd_y = _ssd(head, wide, conv_w, conv_b, dt_bias, a_log, d_skip, ssd_norm)
    x1 = _mix(attn_o, ssd_y, wide, x, mod3, post_norm_mix, w_o_attn, w_o_ssd, w_out)
    return _ffn(x1, mod3, pre_norm_ffn, post_norm_ffn, w_ffn_in, w_ffn_out)
```

```python
import functools
import math
import statistics

import numpy as np
import jax
import jax.numpy as jnp
from jax import lax
from jax.experimental import pallas as pl
from jax.experimental.pallas import tpu as pltpu

F32 = jnp.float32
BF16 = jnp.bfloat16

D_MODEL = 1024
CHUNK = 64
Q_BLOCK = 128
EPS = 1e-6
ATTN_HEADS = 16
HEAD_DIM = 64
Q_RANK = 256
KV_RANK = 128
IDX_HEADS = 16
IDX_DIM = 64
TOPK_MAX = 256
NUM_BUCKETS = 32
MAX_DISTANCE = 128
D_INNER = 2 * D_MODEL
SSD_HEADDIM = 64
SSD_HEADS = D_INNER // SSD_HEADDIM
SSD_GROUPS = 8
D_STATE = 128
CONV_W = 4
CONV_DIM = D_INNER + 2 * SSD_GROUPS * D_STATE
D_FF = -(-8 * D_MODEL // (3 * 256)) * 256

LANES = 128
SUBLANES = 8
KEY_TILE = 256
SSD_Q = 128
CONV_BLOCK = 512
VMEM_LIMIT = 56 * 1024 * 1024
BISECT_MAX_ITERS = 320
BISECT_UNROLL = 4
BISECT_COARSE_ITERS = 10
BRACKET_HALF_WIDTH = 0.3
FLOAT_BIG = 3.0e38
LOG2E = math.log2(math.e)
SUM_MIN = 2.0 ** -80
SUM_MAX = 2.0 ** 40
SUM_ROWS = 16

HEAD_COLS = 768
HEAD_KV, HEAD_KIDX, HEAD_W, HEAD_DT = 256, 384, 512, 640
WIDE_XBC, WIDE_Z, WIDE_GA, WIDE_GB = 0, 4096, 6144, 7168
WIDE_COLS = 8192
INPROJ_TN = 1024


def _cparams(sem):
    return pltpu.CompilerParams(dimension_semantics=sem, vmem_limit_bytes=VMEM_LIMIT)


def _const_spec(shape):
    nd = len(shape)
    return pl.BlockSpec(shape, lambda *_: (0,) * nd, pipeline_mode=pl.Buffered(1))


def _rms(x, w, n=None):
    n = x.shape[-1] if n is None else n
    return x * lax.rsqrt(jnp.sum(x * x, axis=-1, keepdims=True) * (1.0 / n) + EPS) * w


def _silu_of_half(h):
    return h + h * jnp.tanh(h)


def _silu(x):
    return _silu_of_half(0.5 * x)


def _mod_kernel(c_ref, w_ref, b_ref, o_ref):
    c = c_ref[...]
    s = _silu(c).astype(BF16)
    o_ref[...] = jnp.dot(s, w_ref[...].astype(BF16), preferred_element_type=F32) + b_ref[...]


def _mod(c, ada_w, ada_b):
    bsz = c.shape[0]
    return pl.pallas_call(
        _mod_kernel,
        grid=(6,),
        in_specs=[pl.BlockSpec((bsz, D_MODEL), lambda j: (0, 0)),
                  pl.BlockSpec((D_MODEL, D_MODEL), lambda j: (0, j)),
                  pl.BlockSpec((1, D_MODEL), lambda j: (0, j))],
        out_specs=pl.BlockSpec((bsz, D_MODEL), lambda j: (0, j)),
        out_shape=jax.ShapeDtypeStruct((bsz, 6 * D_MODEL), F32),
        compiler_params=_cparams(("parallel",)),
        name="mod",
    )(c, ada_w, ada_b.reshape(1, 6 * D_MODEL))


def _t5_bucket_np(rel):
    half = NUM_BUCKETS // 2
    max_exact = half // 2
    side = np.where(rel > 0, half, 0)
    n = np.abs(rel)
    large = max_exact + (np.log(np.maximum(n, max_exact).astype(np.float64) / max_exact)
                         / math.log(MAX_DISTANCE / max_exact) * (half - max_exact)).astype(np.int64)
    large = np.minimum(large, half - 1)
    return (side + np.where(n < max_exact, n, large)).astype(np.int32)


def _bias_kernel(idx_ref, rb_ref, o_ref):
    h = pl.program_id(0)
    idx = idx_ref[...]
    far = rb_ref[h, NUM_BUCKETS // 2 - 1]
    acc = jnp.zeros(idx.shape, F32)
    for b in range(NUM_BUCKETS):
        acc = jnp.where(idx == b, (rb_ref[h, b] - far) * LOG2E, acc)
    o_ref[0] = acc


def _bias_tables(rel_bias):
    kk = np.arange(2 * KEY_TILE)[:, None]
    ql = np.arange(Q_BLOCK)[None, :]
    idx = jnp.asarray(_t5_bucket_np(kk - KEY_TILE - ql))
    t = pl.pallas_call(
        _bias_kernel,
        grid=(ATTN_HEADS,),
        in_specs=[pl.BlockSpec((2 * KEY_TILE, Q_BLOCK), lambda h: (0, 0)),
                  pl.BlockSpec(memory_space=pltpu.SMEM)],
        out_specs=pl.BlockSpec((1, 2 * KEY_TILE, Q_BLOCK), lambda h: (h, 0, 0)),
        out_shape=jax.ShapeDtypeStruct((ATTN_HEADS, 2 * KEY_TILE, Q_BLOCK), F32),
        compiler_params=_cparams(("arbitrary",)),
        name="bias",
    )(idx, rel_bias.T)
    return jnp.stack([t[:, 256:512], t[:, 128:384], t[:, 0:256]])


def _inproj_kernel(x_ref, mod_ref, nw_ref, w_ref, head_ref, wide_ref):
    m = mod_ref[0]
    hn = (_rms(x_ref[0], nw_ref[...]) * (1.0 + m[1:2]) + m[0:1]).astype(BF16)
    head_ref[0] = _dot_nt(hn, w_ref[0:HEAD_COLS, :])
    for c in range(WIDE_COLS // INPROJ_TN):
        rows = slice(HEAD_COLS + c * INPROJ_TN, HEAD_COLS + (c + 1) * INPROJ_TN)
        wide_ref[0, :, c * INPROJ_TN:(c + 1) * INPROJ_TN] = _dot_nt(hn, w_ref[rows, :]).astype(BF16)


def _pack_w_in(w_in):
    sizes = [Q_RANK, KV_RANK, IDX_DIM, IDX_HEADS, D_INNER, CONV_DIM, SSD_HEADS, D_MODEL, D_MODEL]
    offs = np.cumsum([0] + sizes)
    wt = w_in.T
    q, kv, ki, wi, z, xbc, dt, ga, gb = [wt[offs[i]:offs[i + 1]] for i in range(9)]

    def zr(n):
        return jnp.zeros((n, D_MODEL), w_in.dtype)

    return jnp.concatenate([q, kv, ki, zr(LANES - IDX_DIM), wi, zr(LANES - IDX_HEADS),
                            dt, zr(LANES - SSD_HEADS), xbc, 0.5 * z, ga, gb], axis=0).astype(BF16)


def _dot_nt(a, b):
    return lax.dot_general(a, b, (((1,), (1,)), ((), ())), preferred_element_type=F32)


def _inproj(x, mod3, pre_norm, w_packed):
    bsz, seq, _ = x.shape
    tm = min(seq, 512)
    return pl.pallas_call(
        _inproj_kernel,
        grid=(bsz, seq // tm),
        in_specs=[pl.BlockSpec((1, tm, D_MODEL), lambda b, i: (b, i, 0)),
                  pl.BlockSpec((1, 6, D_MODEL), lambda b, i: (b, 0, 0)),
                  _const_spec((1, D_MODEL)),
                  _const_spec((HEAD_COLS + WIDE_COLS, D_MODEL))],
        out_specs=[pl.BlockSpec((1, tm, HEAD_COLS), lambda b, i: (b, i, 0)),
                   pl.BlockSpec((1, tm, WIDE_COLS), lambda b, i: (b, i, 0))],
        out_shape=[jax.ShapeDtypeStruct((bsz, seq, HEAD_COLS), F32),
                   jax.ShapeDtypeStruct((bsz, seq, WIDE_COLS), BF16)],
        compiler_params=_cparams(("parallel", "parallel")),
        name="inproj",
    )(x, mod3, pre_norm.reshape(1, D_MODEL), w_packed)


def _prep_kernel(p_ref, qn_ref, kvn_ref, kin_ref, wuqt_ref, wqit_ref,
                 qt_ref, qit_ref, kv_ref, kvt_ref, ki_ref, wt_ref, *, tc):
    p = p_ref[0]
    qnt = _rms(p[:, :Q_RANK], qn_ref[...]).T.astype(BF16)
    qt = (jnp.dot(wuqt_ref[...], qnt, preferred_element_type=F32) * (KV_RANK ** -0.5 * LOG2E)).astype(BF16)
    qit = jnp.dot(wqit_ref[...], qnt, preferred_element_type=F32).astype(BF16)
    wt = (p[:, HEAD_W:HEAD_W + LANES] * (IDX_HEADS ** -0.5 * IDX_DIM ** -0.5)).T
    for blk in range(tc // Q_BLOCK):
        cols = slice(blk * Q_BLOCK, (blk + 1) * Q_BLOCK)
        for h in range(ATTN_HEADS):
            qt_ref[0, blk, :, h * LANES:(h + 1) * LANES] = qt[h * LANES:(h + 1) * LANES, cols]
            qit_ref[0, blk, :, h * LANES:(h + 1) * LANES] = qit[h * IDX_DIM:(h + 1) * IDX_DIM, cols]
        wt_ref[0, blk] = wt[0:IDX_HEADS, cols]
    kv = _rms(p[:, HEAD_KV:HEAD_KV + KV_RANK], kvn_ref[...])
    kv_ref[0] = kv.astype(BF16)
    kvt = kv.T
    for c in range(tc // KEY_TILE):
        kvt_ref[0, c] = kvt[:, c * KEY_TILE:(c + 1) * KEY_TILE].astype(BF16)
    ki_ref[0] = _rms(p[:, HEAD_KIDX:HEAD_KIDX + LANES], kin_ref[...], n=IDX_DIM).astype(BF16)


def _prep(proj, q_norm, kv_norm, kidx_norm, w_uq, w_qidx):
    bsz, seq, _ = proj.shape
    tc = min(seq, 1024)
    kin = jnp.concatenate([kidx_norm, jnp.zeros((LANES - IDX_DIM,), F32)]).reshape(1, LANES)
    nb = seq // Q_BLOCK

    def slab(rows):
        return (jax.ShapeDtypeStruct((bsz, nb, rows, ATTN_HEADS * Q_BLOCK), BF16),
                pl.BlockSpec((1, tc // Q_BLOCK, rows, ATTN_HEADS * Q_BLOCK), lambda b, i: (b, i, 0, 0)))

    (q_slab, q_spec), (qi_slab, qi_spec) = slab(KV_RANK), slab(IDX_DIM)
    row_spec = pl.BlockSpec((1, tc, LANES), lambda b, i: (b, i, 0))
    return pl.pallas_call(
        functools.partial(_prep_kernel, tc=tc),
        grid=(bsz, seq // tc),
        in_specs=[pl.BlockSpec((1, tc, HEAD_COLS), lambda b, i: (b, i, 0)),
                  _const_spec((1, Q_RANK)), _const_spec((1, KV_RANK)), _const_spec((1, LANES)),
                  _const_spec((ATTN_HEADS * KV_RANK, Q_RANK)), _const_spec((IDX_HEADS * IDX_DIM, Q_RANK))],
        out_specs=[q_spec, qi_spec, row_spec,
                   pl.BlockSpec((1, tc // KEY_TILE, LANES, KEY_TILE), lambda b, i: (b, i, 0, 0)),
                   row_spec,
                   pl.BlockSpec((1, tc // Q_BLOCK, IDX_HEADS, Q_BLOCK), lambda b, i: (b, i, 0, 0))],
        out_shape=[q_slab, qi_slab, jax.ShapeDtypeStruct((bsz, seq, LANES), BF16),
                   jax.ShapeDtypeStruct((bsz, seq // KEY_TILE, LANES, KEY_TILE), BF16),
                   jax.ShapeDtypeStruct((bsz, seq, LANES), BF16),
                   jax.ShapeDtypeStruct((bsz, nb, IDX_HEADS, Q_BLOCK), F32)],
        compiler_params=_cparams(("parallel", "parallel")),
        name="prep",
    )(proj, q_norm.reshape(1, Q_RANK), kv_norm.reshape(1, KV_RANK), kin, w_uq.T.astype(BF16),
      w_qidx.T.astype(BF16))


def _dsa_kernel(qt_ref, zq_ref, qit0_ref, wt0_ref, qitn_ref, wtn_ref, ki_ref, kv_ref, kvt_ref, tb_ref,
                wuv_ref, o_ref, iscbuf_ref, mom_ref, sbuf_ref, acc_ref, *, k_sel, n_blocks):
    i = pl.program_id(1)
    last = i // 2
    odd = i % 2
    n_tiles = last + 1
    isct_ref = iscbuf_ref.at[i % 2]
    qt = qt_ref[0, 0]
    row = lax.broadcasted_iota(jnp.int32, (KEY_TILE, Q_BLOCK), 0)
    col = lax.broadcasted_iota(jnp.int32, (KEY_TILE, Q_BLOCK), 1)

    def key_rows(j):
        return pl.ds(pl.multiple_of(j * KEY_TILE, KEY_TILE), KEY_TILE)

    n_pairs = ATTN_HEADS // 2

    def pair_cols(p):
        return slice(p * KEY_TILE, (p + 1) * KEY_TILE)

    def half(x, hh):
        return x[:, hh * Q_BLOCK:(hh + 1) * Q_BLOCK]

    ACCS = 4

    def fold_rows(x, op):
        x = x.reshape(KEY_TILE // (ACCS * SUBLANES), ACCS, SUBLANES, LANES)
        return op(op(x, axis=0), axis=0)

    def idx_tile(j, blk, qit_ref, wt_ref, slot):
        dst_ref = iscbuf_ref.at[slot]
        keys = ki_ref[0, key_rows(j), :]
        wt = wt_ref[0, 0]
        acc = jnp.zeros((KEY_TILE, Q_BLOCK), F32)
        pad = jnp.zeros((LANES - IDX_DIM, KEY_TILE), BF16)
        for p in range(n_pairs):
            qi = jnp.concatenate([qit_ref[0, 0, :, pair_cols(p)], pad], axis=0)
            s = jnp.dot(keys, qi, preferred_element_type=F32)
            for hh in range(2):
                h = 2 * p + hh
                acc = acc + wt[h:h + 1, :] * jnp.maximum(half(s, hh), 0.0)
        key_limit = blk * Q_BLOCK + jnp.where(col < CHUNK, CHUNK, 2 * CHUNK)
        visible = j * KEY_TILE + row < key_limit
        dst_ref[j] = jnp.where(visible, acc, -jnp.inf)
        seen = jnp.where(visible, acc, 0.0)
        old = jnp.where(j == 0, 0.0, mom_ref[slot])
        mom_ref[slot] = old + jnp.stack([fold_rows(seen, jnp.sum), fold_rows(seen * seen, jnp.sum)])

    def tile_loop(n, body, carry):
        def pair(jj, c):
            return body(2 * jj + 1, body(2 * jj, c))
        carry = lax.fori_loop(0, n // 2, pair, carry)
        return lax.cond(n % 2 == 1, lambda c: body(n - 1, c), lambda c: c, carry)

    @pl.when(i == 0)
    def _():
        idx_tile(0, 0, qit0_ref, wt0_ref, 0)

    def rows_all(x, op):
        return jnp.broadcast_to(op(x, axis=0, keepdims=True), (SUBLANES, LANES))

    def tile_rows(j):
        return isct_ref[j].reshape(KEY_TILE // (ACCS * SUBLANES), ACCS, SUBLANES, LANES)

    def count_where(pred):
        def body(j, c):
            x = tile_rows(j)
            for r in range(x.shape[0]):
                c = c + jnp.where(pred(x[r]), 1.0, 0.0)
            return c
        c = tile_loop(n_tiles, body, jnp.zeros((ACCS, SUBLANES, LANES), F32))
        return rows_all(jnp.sum(c, axis=0), jnp.sum)

    kf = float(k_sel)

    def is_open(cnt, stalled):
        return jnp.logical_and(cnt > kf, stalled == 0.0)

    def any_lane(mask):
        return jnp.max(jnp.where(mask, 1.0, 0.0)) > 0.0

    lane8 = lax.broadcasted_iota(jnp.int32, (SUBLANES, LANES), 1)
    n_vis = (i * Q_BLOCK + jnp.where(lane8 < CHUNK, CHUNK, 2 * CHUNK)).astype(F32)
    mom = mom_ref[i % 2]
    mean = rows_all(mom[0], jnp.sum) / n_vis
    std = jnp.sqrt(jnp.maximum(rows_all(mom[1], jnp.sum) / n_vis - mean * mean, 0.0))
    z = zq_ref[0]
    takes_all = n_vis <= kf
    lo = jnp.where(takes_all, -FLOAT_BIG, mean + (z - BRACKET_HALF_WIDTH) * std)
    hi = jnp.where(takes_all, FLOAT_BIG, mean + (z + BRACKET_HALF_WIDTH) * std)
    cnt = count_where(lambda x: x >= lo[0:1])
    cnt_hi = count_where(lambda x: x >= hi[0:1])
    low_ok = cnt >= kf
    high_ok = cnt_hi < kf
    lo = jnp.where(low_ok, lo, -FLOAT_BIG)
    cnt = jnp.where(low_ok, cnt, n_vis)
    hi = jnp.where(high_ok, hi, FLOAT_BIG)
    cnt_hi = jnp.where(high_ok, cnt_hi, 0.0)

    def bis_step(c):
        lo, hi, cnt, cnt_hi, stalled = c
        mid = 0.5 * lo + 0.5 * hi
        cm = count_where(lambda x: x >= mid[0:1])
        active = is_open(cnt, stalled)
        noprog = jnp.logical_or(mid <= lo, mid >= hi)
        move = jnp.logical_and(active, jnp.logical_not(noprog))
        up = jnp.logical_and(move, cm >= kf)
        down = jnp.logical_and(move, cm < kf)
        return (jnp.where(up, mid, lo), jnp.where(down, mid, hi), jnp.where(up, cm, cnt),
                jnp.where(down, cm, cnt_hi), jnp.where(jnp.logical_and(active, noprog), 1.0, stalled))

    def bisect(state, max_iters):
        def cond(c):
            return jnp.logical_and(c[5] < max_iters, any_lane(is_open(c[2], c[4])))

        def body(c):
            state = c[:5]
            for _ in range(BISECT_UNROLL):
                state = bis_step(state)
            return state + (c[5] + BISECT_UNROLL,)

        return lax.while_loop(cond, body, state + (jnp.int32(0),))[:5]

    def max_below(t):
        def body(j, c):
            x = tile_rows(j)
            return jnp.maximum(c, jnp.max(jnp.where(x < t, x, -jnp.inf), axis=0))
        c = lax.fori_loop(0, n_tiles, body, jnp.full((ACCS, SUBLANES, LANES), -jnp.inf, F32))
        return rows_all(jnp.max(c, axis=0), jnp.max)

    def walk(c):
        top, left = c
        return jnp.where(left > 0.0, max_below(top), top), jnp.maximum(left - 1.0, 0.0)

    state = (lo, hi, cnt, cnt_hi, jnp.zeros((SUBLANES, LANES), F32))
    for _ in range(BISECT_COARSE_ITERS):
        state = bis_step(state)
    lo, hi, cnt, cnt_hi, stalled = state
    walked = is_open(cnt, stalled)
    top, left = walk(walk((hi, jnp.where(walked, kf - cnt_hi, 0.0))))
    arrived = jnp.logical_and(walked, left == 0.0)
    lo = jnp.where(arrived, top, lo)
    hi = jnp.where(jnp.logical_and(walked, left > 0.0), top, hi)
    cnt = count_where(lambda x: x >= lo[0:1])

    def finish(state):
        thr8, _, cnt, _, _ = bisect(state, BISECT_MAX_ITERS)
        tied = cnt > kf

        @pl.when(any_lane(tied))
        def _():
            need = kf - count_where(lambda x: x > thr8[0:1])

            def body(j, seen):
                x = isct_ref[j]
                eq = jnp.where(x == thr8[0:1], 1.0, 0.0)
                inc = _cumsum_rows(eq)
                rank = inc - eq + seen[0:1]
                drop = jnp.logical_and(jnp.logical_and(tied[0:1], eq > 0.0), rank >= need[0:1])
                isct_ref[j] = jnp.where(drop, -jnp.inf, x)
                return seen + inc[KEY_TILE - 1:KEY_TILE]

            lax.fori_loop(0, n_tiles, body, jnp.zeros((SUBLANES, LANES), F32))

        return thr8

    thr8 = lax.cond(any_lane(cnt > kf), finish, lambda s: s[0], (lo, hi, cnt, cnt_hi, stalled))
    thr = thr8[0:1]

    n_far = jnp.maximum(last - 1 + odd, 0)
    next_slot = (i + 1) % 2

    zero_l = jnp.zeros((ATTN_HEADS, SUBLANES, LANES), F32)

    ones_rows = jnp.ones((SUM_ROWS, KEY_TILE), BF16)

    def sweep_tile(j, c, near, has_next):
        keys = kv_ref[0, key_rows(j), :]
        values_t = jnp.concatenate([kvt_ref[0, j], ones_rows], axis=0)
        keep = jnp.where(isct_ref[j] >= thr, 1.0, 0.0).astype(BF16)
        table = jnp.where(j == last, odd, 2)
        for p in range(n_pairs):
            s = jnp.dot(keys, qt[:, pair_cols(p)], preferred_element_type=F32)
            for hh in range(2):
                h = 2 * p + hh
                sbuf_ref[j, h] = half(s, hh) + tb_ref[table, h] if near else half(s, hh)
        for p in range(n_pairs):
            probs = [jnp.exp2(sbuf_ref[j, 2 * p + hh]).astype(BF16) * keep for hh in range(2)]
            acc_ref[p] += jnp.dot(values_t, jnp.concatenate(probs, axis=1),
                                  preferred_element_type=F32)
        if has_next:
            idx_tile(j, i + 1, qitn_ref, wtn_ref, next_slot)
        return c

    def sweep(has_next):
        def run(c):
            c = tile_loop(n_far, lambda j, c: sweep_tile(j, c, False, has_next), c)
            def two(c):
                return sweep_tile(last, sweep_tile(last - 1, c, True, has_next), True, has_next)

            def one(c):
                return sweep_tile(last, c, True, has_next)

            def one_and_extra(c):
                c = one(c)
                idx_tile(n_tiles, i + 1, qitn_ref, wtn_ref, next_slot)
                return c

            if not has_next:
                return lax.cond(n_tiles - n_far == 2, two, one, c)
            return lax.switch(jnp.where(n_tiles - n_far == 2, 0, 1 + odd), [two, one, one_and_extra], c)
        return run

    acc_ref[...] = jnp.zeros_like(acc_ref)
    lax.cond(i + 1 < n_blocks, sweep(True), sweep(False), jnp.int32(0))
    l = jnp.stack([acc_ref[h // 2, KV_RANK:KV_RANK + 1, (h % 2) * Q_BLOCK:(h % 2 + 1) * Q_BLOCK]
                   for h in range(ATTN_HEADS)])

    def p1_tile(j, m, near):
        keys = kv_ref[0, key_rows(j), :]
        neg = jnp.where(isct_ref[j] >= thr, 0.0, -jnp.inf)
        table = jnp.where(j == last, odd, 2)
        out = []
        for p in range(n_pairs):
            s = jnp.dot(keys, qt[:, pair_cols(p)], preferred_element_type=F32)
            for hh in range(2):
                h = 2 * p + hh
                sh = half(s, hh) + neg
                if near:
                    sh = sh + tb_ref[table, h]
                sbuf_ref[j, h] = sh
                out.append(jnp.maximum(m[h], fold_rows(sh, jnp.max)))
        return jnp.stack(out)

    def exact_softmax(_):
        m = jnp.full((ATTN_HEADS, SUBLANES, LANES), -jnp.inf, F32)
        m = tile_loop(n_far, lambda j, m: p1_tile(j, m, False), m)
        m = lax.fori_loop(n_far, n_tiles, lambda j, m: p1_tile(j, m, True), m)
        m = jnp.max(m, axis=1, keepdims=True)
        acc_ref[...] = jnp.zeros_like(acc_ref)

        def p2_body(j, l):
            values_t = kvt_ref[0, j]
            out = []
            for p in range(n_pairs):
                probs = []
                for hh in range(2):
                    h = 2 * p + hh
                    e = jnp.exp2(sbuf_ref[j, h] - m[h])
                    out.append(l[h] + fold_rows(e, jnp.sum))
                    probs.append(e.astype(BF16))
                acc_ref[p, 0:KV_RANK] += jnp.dot(values_t, jnp.concatenate(probs, axis=1),
                                                 preferred_element_type=F32)
            return jnp.stack(out)

        return jnp.sum(tile_loop(n_tiles, p2_body, zero_l), axis=1, keepdims=True)

    in_range = jnp.logical_and(l > SUM_MIN, l < SUM_MAX)
    out_of_range = jnp.max(jnp.where(in_range, 0.0, 1.0)) > 0.0
    l = lax.cond(out_of_range, exact_softmax, lambda l: l, l)

    outs = [(acc_ref[h // 2, 0:KV_RANK, (h % 2) * Q_BLOCK:(h % 2 + 1) * Q_BLOCK] / l[h]).T.astype(BF16)
            for h in range(ATTN_HEADS)]
    for p in range(ATTN_HEADS // 2):
        pair = jnp.concatenate(outs[2 * p:2 * p + 2], axis=1)
        o_ref[0, :, p * LANES:(p + 1) * LANES] = jnp.dot(
            pair, wuv_ref[p], preferred_element_type=F32).astype(BF16)


def _pack_w_uv(w_uv):
    eye = jnp.eye(2, dtype=w_uv.dtype)
    w = w_uv.reshape(ATTN_HEADS // 2, 2, KV_RANK, 1, HEAD_DIM) * eye[None, :, None, :, None]
    return w.reshape(ATTN_HEADS // 2, 2 * KV_RANK, 2 * HEAD_DIM).astype(BF16)


def _dsa(qt, qit, wt, ki, kv, kvt, tables, wuv):
    bsz, seq, _ = kv.shape
    nkt = seq // KEY_TILE
    k_sel = min(TOPK_MAX, seq // 4)
    nb = seq // Q_BLOCK
    slab = (1, 1, KV_RANK, ATTN_HEADS * Q_BLOCK)
    islab = (1, 1, IDX_DIM, IDX_HEADS * Q_BLOCK)
    wslab = (1, 1, IDX_HEADS, Q_BLOCK)

    def first(b, i):
        return (b, 0, 0, 0)

    def following(b, i):
        return (b, jnp.minimum(i + 1, nb - 1), 0, 0)

    seq_spec = pl.BlockSpec((1, seq, LANES), lambda b, i: (b, 0, 0))
    n_vis = (np.arange(nb)[:, None] * Q_BLOCK + np.where(np.arange(Q_BLOCK) < CHUNK, CHUNK, 2 * CHUNK)[None, :])
    quantile = np.array([[statistics.NormalDist().inv_cdf(1.0 - k_sel / n) if n > k_sel else 0.0 for n in r]
                         for r in n_vis], np.float32)
    zq = jnp.asarray(np.broadcast_to(quantile[:, None, :], (nb, SUBLANES, Q_BLOCK)))
    return pl.pallas_call(
        functools.partial(_dsa_kernel, k_sel=k_sel, n_blocks=nb),
        grid=(bsz, nb),
        in_specs=[pl.BlockSpec(slab, lambda b, i: (b, i, 0, 0)),
                  pl.BlockSpec((1, SUBLANES, Q_BLOCK), lambda b, i: (i, 0, 0)),
                  pl.BlockSpec(islab, first), pl.BlockSpec(wslab, first),
                  pl.BlockSpec(islab, following), pl.BlockSpec(wslab, following),
                  seq_spec, seq_spec,
                  pl.BlockSpec((1, nkt, LANES, KEY_TILE), lambda b, i: (b, 0, 0, 0)),
                  _const_spec((3, ATTN_HEADS, KEY_TILE, Q_BLOCK)),
                  _const_spec((ATTN_HEADS // 2, 2 * KV_RANK, 2 * HEAD_DIM))],
        out_specs=pl.BlockSpec((1, Q_BLOCK, ATTN_HEADS * HEAD_DIM), lambda b, i: (b, i, 0)),
        out_shape=jax.ShapeDtypeStruct((bsz, seq, ATTN_HEADS * HEAD_DIM), BF16),
        scratch_shapes=[pltpu.VMEM((2, nkt, KEY_TILE, Q_BLOCK), F32),
                        pltpu.VMEM((2, 2, SUBLANES, LANES), F32),
                        pltpu.VMEM((nkt, ATTN_HEADS, KEY_TILE, Q_BLOCK), F32),
                        pltpu.VMEM((ATTN_HEADS // 2, KV_RANK + SUM_ROWS, 2 * Q_BLOCK), F32)],
        compiler_params=_cparams(("parallel", "arbitrary")),
        name="dsa",
    )(qt, zq, qit, wt, qit, wt, ki, kv, kvt, tables, wuv)


def _pack3(v):
    lane = lax.broadcasted_iota(jnp.int32, v.shape, 1)
    v = jnp.where(lane < SSD_HEADS, v, 0.0)
    hi = v.astype(BF16).astype(F32)
    r = v - hi
    mid = r.astype(BF16).astype(F32)
    lo = r - mid
    return (hi + pltpu.roll(mid, SSD_HEADS, axis=1) + pltpu.roll(lo, 2 * SSD_HEADS, axis=1)).astype(BF16)


def _cumsum_rows(x):
    n = x.shape[0]
    r = lax.broadcasted_iota(jnp.int32, x.shape, 0)
    s = 1
    while s < n:
        x = x + jnp.where(r >= s, pltpu.roll(x, s, axis=0), 0.0)
        s *= 2
    return x


def _shift_rows(x, s):
    r = pltpu.roll(x, s, axis=1)
    prev = jnp.concatenate([r[-1:], r[:-1]], axis=0)
    sub = lax.broadcasted_iota(jnp.int32, x.shape, 1)
    return jnp.where(sub >= s, r, prev)


def _ssd_kernel(z_ref, xbc_ref, dt_ref, cw_ref, cb_ref, dtb_ref, alog_ref, dsk_ref, nw_ref, e_ref,
                y_ref, tail_ref, u_ref, g_ref, state_ref):
    nq = SSD_Q

    @pl.when(pl.program_id(1) == 0)
    def _():
        tail_ref[...] = jnp.zeros_like(tail_ref)
        state_ref[...] = jnp.zeros_like(state_ref)

    assert CONV_W == 4
    for blk in range(CONV_DIM // CONV_BLOCK):
        cols = slice(blk * CONV_BLOCK, (blk + 1) * CONV_BLOCK)
        ext = jnp.concatenate([tail_ref[:, cols], xbc_ref[0, :, cols].astype(F32)], axis=0)
        ext = ext.reshape(1 + nq // SUBLANES, SUBLANES, CONV_BLOCK)
        s1 = _shift_rows(ext, 1)
        a = cw_ref[3:4, cols] * ext + cw_ref[2:3, cols] * s1 + cb_ref[:, cols]
        b = cw_ref[1:2, cols] * ext + cw_ref[0:1, cols] * s1
        conv = (a + _shift_rows(b, 2))[1:].reshape(nq, CONV_BLOCK)
        u_ref[:, cols] = _silu_of_half(conv)
        tail_ref[:, cols] = ext[nq // SUBLANES]

    t = dt_ref[0] + dtb_ref[...]
    dt = jnp.maximum(t, 0.0) + jnp.log1p(jnp.exp(-jnp.abs(t)))
    a2 = _cumsum_rows(dt * (-jnp.exp(alog_ref[...]))) * LOG2E
    a2_t = a2.T
    dt_p = _pack3(dt)
    dec_p = _pack3(dt * jnp.exp2(a2[nq - 1:nq, :] - a2))
    expa_p = _pack3(jnp.exp2(a2))

    r = lax.broadcasted_iota(jnp.int32, (nq, nq), 0)
    c = lax.broadcasted_iota(jnp.int32, (nq, nq), 1)
    causal = r >= c
    lane = lax.broadcasted_iota(jnp.int32, (nq, LANES), 1)
    heads_per_group = SSD_HEADS // SSD_GROUPS
    gw = heads_per_group * SSD_HEADDIM
    b_col = D_INNER
    c_col = D_INNER + SSD_GROUPS * D_STATE
    ssq = jnp.zeros((nq, LANES), F32)

    for g in range(SSD_GROUPS):
        gcols = slice(g * gw, (g + 1) * gw)
        eg = e_ref[:, gcols]
        dt_e = jnp.dot(dt_p, eg, preferred_element_type=F32)
        dec_e = jnp.dot(dec_p, eg, preferred_element_type=F32)
        expa_e = jnp.dot(expa_p, eg, preferred_element_type=F32)
        xs = u_ref[:, gcols]
        xdt_b = (xs * dt_e).astype(BF16)
        xdec_b = (xs * dec_e).astype(BF16)
        cg = u_ref[:, c_col + g * D_STATE:c_col + (g + 1) * D_STATE].astype(BF16)
        bgt = u_ref[:, b_col + g * D_STATE:b_col + (g + 1) * D_STATE].T.astype(BF16)
        cb = jnp.dot(cg, bgt, preferred_element_type=F32)
        prev = state_ref[g]
        y_off = jnp.dot(cg, prev.astype(BF16), preferred_element_type=F32) * expa_e
        pairs = []
        for pp in range(heads_per_group // 2):
            xp = xdt_b[:, pp * LANES:(pp + 1) * LANES]
            yh = []
            for hh in range(2):
                h = heads_per_group * g + 2 * pp + hh
                seg = a2[:, h:h + 1] - a2_t[h:h + 1, :]
                m = (cb * jnp.where(causal, jnp.exp2(seg), 0.0)).astype(BF16)
                yh.append(jnp.dot(m, xp, preferred_element_type=F32))
            pairs.append(jnp.where(lane < SSD_HEADDIM, yh[0], yh[1]))
        y = jnp.concatenate(pairs, axis=1) + y_off + dsk_ref[:, gcols] * xs
        gated = y * _silu_of_half(z_ref[0, :, gcols].astype(F32))
        g_ref[:, gcols] = gated
        sq = gated * gated
        ssq = ssq + sq[:, :LANES] + sq[:, LANES:]
        new = jnp.dot(bgt, xdec_b, preferred_element_type=F32)
        state_ref[g] = prev * expa_e[nq - 1:nq, :] + new
    scale = lax.rsqrt(jnp.sum(ssq, axis=-1, keepdims=True) * (1.0 / D_INNER) + EPS)
    y_ref[0] = (g_ref[...] * scale * nw_ref[...]).astype(BF16)


def _ssd(head, wide, conv_w, conv_b, dt_bias, a_log, d_skip, ssd_norm):
    bsz, seq, _ = head.shape
    nq = SSD_Q

    def pad_heads(v):
        return jnp.concatenate([v, jnp.zeros((LANES - SSD_HEADS,), F32)]).reshape(1, LANES)

    sel = np.concatenate([np.eye(SSD_HEADS)] * 3 + [np.zeros((LANES - 3 * SSD_HEADS, SSD_HEADS))], axis=0)
    e = jnp.asarray(np.kron(sel, np.ones((1, SSD_HEADDIM))), BF16)
    return pl.pallas_call(
        _ssd_kernel,
        grid=(bsz, seq // nq),
        in_specs=[pl.BlockSpec((1, nq, D_INNER), lambda b, i: (b, i, WIDE_Z // D_INNER)),
                  pl.BlockSpec((1, nq, CONV_DIM), lambda b, i: (b, i, WIDE_XBC // CONV_DIM)),
                  pl.BlockSpec((1, nq, LANES), lambda b, i: (b, i, HEAD_DT // LANES)),
                  _const_spec((CONV_W, CONV_DIM)), _const_spec((1, CONV_DIM)),
                  _const_spec((1, LANES)), _const_spec((1, LANES)),
                  _const_spec((1, D_INNER)), _const_spec((1, D_INNER)),
                  _const_spec((LANES, D_INNER))],
        out_specs=pl.BlockSpec((1, nq, D_INNER), lambda b, i: (b, i, 0)),
        out_shape=jax.ShapeDtypeStruct((bsz, seq, D_INNER), BF16),
        scratch_shapes=[pltpu.VMEM((SUBLANES, CONV_DIM), F32),
                        pltpu.VMEM((nq, CONV_DIM), F32),
                        pltpu.VMEM((nq, D_INNER), F32),
                        pltpu.VMEM((SSD_GROUPS, D_STATE, 4 * SSD_HEADDIM), F32)],
        compiler_params=_cparams(("parallel", "arbitrary")),
        name="ssd",
    )(wide, wide, head, 0.5 * conv_w, 0.5 * conv_b.reshape(1, CONV_DIM), pad_heads(dt_bias), pad_heads(a_log),
      jnp.repeat(d_skip, SSD_HEADDIM).reshape(1, D_INNER), ssd_norm.reshape(1, D_INNER), e)


def _mix_kernel(ao_ref, sy_ref, ga_ref, gb_ref, x_ref, mod_ref, nw_ref, woa_ref, wos_ref, wout_ref, o_ref):
    ya = jnp.dot(ao_ref[0], woa_ref[...], preferred_element_type=F32)
    yb = jnp.dot(sy_ref[0], wos_ref[...], preferred_element_type=F32)
    mix = jax.nn.sigmoid(ga_ref[0].astype(F32)) * ya + jax.nn.sigmoid(gb_ref[0].astype(F32)) * yb
    m2 = jnp.dot(mix.astype(BF16), wout_ref[...], preferred_element_type=F32)
    o_ref[0] = x_ref[0] + mod_ref[0][2:3] * _rms(m2, nw_ref[...])


def _mix(attn_o, ssd_y, wide, x, mod3, post_norm, w_o_attn, w_o_ssd, w_out):
    bsz, seq, _ = x.shape
    tm = min(seq, 512)

    def rows(width, col_block=0):
        return pl.BlockSpec((1, tm, width), lambda b, i: (b, i, col_block))

    return pl.pallas_call(
        _mix_kernel,
        grid=(bsz, seq // tm),
        in_specs=[rows(D_MODEL), rows(D_INNER), rows(D_MODEL, WIDE_GA // D_MODEL), rows(D_MODEL, WIDE_GB // D_MODEL),
                  rows(D_MODEL), pl.BlockSpec((1, 6, D_MODEL), lambda b, i: (b, 0, 0)),
                  _const_spec((1, D_MODEL)), _const_spec((D_MODEL, D_MODEL)),
                  _const_spec((D_INNER, D_MODEL)), _const_spec((D_MODEL, D_MODEL))],
        out_specs=rows(D_MODEL),
        out_shape=jax.ShapeDtypeStruct((bsz, seq, D_MODEL), F32),
        compiler_params=_cparams(("parallel", "parallel")),
        name="mix",
    )(attn_o, ssd_y, wide, wide, x, mod3, post_norm.reshape(1, D_MODEL),
      w_o_attn.astype(BF16), w_o_ssd.astype(BF16), w_out.astype(BF16))


def _ffn_kernel(x_ref, mod_ref, nw1_ref, nw2_ref, wg_ref, wu_ref, wo_ref, o_ref):
    x = x_ref[0]
    m = mod_ref[0]
    h2 = (_rms(x, nw1_ref[...]) * (1.0 + m[4:5]) + m[3:4]).astype(BF16)
    ug = jnp.dot(h2, wg_ref[...], preferred_element_type=F32)
    uu = jnp.dot(h2, wu_ref[...], preferred_element_type=F32)
    f = jnp.dot((_silu(ug) * uu).astype(BF16), wo_ref[...], preferred_element_type=F32)
    o_ref[0] = x + m[5:6] * _rms(f, nw2_ref[...])


def _ffn(x, mod3, pre_norm, post_norm, w_ffn_in, w_ffn_out):
    bsz, seq, _ = x.shape
    tm = min(seq, 512)
    rows = pl.BlockSpec((1, tm, D_MODEL), lambda b, i: (b, i, 0))
    w_in = w_ffn_in.astype(BF16)

    def half(k):
        return pl.BlockSpec((D_MODEL, D_FF), lambda b, i: (0, k), pipeline_mode=pl.Buffered(1))

    return pl.pallas_call(
        _ffn_kernel,
        grid=(bsz, seq // tm),
        in_specs=[rows, pl.BlockSpec((1, 6, D_MODEL), lambda b, i: (b, 0, 0)),
                  _const_spec((1, D_MODEL)), _const_spec((1, D_MODEL)),
                  half(0), half(1), _const_spec((D_FF, D_MODEL))],
        out_specs=rows,
        out_shape=jax.ShapeDtypeStruct((bsz, seq, D_MODEL), F32),
        compiler_params=_cparams(("parallel", "parallel")),
        name="ffn",
    )(x, mod3, pre_norm.reshape(1, D_MODEL), post_norm.reshape(1, D_MODEL), w_in, w_in, w_ffn_out.astype(BF16))


def kernel(x, c, positions, ada_w, ada_b, pre_norm_mix, post_norm_mix, pre_norm_ffn, post_norm_ffn, w_in, q_norm, kv_norm, w_uq, w_uv, rel_bias, w_qidx, kidx_norm, conv_w, conv_b, dt_bias, a_log, d_skip, ssd_norm, w_o_attn, w_o_ssd, w_out, w_ffn_in, w_ffn_out):
    del positions
    bsz, seq, _ = x.shape
    assert seq % (2 * KEY_TILE) == 0 and x.shape[-1] == D_MODEL
    mod3 = _mod(c, ada_w, ada_b).reshape(bsz, 6, D_MODEL)
    head, wide = _inproj(x, mod3, pre_norm_mix, _pack_w_in(w_in))
    qt, qit, kv, kvt, ki, wt = _prep(head, q_norm, kv_norm, kidx_norm, w_uq, w_qidx)
    attn_o = _dsa(qt, qit, wt, ki, kv, kvt, _bias_tables(rel_bias), _pack_w_uv(w_uv))
    ssd_y = _ssd(head, wide, conv_w, conv_b, dt_bias, a_log, d_skip, ssd_norm)
    x1 = _mix(attn_o, ssd_y, wide, x, mod3, post_norm_mix, w_o_attn, w_o_ssd, w_out)
    return _ffn(x1, mod3, pre_norm_ffn, post_norm_ffn, w_ffn_in, w_ffn_out)
```

```python
import functools
import math
import statistics

import numpy as np
import jax
import jax.numpy as jnp
from jax import lax
from jax.experimental import pallas as pl
from jax.experimental.pallas import tpu as pltpu

F32 = jnp.float32
BF16 = jnp.bfloat16

D_MODEL = 1024
CHUNK = 64
Q_BLOCK = 128
EPS = 1e-6
ATTN_HEADS = 16
HEAD_DIM = 64
Q_RANK = 256
KV_RANK = 128
IDX_HEADS = 16
IDX_DIM = 64
TOPK_MAX = 256
NUM_BUCKETS = 32
MAX_DISTANCE = 128
D_INNER = 2 * D_MODEL
SSD_HEADDIM = 64
SSD_HEADS = D_INNER // SSD_HEADDIM
SSD_GROUPS = 8
D_STATE = 128
CONV_W = 4
CONV_DIM = D_INNER + 2 * SSD_GROUPS * D_STATE
D_FF = -(-8 * D_MODEL // (3 * 256)) * 256

LANES = 128
SUBLANES = 8
KEY_TILE = 256
SSD_Q = 128
CONV_BLOCK = 512
VMEM_LIMIT = 56 * 1024 * 1024
BISECT_MAX_ITERS = 320
BISECT_UNROLL = 4
BISECT_COARSE_ITERS = 10
BRACKET_HALF_WIDTH = 0.3
FLOAT_BIG = 3.0e38
LOG2E = math.log2(math.e)
SUM_MIN = 2.0 ** -80
SUM_MAX = 2.0 ** 40
SUM_ROWS = 16

HEAD_COLS = 768
HEAD_KV, HEAD_KIDX, HEAD_W, HEAD_DT = 256, 384, 512, 640
WIDE_XBC, WIDE_Z, WIDE_GA, WIDE_GB = 0, 4096, 6144, 7168
WIDE_COLS = 8192
INPROJ_TN = 1024


def _cparams(sem):
    return pltpu.CompilerParams(dimension_semantics=sem, vmem_limit_bytes=VMEM_LIMIT)


def _const_spec(shape):
    nd = len(shape)
    return pl.BlockSpec(shape, lambda *_: (0,) * nd, pipeline_mode=pl.Buffered(1))


def _rms(x, w, n=None):
    n = x.shape[-1] if n is None else n
    return x * lax.rsqrt(jnp.sum(x * x, axis=-1, keepdims=True) * (1.0 / n) + EPS) * w


def _silu_of_half(h):
    return h + h * jnp.tanh(h)


def _silu(x):
    return _silu_of_half(0.5 * x)


def _mod_kernel(c_ref, w_ref, b_ref, o_ref):
    c = c_ref[...]
    s = _silu(c).astype(BF16)
    o_ref[...] = jnp.dot(s, w_ref[...].astype(BF16), preferred_element_type=F32) + b_ref[...]


def _mod(c, ada_w, ada_b):
    bsz = c.shape[0]
    return pl.pallas_call(
        _mod_kernel,
        grid=(6,),
        in_specs=[pl.BlockSpec((bsz, D_MODEL), lambda j: (0, 0)),
                  pl.BlockSpec((D_MODEL, D_MODEL), lambda j: (0, j)),
                  pl.BlockSpec((1, D_MODEL), lambda j: (0, j))],
        out_specs=pl.BlockSpec((bsz, D_MODEL), lambda j: (0, j)),
        out_shape=jax.ShapeDtypeStruct((bsz, 6 * D_MODEL), F32),
        compiler_params=_cparams(("parallel",)),
        name="mod",
    )(c, ada_w, ada_b.reshape(1, 6 * D_MODEL))


def _t5_bucket_np(rel):
    half = NUM_BUCKETS // 2
    max_exact = half // 2
    side = np.where(rel > 0, half, 0)
    n = np.abs(rel)
    large = max_exact + (np.log(np.maximum(n, max_exact).astype(np.float64) / max_exact)
                         / math.log(MAX_DISTANCE / max_exact) * (half - max_exact)).astype(np.int64)
    large = np.minimum(large, half - 1)
    return (side + np.where(n < max_exact, n, large)).astype(np.int32)


def _bias_kernel(idx_ref, rb_ref, o_ref):
    h = pl.program_id(0)
    idx = idx_ref[...]
    far = rb_ref[h, NUM_BUCKETS // 2 - 1]
    acc = jnp.zeros(idx.shape, F32)
    for b in range(NUM_BUCKETS):
        acc = jnp.where(idx == b, (rb_ref[h, b] - far) * LOG2E, acc)
    o_ref[0] = acc


def _bias_tables(rel_bias):
    kk = np.arange(2 * KEY_TILE)[:, None]
    ql = np.arange(Q_BLOCK)[None, :]
    idx = jnp.asarray(_t5_bucket_np(kk - KEY_TILE - ql))
    t = pl.pallas_call(
        _bias_kernel,
        grid=(ATTN_HEADS,),
        in_specs=[pl.BlockSpec((2 * KEY_TILE, Q_BLOCK), lambda h: (0, 0)),
                  pl.BlockSpec(memory_space=pltpu.SMEM)],
        out_specs=pl.BlockSpec((1, 2 * KEY_TILE, Q_BLOCK), lambda h: (h, 0, 0)),
        out_shape=jax.ShapeDtypeStruct((ATTN_HEADS, 2 * KEY_TILE, Q_BLOCK), F32),
        compiler_params=_cparams(("arbitrary",)),
        name="bias",
    )(idx, rel_bias.T)
    return jnp.stack([t[:, 256:512], t[:, 128:384], t[:, 0:256]])


def _inproj_kernel(x_ref, mod_ref, nw_ref, w_ref, head_ref, wide_ref):
    m = mod_ref[0]
    hn = (_rms(x_ref[0], nw_ref[...]) * (1.0 + m[1:2]) + m[0:1]).astype(BF16)
    head_ref[0] = _dot_nt(hn, w_ref[0:HEAD_COLS, :])
    for c in range(WIDE_COLS // INPROJ_TN):
        rows = slice(HEAD_COLS + c * INPROJ_TN, HEAD_COLS + (c + 1) * INPROJ_TN)
        wide_ref[0, :, c * INPROJ_TN:(c + 1) * INPROJ_TN] = _dot_nt(hn, w_ref[rows, :]).astype(BF16)


def _pack_w_in(w_in):
    sizes = [Q_RANK, KV_RANK, IDX_DIM, IDX_HEADS, D_INNER, CONV_DIM, SSD_HEADS, D_MODEL, D_MODEL]
    offs = np.cumsum([0] + sizes)
    wt = w_in.T
    q, kv, ki, wi, z, xbc, dt, ga, gb = [wt[offs[i]:offs[i + 1]] for i in range(9)]

    def zr(n):
        return jnp.zeros((n, D_MODEL), w_in.dtype)

    return jnp.concatenate([q, kv, ki, zr(LANES - IDX_DIM), wi, zr(LANES - IDX_HEADS),
                            dt, zr(LANES - SSD_HEADS), xbc, 0.5 * z, ga, gb], axis=0).astype(BF16)


def _dot_nt(a, b):
    return lax.dot_general(a, b, (((1,), (1,)), ((), ())), preferred_element_type=F32)


def _inproj(x, mod3, pre_norm, w_packed):
    bsz, seq, _ = x.shape
    tm = min(seq, 512)
    return pl.pallas_call(
        _inproj_kernel,
        grid=(bsz, seq // tm),
        in_specs=[pl.BlockSpec((1, tm, D_MODEL), lambda b, i: (b, i, 0)),
                  pl.BlockSpec((1, 6, D_MODEL), lambda b, i: (b, 0, 0)),
                  _const_spec((1, D_MODEL)),
                  _const_spec((HEAD_COLS + WIDE_COLS, D_MODEL))],
        out_specs=[pl.BlockSpec((1, tm, HEAD_COLS), lambda b, i: (b, i, 0)),
                   pl.BlockSpec((1, tm, WIDE_COLS), lambda b, i: (b, i, 0))],
        out_shape=[jax.ShapeDtypeStruct((bsz, seq, HEAD_COLS), F32),
                   jax.ShapeDtypeStruct((bsz, seq, WIDE_COLS), BF16)],
        compiler_params=_cparams(("parallel", "parallel")),
        name="inproj",
    )(x, mod3, pre_norm.reshape(1, D_MODEL), w_packed)


def _prep_kernel(p_ref, qn_ref, kvn_ref, kin_ref, wuqt_ref, wqit_ref,
                 qt_ref, qit_ref, kv_ref, kvt_ref, ki_ref, wt_ref, *, tc):
    p = p_ref[0]
    qnt = _rms(p[:, :Q_RANK], qn_ref[...]).T.astype(BF16)
    qt = (jnp.dot(wuqt_ref[...], qnt, preferred_element_type=F32) * (KV_RANK ** -0.5 * LOG2E)).astype(BF16)
    qit = jnp.dot(wqit_ref[...], qnt, preferred_element_type=F32).astype(BF16)
    wt = (p[:, HEAD_W:HEAD_W + LANES] * (IDX_HEADS ** -0.5 * IDX_DIM ** -0.5)).T
    for blk in range(tc // Q_BLOCK):
        cols = slice(blk * Q_BLOCK, (blk + 1) * Q_BLOCK)
        for h in range(ATTN_HEADS):
            qt_ref[0, blk, :, h * LANES:(h + 1) * LANES] = qt[h * LANES:(h + 1) * LANES, cols]
            qit_ref[0, blk, :, h * LANES:(h + 1) * LANES] = qit[h * IDX_DIM:(h + 1) * IDX_DIM, cols]
        wt_ref[0, blk] = wt[0:IDX_HEADS, cols]
    kv = _rms(p[:, HEAD_KV:HEAD_KV + KV_RANK], kvn_ref[...])
    kv_ref[0] = kv.astype(BF16)
    kvt = kv.T
    for c in range(tc // KEY_TILE):
        kvt_ref[0, c] = kvt[:, c * KEY_TILE:(c + 1) * KEY_TILE].astype(BF16)
    ki_ref[0] = _rms(p[:, HEAD_KIDX:HEAD_KIDX + LANES], kin_ref[...], n=IDX_DIM).astype(BF16)


def _prep(proj, q_norm, kv_norm, kidx_norm, w_uq, w_qidx):
    bsz, seq, _ = proj.shape
    tc = min(seq, 1024)
    kin = jnp.concatenate([kidx_norm, jnp.zeros((LANES - IDX_DIM,), F32)]).reshape(1, LANES)
    nb = seq // Q_BLOCK

    def slab(rows):
        return (jax.ShapeDtypeStruct((bsz, nb, rows, ATTN_HEADS * Q_BLOCK), BF16),
                pl.BlockSpec((1, tc // Q_BLOCK, rows, ATTN_HEADS * Q_BLOCK), lambda b, i: (b, i, 0, 0)))

    (q_slab, q_spec), (qi_slab, qi_spec) = slab(KV_RANK), slab(IDX_DIM)
    row_spec = pl.BlockSpec((1, tc, LANES), lambda b, i: (b, i, 0))
    return pl.pallas_call(
        functools.partial(_prep_kernel, tc=tc),
        grid=(bsz, seq // tc),
        in_specs=[pl.BlockSpec((1, tc, HEAD_COLS), lambda b, i: (b, i, 0)),
                  _const_spec((1, Q_RANK)), _const_spec((1, KV_RANK)), _const_spec((1, LANES)),
                  _const_spec((ATTN_HEADS * KV_RANK, Q_RANK)), _const_spec((IDX_HEADS * IDX_DIM, Q_RANK))],
        out_specs=[q_spec, qi_spec, row_spec,
                   pl.BlockSpec((1, tc // KEY_TILE, LANES, KEY_TILE), lambda b, i: (b, i, 0, 0)),
                   row_spec,
                   pl.BlockSpec((1, tc // Q_BLOCK, IDX_HEADS, Q_BLOCK), lambda b, i: (b, i, 0, 0))],
        out_shape=[q_slab, qi_slab, jax.ShapeDtypeStruct((bsz, seq, LANES), BF16),
                   jax.ShapeDtypeStruct((bsz, seq // KEY_TILE, LANES, KEY_TILE), BF16),
                   jax.ShapeDtypeStruct((bsz, seq, LANES), BF16),
                   jax.ShapeDtypeStruct((bsz, nb, IDX_HEADS, Q_BLOCK), F32)],
        compiler_params=_cparams(("parallel", "parallel")),
        name="prep",
    )(proj, q_norm.reshape(1, Q_RANK), kv_norm.reshape(1, KV_RANK), kin, w_uq.T.astype(BF16),
      w_qidx.T.astype(BF16))


def _dsa_kernel(qt_ref, zq_ref, qit0_ref, wt0_ref, qitn_ref, wtn_ref, ki_ref, kv_ref, kvt_ref, tb_ref,
                wuv_ref, o_ref, iscbuf_ref, mom_ref, sbuf_ref, acc_ref, *, k_sel, n_blocks):
    i = pl.program_id(1)
    last = i // 2
    odd = i % 2
    n_tiles = last + 1
    isct_ref = iscbuf_ref.at[i % 2]
    qt = qt_ref[0, 0]
    row = lax.broadcasted_iota(jnp.int32, (KEY_TILE, Q_BLOCK), 0)
    col = lax.broadcasted_iota(jnp.int32, (KEY_TILE, Q_BLOCK), 1)

    def key_rows(j):
        return pl.ds(pl.multiple_of(j * KEY_TILE, KEY_TILE), KEY_TILE)

    n_pairs = ATTN_HEADS // 2

    def pair_cols(p):
        return slice(p * KEY_TILE, (p + 1) * KEY_TILE)

    def half(x, hh):
        return x[:, hh * Q_BLOCK:(hh + 1) * Q_BLOCK]

    ACCS = 4

    def fold_rows(x, op):
        x = x.reshape(KEY_TILE // (ACCS * SUBLANES), ACCS, SUBLANES, LANES)
        return op(op(x, axis=0), axis=0)

    def idx_tile(j, blk, qit_ref, wt_ref, slot, after_pair=lambda p: None):
        dst_ref = iscbuf_ref.at[slot]
        keys = ki_ref[0, key_rows(j), :]
        wt = wt_ref[0, 0]
        acc = jnp.zeros((KEY_TILE, Q_BLOCK), F32)
        pad = jnp.zeros((LANES - IDX_DIM, KEY_TILE), BF16)
        for p in range(n_pairs):
            qi = jnp.concatenate([qit_ref[0, 0, :, pair_cols(p)], pad], axis=0)
            s = jnp.dot(keys, qi, preferred_element_type=F32)
            for hh in range(2):
                h = 2 * p + hh
                acc = acc + wt[h:h + 1, :] * jnp.maximum(half(s, hh), 0.0)
            after_pair(p)
        key_limit = blk * Q_BLOCK + jnp.where(col < CHUNK, CHUNK, 2 * CHUNK)
        visible = j * KEY_TILE + row < key_limit
        dst_ref[j] = jnp.where(visible, acc, -jnp.inf)
        seen = jnp.where(visible, acc, 0.0)
        old = jnp.where(j == 0, 0.0, mom_ref[slot])
        mom_ref[slot] = old + jnp.stack([fold_rows(seen, jnp.sum), fold_rows(seen * seen, jnp.sum)])

    def tile_loop(n, body, carry):
        def pair(jj, c):
            return body(2 * jj + 1, body(2 * jj, c))
        carry = lax.fori_loop(0, n // 2, pair, carry)
        return lax.cond(n % 2 == 1, lambda c: body(n - 1, c), lambda c: c, carry)

    @pl.when(i == 0)
    def _():
        idx_tile(0, 0, qit0_ref, wt0_ref, 0)

    def rows_all(x, op):
        return jnp.broadcast_to(op(x, axis=0, keepdims=True), (SUBLANES, LANES))

    def tile_rows(j):
        return isct_ref[j].reshape(KEY_TILE // (ACCS * SUBLANES), ACCS, SUBLANES, LANES)

    def count_where(pred):
        def body(j, c):
            x = tile_rows(j)
            for r in range(x.shape[0]):
                c = c + jnp.where(pred(x[r]), 1.0, 0.0)
            return c
        c = tile_loop(n_tiles, body, jnp.zeros((ACCS, SUBLANES, LANES), F32))
        return rows_all(jnp.sum(c, axis=0), jnp.sum)

    kf = float(k_sel)

    def is_open(cnt, stalled):
        return jnp.logical_and(cnt > kf, stalled == 0.0)

    def any_lane(mask):
        return jnp.max(jnp.where(mask, 1.0, 0.0)) > 0.0

    lane8 = lax.broadcasted_iota(jnp.int32, (SUBLANES, LANES), 1)
    n_vis = (i * Q_BLOCK + jnp.where(lane8 < CHUNK, CHUNK, 2 * CHUNK)).astype(F32)
    mom = mom_ref[i % 2]
    mean = rows_all(mom[0], jnp.sum) / n_vis
    std = jnp.sqrt(jnp.maximum(rows_all(mom[1], jnp.sum) / n_vis - mean * mean, 0.0))
    z = zq_ref[0]
    takes_all = n_vis <= kf
    lo = jnp.where(takes_all, -FLOAT_BIG, mean + (z - BRACKET_HALF_WIDTH) * std)
    hi = jnp.where(takes_all, FLOAT_BIG, mean + (z + BRACKET_HALF_WIDTH) * std)
    cnt = count_where(lambda x: x >= lo[0:1])
    cnt_hi = count_where(lambda x: x >= hi[0:1])
    low_ok = cnt >= kf
    high_ok = cnt_hi < kf
    lo = jnp.where(low_ok, lo, -FLOAT_BIG)
    cnt = jnp.where(low_ok, cnt, n_vis)
    hi = jnp.where(high_ok, hi, FLOAT_BIG)
    cnt_hi = jnp.where(high_ok, cnt_hi, 0.0)

    def bis_step(c):
        lo, hi, cnt, cnt_hi, stalled = c
        mid = 0.5 * lo + 0.5 * hi
        cm = count_where(lambda x: x >= mid[0:1])
        active = is_open(cnt, stalled)
        noprog = jnp.logical_or(mid <= lo, mid >= hi)
        move = jnp.logical_and(active, jnp.logical_not(noprog))
        up = jnp.logical_and(move, cm >= kf)
        down = jnp.logical_and(move, cm < kf)
        return (jnp.where(up, mid, lo), jnp.where(down, mid, hi), jnp.where(up, cm, cnt),
                jnp.where(down, cm, cnt_hi), jnp.where(jnp.logical_and(active, noprog), 1.0, stalled))

    def bisect(state, max_iters):
        def cond(c):
            return jnp.logical_and(c[5] < max_iters, any_lane(is_open(c[2], c[4])))

        def body(c):
            state = c[:5]
            for _ in range(BISECT_UNROLL):
                state = bis_step(state)
            return state + (c[5] + BISECT_UNROLL,)

        return lax.while_loop(cond, body, state + (jnp.int32(0),))[:5]

    def max_below(t):
        def body(j, c):
            x = tile_rows(j)
            return jnp.maximum(c, jnp.max(jnp.where(x < t, x, -jnp.inf), axis=0))
        c = lax.fori_loop(0, n_tiles, body, jnp.full((ACCS, SUBLANES, LANES), -jnp.inf, F32))
        return rows_all(jnp.max(c, axis=0), jnp.max)

    def walk(c):
        top, left = c
        return jnp.where(left > 0.0, max_below(top), top), jnp.maximum(left - 1.0, 0.0)

    state = (lo, hi, cnt, cnt_hi, jnp.zeros((SUBLANES, LANES), F32))
    for _ in range(BISECT_COARSE_ITERS):
        state = bis_step(state)
    lo, hi, cnt, cnt_hi, stalled = state
    walked = is_open(cnt, stalled)
    top, left = walk(walk((hi, jnp.where(walked, kf - cnt_hi, 0.0))))
    arrived = jnp.logical_and(walked, left == 0.0)
    lo = jnp.where(arrived, top, lo)
    hi = jnp.where(jnp.logical_and(walked, left > 0.0), top, hi)
    cnt = count_where(lambda x: x >= lo[0:1])

    def finish(state):
        thr8, _, cnt, _, _ = bisect(state, BISECT_MAX_ITERS)
        tied = cnt > kf

        @pl.when(any_lane(tied))
        def _():
            need = kf - count_where(lambda x: x > thr8[0:1])

            def body(j, seen):
                x = isct_ref[j]
                eq = jnp.where(x == thr8[0:1], 1.0, 0.0)
                inc = _cumsum_rows(eq)
                rank = inc - eq + seen[0:1]
                drop = jnp.logical_and(jnp.logical_and(tied[0:1], eq > 0.0), rank >= need[0:1])
                isct_ref[j] = jnp.where(drop, -jnp.inf, x)
                return seen + inc[KEY_TILE - 1:KEY_TILE]

            lax.fori_loop(0, n_tiles, body, jnp.zeros((SUBLANES, LANES), F32))

        return thr8

    thr8 = lax.cond(any_lane(cnt > kf), finish, lambda s: s[0], (lo, hi, cnt, cnt_hi, stalled))
    thr = thr8[0:1]

    n_far = jnp.maximum(last - 1 + odd, 0)
    next_slot = (i + 1) % 2

    zero_l = jnp.zeros((ATTN_HEADS, SUBLANES, LANES), F32)

    ones_rows = jnp.ones((SUM_ROWS, KEY_TILE), BF16)

    def sweep_tile(j, c, near, has_next):
        keys = kv_ref[0, key_rows(j), :]
        values_t = jnp.concatenate([kvt_ref[0, j], ones_rows], axis=0)
        keep = jnp.where(isct_ref[j] >= thr, 1.0, 0.0).astype(BF16)
        table = jnp.where(j == last, odd, 2)
        for p in range(n_pairs):
            s = jnp.dot(keys, qt[:, pair_cols(p)], preferred_element_type=F32)
            for hh in range(2):
                h = 2 * p + hh
                sbuf_ref[j, h] = half(s, hh) + tb_ref[table, h] if near else half(s, hh)

        def pv_pair(p):
            probs = [jnp.exp2(sbuf_ref[j, 2 * p + hh]).astype(BF16) * keep for hh in range(2)]
            acc_ref[p] += jnp.dot(values_t, jnp.concatenate(probs, axis=1),
                                  preferred_element_type=F32)

        if has_next:
            idx_tile(j, i + 1, qitn_ref, wtn_ref, next_slot, after_pair=pv_pair)
        else:
            for p in range(n_pairs):
                pv_pair(p)
        return c

    def sweep(has_next):
        def run(c):
            c = tile_loop(n_far, lambda j, c: sweep_tile(j, c, False, has_next), c)
            def two(c):
                return sweep_tile(last, sweep_tile(last - 1, c, True, has_next), True, has_next)

            def one(c):
                return sweep_tile(last, c, True, has_next)

            def one_and_extra(c):
                c = one(c)
                idx_tile(n_tiles, i + 1, qitn_ref, wtn_ref, next_slot)
                return c

            if not has_next:
                return lax.cond(n_tiles - n_far == 2, two, one, c)
            return lax.switch(jnp.where(n_tiles - n_far == 2, 0, 1 + odd), [two, one, one_and_extra], c)
        return run

    acc_ref[...] = jnp.zeros_like(acc_ref)
    lax.cond(i + 1 < n_blocks, sweep(True), sweep(False), jnp.int32(0))
    l = jnp.stack([acc_ref[h // 2, KV_RANK:KV_RANK + 1, (h % 2) * Q_BLOCK:(h % 2 + 1) * Q_BLOCK]
                   for h in range(ATTN_HEADS)])

    def p1_tile(j, m, near):
        keys = kv_ref[0, key_rows(j), :]
        neg = jnp.where(isct_ref[j] >= thr, 0.0, -jnp.inf)
        table = jnp.where(j == last, odd, 2)
        out = []
        for p in range(n_pairs):
            s = jnp.dot(keys, qt[:, pair_cols(p)], preferred_element_type=F32)
            for hh in range(2):
                h = 2 * p + hh
                sh = half(s, hh) + neg
                if near:
                    sh = sh + tb_ref[table, h]
                sbuf_ref[j, h] = sh
                out.append(jnp.maximum(m[h], fold_rows(sh, jnp.max)))
        return jnp.stack(out)

    def exact_softmax(_):
        m = jnp.full((ATTN_HEADS, SUBLANES, LANES), -jnp.inf, F32)
        m = tile_loop(n_far, lambda j, m: p1_tile(j, m, False), m)
        m = lax.fori_loop(n_far, n_tiles, lambda j, m: p1_tile(j, m, True), m)
        m = jnp.max(m, axis=1, keepdims=True)
        acc_ref[...] = jnp.zeros_like(acc_ref)

        def p2_body(j, l):
            values_t = kvt_ref[0, j]
            out = []
            for p in range(n_pairs):
                probs = []
                for hh in range(2):
                    h = 2 * p + hh
                    e = jnp.exp2(sbuf_ref[j, h] - m[h])
                    out.append(l[h] + fold_rows(e, jnp.sum))
                    probs.append(e.astype(BF16))
                acc_ref[p, 0:KV_RANK] += jnp.dot(values_t, jnp.concatenate(probs, axis=1),
                                                 preferred_element_type=F32)
            return jnp.stack(out)

        return jnp.sum(tile_loop(n_tiles, p2_body, zero_l), axis=1, keepdims=True)

    in_range = jnp.logical_and(l > SUM_MIN, l < SUM_MAX)
    out_of_range = jnp.max(jnp.where(in_range, 0.0, 1.0)) > 0.0
    l = lax.cond(out_of_range, exact_softmax, lambda l: l, l)

    outs = [(acc_ref[h // 2, 0:KV_RANK, (h % 2) * Q_BLOCK:(h % 2 + 1) * Q_BLOCK] / l[h]).T.astype(BF16)
            for h in range(ATTN_HEADS)]
    for p in range(ATTN_HEADS // 2):
        pair = jnp.concatenate(outs[2 * p:2 * p + 2], axis=1)
        o_ref[0, :, p * LANES:(p + 1) * LANES] = jnp.dot(
            pair, wuv_ref[p], preferred_element_type=F32).astype(BF16)


def _pack_w_uv(w_uv):
    eye = jnp.eye(2, dtype=w_uv.dtype)
    w = w_uv.reshape(ATTN_HEADS // 2, 2, KV_RANK, 1, HEAD_DIM) * eye[None, :, None, :, None]
    return w.reshape(ATTN_HEADS // 2, 2 * KV_RANK, 2 * HEAD_DIM).astype(BF16)


def _dsa(qt, qit, wt, ki, kv, kvt, tables, wuv):
    bsz, seq, _ = kv.shape
    nkt = seq // KEY_TILE
    k_sel = min(TOPK_MAX, seq // 4)
    nb = seq // Q_BLOCK
    slab = (1, 1, KV_RANK, ATTN_HEADS * Q_BLOCK)
    islab = (1, 1, IDX_DIM, IDX_HEADS * Q_BLOCK)
    wslab = (1, 1, IDX_HEADS, Q_BLOCK)

    def first(b, i):
        return (b, 0, 0, 0)

    def following(b, i):
        return (b, jnp.minimum(i + 1, nb - 1), 0, 0)

    seq_spec = pl.BlockSpec((1, seq, LANES), lambda b, i: (b, 0, 0))
    n_vis = (np.arange(nb)[:, None] * Q_BLOCK + np.where(np.arange(Q_BLOCK) < CHUNK, CHUNK, 2 * CHUNK)[None, :])
    quantile = np.array([[statistics.NormalDist().inv_cdf(1.0 - k_sel / n) if n > k_sel else 0.0 for n in r]
                         for r in n_vis], np.float32)
    zq = jnp.asarray(np.broadcast_to(quantile[:, None, :], (nb, SUBLANES, Q_BLOCK)))
    return pl.pallas_call(
        functools.partial(_dsa_kernel, k_sel=k_sel, n_blocks=nb),
        grid=(bsz, nb),
        in_specs=[pl.BlockSpec(slab, lambda b, i: (b, i, 0, 0)),
                  pl.BlockSpec((1, SUBLANES, Q_BLOCK), lambda b, i: (i, 0, 0)),
                  pl.BlockSpec(islab, first), pl.BlockSpec(wslab, first),
                  pl.BlockSpec(islab, following), pl.BlockSpec(wslab, following),
                  seq_spec, seq_spec,
                  pl.BlockSpec((1, nkt, LANES, KEY_TILE), lambda b, i: (b, 0, 0, 0)),
                  _const_spec((3, ATTN_HEADS, KEY_TILE, Q_BLOCK)),
                  _const_spec((ATTN_HEADS // 2, 2 * KV_RANK, 2 * HEAD_DIM))],
        out_specs=pl.BlockSpec((1, Q_BLOCK, ATTN_HEADS * HEAD_DIM), lambda b, i: (b, i, 0)),
        out_shape=jax.ShapeDtypeStruct((bsz, seq, ATTN_HEADS * HEAD_DIM), BF16),
        scratch_shapes=[pltpu.VMEM((2, nkt, KEY_TILE, Q_BLOCK), F32),
                        pltpu.VMEM((2, 2, SUBLANES, LANES), F32),
                        pltpu.VMEM((nkt, ATTN_HEADS, KEY_TILE, Q_BLOCK), F32),
                        pltpu.VMEM((ATTN_HEADS // 2, KV_RANK + SUM_ROWS, 2 * Q_BLOCK), F32)],
        compiler_params=_cparams(("parallel", "arbitrary")),
        name="dsa",
    )(qt, zq, qit, wt, qit, wt, ki, kv, kvt, tables, wuv)


def _pack3(v):
    lane = lax.broadcasted_iota(jnp.int32, v.shape, 1)
    v = jnp.where(lane < SSD_HEADS, v, 0.0)
    hi = v.astype(BF16).astype(F32)
    r = v - hi
    mid = r.astype(BF16).astype(F32)
    lo = r - mid
    return (hi + pltpu.roll(mid, SSD_HEADS, axis=1) + pltpu.roll(lo, 2 * SSD_HEADS, axis=1)).astype(BF16)


def _cumsum_rows(x):
    n = x.shape[0]
    r = lax.broadcasted_iota(jnp.int32, x.shape, 0)
    s = 1
    while s < n:
        x = x + jnp.where(r >= s, pltpu.roll(x, s, axis=0), 0.0)
        s *= 2
    return x


def _shift_rows(x, s):
    r = pltpu.roll(x, s, axis=1)
    prev = jnp.concatenate([r[-1:], r[:-1]], axis=0)
    sub = lax.broadcasted_iota(jnp.int32, x.shape, 1)
    return jnp.where(sub >= s, r, prev)


def _ssd_kernel(z_ref, xbc_ref, dt_ref, cw_ref, cb_ref, dtb_ref, alog_ref, dsk_ref, nw_ref, e_ref,
                y_ref, tail_ref, u_ref, g_ref, state_ref):
    nq = SSD_Q

    @pl.when(pl.program_id(1) == 0)
    def _():
        tail_ref[...] = jnp.zeros_like(tail_ref)
        state_ref[...] = jnp.zeros_like(state_ref)

    assert CONV_W == 4
    for blk in range(CONV_DIM // CONV_BLOCK):
        cols = slice(blk * CONV_BLOCK, (blk + 1) * CONV_BLOCK)
        ext = jnp.concatenate([tail_ref[:, cols], xbc_ref[0, :, cols].astype(F32)], axis=0)
        ext = ext.reshape(1 + nq // SUBLANES, SUBLANES, CONV_BLOCK)
        s1 = _shift_rows(ext, 1)
        a = cw_ref[3:4, cols] * ext + cw_ref[2:3, cols] * s1 + cb_ref[:, cols]
        b = cw_ref[1:2, cols] * ext + cw_ref[0:1, cols] * s1
        conv = (a + _shift_rows(b, 2))[1:].reshape(nq, CONV_BLOCK)
        u_ref[:, cols] = _silu_of_half(conv)
        tail_ref[:, cols] = ext[nq // SUBLANES]

    t = dt_ref[0] + dtb_ref[...]
    dt = jnp.maximum(t, 0.0) + jnp.log1p(jnp.exp(-jnp.abs(t)))
    a2 = _cumsum_rows(dt * (-jnp.exp(alog_ref[...]))) * LOG2E
    a2_t = a2.T
    dt_p = _pack3(dt)
    dec_p = _pack3(dt * jnp.exp2(a2[nq - 1:nq, :] - a2))
    expa_p = _pack3(jnp.exp2(a2))

    r = lax.broadcasted_iota(jnp.int32, (nq, nq), 0)
    c = lax.broadcasted_iota(jnp.int32, (nq, nq), 1)
    causal = r >= c
    lane = lax.broadcasted_iota(jnp.int32, (nq, LANES), 1)
    heads_per_group = SSD_HEADS // SSD_GROUPS
    gw = heads_per_group * SSD_HEADDIM
    b_col = D_INNER
    c_col = D_INNER + SSD_GROUPS * D_STATE
    ssq = jnp.zeros((nq, LANES), F32)

    for g in range(SSD_GROUPS):
        gcols = slice(g * gw, (g + 1) * gw)
        eg = e_ref[:, gcols]
        dt_e = jnp.dot(dt_p, eg, preferred_element_type=F32)
        dec_e = jnp.dot(dec_p, eg, preferred_element_type=F32)
        expa_e = jnp.dot(expa_p, eg, preferred_element_type=F32)
        xs = u_ref[:, gcols]
        xdt_b = (xs * dt_e).astype(BF16)
        xdec_b = (xs * dec_e).astype(BF16)
        cg = u_ref[:, c_col + g * D_STATE:c_col + (g + 1) * D_STATE].astype(BF16)
        bgt = u_ref[:, b_col + g * D_STATE:b_col + (g + 1) * D_STATE].T.astype(BF16)
        cb = jnp.dot(cg, bgt, preferred_element_type=F32)
        prev = state_ref[g]
        y_off = jnp.dot(cg, prev.astype(BF16), preferred_element_type=F32) * expa_e
        pairs = []
        for pp in range(heads_per_group // 2):
            xp = xdt_b[:, pp * LANES:(pp + 1) * LANES]
            yh = []
            for hh in range(2):
                h = heads_per_group * g + 2 * pp + hh
                seg = a2[:, h:h + 1] - a2_t[h:h + 1, :]
                m = (cb * jnp.where(causal, jnp.exp2(seg), 0.0)).astype(BF16)
                yh.append(jnp.dot(m, xp, preferred_element_type=F32))
            pairs.append(jnp.where(lane < SSD_HEADDIM, yh[0], yh[1]))
        y = jnp.concatenate(pairs, axis=1) + y_off + dsk_ref[:, gcols] * xs
        gated = y * _silu_of_half(z_ref[0, :, gcols].astype(F32))
        g_ref[:, gcols] = gated
        sq = gated * gated
        ssq = ssq + sq[:, :LANES] + sq[:, LANES:]
        new = jnp.dot(bgt, xdec_b, preferred_element_type=F32)
        state_ref[g] = prev * expa_e[nq - 1:nq, :] + new
    scale = lax.rsqrt(jnp.sum(ssq, axis=-1, keepdims=True) * (1.0 / D_INNER) + EPS)
    y_ref[0] = (g_ref[...] * scale * nw_ref[...]).astype(BF16)


def _ssd(head, wide, conv_w, conv_b, dt_bias, a_log, d_skip, ssd_norm):
    bsz, seq, _ = head.shape
    nq = SSD_Q

    def pad_heads(v):
        return jnp.concatenate([v, jnp.zeros((LANES - SSD_HEADS,), F32)]).reshape(1, LANES)

    sel = np.concatenate([np.eye(SSD_HEADS)] * 3 + [np.zeros((LANES - 3 * SSD_HEADS, SSD_HEADS))], axis=0)
    e = jnp.asarray(np.kron(sel, np.ones((1, SSD_HEADDIM))), BF16)
    return pl.pallas_call(
        _ssd_kernel,
        grid=(bsz, seq // nq),
        in_specs=[pl.BlockSpec((1, nq, D_INNER), lambda b, i: (b, i, WIDE_Z // D_INNER)),
                  pl.BlockSpec((1, nq, CONV_DIM), lambda b, i: (b, i, WIDE_XBC // CONV_DIM)),
                  pl.BlockSpec((1, nq, LANES), lambda b, i: (b, i, HEAD_DT // LANES)),
                  _const_spec((CONV_W, CONV_DIM)), _const_spec((1, CONV_DIM)),
                  _const_spec((1, LANES)), _const_spec((1, LANES)),
                  _const_spec((1, D_INNER)), _const_spec((1, D_INNER)),
                  _const_spec((LANES, D_INNER))],
        out_specs=pl.BlockSpec((1, nq, D_INNER), lambda b, i: (b, i, 0)),
        out_shape=jax.ShapeDtypeStruct((bsz, seq, D_INNER), BF16),
        scratch_shapes=[pltpu.VMEM((SUBLANES, CONV_DIM), F32),
                        pltpu.VMEM((nq, CONV_DIM), F32),
                        pltpu.VMEM((nq, D_INNER), F32),
                        pltpu.VMEM((SSD_GROUPS, D_STATE, 4 * SSD_HEADDIM), F32)],
        compiler_params=_cparams(("parallel", "arbitrary")),
        name="ssd",
    )(wide, wide, head, 0.5 * conv_w, 0.5 * conv_b.reshape(1, CONV_DIM), pad_heads(dt_bias), pad_heads(a_log),
      jnp.repeat(d_skip, SSD_HEADDIM).reshape(1, D_INNER), ssd_norm.reshape(1, D_INNER), e)


def _mix_kernel(ao_ref, sy_ref, ga_ref, gb_ref, x_ref, mod_ref, nw_ref, woa_ref, wos_ref, wout_ref, o_ref):
    ya = jnp.dot(ao_ref[0], woa_ref[...], preferred_element_type=F32)
    yb = jnp.dot(sy_ref[0], wos_ref[...], preferred_element_type=F32)
    mix = jax.nn.sigmoid(ga_ref[0].astype(F32)) * ya + jax.nn.sigmoid(gb_ref[0].astype(F32)) * yb
    m2 = jnp.dot(mix.astype(BF16), wout_ref[...], preferred_element_type=F32)
    o_ref[0] = x_ref[0] + mod_ref[0][2:3] * _rms(m2, nw_ref[...])


def _mix(attn_o, ssd_y, wide, x, mod3, post_norm, w_o_attn, w_o_ssd, w_out):
    bsz, seq, _ = x.shape
    tm = min(seq, 512)

    def rows(width, col_block=0):
        return pl.BlockSpec((1, tm, width), lambda b, i: (b, i, col_block))

    return pl.pallas_call(
        _mix_kernel,
        grid=(bsz, seq // tm),
        in_specs=[rows(D_MODEL), rows(D_INNER), rows(D_MODEL, WIDE_GA // D_MODEL), rows(D_MODEL, WIDE_GB // D_MODEL),
                  rows(D_MODEL), pl.BlockSpec((1, 6, D_MODEL), lambda b, i: (b, 0, 0)),
                  _const_spec((1, D_MODEL)), _const_spec((D_MODEL, D_MODEL)),
                  _const_spec((D_INNER, D_MODEL)), _const_spec((D_MODEL, D_MODEL))],
        out_specs=rows(D_MODEL),
        out_shape=jax.ShapeDtypeStruct((bsz, seq, D_MODEL), F32),
        compiler_params=_cparams(("parallel", "parallel")),
        name="mix",
    )(attn_o, ssd_y, wide, wide, x, mod3, post_norm.reshape(1, D_MODEL),
      w_o_attn.astype(BF16), w_o_ssd.astype(BF16), w_out.astype(BF16))


def _ffn_kernel(x_ref, mod_ref, nw1_ref, nw2_ref, wg_ref, wu_ref, wo_ref, o_ref):
    x = x_ref[0]
    m = mod_ref[0]
    h2 = (_rms(x, nw1_ref[...]) * (1.0 + m[4:5]) + m[3:4]).astype(BF16)
    ug = jnp.dot(h2, wg_ref[...], preferred_element_type=F32)
    uu = jnp.dot(h2, wu_ref[...], preferred_element_type=F32)
    f = jnp.dot((_silu(ug) * uu).astype(BF16), wo_ref[...], preferred_element_type=F32)
    o_ref[0] = x + m[5:6] * _rms(f, nw2_ref[...])


def _ffn(x, mod3, pre_norm, post_norm, w_ffn_in, w_ffn_out):
    bsz, seq, _ = x.shape
    tm = min(seq, 512)
    rows = pl.BlockSpec((1, tm, D_MODEL), lambda b, i: (b, i, 0))
    w_in = w_ffn_in.astype(BF16)

    def half(k):
        return pl.BlockSpec((D_MODEL, D_FF), lambda b, i: (0, k), pipeline_mode=pl.Buffered(1))

    return pl.pallas_call(
        _ffn_kernel,
        grid=(bsz, seq // tm),
        in_specs=[rows, pl.BlockSpec((1, 6, D_MODEL), lambda b, i: (b, 0, 0)),
                  _const_spec((1, D_MODEL)), _const_spec((1, D_MODEL)),
                  half(0), half(1), _const_spec((D_FF, D_MODEL))],
        out_specs=rows,
        out_shape=jax.ShapeDtypeStruct((bsz, seq, D_MODEL), F32),
        compiler_params=_cparams(("parallel", "parallel")),
        name="ffn",
    )(x, mod3, pre_norm.reshape(1, D_MODEL), post_norm.reshape(1, D_MODEL), w_in, w_in, w_ffn_out.astype(BF16))


def kernel(x, c, positions, ada_w, ada_b, pre_norm_mix, post_norm_mix, pre_norm_ffn, post_norm_ffn, w_in, q_norm, kv_norm, w_uq, w_uv, rel_bias, w_qidx, kidx_norm, conv_w, conv_b, dt_bias, a_log, d_skip, ssd_norm, w_o_attn, w_o_ssd, w_out, w_ffn_in, w_ffn_out):
    del positions
    bsz, seq, _ = x.shape
    assert seq % (2 * KEY_TILE) == 0 and x.shape[-1] == D_MODEL
    mod3 = _mod(c, ada_w, ada_b).reshape(bsz, 6, D_MODEL)
    head, wide = _inproj(x, mod3, pre_norm_mix, _pack_w_in(w_in))
    qt, qit, kv, kvt, ki, wt = _prep(head, q_norm, kv_norm, kidx_norm, w_uq, w_qidx)
    attn_o = _dsa(qt, qit, wt, ki, kv, kvt, _bias_tables(rel_bias), _pack_w_uv(w_uv))
    ssd_y = _ssd(head, wide, conv_w, conv_b, dt_bias, a_log, d_skip, ssd_norm)
    x1 = _mix(attn_o, ssd_y, wide, x, mod3, post_norm_mix, w_o_attn, w_o_ssd, w_out)
    return _ffn(x1, mod3, pre_norm_ffn, post_norm_ffn, w_ffn_in, w_ffn_out)
```

```python
import functools
import math
import statistics

import numpy as np
import jax
import jax.numpy as jnp
from jax import lax
from jax.experimental import pallas as pl
from jax.experimental.pallas import tpu as pltpu

F32 = jnp.float32
BF16 = jnp.bfloat16

D_MODEL = 1024
CHUNK = 64
Q_BLOCK = 128
EPS = 1e-6
ATTN_HEADS = 16
HEAD_DIM = 64
Q_RANK = 256
KV_RANK = 128
IDX_HEADS = 16
IDX_DIM = 64
TOPK_MAX = 256
NUM_BUCKETS = 32
MAX_DISTANCE = 128
D_INNER = 2 * D_MODEL
SSD_HEADDIM = 64
SSD_HEADS = D_INNER // SSD_HEADDIM
SSD_GROUPS = 8
D_STATE = 128
CONV_W = 4
CONV_DIM = D_INNER + 2 * SSD_GROUPS * D_STATE
D_FF = -(-8 * D_MODEL // (3 * 256)) * 256

LANES = 128
SUBLANES = 8
KEY_TILE = 256
SSD_Q = 128
CONV_BLOCK = 512
VMEM_LIMIT = 56 * 1024 * 1024
BISECT_MAX_ITERS = 320
BISECT_UNROLL = 4
BISECT_COARSE_ITERS = 10
BRACKET_HALF_WIDTH = 0.3
FLOAT_BIG = 3.0e38
LOG2E = math.log2(math.e)
SUM_MIN = 2.0 ** -80
SUM_MAX = 2.0 ** 40
SUM_ROWS = 16

HEAD_COLS = 768
HEAD_KV, HEAD_KIDX, HEAD_W, HEAD_DT = 256, 384, 512, 640
WIDE_XBC, WIDE_Z, WIDE_GA, WIDE_GB = 0, 4096, 6144, 7168
WIDE_COLS = 8192
INPROJ_TN = 1024


def _cparams(sem):
    return pltpu.CompilerParams(dimension_semantics=sem, vmem_limit_bytes=VMEM_LIMIT)


def _const_spec(shape):
    nd = len(shape)
    return pl.BlockSpec(shape, lambda *_: (0,) * nd, pipeline_mode=pl.Buffered(1))


def _rms(x, w, n=None):
    n = x.shape[-1] if n is None else n
    return x * lax.rsqrt(jnp.sum(x * x, axis=-1, keepdims=True) * (1.0 / n) + EPS) * w


def _silu_of_half(h):
    return h + h * jnp.tanh(h)


def _silu(x):
    return _silu_of_half(0.5 * x)


def _mod_kernel(c_ref, w_ref, b_ref, o_ref):
    c = c_ref[...]
    s = _silu(c).astype(BF16)
    o_ref[...] = jnp.dot(s, w_ref[...].astype(BF16), preferred_element_type=F32) + b_ref[...]


def _mod(c, ada_w, ada_b):
    bsz = c.shape[0]
    return pl.pallas_call(
        _mod_kernel,
        grid=(6,),
        in_specs=[pl.BlockSpec((bsz, D_MODEL), lambda j: (0, 0)),
                  pl.BlockSpec((D_MODEL, D_MODEL), lambda j: (0, j)),
                  pl.BlockSpec((1, D_MODEL), lambda j: (0, j))],
        out_specs=pl.BlockSpec((bsz, D_MODEL), lambda j: (0, j)),
        out_shape=jax.ShapeDtypeStruct((bsz, 6 * D_MODEL), F32),
        compiler_params=_cparams(("parallel",)),
        name="mod",
    )(c, ada_w, ada_b.reshape(1, 6 * D_MODEL))


def _t5_bucket_np(rel):
    half = NUM_BUCKETS // 2
    max_exact = half // 2
    side = np.where(rel > 0, half, 0)
    n = np.abs(rel)
    large = max_exact + (np.log(np.maximum(n, max_exact).astype(np.float64) / max_exact)
                         / math.log(MAX_DISTANCE / max_exact) * (half - max_exact)).astype(np.int64)
    large = np.minimum(large, half - 1)
    return (side + np.where(n < max_exact, n, large)).astype(np.int32)


def _bias_kernel(idx_ref, rb_ref, o_ref):
    h = pl.program_id(0)
    idx = idx_ref[...]
    far = rb_ref[h, NUM_BUCKETS // 2 - 1]
    acc = jnp.zeros(idx.shape, F32)
    for b in range(NUM_BUCKETS):
        acc = jnp.where(idx == b, (rb_ref[h, b] - far) * LOG2E, acc)
    o_ref[0] = acc


def _bias_tables(rel_bias):
    kk = np.arange(2 * KEY_TILE)[:, None]
    ql = np.arange(Q_BLOCK)[None, :]
    idx = jnp.asarray(_t5_bucket_np(kk - KEY_TILE - ql))
    t = pl.pallas_call(
        _bias_kernel,
        grid=(ATTN_HEADS,),
        in_specs=[pl.BlockSpec((2 * KEY_TILE, Q_BLOCK), lambda h: (0, 0)),
                  pl.BlockSpec(memory_space=pltpu.SMEM)],
        out_specs=pl.BlockSpec((1, 2 * KEY_TILE, Q_BLOCK), lambda h: (h, 0, 0)),
        out_shape=jax.ShapeDtypeStruct((ATTN_HEADS, 2 * KEY_TILE, Q_BLOCK), F32),
        compiler_params=_cparams(("arbitrary",)),
        name="bias",
    )(idx, rel_bias.T)
    return jnp.stack([t[:, 256:512], t[:, 128:384], t[:, 0:256]])


def _inproj_kernel(x_ref, mod_ref, nw_ref, w_ref, head_ref, wide_ref):
    m = mod_ref[0]
    hn = (_rms(x_ref[0], nw_ref[...]) * (1.0 + m[1:2]) + m[0:1]).astype(BF16)
    head_ref[0] = _dot_nt(hn, w_ref[0:HEAD_COLS, :])
    for c in range(WIDE_COLS // INPROJ_TN):
        rows = slice(HEAD_COLS + c * INPROJ_TN, HEAD_COLS + (c + 1) * INPROJ_TN)
        wide_ref[0, :, c * INPROJ_TN:(c + 1) * INPROJ_TN] = _dot_nt(hn, w_ref[rows, :]).astype(BF16)


def _pack_w_in(w_in):
    sizes = [Q_RANK, KV_RANK, IDX_DIM, IDX_HEADS, D_INNER, CONV_DIM, SSD_HEADS, D_MODEL, D_MODEL]
    offs = np.cumsum([0] + sizes)
    wt = w_in.T
    q, kv, ki, wi, z, xbc, dt, ga, gb = [wt[offs[i]:offs[i + 1]] for i in range(9)]

    def zr(n):
        return jnp.zeros((n, D_MODEL), w_in.dtype)

    return jnp.concatenate([q, kv, ki, zr(LANES - IDX_DIM), wi, zr(LANES - IDX_HEADS),
                            dt, zr(LANES - SSD_HEADS), xbc, 0.5 * z, ga, gb], axis=0).astype(BF16)


def _dot_nt(a, b):
    return lax.dot_general(a, b, (((1,), (1,)), ((), ())), preferred_element_type=F32)


def _inproj(x, mod3, pre_norm, w_packed):
    bsz, seq, _ = x.shape
    tm = min(seq, 512)
    return pl.pallas_call(
        _inproj_kernel,
        grid=(bsz, seq // tm),
        in_specs=[pl.BlockSpec((1, tm, D_MODEL), lambda b, i: (b, i, 0)),
                  pl.BlockSpec((1, 6, D_MODEL), lambda b, i: (b, 0, 0)),
                  _const_spec((1, D_MODEL)),
                  _const_spec((HEAD_COLS + WIDE_COLS, D_MODEL))],
        out_specs=[pl.BlockSpec((1, tm, HEAD_COLS), lambda b, i: (b, i, 0)),
                   pl.BlockSpec((1, tm, WIDE_COLS), lambda b, i: (b, i, 0))],
        out_shape=[jax.ShapeDtypeStruct((bsz, seq, HEAD_COLS), F32),
                   jax.ShapeDtypeStruct((bsz, seq, WIDE_COLS), BF16)],
        compiler_params=_cparams(("parallel", "parallel")),
        name="inproj",
    )(x, mod3, pre_norm.reshape(1, D_MODEL), w_packed)


def _prep_kernel(p_ref, qn_ref, kvn_ref, kin_ref, wuqt_ref, wqit_ref,
                 qt_ref, qit_ref, kv_ref, kvt_ref, ki_ref, wt_ref, *, tc):
    p = p_ref[0]
    qnt = _rms(p[:, :Q_RANK], qn_ref[...]).T.astype(BF16)
    qt = (jnp.dot(wuqt_ref[...], qnt, preferred_element_type=F32) * (KV_RANK ** -0.5 * LOG2E)).astype(BF16)
    qit = jnp.dot(wqit_ref[...], qnt, preferred_element_type=F32).astype(BF16)
    wt = (p[:, HEAD_W:HEAD_W + LANES] * (IDX_HEADS ** -0.5 * IDX_DIM ** -0.5)).T
    for blk in range(tc // Q_BLOCK):
        cols = slice(blk * Q_BLOCK, (blk + 1) * Q_BLOCK)
        for h in range(ATTN_HEADS):
            qt_ref[0, blk, :, h * LANES:(h + 1) * LANES] = qt[h * LANES:(h + 1) * LANES, cols]
            qit_ref[0, blk, :, h * LANES:(h + 1) * LANES] = qit[h * IDX_DIM:(h + 1) * IDX_DIM, cols]
        wt_ref[0, blk] = wt[0:IDX_HEADS, cols]
    kv = _rms(p[:, HEAD_KV:HEAD_KV + KV_RANK], kvn_ref[...])
    kv_ref[0] = kv.astype(BF16)
    kvt = kv.T
    for c in range(tc // KEY_TILE):
        kvt_ref[0, c] = kvt[:, c * KEY_TILE:(c + 1) * KEY_TILE].astype(BF16)
    ki_ref[0] = _rms(p[:, HEAD_KIDX:HEAD_KIDX + LANES], kin_ref[...], n=IDX_DIM).astype(BF16)


def _prep(proj, q_norm, kv_norm, kidx_norm, w_uq, w_qidx):
    bsz, seq, _ = proj.shape
    tc = min(seq, 1024)
    kin = jnp.concatenate([kidx_norm, jnp.zeros((LANES - IDX_DIM,), F32)]).reshape(1, LANES)
    nb = seq // Q_BLOCK

    def slab(rows):
        return (jax.ShapeDtypeStruct((bsz, nb, rows, ATTN_HEADS * Q_BLOCK), BF16),
                pl.BlockSpec((1, tc // Q_BLOCK, rows, ATTN_HEADS * Q_BLOCK), lambda b, i: (b, i, 0, 0)))

    (q_slab, q_spec), (qi_slab, qi_spec) = slab(KV_RANK), slab(IDX_DIM)
    row_spec = pl.BlockSpec((1, tc, LANES), lambda b, i: (b, i, 0))
    return pl.pallas_call(
        functools.partial(_prep_kernel, tc=tc),
        grid=(bsz, seq // tc),
        in_specs=[pl.BlockSpec((1, tc, HEAD_COLS), lambda b, i: (b, i, 0)),
                  _const_spec((1, Q_RANK)), _const_spec((1, KV_RANK)), _const_spec((1, LANES)),
                  _const_spec((ATTN_HEADS * KV_RANK, Q_RANK)), _const_spec((IDX_HEADS * IDX_DIM, Q_RANK))],
        out_specs=[q_spec, qi_spec, row_spec,
                   pl.BlockSpec((1, tc // KEY_TILE, LANES, KEY_TILE), lambda b, i: (b, i, 0, 0)),
                   row_spec,
                   pl.BlockSpec((1, tc // Q_BLOCK, IDX_HEADS, Q_BLOCK), lambda b, i: (b, i, 0, 0))],
        out_shape=[q_slab, qi_slab, jax.ShapeDtypeStruct((bsz, seq, LANES), BF16),
                   jax.ShapeDtypeStruct((bsz, seq // KEY_TILE, LANES, KEY_TILE), BF16),
                   jax.ShapeDtypeStruct((bsz, seq, LANES), BF16),
                   jax.ShapeDtypeStruct((bsz, nb, IDX_HEADS, Q_BLOCK), F32)],
        compiler_params=_cparams(("parallel", "parallel")),
        name="prep",
    )(proj, q_norm.reshape(1, Q_RANK), kv_norm.reshape(1, KV_RANK), kin, w_uq.T.astype(BF16),
      w_qidx.T.astype(BF16))


def _dsa_kernel(qt_ref, zq_ref, qit0_ref, wt0_ref, qitn_ref, wtn_ref, ki_ref, kv_ref, kvt_ref, tb_ref,
                wuv_ref, o_ref, iscbuf_ref, mom_ref, sbuf_ref, acc_ref, *, k_sel, n_blocks):
    i = pl.program_id(1)
    last = i // 2
    odd = i % 2
    n_tiles = last + 1
    isct_ref = iscbuf_ref.at[i % 2]
    qt = qt_ref[0, 0]
    row = lax.broadcasted_iota(jnp.int32, (KEY_TILE, Q_BLOCK), 0)
    col = lax.broadcasted_iota(jnp.int32, (KEY_TILE, Q_BLOCK), 1)

    def key_rows(j):
        return pl.ds(pl.multiple_of(j * KEY_TILE, KEY_TILE), KEY_TILE)

    n_pairs = ATTN_HEADS // 2

    def pair_cols(p):
        return slice(p * KEY_TILE, (p + 1) * KEY_TILE)

    def half(x, hh):
        return x[:, hh * Q_BLOCK:(hh + 1) * Q_BLOCK]

    ACCS = 4

    def fold_rows(x, op):
        x = x.reshape(KEY_TILE // (ACCS * SUBLANES), ACCS, SUBLANES, LANES)
        return op(op(x, axis=0), axis=0)

    def idx_tile(j, blk, qit_ref, wt_ref, slot, after_pair=lambda p: None):
        dst_ref = iscbuf_ref.at[slot]
        keys = ki_ref[0, key_rows(j), :]
        wt = wt_ref[0, 0]
        acc = jnp.zeros((KEY_TILE, Q_BLOCK), F32)
        pad = jnp.zeros((LANES - IDX_DIM, KEY_TILE), BF16)
        for p in range(n_pairs):
            qi = jnp.concatenate([qit_ref[0, 0, :, pair_cols(p)], pad], axis=0)
            s = jnp.dot(keys, qi, preferred_element_type=F32)
            for hh in range(2):
                h = 2 * p + hh
                acc = acc + wt[h:h + 1, :] * jnp.maximum(half(s, hh), 0.0)
            after_pair(p)
        key_limit = blk * Q_BLOCK + jnp.where(col < CHUNK, CHUNK, 2 * CHUNK)
        visible = j * KEY_TILE + row < key_limit
        dst_ref[j] = jnp.where(visible, acc, -jnp.inf)
        seen = jnp.where(visible, acc, 0.0)
        old = jnp.where(j == 0, 0.0, mom_ref[slot])
        mom_ref[slot] = old + jnp.stack([fold_rows(seen, jnp.sum), fold_rows(seen * seen, jnp.sum)])

    def tile_loop(n, body, carry):
        def pair(jj, c):
            return body(2 * jj + 1, body(2 * jj, c))
        carry = lax.fori_loop(0, n // 2, pair, carry)
        return lax.cond(n % 2 == 1, lambda c: body(n - 1, c), lambda c: c, carry)

    @pl.when(i == 0)
    def _():
        idx_tile(0, 0, qit0_ref, wt0_ref, 0)

    def rows_all(x, op):
        return jnp.broadcast_to(op(x, axis=0, keepdims=True), (SUBLANES, LANES))

    def tile_rows(j):
        return isct_ref[j].reshape(KEY_TILE // (ACCS * SUBLANES), ACCS, SUBLANES, LANES)

    def count_where(pred):
        def body(j, c):
            x = tile_rows(j)
            for r in range(x.shape[0]):
                c = c + jnp.where(pred(x[r]), 1.0, 0.0)
            return c
        c = tile_loop(n_tiles, body, jnp.zeros((ACCS, SUBLANES, LANES), F32))
        return rows_all(jnp.sum(c, axis=0), jnp.sum)

    kf = float(k_sel)

    def is_open(cnt, stalled):
        return jnp.logical_and(cnt > kf, stalled == 0.0)

    def any_lane(mask):
        return jnp.max(jnp.where(mask, 1.0, 0.0)) > 0.0

    lane8 = lax.broadcasted_iota(jnp.int32, (SUBLANES, LANES), 1)
    n_vis = (i * Q_BLOCK + jnp.where(lane8 < CHUNK, CHUNK, 2 * CHUNK)).astype(F32)
    mom = mom_ref[i % 2]
    mean = rows_all(mom[0], jnp.sum) / n_vis
    std = jnp.sqrt(jnp.maximum(rows_all(mom[1], jnp.sum) / n_vis - mean * mean, 0.0))
    z = zq_ref[0]
    takes_all = n_vis <= kf
    lo = jnp.where(takes_all, -FLOAT_BIG, mean + (z - BRACKET_HALF_WIDTH) * std)
    hi = jnp.where(takes_all, FLOAT_BIG, mean + (z + BRACKET_HALF_WIDTH) * std)
    cnt = count_where(lambda x: x >= lo[0:1])
    cnt_hi = count_where(lambda x: x >= hi[0:1])
    low_ok = cnt >= kf
    high_ok = cnt_hi < kf
    lo = jnp.where(low_ok, lo, -FLOAT_BIG)
    cnt = jnp.where(low_ok, cnt, n_vis)
    hi = jnp.where(high_ok, hi, FLOAT_BIG)
    cnt_hi = jnp.where(high_ok, cnt_hi, 0.0)

    def bis_step(c):
        lo, hi, cnt, cnt_hi, stalled = c
        mid = 0.5 * lo + 0.5 * hi
        cm = count_where(lambda x: x >= mid[0:1])
        active = is_open(cnt, stalled)
        noprog = jnp.logical_or(mid <= lo, mid >= hi)
        move = jnp.logical_and(active, jnp.logical_not(noprog))
        up = jnp.logical_and(move, cm >= kf)
        down = jnp.logical_and(move, cm < kf)
        return (jnp.where(up, mid, lo), jnp.where(down, mid, hi), jnp.where(up, cm, cnt),
                jnp.where(down, cm, cnt_hi), jnp.where(jnp.logical_and(active, noprog), 1.0, stalled))

    def bisect(state, max_iters):
        def cond(c):
            return jnp.logical_and(c[5] < max_iters, any_lane(is_open(c[2], c[4])))

        def body(c):
            state = c[:5]
            for _ in range(BISECT_UNROLL):
                state = bis_step(state)
            return state + (c[5] + BISECT_UNROLL,)

        return lax.while_loop(cond, body, state + (jnp.int32(0),))[:5]

    def max_below(t):
        def body(j, c):
            x = tile_rows(j)
            return jnp.maximum(c, jnp.max(jnp.where(x < t, x, -jnp.inf), axis=0))
        c = lax.fori_loop(0, n_tiles, body, jnp.full((ACCS, SUBLANES, LANES), -jnp.inf, F32))
        return rows_all(jnp.max(c, axis=0), jnp.max)

    def walk(c):
        top, left = c
        return jnp.where(left > 0.0, max_below(top), top), jnp.maximum(left - 1.0, 0.0)

    state = (lo, hi, cnt, cnt_hi, jnp.zeros((SUBLANES, LANES), F32))
    for _ in range(BISECT_COARSE_ITERS):
        state = bis_step(state)
    lo, hi, cnt, cnt_hi, stalled = state
    walked = is_open(cnt, stalled)
    top, left = walk(walk((hi, jnp.where(walked, kf - cnt_hi, 0.0))))
    arrived = jnp.logical_and(walked, left == 0.0)
    lo = jnp.where(arrived, top, lo)
    hi = jnp.where(jnp.logical_and(walked, left > 0.0), top, hi)
    cnt = count_where(lambda x: x >= lo[0:1])

    def finish(state):
        thr8, _, cnt, _, _ = bisect(state, BISECT_MAX_ITERS)
        tied = cnt > kf

        @pl.when(any_lane(tied))
        def _():
            need = kf - count_where(lambda x: x > thr8[0:1])

            def body(j, seen):
                x = isct_ref[j]
                eq = jnp.where(x == thr8[0:1], 1.0, 0.0)
                inc = _cumsum_rows(eq)
                rank = inc - eq + seen[0:1]
                drop = jnp.logical_and(jnp.logical_and(tied[0:1], eq > 0.0), rank >= need[0:1])
                isct_ref[j] = jnp.where(drop, -jnp.inf, x)
                return seen + inc[KEY_TILE - 1:KEY_TILE]

            lax.fori_loop(0, n_tiles, body, jnp.zeros((SUBLANES, LANES), F32))

        return thr8

    thr8 = lax.cond(any_lane(cnt > kf), finish, lambda s: s[0], (lo, hi, cnt, cnt_hi, stalled))
    thr = thr8[0:1]

    n_far = jnp.maximum(last - 1 + odd, 0)
    next_slot = (i + 1) % 2

    zero_l = jnp.zeros((ATTN_HEADS, SUBLANES, LANES), F32)

    ones_rows = jnp.ones((SUM_ROWS, KEY_TILE), BF16)

    def sweep_tile(j, c, near, has_next, stage):
        keys = kv_ref[0, key_rows(j), :]
        values_t = jnp.concatenate([kvt_ref[0, j], ones_rows], axis=0)
        keep = jnp.where(isct_ref[j] >= thr, 1.0, 0.0).astype(BF16)
        table = jnp.where(j == last, odd, 2)
        for p in range(n_pairs):
            s = jnp.dot(keys, qt[:, pair_cols(p)], preferred_element_type=F32)
            for hh in range(2):
                h = 2 * p + hh
                sbuf_ref[stage, h] = half(s, hh) + tb_ref[table, h] if near else half(s, hh)

        def pv_pair(p):
            probs = [jnp.exp2(sbuf_ref[stage, 2 * p + hh]).astype(BF16) * keep for hh in range(2)]
            acc_ref[p] += jnp.dot(values_t, jnp.concatenate(probs, axis=1),
                                  preferred_element_type=F32)

        if has_next:
            idx_tile(j, i + 1, qitn_ref, wtn_ref, next_slot, after_pair=pv_pair)
        else:
            for p in range(n_pairs):
                pv_pair(p)
        return c

    def sweep(has_next):
        def run(c):
            def pair(jj, c):
                return sweep_tile(2 * jj + 1, sweep_tile(2 * jj, c, False, has_next, 0), False, has_next, 1)

            c = lax.fori_loop(0, n_far // 2, pair, c)
            c = lax.cond(n_far % 2 == 1, lambda c: sweep_tile(n_far - 1, c, False, has_next, 0), lambda c: c, c)

            def two(c):
                return sweep_tile(last, sweep_tile(last - 1, c, True, has_next, 0), True, has_next, 1)

            def one(c):
                return sweep_tile(last, c, True, has_next, 0)

            def one_and_extra(c):
                c = one(c)
                idx_tile(n_tiles, i + 1, qitn_ref, wtn_ref, next_slot)
                return c

            if not has_next:
                return lax.cond(n_tiles - n_far == 2, two, one, c)
            return lax.switch(jnp.where(n_tiles - n_far == 2, 0, 1 + odd), [two, one, one_and_extra], c)
        return run

    acc_ref[...] = jnp.zeros_like(acc_ref)
    lax.cond(i + 1 < n_blocks, sweep(True), sweep(False), jnp.int32(0))
    l = jnp.stack([acc_ref[h // 2, KV_RANK:KV_RANK + 1, (h % 2) * Q_BLOCK:(h % 2 + 1) * Q_BLOCK]
                   for h in range(ATTN_HEADS)])

    def p1_tile(j, m, near):
        keys = kv_ref[0, key_rows(j), :]
        neg = jnp.where(isct_ref[j] >= thr, 0.0, -jnp.inf)
        table = jnp.where(j == last, odd, 2)
        out = []
        for p in range(n_pairs):
            s = jnp.dot(keys, qt[:, pair_cols(p)], preferred_element_type=F32)
            for hh in range(2):
                h = 2 * p + hh
                sh = half(s, hh) + neg
                if near:
                    sh = sh + tb_ref[table, h]
                sbuf_ref[j, h] = sh
                out.append(jnp.maximum(m[h], fold_rows(sh, jnp.max)))
        return jnp.stack(out)

    def exact_softmax(_):
        m = jnp.full((ATTN_HEADS, SUBLANES, LANES), -jnp.inf, F32)
        m = tile_loop(n_far, lambda j, m: p1_tile(j, m, False), m)
        m = lax.fori_loop(n_far, n_tiles, lambda j, m: p1_tile(j, m, True), m)
        m = jnp.max(m, axis=1, keepdims=True)
        acc_ref[...] = jnp.zeros_like(acc_ref)

        def p2_body(j, l):
            values_t = kvt_ref[0, j]
            out = []
            for p in range(n_pairs):
                probs = []
                for hh in range(2):
                    h = 2 * p + hh
                    e = jnp.exp2(sbuf_ref[j, h] - m[h])
                    out.append(l[h] + fold_rows(e, jnp.sum))
                    probs.append(e.astype(BF16))
                acc_ref[p, 0:KV_RANK] += jnp.dot(values_t, jnp.concatenate(probs, axis=1),
                                                 preferred_element_type=F32)
            return jnp.stack(out)

        return jnp.sum(tile_loop(n_tiles, p2_body, zero_l), axis=1, keepdims=True)

    in_range = jnp.logical_and(l > SUM_MIN, l < SUM_MAX)
    out_of_range = jnp.max(jnp.where(in_range, 0.0, 1.0)) > 0.0
    l = lax.cond(out_of_range, exact_softmax, lambda l: l, l)

    outs = [(acc_ref[h // 2, 0:KV_RANK, (h % 2) * Q_BLOCK:(h % 2 + 1) * Q_BLOCK] / l[h]).T.astype(BF16)
            for h in range(ATTN_HEADS)]
    for p in range(ATTN_HEADS // 2):
        pair = jnp.concatenate(outs[2 * p:2 * p + 2], axis=1)
        o_ref[0, :, p * LANES:(p + 1) * LANES] = jnp.dot(
            pair, wuv_ref[p], preferred_element_type=F32).astype(BF16)


def _pack_w_uv(w_uv):
    eye = jnp.eye(2, dtype=w_uv.dtype)
    w = w_uv.reshape(ATTN_HEADS // 2, 2, KV_RANK, 1, HEAD_DIM) * eye[None, :, None, :, None]
    return w.reshape(ATTN_HEADS // 2, 2 * KV_RANK, 2 * HEAD_DIM).astype(BF16)


def _dsa(qt, qit, wt, ki, kv, kvt, tables, wuv):
    bsz, seq, _ = kv.shape
    nkt = seq // KEY_TILE
    k_sel = min(TOPK_MAX, seq // 4)
    nb = seq // Q_BLOCK
    slab = (1, 1, KV_RANK, ATTN_HEADS * Q_BLOCK)
    islab = (1, 1, IDX_DIM, IDX_HEADS * Q_BLOCK)
    wslab = (1, 1, IDX_HEADS, Q_BLOCK)

    def first(b, i):
        return (b, 0, 0, 0)

    def following(b, i):
        return (b, jnp.minimum(i + 1, nb - 1), 0, 0)

    seq_spec = pl.BlockSpec((1, seq, LANES), lambda b, i: (b, 0, 0))
    n_vis = (np.arange(nb)[:, None] * Q_BLOCK + np.where(np.arange(Q_BLOCK) < CHUNK, CHUNK, 2 * CHUNK)[None, :])
    quantile = np.array([[statistics.NormalDist().inv_cdf(1.0 - k_sel / n) if n > k_sel else 0.0 for n in r]
                         for r in n_vis], np.float32)
    zq = jnp.asarray(np.broadcast_to(quantile[:, None, :], (nb, SUBLANES, Q_BLOCK)))
    return pl.pallas_call(
        functools.partial(_dsa_kernel, k_sel=k_sel, n_blocks=nb),
        grid=(bsz, nb),
        in_specs=[pl.BlockSpec(slab, lambda b, i: (b, i, 0, 0)),
                  pl.BlockSpec((1, SUBLANES, Q_BLOCK), lambda b, i: (i, 0, 0)),
                  pl.BlockSpec(islab, first), pl.BlockSpec(wslab, first),
                  pl.BlockSpec(islab, following), pl.BlockSpec(wslab, following),
                  seq_spec, seq_spec,
                  pl.BlockSpec((1, nkt, LANES, KEY_TILE), lambda b, i: (b, 0, 0, 0)),
                  _const_spec((3, ATTN_HEADS, KEY_TILE, Q_BLOCK)),
                  _const_spec((ATTN_HEADS // 2, 2 * KV_RANK, 2 * HEAD_DIM))],
        out_specs=pl.BlockSpec((1, Q_BLOCK, ATTN_HEADS * HEAD_DIM), lambda b, i: (b, i, 0)),
        out_shape=jax.ShapeDtypeStruct((bsz, seq, ATTN_HEADS * HEAD_DIM), BF16),
        scratch_shapes=[pltpu.VMEM((2, nkt, KEY_TILE, Q_BLOCK), F32),
                        pltpu.VMEM((2, 2, SUBLANES, LANES), F32),
                        pltpu.VMEM((nkt, ATTN_HEADS, KEY_TILE, Q_BLOCK), F32),
                        pltpu.VMEM((ATTN_HEADS // 2, KV_RANK + SUM_ROWS, 2 * Q_BLOCK), F32)],
        compiler_params=_cparams(("parallel", "arbitrary")),
        name="dsa",
    )(qt, zq, qit, wt, qit, wt, ki, kv, kvt, tables, wuv)


def _pack3(v):
    lane = lax.broadcasted_iota(jnp.int32, v.shape, 1)
    v = jnp.where(lane < SSD_HEADS, v, 0.0)
    hi = v.astype(BF16).astype(F32)
    r = v - hi
    mid = r.astype(BF16).astype(F32)
    lo = r - mid
    return (hi + pltpu.roll(mid, SSD_HEADS, axis=1) + pltpu.roll(lo, 2 * SSD_HEADS, axis=1)).astype(BF16)


def _cumsum_rows(x):
    n = x.shape[0]
    r = lax.broadcasted_iota(jnp.int32, x.shape, 0)
    s = 1
    while s < n:
        x = x + jnp.where(r >= s, pltpu.roll(x, s, axis=0), 0.0)
        s *= 2
    return x


def _shift_rows(x, s):
    r = pltpu.roll(x, s, axis=1)
    prev = jnp.concatenate([r[-1:], r[:-1]], axis=0)
    sub = lax.broadcasted_iota(jnp.int32, x.shape, 1)
    return jnp.where(sub >= s, r, prev)


def _ssd_kernel(z_ref, xbc_ref, dt_ref, cw_ref, cb_ref, dtb_ref, alog_ref, dsk_ref, nw_ref, e_ref,
                y_ref, tail_ref, u_ref, g_ref, state_ref):
    nq = SSD_Q

    @pl.when(pl.program_id(1) == 0)
    def _():
        tail_ref[...] = jnp.zeros_like(tail_ref)
        state_ref[...] = jnp.zeros_like(state_ref)

    assert CONV_W == 4
    for blk in range(CONV_DIM // CONV_BLOCK):
        cols = slice(blk * CONV_BLOCK, (blk + 1) * CONV_BLOCK)
        ext = jnp.concatenate([tail_ref[:, cols], xbc_ref[0, :, cols].astype(F32)], axis=0)
        ext = ext.reshape(1 + nq // SUBLANES, SUBLANES, CONV_BLOCK)
        s1 = _shift_rows(ext, 1)
        a = cw_ref[3:4, cols] * ext + cw_ref[2:3, cols] * s1 + cb_ref[:, cols]
        b = cw_ref[1:2, cols] * ext + cw_ref[0:1, cols] * s1
        conv = (a + _shift_rows(b, 2))[1:].reshape(nq, CONV_BLOCK)
        u_ref[:, cols] = _silu_of_half(conv)
        tail_ref[:, cols] = ext[nq // SUBLANES]

    t = dt_ref[0] + dtb_ref[...]
    dt = jnp.maximum(t, 0.0) + jnp.log1p(jnp.exp(-jnp.abs(t)))
    a2 = _cumsum_rows(dt * (-jnp.exp(alog_ref[...]))) * LOG2E
    a2_t = a2.T
    dt_p = _pack3(dt)
    dec_p = _pack3(dt * jnp.exp2(a2[nq - 1:nq, :] - a2))
    expa_p = _pack3(jnp.exp2(a2))

    r = lax.broadcasted_iota(jnp.int32, (nq, nq), 0)
    c = lax.broadcasted_iota(jnp.int32, (nq, nq), 1)
    causal = r >= c
    lane = lax.broadcasted_iota(jnp.int32, (nq, LANES), 1)
    heads_per_group = SSD_HEADS // SSD_GROUPS
    gw = heads_per_group * SSD_HEADDIM
    b_col = D_INNER
    c_col = D_INNER + SSD_GROUPS * D_STATE
    ssq = jnp.zeros((nq, LANES), F32)

    for g in range(SSD_GROUPS):
        gcols = slice(g * gw, (g + 1) * gw)
        eg = e_ref[:, gcols]
        dt_e = jnp.dot(dt_p, eg, preferred_element_type=F32)
        dec_e = jnp.dot(dec_p, eg, preferred_element_type=F32)
        expa_e = jnp.dot(expa_p, eg, preferred_element_type=F32)
        xs = u_ref[:, gcols]
        xdt_b = (xs * dt_e).astype(BF16)
        xdec_b = (xs * dec_e).astype(BF16)
        cg = u_ref[:, c_col + g * D_STATE:c_col + (g + 1) * D_STATE].astype(BF16)
        bgt = u_ref[:, b_col + g * D_STATE:b_col + (g + 1) * D_STATE].T.astype(BF16)
        cb = jnp.dot(cg, bgt, preferred_element_type=F32)
        prev = state_ref[g]
        y_off = jnp.dot(cg, prev.astype(BF16), preferred_element_type=F32) * expa_e
        pairs = []
        for pp in range(heads_per_group // 2):
            xp = xdt_b[:, pp * LANES:(pp + 1) * LANES]
            yh = []
            for hh in range(2):
                h = heads_per_group * g + 2 * pp + hh
                seg = a2[:, h:h + 1] - a2_t[h:h + 1, :]
                m = (cb * jnp.where(causal, jnp.exp2(seg), 0.0)).astype(BF16)
                yh.append(jnp.dot(m, xp, preferred_element_type=F32))
            pairs.append(jnp.where(lane < SSD_HEADDIM, yh[0], yh[1]))
        y = jnp.concatenate(pairs, axis=1) + y_off + dsk_ref[:, gcols] * xs
        gated = y * _silu_of_half(z_ref[0, :, gcols].astype(F32))
        g_ref[:, gcols] = gated
        sq = gated * gated
        ssq = ssq + sq[:, :LANES] + sq[:, LANES:]
        new = jnp.dot(bgt, xdec_b, preferred_element_type=F32)
        state_ref[g] = prev * expa_e[nq - 1:nq, :] + new
    scale = lax.rsqrt(jnp.sum(ssq, axis=-1, keepdims=True) * (1.0 / D_INNER) + EPS)
    y_ref[0] = (g_ref[...] * scale * nw_ref[...]).astype(BF16)


def _ssd(head, wide, conv_w, conv_b, dt_bias, a_log, d_skip, ssd_norm):
    bsz, seq, _ = head.shape
    nq = SSD_Q

    def pad_heads(v):
        return jnp.concatenate([v, jnp.zeros((LANES - SSD_HEADS,), F32)]).reshape(1, LANES)

    sel = np.concatenate([np.eye(SSD_HEADS)] * 3 + [np.zeros((LANES - 3 * SSD_HEADS, SSD_HEADS))], axis=0)
    e = jnp.asarray(np.kron(sel, np.ones((1, SSD_HEADDIM))), BF16)
    return pl.pallas_call(
        _ssd_kernel,
        grid=(bsz, seq // nq),
        in_specs=[pl.BlockSpec((1, nq, D_INNER), lambda b, i: (b, i, WIDE_Z // D_INNER)),
                  pl.BlockSpec((1, nq, CONV_DIM), lambda b, i: (b, i, WIDE_XBC // CONV_DIM)),
                  pl.BlockSpec((1, nq, LANES), lambda b, i: (b, i, HEAD_DT // LANES)),
                  _const_spec((CONV_W, CONV_DIM)), _const_spec((1, CONV_DIM)),
                  _const_spec((1, LANES)), _const_spec((1, LANES)),
                  _const_spec((1, D_INNER)), _const_spec((1, D_INNER)),
                  _const_spec((LANES, D_INNER))],
        out_specs=pl.BlockSpec((1, nq, D_INNER), lambda b, i: (b, i, 0)),
        out_shape=jax.ShapeDtypeStruct((bsz, seq, D_INNER), BF16),
        scratch_shapes=[pltpu.VMEM((SUBLANES, CONV_DIM), F32),
                        pltpu.VMEM((nq, CONV_DIM), F32),
                        pltpu.VMEM((nq, D_INNER), F32),
                        pltpu.VMEM((SSD_GROUPS, D_STATE, 4 * SSD_HEADDIM), F32)],
        compiler_params=_cparams(("parallel", "arbitrary")),
        name="ssd",
    )(wide, wide, head, 0.5 * conv_w, 0.5 * conv_b.reshape(1, CONV_DIM), pad_heads(dt_bias), pad_heads(a_log),
      jnp.repeat(d_skip, SSD_HEADDIM).reshape(1, D_INNER), ssd_norm.reshape(1, D_INNER), e)


def _mix_kernel(ao_ref, sy_ref, ga_ref, gb_ref, x_ref, mod_ref, nw_ref, woa_ref, wos_ref, wout_ref, o_ref):
    ya = jnp.dot(ao_ref[0], woa_ref[...], preferred_element_type=F32)
    yb = jnp.dot(sy_ref[0], wos_ref[...], preferred_element_type=F32)
    mix = jax.nn.sigmoid(ga_ref[0].astype(F32)) * ya + jax.nn.sigmoid(gb_ref[0].astype(F32)) * yb
    m2 = jnp.dot(mix.astype(BF16), wout_ref[...], preferred_element_type=F32)
    o_ref[0] = x_ref[0] + mod_ref[0][2:3] * _rms(m2, nw_ref[...])


def _mix(attn_o, ssd_y, wide, x, mod3, post_norm, w_o_attn, w_o_ssd, w_out):
    bsz, seq, _ = x.shape
    tm = min(seq, 512)

    def rows(width, col_block=0):
        return pl.BlockSpec((1, tm, width), lambda b, i: (b, i, col_block))

    return pl.pallas_call(
        _mix_kernel,
        grid=(bsz, seq // tm),
        in_specs=[rows(D_MODEL), rows(D_INNER), rows(D_MODEL, WIDE_GA // D_MODEL), rows(D_MODEL, WIDE_GB // D_MODEL),
                  rows(D_MODEL), pl.BlockSpec((1, 6, D_MODEL), lambda b, i: (b, 0, 0)),
                  _const_spec((1, D_MODEL)), _const_spec((D_MODEL, D_MODEL)),
                  _const_spec((D_INNER, D_MODEL)), _const_spec((D_MODEL, D_MODEL))],
        out_specs=rows(D_MODEL),
        out_shape=jax.ShapeDtypeStruct((bsz, seq, D_MODEL), F32),
        compiler_params=_cparams(("parallel", "parallel")),
        name="mix",
    )(attn_o, ssd_y, wide, wide, x, mod3, post_norm.reshape(1, D_MODEL),
      w_o_attn.astype(BF16), w_o_ssd.astype(BF16), w_out.astype(BF16))


def _ffn_kernel(x_ref, mod_ref, nw1_ref, nw2_ref, wg_ref, wu_ref, wo_ref, o_ref):
    x = x_ref[0]
    m = mod_ref[0]
    h2 = (_rms(x, nw1_ref[...]) * (1.0 + m[4:5]) + m[3:4]).astype(BF16)
    ug = jnp.dot(h2, wg_ref[...], preferred_element_type=F32)
    uu = jnp.dot(h2, wu_ref[...], preferred_element_type=F32)
    f = jnp.dot((_silu(ug) * uu).astype(BF16), wo_ref[...], preferred_element_type=F32)
    o_ref[0] = x + m[5:6] * _rms(f, nw2_ref[...])


def _ffn(x, mod3, pre_norm, post_norm, w_ffn_in, w_ffn_out):
    bsz, seq, _ = x.shape
    tm = min(seq, 512)
    rows = pl.BlockSpec((1, tm, D_MODEL), lambda b, i: (b, i, 0))
    w_in = w_ffn_in.astype(BF16)

    def half(k):
        return pl.BlockSpec((D_MODEL, D_FF), lambda b, i: (0, k), pipeline_mode=pl.Buffered(1))

    return pl.pallas_call(
        _ffn_kernel,
        grid=(bsz, seq // tm),
        in_specs=[rows, pl.BlockSpec((1, 6, D_MODEL), lambda b, i: (b, 0, 0)),
                  _const_spec((1, D_MODEL)), _const_spec((1, D_MODEL)),
                  half(0), half(1), _const_spec((D_FF, D_MODEL))],
        out_specs=rows,
        out_shape=jax.ShapeDtypeStruct((bsz, seq, D_MODEL), F32),
        compiler_params=_cparams(("parallel", "parallel")),
        name="ffn",
    )(x, mod3, pre_norm.reshape(1, D_MODEL), post_norm.reshape(1, D_MODEL), w_in, w_in, w_ffn_out.astype(BF16))


def kernel(x, c, positions, ada_w, ada_b, pre_norm_mix, post_norm_mix, pre_norm_ffn, post_norm_ffn, w_in, q_norm, kv_norm, w_uq, w_uv, rel_bias, w_qidx, kidx_norm, conv_w, conv_b, dt_bias, a_log, d_skip, ssd_norm, w_o_attn, w_o_ssd, w_out, w_ffn_in, w_ffn_out):
    del positions
    bsz, seq, _ = x.shape
    assert seq % (2 * KEY_TILE) == 0 and x.shape[-1] == D_MODEL
    mod3 = _mod(c, ada_w, ada_b).reshape(bsz, 6, D_MODEL)
    head, wide = _inproj(x, mod3, pre_norm_mix, _pack_w_in(w_in))
    qt, qit, kv, kvt, ki, wt = _prep(head, q_norm, kv_norm, kidx_norm, w_uq, w_qidx)
    attn_o = _dsa(qt, qit, wt, ki, kv, kvt, _bias_tables(rel_bias), _pack_w_uv(w_uv))
    ssd_y = _ssd(head, wide, conv_w, conv_b, dt_bias, a_log, d_skip, ssd_norm)
    x1 = _mix(attn_o, ssd_y, wide, x, mod3, post_norm_mix, w_o_attn, w_o_ssd, w_out)
    return _ffn(x1, mod3, pre_norm_ffn, post_norm_ffn, w_ffn_in, w_ffn_out)
```

```python
import functools
import math
import statistics

import numpy as np
import jax
import jax.numpy as jnp
from jax import lax
from jax.experimental import pallas as pl
from jax.experimental.pallas import tpu as pltpu

F32 = jnp.float32
BF16 = jnp.bfloat16

D_MODEL = 1024
CHUNK = 64
Q_BLOCK = 128
EPS = 1e-6
ATTN_HEADS = 16
HEAD_DIM = 64
Q_RANK = 256
KV_RANK = 128
IDX_HEADS = 16
IDX_DIM = 64
TOPK_MAX = 256
NUM_BUCKETS = 32
MAX_DISTANCE = 128
D_INNER = 2 * D_MODEL
SSD_HEADDIM = 64
SSD_HEADS = D_INNER // SSD_HEADDIM
SSD_GROUPS = 8
D_STATE = 128
CONV_W = 4
CONV_DIM = D_INNER + 2 * SSD_GROUPS * D_STATE
D_FF = -(-8 * D_MODEL // (3 * 256)) * 256

LANES = 128
SUBLANES = 8
KEY_TILE = 256
SSD_Q = 128
CONV_BLOCK = 512
VMEM_LIMIT = 56 * 1024 * 1024
BISECT_MAX_ITERS = 320
BISECT_UNROLL = 4
BISECT_COARSE_ITERS = 10
BRACKET_HALF_WIDTH = 0.3
FLOAT_BIG = 3.0e38
LOG2E = math.log2(math.e)
SUM_MIN = 2.0 ** -80
SUM_MAX = 2.0 ** 40
SUM_ROWS = 16

HEAD_COLS = 768
HEAD_KV, HEAD_KIDX, HEAD_W, HEAD_DT = 256, 384, 512, 640
WIDE_XBC, WIDE_Z, WIDE_GA, WIDE_GB = 0, 4096, 6144, 7168
WIDE_COLS = 8192
INPROJ_TN = 1024


def _cparams(sem):
    return pltpu.CompilerParams(dimension_semantics=sem, vmem_limit_bytes=VMEM_LIMIT)


def _const_spec(shape):
    nd = len(shape)
    return pl.BlockSpec(shape, lambda *_: (0,) * nd, pipeline_mode=pl.Buffered(1))


def _rms(x, w, n=None):
    n = x.shape[-1] if n is None else n
    return x * lax.rsqrt(jnp.sum(x * x, axis=-1, keepdims=True) * (1.0 / n) + EPS) * w


def _silu_of_half(h):
    return h + h * jnp.tanh(h)


def _silu(x):
    return _silu_of_half(0.5 * x)


def _mod_kernel(c_ref, w_ref, b_ref, o_ref):
    c = c_ref[...]
    s = _silu(c).astype(BF16)
    o_ref[...] = jnp.dot(s, w_ref[...].astype(BF16), preferred_element_type=F32) + b_ref[...]


def _mod(c, ada_w, ada_b):
    bsz = c.shape[0]
    return pl.pallas_call(
        _mod_kernel,
        grid=(6,),
        in_specs=[pl.BlockSpec((bsz, D_MODEL), lambda j: (0, 0)),
                  pl.BlockSpec((D_MODEL, D_MODEL), lambda j: (0, j)),
                  pl.BlockSpec((1, D_MODEL), lambda j: (0, j))],
        out_specs=pl.BlockSpec((bsz, D_MODEL), lambda j: (0, j)),
        out_shape=jax.ShapeDtypeStruct((bsz, 6 * D_MODEL), F32),
        compiler_params=_cparams(("parallel",)),
        name="mod",
    )(c, ada_w, ada_b.reshape(1, 6 * D_MODEL))


def _t5_bucket_np(rel):
    half = NUM_BUCKETS // 2
    max_exact = half // 2
    side = np.where(rel > 0, half, 0)
    n = np.abs(rel)
    large = max_exact + (np.log(np.maximum(n, max_exact).astype(np.float64) / max_exact)
                         / math.log(MAX_DISTANCE / max_exact) * (half - max_exact)).astype(np.int64)
    large = np.minimum(large, half - 1)
    return (side + np.where(n < max_exact, n, large)).astype(np.int32)


def _bias_kernel(idx_ref, rb_ref, o_ref):
    h = pl.program_id(0)
    idx = idx_ref[...]
    far = rb_ref[h, NUM_BUCKETS // 2 - 1]
    acc = jnp.zeros(idx.shape, F32)
    for b in range(NUM_BUCKETS):
        acc = jnp.where(idx == b, (rb_ref[h, b] - far) * LOG2E, acc)
    o_ref[0] = acc


def _bias_tables(rel_bias):
    kk = np.arange(2 * KEY_TILE)[:, None]
    ql = np.arange(Q_BLOCK)[None, :]
    idx = jnp.asarray(_t5_bucket_np(kk - KEY_TILE - ql))
    t = pl.pallas_call(
        _bias_kernel,
        grid=(ATTN_HEADS,),
        in_specs=[pl.BlockSpec((2 * KEY_TILE, Q_BLOCK), lambda h: (0, 0)),
                  pl.BlockSpec(memory_space=pltpu.SMEM)],
        out_specs=pl.BlockSpec((1, 2 * KEY_TILE, Q_BLOCK), lambda h: (h, 0, 0)),
        out_shape=jax.ShapeDtypeStruct((ATTN_HEADS, 2 * KEY_TILE, Q_BLOCK), F32),
        compiler_params=_cparams(("arbitrary",)),
        name="bias",
    )(idx, rel_bias.T)
    return jnp.stack([t[:, 256:512], t[:, 128:384], t[:, 0:256]])


def _inproj_kernel(x_ref, mod_ref, nw_ref, w_ref, head_ref, wide_ref):
    m = mod_ref[0]
    hn = (_rms(x_ref[0], nw_ref[...]) * (1.0 + m[1:2]) + m[0:1]).astype(BF16)
    head_ref[0] = _dot_nt(hn, w_ref[0:HEAD_COLS, :])
    for c in range(WIDE_COLS // INPROJ_TN):
        rows = slice(HEAD_COLS + c * INPROJ_TN, HEAD_COLS + (c + 1) * INPROJ_TN)
        wide_ref[0, :, c * INPROJ_TN:(c + 1) * INPROJ_TN] = _dot_nt(hn, w_ref[rows, :]).astype(BF16)


def _pack_w_in(w_in):
    sizes = [Q_RANK, KV_RANK, IDX_DIM, IDX_HEADS, D_INNER, CONV_DIM, SSD_HEADS, D_MODEL, D_MODEL]
    offs = np.cumsum([0] + sizes)
    wt = w_in.T
    q, kv, ki, wi, z, xbc, dt, ga, gb = [wt[offs[i]:offs[i + 1]] for i in range(9)]

    def zr(n):
        return jnp.zeros((n, D_MODEL), w_in.dtype)

    return jnp.concatenate([q, kv, ki, zr(LANES - IDX_DIM), wi, zr(LANES - IDX_HEADS),
                            dt, zr(LANES - SSD_HEADS), xbc, 0.5 * z, ga, gb], axis=0).astype(BF16)


def _dot_nt(a, b):
    return lax.dot_general(a, b, (((1,), (1,)), ((), ())), preferred_element_type=F32)


def _inproj(x, mod3, pre_norm, w_packed):
    bsz, seq, _ = x.shape
    tm = min(seq, 512)
    return pl.pallas_call(
        _inproj_kernel,
        grid=(bsz, seq // tm),
        in_specs=[pl.BlockSpec((1, tm, D_MODEL), lambda b, i: (b, i, 0)),
                  pl.BlockSpec((1, 6, D_MODEL), lambda b, i: (b, 0, 0)),
                  _const_spec((1, D_MODEL)),
                  _const_spec((HEAD_COLS + WIDE_COLS, D_MODEL))],
        out_specs=[pl.BlockSpec((1, tm, HEAD_COLS), lambda b, i: (b, i, 0)),
                   pl.BlockSpec((1, tm, WIDE_COLS), lambda b, i: (b, i, 0))],
        out_shape=[jax.ShapeDtypeStruct((bsz, seq, HEAD_COLS), F32),
                   jax.ShapeDtypeStruct((bsz, seq, WIDE_COLS), BF16)],
        compiler_params=_cparams(("parallel", "parallel")),
        name="inproj",
    )(x, mod3, pre_norm.reshape(1, D_MODEL), w_packed)


def _prep_kernel(p_ref, qn_ref, kvn_ref, kin_ref, wuqt_ref, wqit_ref,
                 qt_ref, qit_ref, kv_ref, kvt_ref, ki_ref, wt_ref, *, tc):
    p = p_ref[0]
    qnt = _rms(p[:, :Q_RANK], qn_ref[...]).T.astype(BF16)
    qt = (jnp.dot(wuqt_ref[...], qnt, preferred_element_type=F32) * (KV_RANK ** -0.5 * LOG2E)).astype(BF16)
    qit = jnp.dot(wqit_ref[...], qnt, preferred_element_type=F32).astype(BF16)
    wt = (p[:, HEAD_W:HEAD_W + LANES] * (IDX_HEADS ** -0.5 * IDX_DIM ** -0.5)).T
    for blk in range(tc // Q_BLOCK):
        cols = slice(blk * Q_BLOCK, (blk + 1) * Q_BLOCK)
        for h in range(ATTN_HEADS):
            qt_ref[0, blk, :, h * LANES:(h + 1) * LANES] = qt[h * LANES:(h + 1) * LANES, cols]
            qit_ref[0, blk, :, h * LANES:(h + 1) * LANES] = qit[h * IDX_DIM:(h + 1) * IDX_DIM, cols]
        wt_ref[0, blk] = wt[0:IDX_HEADS, cols]
    kv = _rms(p[:, HEAD_KV:HEAD_KV + KV_RANK], kvn_ref[...])
    kv_ref[0] = kv.astype(BF16)
    kvt = kv.T
    for c in range(tc // KEY_TILE):
        kvt_ref[0, c] = kvt[:, c * KEY_TILE:(c + 1) * KEY_TILE].astype(BF16)
    ki_ref[0] = _rms(p[:, HEAD_KIDX:HEAD_KIDX + LANES], kin_ref[...], n=IDX_DIM).astype(BF16)


def _prep(proj, q_norm, kv_norm, kidx_norm, w_uq, w_qidx):
    bsz, seq, _ = proj.shape
    tc = min(seq, 1024)
    kin = jnp.concatenate([kidx_norm, jnp.zeros((LANES - IDX_DIM,), F32)]).reshape(1, LANES)
    nb = seq // Q_BLOCK

    def slab(rows):
        return (jax.ShapeDtypeStruct((bsz, nb, rows, ATTN_HEADS * Q_BLOCK), BF16),
                pl.BlockSpec((1, tc // Q_BLOCK, rows, ATTN_HEADS * Q_BLOCK), lambda b, i: (b, i, 0, 0)))

    (q_slab, q_spec), (qi_slab, qi_spec) = slab(KV_RANK), slab(IDX_DIM)
    row_spec = pl.BlockSpec((1, tc, LANES), lambda b, i: (b, i, 0))
    return pl.pallas_call(
        functools.partial(_prep_kernel, tc=tc),
        grid=(bsz, seq // tc),
        in_specs=[pl.BlockSpec((1, tc, HEAD_COLS), lambda b, i: (b, i, 0)),
                  _const_spec((1, Q_RANK)), _const_spec((1, KV_RANK)), _const_spec((1, LANES)),
                  _const_spec((ATTN_HEADS * KV_RANK, Q_RANK)), _const_spec((IDX_HEADS * IDX_DIM, Q_RANK))],
        out_specs=[q_spec, qi_spec, row_spec,
                   pl.BlockSpec((1, tc // KEY_TILE, LANES, KEY_TILE), lambda b, i: (b, i, 0, 0)),
                   row_spec,
                   pl.BlockSpec((1, tc // Q_BLOCK, IDX_HEADS, Q_BLOCK), lambda b, i: (b, i, 0, 0))],
        out_shape=[q_slab, qi_slab, jax.ShapeDtypeStruct((bsz, seq, LANES), BF16),
                   jax.ShapeDtypeStruct((bsz, seq // KEY_TILE, LANES, KEY_TILE), BF16),
                   jax.ShapeDtypeStruct((bsz, seq, LANES), BF16),
                   jax.ShapeDtypeStruct((bsz, nb, IDX_HEADS, Q_BLOCK), F32)],
        compiler_params=_cparams(("parallel", "parallel")),
        name="prep",
    )(proj, q_norm.reshape(1, Q_RANK), kv_norm.reshape(1, KV_RANK), kin, w_uq.T.astype(BF16),
      w_qidx.T.astype(BF16))


def _dsa_kernel(qt_ref, zq_ref, qit0_ref, wt0_ref, qitn_ref, wtn_ref, ki_ref, kv_ref, kvt_ref, tb_ref,
                wuv_ref, o_ref, iscbuf_ref, mom_ref, sbuf_ref, acc_ref, *, k_sel, n_blocks):
    i = pl.program_id(1)
    last = i // 2
    odd = i % 2
    n_tiles = last + 1
    isct_ref = iscbuf_ref.at[i % 2]
    row = lax.broadcasted_iota(jnp.int32, (KEY_TILE, Q_BLOCK), 0)
    col = lax.broadcasted_iota(jnp.int32, (KEY_TILE, Q_BLOCK), 1)

    def key_rows(j):
        return pl.ds(pl.multiple_of(j * KEY_TILE, KEY_TILE), KEY_TILE)

    n_pairs = ATTN_HEADS // 2

    def pair_cols(p):
        return slice(p * KEY_TILE, (p + 1) * KEY_TILE)

    def half(x, hh):
        return x[:, hh * Q_BLOCK:(hh + 1) * Q_BLOCK]

    ACCS = 4

    def fold_rows(x, op):
        x = x.reshape(KEY_TILE // (ACCS * SUBLANES), ACCS, SUBLANES, LANES)
        return op(op(x, axis=0), axis=0)

    def idx_tile(j, blk, qit_ref, wt_ref, slot, after_pair=lambda p: None):
        dst_ref = iscbuf_ref.at[slot]
        keys = ki_ref[0, key_rows(j), :]
        wt = wt_ref[0, 0]
        acc = jnp.zeros((KEY_TILE, Q_BLOCK), F32)
        pad = jnp.zeros((LANES - IDX_DIM, KEY_TILE), BF16)
        for p in range(n_pairs):
            qi = jnp.concatenate([qit_ref[0, 0, :, pair_cols(p)], pad], axis=0)
            s = jnp.dot(keys, qi, preferred_element_type=F32)
            for hh in range(2):
                h = 2 * p + hh
                acc = acc + wt[h:h + 1, :] * jnp.maximum(half(s, hh), 0.0)
            after_pair(p)
        key_limit = blk * Q_BLOCK + jnp.where(col < CHUNK, CHUNK, 2 * CHUNK)
        visible = j * KEY_TILE + row < key_limit
        dst_ref[j] = jnp.where(visible, acc, -jnp.inf)
        seen = jnp.where(visible, acc, 0.0)
        old = jnp.where(j == 0, 0.0, mom_ref[slot])
        mom_ref[slot] = old + jnp.stack([fold_rows(seen, jnp.sum), fold_rows(seen * seen, jnp.sum)])

    def tile_loop(n, body, carry):
        def pair(jj, c):
            return body(2 * jj + 1, body(2 * jj, c))
        carry = lax.fori_loop(0, n // 2, pair, carry)
        return lax.cond(n % 2 == 1, lambda c: body(n - 1, c), lambda c: c, carry)

    @pl.when(i == 0)
    def _():
        idx_tile(0, 0, qit0_ref, wt0_ref, 0)

    def rows_all(x, op):
        return jnp.broadcast_to(op(x, axis=0, keepdims=True), (SUBLANES, LANES))

    def tile_rows(j):
        return isct_ref[j].reshape(KEY_TILE // (ACCS * SUBLANES), ACCS, SUBLANES, LANES)

    def count_where(pred):
        def body(j, c):
            x = tile_rows(j)
            for r in range(x.shape[0]):
                c = c + jnp.where(pred(x[r]), 1.0, 0.0)
            return c
        c = tile_loop(n_tiles, body, jnp.zeros((ACCS, SUBLANES, LANES), F32))
        return rows_all(jnp.sum(c, axis=0), jnp.sum)

    kf = float(k_sel)

    def is_open(cnt, stalled):
        return jnp.logical_and(cnt > kf, stalled == 0.0)

    def any_lane(mask):
        return jnp.max(jnp.where(mask, 1.0, 0.0)) > 0.0

    lane8 = lax.broadcasted_iota(jnp.int32, (SUBLANES, LANES), 1)
    n_vis = (i * Q_BLOCK + jnp.where(lane8 < CHUNK, CHUNK, 2 * CHUNK)).astype(F32)
    mom = mom_ref[i % 2]
    mean = rows_all(mom[0], jnp.sum) / n_vis
    std = jnp.sqrt(jnp.maximum(rows_all(mom[1], jnp.sum) / n_vis - mean * mean, 0.0))
    z = zq_ref[0]
    takes_all = n_vis <= kf
    lo = jnp.where(takes_all, -FLOAT_BIG, mean + (z - BRACKET_HALF_WIDTH) * std)
    hi = jnp.where(takes_all, FLOAT_BIG, mean + (z + BRACKET_HALF_WIDTH) * std)
    cnt = count_where(lambda x: x >= lo[0:1])
    cnt_hi = count_where(lambda x: x >= hi[0:1])
    low_ok = cnt >= kf
    high_ok = cnt_hi < kf
    lo = jnp.where(low_ok, lo, -FLOAT_BIG)
    cnt = jnp.where(low_ok, cnt, n_vis)
    hi = jnp.where(high_ok, hi, FLOAT_BIG)
    cnt_hi = jnp.where(high_ok, cnt_hi, 0.0)

    def bis_step(c):
        lo, hi, cnt, cnt_hi, stalled = c
        mid = 0.5 * lo + 0.5 * hi
        cm = count_where(lambda x: x >= mid[0:1])
        active = is_open(cnt, stalled)
        noprog = jnp.logical_or(mid <= lo, mid >= hi)
        move = jnp.logical_and(active, jnp.logical_not(noprog))
        up = jnp.logical_and(move, cm >= kf)
        down = jnp.logical_and(move, cm < kf)
        return (jnp.where(up, mid, lo), jnp.where(down, mid, hi), jnp.where(up, cm, cnt),
                jnp.where(down, cm, cnt_hi), jnp.where(jnp.logical_and(active, noprog), 1.0, stalled))

    def bisect(state, max_iters):
        def cond(c):
            return jnp.logical_and(c[5] < max_iters, any_lane(is_open(c[2], c[4])))

        def body(c):
            state = c[:5]
            for _ in range(BISECT_UNROLL):
                state = bis_step(state)
            return state + (c[5] + BISECT_UNROLL,)

        return lax.while_loop(cond, body, state + (jnp.int32(0),))[:5]

    def max_below(t):
        def body(j, c):
            x = tile_rows(j)
            return jnp.maximum(c, jnp.max(jnp.where(x < t, x, -jnp.inf), axis=0))
        c = lax.fori_loop(0, n_tiles, body, jnp.full((ACCS, SUBLANES, LANES), -jnp.inf, F32))
        return rows_all(jnp.max(c, axis=0), jnp.max)

    def walk(c):
        top, left = c
        return jnp.where(left > 0.0, max_below(top), top), jnp.maximum(left - 1.0, 0.0)

    state = (lo, hi, cnt, cnt_hi, jnp.zeros((SUBLANES, LANES), F32))
    for _ in range(BISECT_COARSE_ITERS):
        state = bis_step(state)
    lo, hi, cnt, cnt_hi, stalled = state
    walked = is_open(cnt, stalled)
    top, left = walk(walk((hi, jnp.where(walked, kf - cnt_hi, 0.0))))
    arrived = jnp.logical_and(walked, left == 0.0)
    lo = jnp.where(arrived, top, lo)
    hi = jnp.where(jnp.logical_and(walked, left > 0.0), top, hi)
    cnt = count_where(lambda x: x >= lo[0:1])

    def finish(state):
        thr8, _, cnt, _, _ = bisect(state, BISECT_MAX_ITERS)
        tied = cnt > kf

        @pl.when(any_lane(tied))
        def _():
            need = kf - count_where(lambda x: x > thr8[0:1])

            def body(j, seen):
                x = isct_ref[j]
                eq = jnp.where(x == thr8[0:1], 1.0, 0.0)
                inc = _cumsum_rows(eq)
                rank = inc - eq + seen[0:1]
                drop = jnp.logical_and(jnp.logical_and(tied[0:1], eq > 0.0), rank >= need[0:1])
                isct_ref[j] = jnp.where(drop, -jnp.inf, x)
                return seen + inc[KEY_TILE - 1:KEY_TILE]

            lax.fori_loop(0, n_tiles, body, jnp.zeros((SUBLANES, LANES), F32))

        return thr8

    thr8 = lax.cond(any_lane(cnt > kf), finish, lambda s: s[0], (lo, hi, cnt, cnt_hi, stalled))
    thr = thr8[0:1]

    n_far = jnp.maximum(last - 1 + odd, 0)
    next_slot = (i + 1) % 2

    zero_l = jnp.zeros((ATTN_HEADS, SUBLANES, LANES), F32)

    ones_rows = jnp.ones((SUM_ROWS, KEY_TILE), BF16)

    def sweep_tile(j, c, near, has_next, stage):
        keys = kv_ref[0, key_rows(j), :]
        values_t = jnp.concatenate([kvt_ref[0, j], ones_rows], axis=0)
        keep = jnp.where(isct_ref[j] >= thr, 1.0, 0.0).astype(BF16)
        table = jnp.where(j == last, odd, 2)
        for p in range(n_pairs):
            s = jnp.dot(keys, qt_ref[0, 0, :, pair_cols(p)], preferred_element_type=F32)
            for hh in range(2):
                h = 2 * p + hh
                sbuf_ref[stage, h] = half(s, hh) + tb_ref[table, h] if near else half(s, hh)

        def pv_pair(p):
            probs = [jnp.exp2(sbuf_ref[stage, 2 * p + hh]).astype(BF16) * keep for hh in range(2)]
            acc_ref[p] += jnp.dot(values_t, jnp.concatenate(probs, axis=1),
                                  preferred_element_type=F32)

        if has_next:
            idx_tile(j, i + 1, qitn_ref, wtn_ref, next_slot, after_pair=pv_pair)
        else:
            for p in range(n_pairs):
                pv_pair(p)
        return c

    def sweep(has_next):
        def run(c):
            def pair(jj, c):
                return sweep_tile(2 * jj + 1, sweep_tile(2 * jj, c, False, has_next, 0), False, has_next, 1)

            c = lax.fori_loop(0, n_far // 2, pair, c)
            c = lax.cond(n_far % 2 == 1, lambda c: sweep_tile(n_far - 1, c, False, has_next, 0), lambda c: c, c)

            def two(c):
                return sweep_tile(last, sweep_tile(last - 1, c, True, has_next, 0), True, has_next, 1)

            def one(c):
                return sweep_tile(last, c, True, has_next, 0)

            def one_and_extra(c):
                c = one(c)
                idx_tile(n_tiles, i + 1, qitn_ref, wtn_ref, next_slot)
                return c

            if not has_next:
                return lax.cond(n_tiles - n_far == 2, two, one, c)
            return lax.switch(jnp.where(n_tiles - n_far == 2, 0, 1 + odd), [two, one, one_and_extra], c)
        return run

    acc_ref[...] = jnp.zeros_like(acc_ref)
    lax.cond(i + 1 < n_blocks, sweep(True), sweep(False), jnp.int32(0))
    l = jnp.stack([acc_ref[h // 2, KV_RANK:KV_RANK + 1, (h % 2) * Q_BLOCK:(h % 2 + 1) * Q_BLOCK]
                   for h in range(ATTN_HEADS)])

    def p1_tile(j, m, near):
        keys = kv_ref[0, key_rows(j), :]
        neg = jnp.where(isct_ref[j] >= thr, 0.0, -jnp.inf)
        table = jnp.where(j == last, odd, 2)
        out = []
        for p in range(n_pairs):
            s = jnp.dot(keys, qt_ref[0, 0, :, pair_cols(p)], preferred_element_type=F32)
            for hh in range(2):
                h = 2 * p + hh
                sh = half(s, hh) + neg
                if near:
                    sh = sh + tb_ref[table, h]
                sbuf_ref[j, h] = sh
                out.append(jnp.maximum(m[h], fold_rows(sh, jnp.max)))
        return jnp.stack(out)

    def exact_softmax(_):
        m = jnp.full((ATTN_HEADS, SUBLANES, LANES), -jnp.inf, F32)
        m = tile_loop(n_far, lambda j, m: p1_tile(j, m, False), m)
        m = lax.fori_loop(n_far, n_tiles, lambda j, m: p1_tile(j, m, True), m)
        m = jnp.max(m, axis=1, keepdims=True)
        acc_ref[...] = jnp.zeros_like(acc_ref)

        def p2_body(j, l):
            values_t = kvt_ref[0, j]
            out = []
            for p in range(n_pairs):
                probs = []
                for hh in range(2):
                    h = 2 * p + hh
                    e = jnp.exp2(sbuf_ref[j, h] - m[h])
                    out.append(l[h] + fold_rows(e, jnp.sum))
                    probs.append(e.astype(BF16))
                acc_ref[p, 0:KV_RANK] += jnp.dot(values_t, jnp.concatenate(probs, axis=1),
                                                 preferred_element_type=F32)
            return jnp.stack(out)

        return jnp.sum(tile_loop(n_tiles, p2_body, zero_l), axis=1, keepdims=True)

    in_range = jnp.logical_and(l > SUM_MIN, l < SUM_MAX)
    out_of_range = jnp.max(jnp.where(in_range, 0.0, 1.0)) > 0.0
    l = lax.cond(out_of_range, exact_softmax, lambda l: l, l)

    outs = [(acc_ref[h // 2, 0:KV_RANK, (h % 2) * Q_BLOCK:(h % 2 + 1) * Q_BLOCK] / l[h]).T.astype(BF16)
            for h in range(ATTN_HEADS)]
    for p in range(ATTN_HEADS // 2):
        pair = jnp.concatenate(outs[2 * p:2 * p + 2], axis=1)
        o_ref[0, :, p * LANES:(p + 1) * LANES] = jnp.dot(
            pair, wuv_ref[p], preferred_element_type=F32).astype(BF16)


def _pack_w_uv(w_uv):
    eye = jnp.eye(2, dtype=w_uv.dtype)
    w = w_uv.reshape(ATTN_HEADS // 2, 2, KV_RANK, 1, HEAD_DIM) * eye[None, :, None, :, None]
    return w.reshape(ATTN_HEADS // 2, 2 * KV_RANK, 2 * HEAD_DIM).astype(BF16)


def _dsa(qt, qit, wt, ki, kv, kvt, tables, wuv):
    bsz, seq, _ = kv.shape
    nkt = seq // KEY_TILE
    k_sel = min(TOPK_MAX, seq // 4)
    nb = seq // Q_BLOCK
    slab = (1, 1, KV_RANK, ATTN_HEADS * Q_BLOCK)
    islab = (1, 1, IDX_DIM, IDX_HEADS * Q_BLOCK)
    wslab = (1, 1, IDX_HEADS, Q_BLOCK)

    def first(b, i):
        return (b, 0, 0, 0)

    def following(b, i):
        return (b, jnp.minimum(i + 1, nb - 1), 0, 0)

    seq_spec = pl.BlockSpec((1, seq, LANES), lambda b, i: (b, 0, 0))
    n_vis = (np.arange(nb)[:, None] * Q_BLOCK + np.where(np.arange(Q_BLOCK) < CHUNK, CHUNK, 2 * CHUNK)[None, :])
    quantile = np.array([[statistics.NormalDist().inv_cdf(1.0 - k_sel / n) if n > k_sel else 0.0 for n in r]
                         for r in n_vis], np.float32)
    zq = jnp.asarray(np.broadcast_to(quantile[:, None, :], (nb, SUBLANES, Q_BLOCK)))
    return pl.pallas_call(
        functools.partial(_dsa_kernel, k_sel=k_sel, n_blocks=nb),
        grid=(bsz, nb),
        in_specs=[pl.BlockSpec(slab, lambda b, i: (b, i, 0, 0)),
                  pl.BlockSpec((1, SUBLANES, Q_BLOCK), lambda b, i: (i, 0, 0)),
                  pl.BlockSpec(islab, first), pl.BlockSpec(wslab, first),
                  pl.BlockSpec(islab, following), pl.BlockSpec(wslab, following),
                  seq_spec, seq_spec,
                  pl.BlockSpec((1, nkt, LANES, KEY_TILE), lambda b, i: (b, 0, 0, 0)),
                  _const_spec((3, ATTN_HEADS, KEY_TILE, Q_BLOCK)),
                  _const_spec((ATTN_HEADS // 2, 2 * KV_RANK, 2 * HEAD_DIM))],
        out_specs=pl.BlockSpec((1, Q_BLOCK, ATTN_HEADS * HEAD_DIM), lambda b, i: (b, i, 0)),
        out_shape=jax.ShapeDtypeStruct((bsz, seq, ATTN_HEADS * HEAD_DIM), BF16),
        scratch_shapes=[pltpu.VMEM((2, nkt, KEY_TILE, Q_BLOCK), F32),
                        pltpu.VMEM((2, 2, SUBLANES, LANES), F32),
                        pltpu.VMEM((nkt, ATTN_HEADS, KEY_TILE, Q_BLOCK), F32),
                        pltpu.VMEM((ATTN_HEADS // 2, KV_RANK + SUM_ROWS, 2 * Q_BLOCK), F32)],
        compiler_params=_cparams(("parallel", "arbitrary")),
        name="dsa",
    )(qt, zq, qit, wt, qit, wt, ki, kv, kvt, tables, wuv)


def _pack3(v):
    lane = lax.broadcasted_iota(jnp.int32, v.shape, 1)
    v = jnp.where(lane < SSD_HEADS, v, 0.0)
    hi = v.astype(BF16).astype(F32)
    r = v - hi
    mid = r.astype(BF16).astype(F32)
    lo = r - mid
    return (hi + pltpu.roll(mid, SSD_HEADS, axis=1) + pltpu.roll(lo, 2 * SSD_HEADS, axis=1)).astype(BF16)


def _cumsum_rows(x):
    n = x.shape[0]
    r = lax.broadcasted_iota(jnp.int32, x.shape, 0)
    s = 1
    while s < n:
        x = x + jnp.where(r >= s, pltpu.roll(x, s, axis=0), 0.0)
        s *= 2
    return x


def _shift_rows(x, s):
    r = pltpu.roll(x, s, axis=1)
    prev = jnp.concatenate([r[-1:], r[:-1]], axis=0)
    sub = lax.broadcasted_iota(jnp.int32, x.shape, 1)
    return jnp.where(sub >= s, r, prev)


def _ssd_kernel(z_ref, xbc_ref, dt_ref, cw_ref, cb_ref, dtb_ref, alog_ref, dsk_ref, nw_ref, e_ref,
                y_ref, tail_ref, u_ref, g_ref, state_ref):
    nq = SSD_Q

    @pl.when(pl.program_id(1) == 0)
    def _():
        tail_ref[...] = jnp.zeros_like(tail_ref)
        state_ref[...] = jnp.zeros_like(state_ref)

    assert CONV_W == 4
    for blk in range(CONV_DIM // CONV_BLOCK):
        cols = slice(blk * CONV_BLOCK, (blk + 1) * CONV_BLOCK)
        ext = jnp.concatenate([tail_ref[:, cols], xbc_ref[0, :, cols].astype(F32)], axis=0)
        ext = ext.reshape(1 + nq // SUBLANES, SUBLANES, CONV_BLOCK)
        s1 = _shift_rows(ext, 1)
        a = cw_ref[3:4, cols] * ext + cw_ref[2:3, cols] * s1 + cb_ref[:, cols]
        b = cw_ref[1:2, cols] * ext + cw_ref[0:1, cols] * s1
        conv = (a + _shift_rows(b, 2))[1:].reshape(nq, CONV_BLOCK)
        u_ref[:, cols] = _silu_of_half(conv)
        tail_ref[:, cols] = ext[nq // SUBLANES]

    t = dt_ref[0] + dtb_ref[...]
    dt = jnp.maximum(t, 0.0) + jnp.log1p(jnp.exp(-jnp.abs(t)))
    a2 = _cumsum_rows(dt * (-jnp.exp(alog_ref[...]))) * LOG2E
    a2_t = a2.T
    dt_p = _pack3(dt)
    dec_p = _pack3(dt * jnp.exp2(a2[nq - 1:nq, :] - a2))
    expa_p = _pack3(jnp.exp2(a2))

    r = lax.broadcasted_iota(jnp.int32, (nq, nq), 0)
    c = lax.broadcasted_iota(jnp.int32, (nq, nq), 1)
    causal = r >= c
    lane = lax.broadcasted_iota(jnp.int32, (nq, LANES), 1)
    heads_per_group = SSD_HEADS // SSD_GROUPS
    gw = heads_per_group * SSD_HEADDIM
    b_col = D_INNER
    c_col = D_INNER + SSD_GROUPS * D_STATE
    ssq = jnp.zeros((nq, LANES), F32)

    for g in range(SSD_GROUPS):
        gcols = slice(g * gw, (g + 1) * gw)
        eg = e_ref[:, gcols]
        dt_e = jnp.dot(dt_p, eg, preferred_element_type=F32)
        dec_e = jnp.dot(dec_p, eg, preferred_element_type=F32)
        expa_e = jnp.dot(expa_p, eg, preferred_element_type=F32)
        xs = u_ref[:, gcols]
        xdt_b = (xs * dt_e).astype(BF16)
        xdec_b = (xs * dec_e).astype(BF16)
        cg = u_ref[:, c_col + g * D_STATE:c_col + (g + 1) * D_STATE].astype(BF16)
        bgt = u_ref[:, b_col + g * D_STATE:b_col + (g + 1) * D_STATE].T.astype(BF16)
        cb = jnp.dot(cg, bgt, preferred_element_type=F32)
        prev = state_ref[g]
        y_off = jnp.dot(cg, prev.astype(BF16), preferred_element_type=F32) * expa_e
        pairs = []
        for pp in range(heads_per_group // 2):
            xp = xdt_b[:, pp * LANES:(pp + 1) * LANES]
            yh = []
            for hh in range(2):
                h = heads_per_group * g + 2 * pp + hh
                seg = a2[:, h:h + 1] - a2_t[h:h + 1, :]
                m = (cb * jnp.where(causal, jnp.exp2(seg), 0.0)).astype(BF16)
                yh.append(jnp.dot(m, xp, preferred_element_type=F32))
            pairs.append(jnp.where(lane < SSD_HEADDIM, yh[0], yh[1]))
        y = jnp.concatenate(pairs, axis=1) + y_off + dsk_ref[:, gcols] * xs
        gated = y * _silu_of_half(z_ref[0, :, gcols].astype(F32))
        g_ref[:, gcols] = gated
        sq = gated * gated
        ssq = ssq + sq[:, :LANES] + sq[:, LANES:]
        new = jnp.dot(bgt, xdec_b, preferred_element_type=F32)
        state_ref[g] = prev * expa_e[nq - 1:nq, :] + new
    scale = lax.rsqrt(jnp.sum(ssq, axis=-1, keepdims=True) * (1.0 / D_INNER) + EPS)
    y_ref[0] = (g_ref[...] * scale * nw_ref[...]).astype(BF16)


def _ssd(head, wide, conv_w, conv_b, dt_bias, a_log, d_skip, ssd_norm):
    bsz, seq, _ = head.shape
    nq = SSD_Q

    def pad_heads(v):
        return jnp.concatenate([v, jnp.zeros((LANES - SSD_HEADS,), F32)]).reshape(1, LANES)

    sel = np.concatenate([np.eye(SSD_HEADS)] * 3 + [np.zeros((LANES - 3 * SSD_HEADS, SSD_HEADS))], axis=0)
    e = jnp.asarray(np.kron(sel, np.ones((1, SSD_HEADDIM))), BF16)
    return pl.pallas_call(
        _ssd_kernel,
        grid=(bsz, seq // nq),
        in_specs=[pl.BlockSpec((1, nq, D_INNER), lambda b, i: (b, i, WIDE_Z // D_INNER)),
                  pl.BlockSpec((1, nq, CONV_DIM), lambda b, i: (b, i, WIDE_XBC // CONV_DIM)),
                  pl.BlockSpec((1, nq, LANES), lambda b, i: (b, i, HEAD_DT // LANES)),
                  _const_spec((CONV_W, CONV_DIM)), _const_spec((1, CONV_DIM)),
                  _const_spec((1, LANES)), _const_spec((1, LANES)),
                  _const_spec((1, D_INNER)), _const_spec((1, D_INNER)),
                  _const_spec((LANES, D_INNER))],
        out_specs=pl.BlockSpec((1, nq, D_INNER), lambda b, i: (b, i, 0)),
        out_shape=jax.ShapeDtypeStruct((bsz, seq, D_INNER), BF16),
        scratch_shapes=[pltpu.VMEM((SUBLANES, CONV_DIM), F32),
                        pltpu.VMEM((nq, CONV_DIM), F32),
                        pltpu.VMEM((nq, D_INNER), F32),
                        pltpu.VMEM((SSD_GROUPS, D_STATE, 4 * SSD_HEADDIM), F32)],
        compiler_params=_cparams(("parallel", "arbitrary")),
        name="ssd",
    )(wide, wide, head, 0.5 * conv_w, 0.5 * conv_b.reshape(1, CONV_DIM), pad_heads(dt_bias), pad_heads(a_log),
      jnp.repeat(d_skip, SSD_HEADDIM).reshape(1, D_INNER), ssd_norm.reshape(1, D_INNER), e)


def _mix_kernel(ao_ref, sy_ref, ga_ref, gb_ref, x_ref, mod_ref, nw_ref, woa_ref, wos_ref, wout_ref, o_ref):
    ya = jnp.dot(ao_ref[0], woa_ref[...], preferred_element_type=F32)
    yb = jnp.dot(sy_ref[0], wos_ref[...], preferred_element_type=F32)
    mix = jax.nn.sigmoid(ga_ref[0].astype(F32)) * ya + jax.nn.sigmoid(gb_ref[0].astype(F32)) * yb
    m2 = jnp.dot(mix.astype(BF16), wout_ref[...], preferred_element_type=F32)
    o_ref[0] = x_ref[0] + mod_ref[0][2:3] * _rms(m2, nw_ref[...])


def _mix(attn_o, ssd_y, wide, x, mod3, post_norm, w_o_attn, w_o_ssd, w_out):
    bsz, seq, _ = x.shape
    tm = min(seq, 512)

    def rows(width, col_block=0):
        return pl.BlockSpec((1, tm, width), lambda b, i: (b, i, col_block))

    return pl.pallas_call(
        _mix_kernel,
        grid=(bsz, seq // tm),
        in_specs=[rows(D_MODEL), rows(D_INNER), rows(D_MODEL, WIDE_GA // D_MODEL), rows(D_MODEL, WIDE_GB // D_MODEL),
                  rows(D_MODEL), pl.BlockSpec((1, 6, D_MODEL), lambda b, i: (b, 0, 0)),
                  _const_spec((1, D_MODEL)), _const_spec((D_MODEL, D_MODEL)),
                  _const_spec((D_INNER, D_MODEL)), _const_spec((D_MODEL, D_MODEL))],
        out_specs=rows(D_MODEL),
        out_shape=jax.ShapeDtypeStruct((bsz, seq, D_MODEL), F32),
        compiler_params=_cparams(("parallel", "parallel")),
        name="mix",
    )(attn_o, ssd_y, wide, wide, x, mod3, post_norm.reshape(1, D_MODEL),
      w_o_attn.astype(BF16), w_o_ssd.astype(BF16), w_out.astype(BF16))


def _ffn_kernel(x_ref, mod_ref, nw1_ref, nw2_ref, wg_ref, wu_ref, wo_ref, o_ref):
    x = x_ref[0]
    m = mod_ref[0]
    h2 = (_rms(x, nw1_ref[...]) * (1.0 + m[4:5]) + m[3:4]).astype(BF16)
    ug = jnp.dot(h2, wg_ref[...], preferred_element_type=F32)
    uu = jnp.dot(h2, wu_ref[...], preferred_element_type=F32)
    f = jnp.dot((_silu(ug) * uu).astype(BF16), wo_ref[...], preferred_element_type=F32)
    o_ref[0] = x + m[5:6] * _rms(f, nw2_ref[...])


def _ffn(x, mod3, pre_norm, post_norm, w_ffn_in, w_ffn_out):
    bsz, seq, _ = x.shape
    tm = min(seq, 512)
    rows = pl.BlockSpec((1, tm, D_MODEL), lambda b, i: (b, i, 0))
    w_in = w_ffn_in.astype(BF16)

    def half(k):
        return pl.BlockSpec((D_MODEL, D_FF), lambda b, i: (0, k), pipeline_mode=pl.Buffered(1))

    return pl.pallas_call(
        _ffn_kernel,
        grid=(bsz, seq // tm),
        in_specs=[rows, pl.BlockSpec((1, 6, D_MODEL), lambda b, i: (b, 0, 0)),
                  _const_spec((1, D_MODEL)), _const_spec((1, D_MODEL)),
                  half(0), half(1), _const_spec((D_FF, D_MODEL))],
        out_specs=rows,
        out_shape=jax.ShapeDtypeStruct((bsz, seq, D_MODEL), F32),
        compiler_params=_cparams(("parallel", "parallel")),
        name="ffn",
    )(x, mod3, pre_norm.reshape(1, D_MODEL), post_norm.reshape(1, D_MODEL), w_in, w_in, w_ffn_out.astype(BF16))


def kernel(x, c, positions, ada_w, ada_b, pre_norm_mix, post_norm_mix, pre_norm_ffn, post_norm_ffn, w_in, q_norm, kv_norm, w_uq, w_uv, rel_bias, w_qidx, kidx_norm, conv_w, conv_b, dt_bias, a_log, d_skip, ssd_norm, w_o_attn, w_o_ssd, w_out, w_ffn_in, w_ffn_out):
    del positions
    bsz, seq, _ = x.shape
    assert seq % (2 * KEY_TILE) == 0 and x.shape[-1] == D_MODEL
    mod3 = _mod(c, ada_w, ada_b).reshape(bsz, 6, D_MODEL)
    head, wide = _inproj(x, mod3, pre_norm_mix, _pack_w_in(w_in))
    qt, qit, kv, kvt, ki, wt = _prep(head, q_norm, kv_norm, kidx_norm, w_uq, w_qidx)
    attn_o = _dsa(qt, qit, wt, ki, kv, kvt, _bias_tables(rel_bias), _pack_w_uv(w_uv))
    ssd_y = _ssd(head, wide, conv_w, conv_b, dt_bias, a_log, d_skip, ssd_norm)
    x1 = _mix(attn_o, ssd_y, wide, x, mod3, post_norm_mix, w_o_attn, w_o_ssd, w_out)
    return _ffn(x1, mod3, pre_norm_ffn, post_norm_ffn, w_ffn_in, w_ffn_out)
```
